```python
import jax
import jax.numpy as jnp
from jax import lax
import numpy as np

D_MODEL = 2048
BATCH = 2
SEQ = 16384
DEPTH = 2

GRID_W = 64
CTX_LEN = 256
EPS = 1e-6
HEAD_DIM = 128
A_Q_HEADS = D_MODEL // (2 * HEAD_DIM)
A_KV_HEADS = max(1, A_Q_HEADS // 4)
A_GROUP = A_Q_HEADS // A_KV_HEADS
WINDOW = 128
BLOCK = 128
ROPE_BASE = 10000.0
B_HEADS = D_MODEL // (2 * HEAD_DIM)
B_DK = HEAD_DIM
B_DV = HEAD_DIM
CHUNK = 64
A_Q = A_Q_HEADS * HEAD_DIM
A_KV = A_KV_HEADS * HEAD_DIM
B_W = B_HEADS * B_DK
AB_IN = A_Q + 2 * A_KV + 5 * B_W
AB_OUT = A_Q + B_HEADS * B_DV
C_HEAD = 64
C_HEADS = D_MODEL // C_HEAD
DECAY_LORA = max(32, int(round(1.8 * D_MODEL ** 0.5 / 32)) * 32)
AAA_LORA = max(32, int(round(1.8 * D_MODEL ** 0.5 / 32)) * 32)
GATE_LORA = max(32, int(round(0.6 * D_MODEL ** 0.8 / 32)) * 32)
GN_EPS = 64e-5
D_FF = 4 * D_MODEL
N_EVEN = (DEPTH + 1) // 2
N_ODD = DEPTH // 2
F32 = jnp.float32

kernel_name = 'hybrid_swa_hgrn2_rwkv7_prefix_dit'


def _rmsnorm(x, gain):
    xf = x.astype(F32)
    y = xf * lax.rsqrt(jnp.mean(xf * xf, axis=-1, keepdims=True) + EPS)
    return (y * gain.astype(F32)).astype(x.dtype)


def _adaln(cond, w_mod, b_mod):
    m = (jax.nn.silu(cond) @ w_mod + b_mod)[..., None, :]
    return jnp.split(m, 6, axis=-1)


def _axial_rope(x, rows, cols):
    half = HEAD_DIM // 2
    n_freq = half // 2
    inv = ROPE_BASE ** (-jnp.arange(n_freq, dtype=F32) / n_freq)

    def rot(xa, pos):
        ang = pos.astype(F32)[:, None] * inv[None, :]
        cos = jnp.cos(ang)[None, :, None, :].astype(xa.dtype)
        sin = jnp.sin(ang)[None, :, None, :].astype(xa.dtype)
        x1, x2 = xa[..., :n_freq], xa[..., n_freq:]
        return jnp.concatenate([x1 * cos - x2 * sin, x2 * cos + x1 * sin], axis=-1)

    return jnp.concatenate([rot(x[..., :half], rows), rot(x[..., half:], cols)], axis=-1)


def _window_attention(q, k, v, kc, vc, sink):
    B, L = q.shape[0], q.shape[1]
    nb = L // BLOCK
    scale = HEAD_DIM ** -0.5
    qg = q.reshape(B, L, A_KV_HEADS, A_GROUP, HEAD_DIM)
    pad = ((0, 0), (BLOCK, BLOCK), (0, 0), (0, 0))
    kp, vp = jnp.pad(k, pad), jnp.pad(v, pad)
    sink_b = jnp.broadcast_to(sink.astype(F32).reshape(1, A_KV_HEADS, A_GROUP, 1, 1),
                              (B, A_KV_HEADS, A_GROUP, BLOCK, 1))
    n_band = 3 * BLOCK

    def block(n):
        start = n * BLOCK
        qb = lax.dynamic_slice_in_dim(qg, start, BLOCK, axis=1)
        kb = lax.dynamic_slice_in_dim(kp, start, n_band, axis=1)
        vb = lax.dynamic_slice_in_dim(vp, start, n_band, axis=1)
        qpos = start + jnp.arange(BLOCK)
        kpos = start - BLOCK + jnp.arange(n_band)
        valid = ((jnp.abs(qpos[:, None] - kpos[None, :]) <= WINDOW)
                 & (kpos >= 0)[None, :] & (kpos < L)[None, :])
        s_win = jnp.einsum('bqhgd,bkhd->bhgqk', qb, kb).astype(F32) * scale
        s_win = jnp.where(valid, s_win, -jnp.inf)
        s_ctx = jnp.einsum('bqhgd,bchd->bhgqc', qb, kc).astype(F32) * scale
        p = jax.nn.softmax(jnp.concatenate([sink_b, s_win, s_ctx], axis=-1), axis=-1)
        p_win = p[..., 1:1 + n_band].astype(v.dtype)
        p_ctx = p[..., 1 + n_band:].astype(v.dtype)
        return (jnp.einsum('bhgqk,bkhd->bqhgd', p_win, vb)
                + jnp.einsum('bhgqc,bchd->bqhgd', p_ctx, vc))

    out = lax.map(block, jnp.arange(nb))
    return jnp.moveaxis(out, 0, 1).reshape(B, L, A_Q)


def _context_attention(q, k, v, sink):
    B, Lc = q.shape[0], q.shape[1]
    qg = q.reshape(B, Lc, A_KV_HEADS, A_GROUP, HEAD_DIM)
    s = jnp.einsum('bqhgd,bkhd->bhgqk', qg, k).astype(F32) * HEAD_DIM ** -0.5
    sink_b = jnp.broadcast_to(sink.astype(F32).reshape(1, A_KV_HEADS, A_GROUP, 1, 1),
                              (B, A_KV_HEADS, A_GROUP, Lc, 1))
    p = jax.nn.softmax(jnp.concatenate([sink_b, s], axis=-1), axis=-1)[..., 1:].astype(v.dtype)
    return jnp.einsum('bhgqk,bkhd->bqhgd', p, v).reshape(B, Lc, A_Q)


def _chunk_gla(q, k, v, log_f, s0):
    N, H, L, _ = q.shape
    dv = v.shape[-1]
    nc = L // CHUNK
    out_dtype = v.dtype

    def to_chunks(t):
        t = t.astype(F32)
        return jnp.moveaxis(t.reshape(N, H, nc, CHUNK, t.shape[-1]), 2, 0)

    tril = jnp.tril(jnp.ones((CHUNK, CHUNK), dtype=bool))

    def step(S, inp):
        qc, kc, vc, gc = inp
        b = jnp.cumsum(gc, axis=2)
        o_inter = jnp.einsum('nhtk,nhkv->nhtv', qc * jnp.exp(b), S)
        rel = b[:, :, :, None, :] - b[:, :, None, :, :]
        dec = jnp.exp(jnp.where(tril[:, :, None], rel, -jnp.inf))
        att = jnp.einsum('nhtk,nhsk,nhtsk->nhts', qc, kc, dec)
        o = o_inter + jnp.einsum('nhts,nhsv->nhtv', att, vc)
        b_last = b[:, :, -1:, :]
        S_new = (jnp.exp(b_last[:, :, 0, :])[..., None] * S
                 + jnp.einsum('nhsk,nhsv->nhkv', kc * jnp.exp(b_last - b), vc))
        return S_new, o

    S_fin, o = lax.scan(step, s0, (to_chunks(q), to_chunks(k), to_chunks(v), to_chunks(log_f)))
    o = jnp.moveaxis(o, 0, 2).reshape(N, H, L, dv).astype(out_dtype)
    return o, S_fin


def _bidir_gla(q, k2, v, g2, s0):
    B = q.shape[0]
    flip = lambda t: jnp.flip(t, axis=2)
    o, S = _chunk_gla(jnp.concatenate([q, flip(q)], 0), jnp.concatenate([k2[0], flip(k2[1])], 0),
                      jnp.concatenate([v, flip(v)], 0), jnp.concatenate([g2[0], flip(g2[1])], 0), s0)
    return o[:B] + flip(o[B:]), S


def _hgrn2_features(bq, bi, bff, bfb, lb):
    def heads(t):
        t = t.reshape(t.shape[:-1] + (B_HEADS, t.shape[-1] // B_HEADS))
        return jnp.swapaxes(t, -2, -3)
    f = lb + (1.0 - lb) * jax.nn.sigmoid(jnp.stack([bff, bfb]).astype(F32))
    q = heads(jax.nn.silu(bq).astype(F32))
    v = heads(bi.astype(F32))
    return q, heads(1.0 - f), v, heads(jnp.log(f))


def _hgrn2_readout(o, bg, onorm):
    o = jnp.swapaxes(o, 1, 2).astype(F32)
    o = o * lax.rsqrt(jnp.mean(o * o, axis=-1, keepdims=True) + EPS)
    o = o.reshape(o.shape[0], o.shape[1], B_HEADS * B_DV) * onorm.astype(F32)
    return (o * jax.nn.silu(bg.astype(F32))).astype(bg.dtype)


def _ab_mixer(u_ctx, u_lat, w_in, w_out, sink, lb, onorm, rows, cols, ctx_out):
    B, L, _ = u_lat.shape
    Lc = u_ctx.shape[1]
    cut = np.cumsum([A_Q, A_KV, A_KV, B_W, B_W, B_W, B_W]).tolist()
    pc = jnp.split(u_ctx @ w_in, cut, axis=-1)
    pl = jnp.split(u_lat @ w_in, cut, axis=-1)
    qa = _axial_rope(pl[0].reshape(B, L, A_Q_HEADS, HEAD_DIM), rows, cols)
    ka = _axial_rope(pl[1].reshape(B, L, A_KV_HEADS, HEAD_DIM), rows, cols)
    va = pl[2].reshape(B, L, A_KV_HEADS, HEAD_DIM)
    kca = pc[1].reshape(B, Lc, A_KV_HEADS, HEAD_DIM)
    vca = pc[2].reshape(B, Lc, A_KV_HEADS, HEAD_DIM)
    oa = _window_attention(qa, ka, va, kca, vca, sink)
    qc, kc, vc, gc = _hgrn2_features(pc[3], pc[4], pc[5], pc[6], lb)
    s0 = jnp.zeros((2 * B, B_HEADS, B_DK, B_DV), F32)
    oc_b, s_ctx = _bidir_gla(qc, kc, vc, gc, s0)
    ql, kl, vl, gl = _hgrn2_features(pl[3], pl[4], pl[5], pl[6], lb)
    ol_b, _ = _bidir_gla(ql, kl, vl, gl, s_ctx)
    y_lat = jnp.concatenate([oa, _hgrn2_readout(ol_b, pl[7], onorm)], axis=-1) @ w_out
    y_ctx = None
    if ctx_out:
        oca = _context_attention(pc[0].reshape(B, Lc, A_Q_HEADS, HEAD_DIM), kca, vca, sink)
        y_ctx = jnp.concatenate([oca, _hgrn2_readout(oc_b, pc[7], onorm)], axis=-1) @ w_out
    return y_ctx, y_lat


def _rwkv7_prep(u, mix, wr, wk, wv, w0, w1, w2, a0, a1, a2, g1, g2, k_k, k_a):
    ut = jnp.swapaxes(u, 0, 1)
    up = jnp.pad(ut, ((1, 1), (0, 0), (0, 0)))
    xx = 0.5 * (up[:-2] + up[2:]) - ut
    xr, xw, xk, xv, xa, xg = (ut + xx * mix[i] for i in range(6))
    hd = lambda t: t.reshape(t.shape[:-1] + (C_HEADS, C_HEAD)).astype(F32)
    r = hd(xr @ wr)
    k = xk @ wk
    v = hd(xv @ wv)
    g = jax.nn.sigmoid(xg @ g1) @ g2
    w_lora = jnp.einsum('elbr,erd->elbd', jnp.tanh(jnp.einsum('lbd,edr->elbr', xw, w1)), w2)
    w_log = -jax.nn.softplus(-(w0[:, None, None, :] + w_lora).astype(F32)) - 0.5
    decay = hd(jnp.exp(-jnp.exp(w_log)))
    a = jax.nn.sigmoid((a0[:, None, None, :]
                        + jnp.einsum('elbr,erd->elbd', jnp.einsum('lbd,edr->elbr', xa, a1), a2)).astype(F32))
    kk = hd(k * k_k)
    kk = kk / jnp.maximum(jnp.sqrt(jnp.sum(kk * kk, axis=-1, keepdims=True)), 1e-12)
    kd = hd(k[None].astype(F32) * (1.0 + (a - 1.0) * k_a.astype(F32)))
    b = kk[None] * hd(a)
    return r, decay, kd, v, kk, b, g


def _rwkv7_step(S, inp):
    r_t, w_t, k_t, v_t, kk_t, b_t = inp
    sa = jnp.einsum('bhvk,bhk->bhv', S, kk_t)
    S = S * w_t[:, :, None, :] - sa[..., None] * b_t[:, :, None, :] + v_t[..., None] * k_t[:, :, None, :]
    return S, jnp.einsum('bhvk,bhk->bhv', S, r_t)


def _rwkv7_bidir(r, w, k, v, kk, b, s_f, s_b):
    s_f, y_f = lax.scan(_rwkv7_step, s_f, (r, w[0], k[0], v, kk, b[0]))
    s_b, y_b = lax.scan(_rwkv7_step, s_b, (r, w[1], k[1], v, kk, b[1]), reverse=True)
    return y_f + y_b, s_f, s_b


def _rwkv7_out(y, r, kd, v, g, r_k, ln_g, ln_b, wo, dtype):
    L, B = y.shape[0], y.shape[1]
    mu = jnp.mean(y, axis=-1, keepdims=True)
    var = jnp.mean(jnp.square(y - mu), axis=-1, keepdims=True)
    yn = ((y - mu) * lax.rsqrt(var + GN_EPS)).reshape(L, B, D_MODEL) * ln_g.astype(F32) + ln_b.astype(F32)
    bonus = (jnp.sum(r * (kd[0] + kd[1]) * r_k.astype(F32), axis=-1, keepdims=True) * v).reshape(L, B, D_MODEL)
    o = ((yn + bonus).astype(dtype) * g) @ wo
    return jnp.swapaxes(o, 0, 1)


def _rwkv7_mixer(u_ctx, u_lat, prm, ctx_out):
    mix, wr, wk, wv, wo, w0, w1, w2, a0, a1, a2, g1, g2, k_k, k_a, r_k, ln_g, ln_b = prm
    pre = (mix, wr, wk, wv, w0, w1, w2, a0, a1, a2, g1, g2, k_k, k_a)
    zeros = jnp.zeros((u_lat.shape[0], C_HEADS, C_HEAD, C_HEAD), F32)
    rc, wc, kc, vc, kkc, bc, gc = _rwkv7_prep(u_ctx, *pre)
    yc, s_f, s_b = _rwkv7_bidir(rc, wc, kc, vc, kkc, bc, zeros, zeros)
    r, w, kd, v, kk, b, g = _rwkv7_prep(u_lat, *pre)
    y, _, _ = _rwkv7_bidir(r, w, kd, v, kk, b, s_f, s_b)
    y_lat = _rwkv7_out(y, r, kd, v, g, r_k, ln_g, ln_b, wo, u_lat.dtype)
    y_ctx = _rwkv7_out(yc, rc, kc, vc, gc, r_k, ln_g, ln_b, wo, u_ctx.dtype) if ctx_out else None
    return y_ctx, y_lat


def _sqrelu_mlp(u, w_up, w_down):
    return jnp.square(jax.nn.relu(u @ w_up)) @ w_down


def setup_inputs(seed: int = 0) -> dict:
    key = jax.random.key(seed)
    ks = iter(jax.random.split(key, 40))
    D = D_MODEL

    def nrm(shape, scale):
        return jax.random.normal(next(ks), shape, F32) * scale

    def gain(shape):
        return 1.0 + nrm(shape, 0.02)

    def unif(shape, lo, hi):
        return jax.random.uniform(next(ks), shape, F32, lo, hi)

    return {
        'x': nrm((BATCH, SEQ, D), 1.0),
        'c': nrm((BATCH, D), 1.0),
        'ctx': nrm((BATCH, CTX_LEN, D), 1.0),
        'c_ctx': nrm((D,), 1.0),
        'mod_w': nrm((DEPTH, D, 6 * D), D ** -0.5),
        'mod_b': nrm((DEPTH, 6 * D), 0.02),
        'norm_mix': gain((DEPTH, D)),
        'norm_ffn': gain((DEPTH, D)),
        'ffn_up': nrm((DEPTH, D, D_FF), D ** -0.5),
        'ffn_down': nrm((DEPTH, D_FF, D), D_FF ** -0.5),
        'ab_w_in': nrm((N_EVEN, D, AB_IN), D ** -0.5),
        'ab_w_out': nrm((N_EVEN, AB_OUT, D), AB_OUT ** -0.5),
        'attn_sink': nrm((N_EVEN, A_Q_HEADS), 0.5),
        'hgrn_lb': nrm((N_EVEN + 1, B_W), 0.5),
        'hgrn_onorm': gain((N_EVEN, B_W)),
        'rk_mix': unif((N_ODD, 6, D), 0.0, 1.0),
        'rk_wr': nrm((N_ODD, D, D), D ** -0.5),
        'rk_wk': nrm((N_ODD, D, D), D ** -0.5),
        'rk_wv': nrm((N_ODD, D, D), D ** -0.5),
        'rk_wo': nrm((N_ODD, D, D), D ** -0.5),
        'rk_w0': unif((N_ODD, 2, D), -3.0, 1.0),
        'rk_w1': nrm((N_ODD, 2, D, DECAY_LORA), D ** -0.5),
        'rk_w2': nrm((N_ODD, 2, DECAY_LORA, D), 0.1 * DECAY_LORA ** -0.5),
        'rk_a0': nrm((N_ODD, 2, D), 0.5),
        'rk_a1': nrm((N_ODD, 2, D, AAA_LORA), D ** -0.5),
        'rk_a2': nrm((N_ODD, 2, AAA_LORA, D), 0.1 * AAA_LORA ** -0.5),
        'rk_g1': nrm((N_ODD, D, GATE_LORA), D ** -0.5),
        'rk_g2': nrm((N_ODD, GATE_LORA, D), GATE_LORA ** -0.5),
        'rk_kk': 0.85 + nrm((N_ODD, D), 0.05),
        'rk_ka': gain((N_ODD, D)),
        'rk_rk': nrm((N_ODD, C_HEADS, C_HEAD), 0.1),
        'rk_ln_g': gain((N_ODD, D)),
        'rk_ln_b': nrm((N_ODD, D), 0.02),
        'final_norm': gain((D,)),
    }


def reference(x, c, ctx, c_ctx, mod_w, mod_b, norm_mix, norm_ffn, ffn_up, ffn_down,
              ab_w_in, ab_w_out, attn_sink, hgrn_lb, hgrn_onorm,
              rk_mix, rk_wr, rk_wk, rk_wv, rk_wo, rk_w0, rk_w1, rk_w2, rk_a0, rk_a1, rk_a2,
              rk_g1, rk_g2, rk_kk, rk_ka, rk_rk, rk_ln_g, rk_ln_b, final_norm):
    n_tok = x.shape[1]
    n_rows = n_tok // GRID_W
    rows = jnp.repeat(jnp.arange(n_rows, dtype=jnp.int32), GRID_W)
    cols = jnp.tile(jnp.arange(GRID_W, dtype=jnp.int32), n_rows)
    lb_all = jnp.cumsum(jax.nn.softmax(hgrn_lb.astype(F32), axis=0), axis=0)
    h, hc = x, ctx
    for layer in range(DEPTH):
        last = layer == DEPTH - 1
        j = layer // 2
        sh1, sc1, gt1, sh2, sc2, gt2 = _adaln(c, mod_w[layer], mod_b[layer])
        csh1, csc1, cgt1, csh2, csc2, cgt2 = _adaln(c_ctx, mod_w[layer], mod_b[layer])
        u = _rmsnorm(h, norm_mix[layer]) * (1 + sc1) + sh1
        uc = _rmsnorm(hc, norm_mix[layer]) * (1 + csc1) + csh1
        if layer % 2 == 0:
            yc, y = _ab_mixer(uc, u, ab_w_in[j], ab_w_out[j], attn_sink[j], lb_all[j],
                              hgrn_onorm[j], rows, cols, not last)
        else:
            prm = (rk_mix[j], rk_wr[j], rk_wk[j], rk_wv[j], rk_wo[j], rk_w0[j], rk_w1[j], rk_w2[j],
                   rk_a0[j], rk_a1[j], rk_a2[j], rk_g1[j], rk_g2[j], rk_kk[j], rk_ka[j], rk_rk[j],
                   rk_ln_g[j], rk_ln_b[j])
            yc, y = _rwkv7_mixer(uc, u, prm, not last)
        h = h + gt1 * y
        h = h + gt2 * _sqrelu_mlp(_rmsnorm(h, norm_ffn[layer]) * (1 + sc2) + sh2,
                                  ffn_up[layer], ffn_down[layer])
        if not last:
            hc = hc + cgt1 * yc
            hc = hc + cgt2 * _sqrelu_mlp(_rmsnorm(hc, norm_ffn[layer]) * (1 + csc2) + csh2,
                                         ffn_up[layer], ffn_down[layer])
    return _rmsnorm(h, final_norm)
```

```python
import functools

import jax
import jax.numpy as jnp
import numpy as np
from jax import lax
from jax.experimental import pallas as pl
from jax.experimental.pallas import tpu as pltpu

F32 = jnp.float32
BF16 = jnp.bfloat16
HIGHEST = lax.Precision.HIGHEST

LANES = 128
HEAD_DIM = 128
GRID_W = 64
WINDOW = 128
ROPE_BASE = 10000.0
A_Q_HEADS = 8
A_KV_HEADS = 2
A_GROUP = A_Q_HEADS // A_KV_HEADS
A_Q = A_Q_HEADS * HEAD_DIM
A_KV = A_KV_HEADS * HEAD_DIM
B_HEADS = 8
B_W = B_HEADS * HEAD_DIM
C_HEAD = 64
CHUNK = 64
SUB = 16
EPS = 1e-6
GN_EPS = 64e-5
VMEM_LIMIT = 58 * 1024 * 1024

NT_DIMS = (((1,), (1,)), ((), ()))
TN_DIMS = (((0,), (0,)), ((), ()))


def _dot(a, b, **kw):
    return jnp.dot(a, b, preferred_element_type=F32, **kw)


def _dot_nt(a, b):
    return lax.dot_general(a, b, NT_DIMS, preferred_element_type=F32)


def _dot_tn(a, b):
    return lax.dot_general(a, b, TN_DIMS, preferred_element_type=F32)


def _bf(x):
    return x.astype(BF16)


def _dot_split(x, m):
    hi = _bf(x)
    lo = _bf(x - hi.astype(F32))
    return _dot(hi, m) + _dot(lo, m)


def _sigmoid(x):
    return 1.0 / (1.0 + jnp.exp(-x))


def _silu(x):
    return x * _sigmoid(x)


def _normmod(x, g, sc, sh):
    ms = jnp.mean(x * x, axis=-1, keepdims=True)
    y = (x * lax.rsqrt(ms + EPS)) * g
    return y * (1.0 + sc) + sh


def _params(sem):
    return pltpu.CompilerParams(dimension_semantics=sem, vmem_limit_bytes=VMEM_LIMIT)


def _mod_kernel(c_ref, w_ref, b_ref, o_ref):
    s = _bf(_silu(c_ref[...]))
    o_ref[0] = _dot(s, _bf(w_ref[0])) + b_ref[0]


def _modulation(cond8, mod_w, mod_b):
    depth, d, n = mod_w.shape
    tn = 1024
    return pl.pallas_call(
        _mod_kernel,
        grid=(depth, n // tn),
        in_specs=[
            pl.BlockSpec((8, d), lambda l, j: (0, 0)),
            pl.BlockSpec((1, d, tn), lambda l, j: (l, 0, j)),
            pl.BlockSpec((1, 1, tn), lambda l, j: (l, 0, j)),
        ],
        out_specs=pl.BlockSpec((1, 8, tn), lambda l, j: (l, 0, j)),
        out_shape=jax.ShapeDtypeStruct((depth, 8, n), F32),
        compiler_params=_params(("arbitrary", "arbitrary")),
        name="modulation",
    )(cond8, mod_w, mod_b.reshape(depth, 1, n))


def _inproj_kernel(*refs, n_rope):
    if n_rope:
        x_ref, g_ref, sc_ref, sh_ref, w_ref, cos_ref, sna_ref, snb_ref, o_ref, u_ref = refs
    else:
        x_ref, g_ref, sc_ref, sh_ref, w_ref, o_ref, u_ref = refs
    j = pl.program_id(2)

    @pl.when(j == 0)
    def _():
        u_ref[...] = _bf(_normmod(x_ref[0], g_ref[...], sc_ref[0], sh_ref[0]))

    acc = _dot(u_ref[...], w_ref[...])
    if not n_rope:
        o_ref[0] = acc
        return

    per_tile = acc.shape[1] // HEAD_DIM
    n_full, n_rem = n_rope // per_tile, n_rope % per_tile

    def store(n_rot):
        cos, sna, snb = cos_ref[...], sna_ref[...], snb_ref[...]
        for hd in range(n_rot):
            sl = acc[:, hd * HEAD_DIM:(hd + 1) * HEAD_DIM]
            rot = sl * cos + pltpu.roll(sl, 96, 1) * sna + pltpu.roll(sl, 32, 1) * snb
            o_ref[0, :, hd * HEAD_DIM:(hd + 1) * HEAD_DIM] = rot
        if n_rot < per_tile:
            o_ref[0, :, n_rot * HEAD_DIM:] = acc[:, n_rot * HEAD_DIM:]

    pl.when(j < n_full)(lambda: store(per_tile))
    pl.when(j == n_full)(lambda: store(n_rem))
    pl.when(j > n_full)(lambda: store(0))


def _inproj(x, g, sc, sh, w, rope, tm, tn):
    bsz, seq, d = x.shape
    n = w.shape[1]
    n_rope = 0
    in_specs = [
        pl.BlockSpec((1, tm, d), lambda b, i, j: (b, i, 0)),
        pl.BlockSpec((1, d), lambda b, i, j: (0, 0)),
        pl.BlockSpec((1, 1, d), lambda b, i, j: (b, 0, 0)),
        pl.BlockSpec((1, 1, d), lambda b, i, j: (b, 0, 0)),
        pl.BlockSpec((d, tn), lambda b, i, j: (0, j)),
    ]
    args = [x, g, sc, sh, w]
    if rope is not None:
        n_rope = A_Q_HEADS + A_KV_HEADS
        assert tn % HEAD_DIM == 0
        in_specs += [pl.BlockSpec((tm, HEAD_DIM), lambda b, i, j: (i, 0))] * 3
        args += list(rope)
    return pl.pallas_call(
        functools.partial(_inproj_kernel, n_rope=n_rope),
        grid=(bsz, seq // tm, n // tn),
        in_specs=in_specs,
        out_specs=pl.BlockSpec((1, tm, tn), lambda b, i, j: (b, i, j)),
        out_shape=jax.ShapeDtypeStruct((bsz, seq, n), F32),
        scratch_shapes=[pltpu.VMEM((tm, d), BF16)],
        compiler_params=_params(("arbitrary", "arbitrary", "arbitrary")),
        name="ab_inproj",
    )(*args)


def _softmax_av(s_list, v_list, sink_col):
    m = sink_col
    for s in s_list:
        m = jnp.maximum(m, jnp.max(s, axis=-1, keepdims=True))
    den = jnp.exp(sink_col - m)
    out = None
    for s, v in zip(s_list, v_list):
        p = jnp.exp(s - m)
        den = den + jnp.sum(p, axis=-1, keepdims=True)
        o = _dot(_bf(p), v)
        out = o if out is None else out + o
    return out / den


def _sink_column(sink_ref, hk, rows):
    rowh = lax.broadcasted_iota(jnp.int32, (rows, 1), 0) // WINDOW
    col = jnp.full((rows, 1), sink_ref[hk * A_GROUP + A_GROUP - 1], F32)
    for g in range(A_GROUP - 1):
        col = jnp.where(rowh == g, sink_ref[hk * A_GROUP + g], col)
    return col


def _win_attn_kernel(sink_ref, q_ref, kp_ref, kc_ref, kn_ref, vp_ref, vc_ref, vn_ref,
                     ck_ref, cv_ref, o_ref):
    n = pl.program_id(1)
    nb = pl.num_programs(1)
    scale = HEAD_DIM ** -0.5
    q = q_ref[0]
    kband = jnp.concatenate([kp_ref[0], kc_ref[0], kn_ref[0]], axis=0)
    vband = jnp.concatenate([vp_ref[0], vc_ref[0], vn_ref[0]], axis=0)
    rows = A_GROUP * WINDOW
    tq = lax.broadcasted_iota(jnp.int32, (rows, 3 * WINDOW), 0) % WINDOW
    tk = lax.broadcasted_iota(jnp.int32, (rows, 3 * WINDOW), 1)
    rel = tk - WINDOW - tq
    valid = (jnp.abs(rel) <= WINDOW) & ((tk >= WINDOW) | (n > 0)) & ((tk < 2 * WINDOW) | (n < nb - 1))
    for hk in range(A_KV_HEADS):
        qs = jnp.concatenate(
            [q[:, (hk * A_GROUP + g) * HEAD_DIM:(hk * A_GROUP + g + 1) * HEAD_DIM] for g in range(A_GROUP)],
            axis=0)
        qs = _bf(qs)
        hs = slice(hk * HEAD_DIM, (hk + 1) * HEAD_DIM)
        s_win = _dot_nt(qs, _bf(kband[:, hs])) * scale
        s_win = jnp.where(valid, s_win, -jnp.inf)
        s_ctx = _dot_nt(qs, _bf(ck_ref[0][:, hs])) * scale
        o = _softmax_av([s_win, s_ctx], [_bf(vband[:, hs]), _bf(cv_ref[0][:, hs])],
                        _sink_column(sink_ref, hk, rows))
        for g in range(A_GROUP):
            h = hk * A_GROUP + g
            o_ref[0, :, h * HEAD_DIM:(h + 1) * HEAD_DIM] = _bf(o[g * WINDOW:(g + 1) * WINDOW])


def _win_attn(sink, proj, proj_ctx):
    bsz, seq, _ = proj.shape
    lc = proj_ctx.shape[1]
    nb = seq // WINDOW
    kcol, vcol = A_Q // A_KV, A_Q // A_KV + 1
    prev = lambda b, n: (b, jnp.maximum(n - 1, 0))
    nxt = lambda b, n: (b, jnp.minimum(n + 1, nb - 1))
    cur = lambda b, n: (b, n)

    def band(rowfn, col):
        return pl.BlockSpec((1, WINDOW, A_KV), lambda b, n: rowfn(b, n) + (col,))

    return pl.pallas_call(
        _win_attn_kernel,
        grid=(bsz, nb),
        in_specs=[
            pl.BlockSpec(memory_space=pltpu.SMEM),
            pl.BlockSpec((1, WINDOW, A_Q), lambda b, n: (b, n, 0)),
            band(prev, kcol), band(cur, kcol), band(nxt, kcol),
            band(prev, vcol), band(cur, vcol), band(nxt, vcol),
            pl.BlockSpec((1, lc, A_KV), lambda b, n: (b, 0, kcol)),
            pl.BlockSpec((1, lc, A_KV), lambda b, n: (b, 0, vcol)),
        ],
        out_specs=pl.BlockSpec((1, WINDOW, A_Q), lambda b, n: (b, n, 0)),
        out_shape=jax.ShapeDtypeStruct((bsz, seq, A_Q), BF16),
        compiler_params=_params(("arbitrary", "arbitrary")),
        name="window_attention",
    )(sink, proj, proj, proj, proj, proj, proj, proj, proj_ctx, proj_ctx)


def _ctx_attn_kernel(sink_ref, q_ref, k_ref, v_ref, o_ref):
    scale = HEAD_DIM ** -0.5
    q = q_ref[0]
    lc = q.shape[0]
    for h in range(A_Q_HEADS):
        hk = h // A_GROUP
        hs = slice(hk * HEAD_DIM, (hk + 1) * HEAD_DIM)
        qs = _bf(q[:, h * HEAD_DIM:(h + 1) * HEAD_DIM])
        s = _dot_nt(qs, _bf(k_ref[0][:, hs])) * scale
        sink_col = jnp.full((lc, 1), sink_ref[h], F32)
        o = _softmax_av([s], [_bf(v_ref[0][:, hs])], sink_col)
        o_ref[0, :, h * HEAD_DIM:(h + 1) * HEAD_DIM] = _bf(o)


def _ctx_attn(sink, proj_ctx):
    bsz, lc, _ = proj_ctx.shape
    kcol, vcol = A_Q // A_KV, A_Q // A_KV + 1
    return pl.pallas_call(
        _ctx_attn_kernel,
        grid=(bsz,),
        in_specs=[
            pl.BlockSpec(memory_space=pltpu.SMEM),
            pl.BlockSpec((1, lc, A_Q), lambda b: (b, 0, 0)),
            pl.BlockSpec((1, lc, A_KV), lambda b: (b, 0, kcol)),
            pl.BlockSpec((1, lc, A_KV), lambda b: (b, 0, vcol)),
        ],
        out_specs=pl.BlockSpec((1, lc, A_Q), lambda b: (b, 0, 0)),
        out_shape=jax.ShapeDtypeStruct((bsz, lc, A_Q), BF16),
        compiler_params=_params(("arbitrary",)),
        name="context_attention",
    )(sink, proj_ctx, proj_ctx, proj_ctx)


def _hgrn_chunk(bq, bi, bf, lb, st, rev):
    q = _silu(bq)
    v = bi
    f = lb + (1.0 - lb) * _sigmoid(bf)
    k = 1.0 - f
    g = jnp.log(f)
    ti = lax.broadcasted_iota(jnp.int32, (CHUNK, CHUNK), 0)
    si = lax.broadcasted_iota(jnp.int32, (CHUNK, CHUNK), 1)
    tri = jnp.where((si >= ti) if rev else (si <= ti), 1.0, 0.0).astype(F32)
    b = _dot(tri, g, precision=HIGHEST)
    btot = b[0:1] if rev else b[CHUNK - 1:CHUNK]
    vb = _bf(v)
    o = _dot_nt(_bf(q * jnp.exp(b)), _bf(st))
    nsub = CHUNK // SUB
    row16 = lax.broadcasted_iota(jnp.int32, (SUB, 1), 0)
    outs = []
    for blk in range(nsub):
        r0 = blk * SUB
        qi, ki, vi, bi_ = q[r0:r0 + SUB], k[r0:r0 + SUB], v[r0:r0 + SUB], b[r0:r0 + SUB]
        acc = o[r0:r0 + SUB]
        if rev and blk < nsub - 1:
            bref = b[r0 + SUB:r0 + SUB + 1]
            lo, hi = r0 + SUB, CHUNK
        elif (not rev) and blk > 0:
            bref = b[r0 - 1:r0]
            lo, hi = 0, r0
        else:
            bref = None
        if bref is not None:
            qn = _bf(qi * jnp.exp(bi_ - bref))
            kn = _bf(k[lo:hi] * jnp.exp(bref - b[lo:hi]))
            att = _dot_nt(qn, kn)
            acc = acc + _dot(_bf(att), vb[lo:hi])
        for s in range(SUB):
            mask = (row16 <= s) if rev else (row16 >= s)
            dec = jnp.exp(jnp.where(mask, bi_ - bi_[s:s + 1], -jnp.inf))
            w = jnp.sum(qi * dec * ki[s:s + 1], axis=-1, keepdims=True)
            acc = acc + w * vi[s:s + 1]
        outs.append(acc)
    o = jnp.concatenate(outs, axis=0)
    khat = _bf(k * jnp.exp(btot - b))
    st_new = st * jnp.exp(btot) + _dot_tn(vb, khat)
    return o, st_new


def _hgrn_kernel(q_ref, i_ref, f_ref, lb_ref, s0_ref, o_ref, sfin_ref, st_ref, *, rev, nchunk):
    c = pl.program_id(2)

    @pl.when(c == 0)
    def _():
        st_ref[...] = s0_ref[0, 0]

    lb = lb_ref[...]

    def body(ci, carry):
        cc = (nchunk - 1 - ci) if rev else ci
        rows = pl.ds(pl.multiple_of(cc * CHUNK, CHUNK), CHUNK)
        o, st_new = _hgrn_chunk(q_ref[0, rows, :], i_ref[0, rows, :], f_ref[0, rows, :], lb,
                                st_ref[...], rev)
        o_ref[0, rows, :] = o
        st_ref[...] = st_new
        return carry

    lax.fori_loop(0, nchunk, body, 0)

    @pl.when(c == pl.num_programs(2) - 1)
    def _():
        sfin_ref[0, 0] = st_ref[...]


def _hgrn_scan(proj, lb, s0, rev, tblk):
    bsz, seq, _ = proj.shape
    nblk = seq // tblk
    base = (A_Q + 2 * A_KV) // HEAD_DIM
    qc, ic = base, base + B_HEADS
    fc = base + (3 if rev else 2) * B_HEADS
    blk = (lambda c: nblk - 1 - c) if rev else (lambda c: c)

    def col(c0):
        return pl.BlockSpec((1, tblk, HEAD_DIM), lambda b, h, c: (b, blk(c), c0 + h))

    return pl.pallas_call(
        functools.partial(_hgrn_kernel, rev=rev, nchunk=tblk // CHUNK),
        grid=(bsz, B_HEADS, nblk),
        in_specs=[
            col(qc), col(ic), col(fc),
            pl.BlockSpec((1, HEAD_DIM), lambda b, h, c: (0, h)),
            pl.BlockSpec((1, 1, HEAD_DIM, HEAD_DIM), lambda b, h, c: (b, h, 0, 0)),
        ],
        out_specs=[
            pl.BlockSpec((1, tblk, HEAD_DIM), lambda b, h, c: (b, blk(c), h)),
            pl.BlockSpec((1, 1, HEAD_DIM, HEAD_DIM), lambda b, h, c: (b, h, 0, 0)),
        ],
        out_shape=[
            jax.ShapeDtypeStruct((bsz, seq, B_W), F32),
            jax.ShapeDtypeStruct((bsz, B_HEADS, HEAD_DIM, HEAD_DIM), F32),
        ],
        scratch_shapes=[pltpu.VMEM((HEAD_DIM, HEAD_DIM), F32)],
        compiler_params=_params(("arbitrary", "arbitrary", "arbitrary")),
        name="hgrn2_bwd" if rev else "hgrn2_fwd",
    )(proj, proj, proj, lb, s0)


def _ab_out_kernel(oa_ref, of_ref, ob_ref, g0_ref, g1_ref, on_ref, w_ref, h_ref, gt_ref, o_ref, lhs_ref):
    lhs_ref[:, :A_Q] = oa_ref[0]
    half = B_W // 2
    for hd in range(B_HEADS):
        sl = slice(hd * HEAD_DIM, (hd + 1) * HEAD_DIM)
        o = of_ref[0, :, sl] + ob_ref[0, :, sl]
        o = o * lax.rsqrt(jnp.mean(o * o, axis=-1, keepdims=True) + EPS)
        o = o * on_ref[:, sl]
        gref = g0_ref if hd * HEAD_DIM < half else g1_ref
        gs = slice(hd * HEAD_DIM % half, hd * HEAD_DIM % half + HEAD_DIM)
        o = o * _silu(gref[0, :, gs])
        lhs_ref[:, A_Q + hd * HEAD_DIM:A_Q + (hd + 1) * HEAD_DIM] = _bf(o)
    y = _dot(lhs_ref[...], w_ref[...])
    o_ref[0] = h_ref[0] + gt_ref[0] * y


def _ab_out(oa, of, ob, proj, onorm, w_out, h, gt, tm):
    bsz, seq, d = h.shape
    half = B_W // 2
    gcol = (A_Q + 2 * A_KV + 4 * B_W) // half
    row = lambda b, i: (b, i, 0)
    return pl.pallas_call(
        _ab_out_kernel,
        grid=(bsz, seq // tm),
        in_specs=[
            pl.BlockSpec((1, tm, A_Q), row),
            pl.BlockSpec((1, tm, B_W), row),
            pl.BlockSpec((1, tm, B_W), row),
            pl.BlockSpec((1, tm, half), lambda b, i: (b, i, gcol)),
            pl.BlockSpec((1, tm, half), lambda b, i: (b, i, gcol + 1)),
            pl.BlockSpec((1, B_W), lambda b, i: (0, 0)),
            pl.BlockSpec((A_Q + B_W, d), lambda b, i: (0, 0)),
            pl.BlockSpec((1, tm, d), row),
            pl.BlockSpec((1, 1, d), lambda b, i: (b, 0, 0)),
        ],
        out_specs=pl.BlockSpec((1, tm, d), row),
        out_shape=jax.ShapeDtypeStruct((bsz, seq, d), F32),
        scratch_shapes=[pltpu.VMEM((tm, A_Q + B_W), BF16)],
        compiler_params=_params(("arbitrary", "arbitrary")),
        name="ab_outproj",
    )(oa, of, ob, proj, proj, onorm, w_out, h, gt)


def _mlp_kernel(*refs, final):
    if final:
        x_ref, g_ref, sc_ref, sh_ref, gt_ref, wu_ref, wd_ref, fn_ref, o_ref, u_ref = refs
    else:
        x_ref, g_ref, sc_ref, sh_ref, gt_ref, wu_ref, wd_ref, o_ref, u_ref = refs
    j = pl.program_id(2)

    @pl.when(j == 0)
    def _():
        u_ref[...] = _bf(_normmod(x_ref[0], g_ref[...], sc_ref[0], sh_ref[0]))

    hid = jnp.maximum(_dot(u_ref[...], wu_ref[...]), 0.0)
    hid = _bf(hid * hid)
    ncol = 512

    @pl.when(j == 0)
    def _():
        for n0 in range(0, o_ref.shape[2], ncol):
            o_ref[0, :, n0:n0 + ncol] = _dot(hid, wd_ref[:, n0:n0 + ncol])

    @pl.when(j != 0)
    def _():
        for n0 in range(0, o_ref.shape[2], ncol):
            o_ref[0, :, n0:n0 + ncol] += _dot(hid, wd_ref[:, n0:n0 + ncol])

    @pl.when(j == pl.num_programs(2) - 1)
    def _():
        y = x_ref[0] + gt_ref[0] * o_ref[0]
        if final:
            y = (y * lax.rsqrt(jnp.mean(y * y, axis=-1, keepdims=True) + EPS)) * fn_ref[...]
        o_ref[0] = y


def _mlp(x, g, sc, sh, gt, w_up, w_down, final_gain, tm, tf):
    bsz, seq, d = x.shape
    dff = w_up.shape[1]
    vec = pl.BlockSpec((1, 1, d), lambda b, i, j: (b, 0, 0))
    in_specs = [
        pl.BlockSpec((1, tm, d), lambda b, i, j: (b, i, 0)),
        pl.BlockSpec((1, d), lambda b, i, j: (0, 0)),
        vec, vec, vec,
        pl.BlockSpec((d, tf), lambda b, i, j: (0, j)),
        pl.BlockSpec((tf, d), lambda b, i, j: (j, 0)),
    ]
    args = [x, g, sc, sh, gt, w_up, w_down]
    if final_gain is not None:
        in_specs.append(pl.BlockSpec((1, d), lambda b, i, j: (0, 0)))
        args.append(final_gain)
    return pl.pallas_call(
        functools.partial(_mlp_kernel, final=final_gain is not None),
        grid=(bsz, seq // tm, dff // tf),
        in_specs=in_specs,
        out_specs=pl.BlockSpec((1, tm, d), lambda b, i, j: (b, i, 0)),
        out_shape=jax.ShapeDtypeStruct((bsz, seq, d), F32),
        scratch_shapes=[pltpu.VMEM((tm, d), BF16)],
        compiler_params=_params(("arbitrary", "arbitrary", "arbitrary")),
        name="sqrelu_mlp",
    )(*args)


def _rk_proj_kernel(x_ref, xp_ref, xn_ref, g_ref, sc_ref, sh_ref, mix_ref, w_ref, o_ref,
                    u_ref, lhs_ref, *, nmix, nt):
    i = pl.program_id(1)
    j = pl.program_id(2)
    tm = x_ref.shape[1]

    @pl.when(j == 0)
    def _():
        g, sc, sh = g_ref[...], sc_ref[0], sh_ref[0]
        u = _normmod(x_ref[0], g, sc, sh)
        up = _normmod(xp_ref[0], g, sc, sh)[7:8]
        un = _normmod(xn_ref[0], g, sc, sh)[0:1]
        u_ref[8:8 + tm] = u
        u_ref[7:8] = jnp.where(i == 0, 0.0, up)
        u_ref[8 + tm:9 + tm] = jnp.where(i == pl.num_programs(1) - 1, 0.0, un)
        xx = 0.5 * (u_ref[7:7 + tm] + u_ref[9:9 + tm]) - u
        for m in range(nmix):
            lhs_ref[m] = _bf(u + xx * mix_ref[m:m + 1])

    o_ref[0, 0] = _dot(lhs_ref[j // nt], w_ref[0])


def _rk_proj(x, g, sc, sh, mix, w, tm, tn):
    bsz, seq, d = x.shape
    nmix, _, n = w.shape
    nt = n // tn
    r8 = tm // 8
    last8 = seq // 8 - 1
    vec = pl.BlockSpec((1, 1, d), lambda b, i, j: (b, 0, 0))
    return pl.pallas_call(
        functools.partial(_rk_proj_kernel, nmix=nmix, nt=nt),
        grid=(bsz, seq // tm, nmix * nt),
        in_specs=[
            pl.BlockSpec((1, tm, d), lambda b, i, j: (b, i, 0)),
            pl.BlockSpec((1, 8, d), lambda b, i, j: (b, jnp.maximum(i * r8 - 1, 0), 0)),
            pl.BlockSpec((1, 8, d), lambda b, i, j: (b, jnp.minimum((i + 1) * r8, last8), 0)),
            pl.BlockSpec((1, d), lambda b, i, j: (0, 0)),
            vec, vec,
            pl.BlockSpec((nmix, d), lambda b, i, j: (0, 0)),
            pl.BlockSpec((1, d, tn), lambda b, i, j: (j // nt, 0, j % nt)),
        ],
        out_specs=pl.BlockSpec((1, 1, tm, tn), lambda b, i, j: (j // nt, b, i, j % nt)),
        out_shape=jax.ShapeDtypeStruct((nmix, bsz, seq, n), F32),
        scratch_shapes=[pltpu.VMEM((tm + 16, d), F32), pltpu.VMEM((nmix, tm, d), BF16)],
        compiler_params=_params(("arbitrary", "arbitrary", "arbitrary")),
        name="rwkv_proj",
    )(x, x, x, g, sc, sh, mix, w)


def _softplus(x):
    return jnp.maximum(x, 0.0) + jnp.log(1.0 + jnp.exp(-jnp.abs(x)))


def _pair_consts(rev):
    n = 2 * CHUNK
    ri = lax.broadcasted_iota(jnp.int32, (n, n), 0)
    ci = lax.broadcasted_iota(jnp.int32, (n, n), 1)
    same = (ri // CHUNK) == (ci // CHUNK)
    rt, ct = ri % CHUNK, ci % CHUNK
    strict = same & ((ct > rt) if rev else (ct < rt))
    incl = same & ((ct >= rt) if rev else (ct <= rt))
    top = ri < CHUNK
    eye = jnp.where(ri == ci, 1.0, 0.0).astype(F32)
    ones_bd = jnp.where(same, 1.0, 0.0).astype(BF16)
    ti = lax.broadcasted_iota(jnp.int32, (CHUNK, CHUNK), 0)
    si = lax.broadcasted_iota(jnp.int32, (CHUNK, CHUNK), 1)
    tri = jnp.where((si >= ti) if rev else (si <= ti), 1.0, 0.0).astype(F32)
    lane = lax.broadcasted_iota(jnp.int32, (1, LANES), 1)
    m0 = jnp.where(lane < C_HEAD, 1.0, 0.0).astype(F32)
    m1 = 1.0 - m0
    return dict(strict=strict, incl=incl, top=top, eye=eye, ones_bd=ones_bd, tri=tri, m0=m0, m1=m1, ri=ri, ci=ci)


def _unit_tri_inverse(n, cs):
    ri, ci = cs["ri"], cs["ci"]
    t = cs["eye"] + jnp.where((ri // 2) == (ci // 2), n, 0.0)
    m = 2
    while m < CHUNK:
        off = jnp.where(((ri // (2 * m)) == (ci // (2 * m))) & ((ri // m) != (ci // m)), n, 0.0)
        tb = _bf(t)
        t = t + _dot(_bf(_dot(tb, _bf(off))), tb)
        m *= 2
    return t


def _stack2(x, cs):
    return jnp.concatenate([x * cs["m0"], x * cs["m1"]], axis=0)


def _rwkv_features(k, lw, la, w2, a2, w0, a0, kkv, kav, cs):
    z = w0 + _dot(_bf(jnp.tanh(lw)), w2)
    ld = -jnp.exp(-_softplus(-z) - 0.5)
    a = _sigmoid(a0 + _dot(_bf(la), a2))
    kkr = k * kkv
    ss = _dot_split(kkr * kkr, cs["ones_bd"])
    kk = kkr / jnp.maximum(jnp.sqrt(ss), 1e-12)
    kd = k * (1.0 + (a - 1.0) * kav)
    return ld, a, kk, kd


def _rwkv_chunk(r, k, v, lw, la, w2, a2, w0, a0, kkv, kav, ht, cs, rev):
    ld, a, kk, kd = _rwkv_features(k, lw, la, w2, a2, w0, a0, kkv, kav, cs)
    bb = kk * a
    cl = _dot(cs["tri"], ld, precision=HIGHEST)
    ctot = cl[0:1] if rev else cl[CHUNK - 1:CHUNK]
    e_neg = jnp.exp(-cl)
    e_tail = jnp.exp(ctot - cl)
    at = _bf(_stack2(-kk * jnp.exp(cl - ld), cs))
    rt = _bf(_stack2(r * jnp.exp(cl), cs))
    rhs = _bf(jnp.concatenate([bb * e_neg, kd * e_neg], axis=0))
    g1 = _dot_nt(at, rhs)
    g2 = _dot_nt(rt, rhs)
    g1r = pltpu.roll(g1, C_HEAD, 1)
    g2r = pltpu.roll(g2, C_HEAD, 1)
    top, strict, incl = cs["top"], cs["strict"], cs["incl"]
    n_ab = jnp.where(strict, jnp.where(top, g1, g1r), 0.0)
    n_ak = jnp.where(strict, jnp.where(top, g1r, g1), 0.0)
    n_rb = jnp.where(incl, jnp.where(top, g2, g2r), 0.0)
    n_rk = jnp.where(incl, jnp.where(top, g2r, g2), 0.0)
    t = _unit_tri_inverse(n_ab, cs)
    vs = _bf(_stack2(v, cs))
    htb = _bf(ht)
    x = _dot_nt(at, htb) + _dot(_bf(n_ak), vs)
    u = _bf(_dot(_bf(t), _bf(x)))
    y2 = _dot_nt(rt, htb) + _dot(_bf(n_rb), u) + _dot(_bf(n_rk), vs)
    y = y2[:CHUNK] + y2[CHUNK:]
    bk = _bf(jnp.concatenate([_stack2(bb * e_tail, cs), _stack2(kd * e_tail, cs)], axis=0))
    uv = jnp.concatenate([u, vs], axis=0)
    ht_new = ht * jnp.exp(ctot) + _dot_tn(uv, bk)
    return y, ht_new, a, kd


def _rwkv_scan_kernel(*refs, rev, nchunk, npair, combine):
    if combine:
        (r_ref, k_ref, v_ref, lw_ref, la_ref, w2_ref, a2_ref, w0_ref, a0_ref, kk_ref, ka_ref, s0_ref,
         yf_ref, laf_ref, a2f_ref, a0f_ref, rk_ref, lng_ref, lnb_ref, o_ref, sfin_ref, ht_ref) = refs
    else:
        (r_ref, k_ref, v_ref, lw_ref, la_ref, w2_ref, a2_ref, w0_ref, a0_ref, kk_ref, ka_ref, s0_ref,
         o_ref, sfin_ref, ht_ref) = refs
    c = pl.program_id(2)

    @pl.when(c == 0)
    def _():
        ht_ref[...] = s0_ref[0]

    cs = _pair_consts(rev)

    def body(ci, carry):
        cc = (nchunk - 1 - ci) if rev else ci
        rows = pl.ds(pl.multiple_of(cc * CHUNK, CHUNK), CHUNK)
        lw = lw_ref[0, 0, rows, :]
        la = la_ref[0, 0, rows, :]
        for pr in range(npair):
            ln = slice(pr * LANES, (pr + 1) * LANES)
            r, k, v = r_ref[0, 0, rows, ln], k_ref[0, 0, rows, ln], v_ref[0, 0, rows, ln]
            kav = ka_ref[:, ln]
            y, ht_new, a, kd = _rwkv_chunk(r, k, v, lw, la, w2_ref[0, :, ln], a2_ref[0, :, ln],
                                           w0_ref[0, :, ln], a0_ref[0, :, ln], kk_ref[:, ln], kav,
                                           ht_ref[pr], cs, rev)
            ht_ref[pr] = ht_new
            if combine:
                ysum = y + yf_ref[0, rows, ln]
                inv_n = 1.0 / C_HEAD
                mu = _dot_split(ysum, cs["ones_bd"]) * inv_n
                dev = ysum - mu
                var = _dot_split(dev * dev, cs["ones_bd"]) * inv_n
                yn = dev * lax.rsqrt(var + GN_EPS) * lng_ref[:, ln] + lnb_ref[:, ln]
                a_f = _sigmoid(a0f_ref[0, :, ln] + _dot(_bf(laf_ref[0, 0, rows, :]), a2f_ref[0, :, ln]))
                kd_f = k * (1.0 + (a_f - 1.0) * kav)
                bonus = _dot_split(r * (kd_f + kd) * rk_ref[:, ln], cs["ones_bd"]) * v
                y = yn + bonus
            o_ref[0, rows, ln] = y
        return carry

    lax.fori_loop(0, nchunk, body, 0)

    @pl.when(c == pl.num_programs(2) - 1)
    def _():
        sfin_ref[0] = ht_ref[...]


def _rwkv_scan(rkv, small, prm, s0, rev, tblk, npair, y_fwd=None):
    _, bsz, seq, d = rkv.shape
    nblk = seq // tblk
    width = npair * LANES
    e = 1 if rev else 0
    blk = (lambda c: nblk - 1 - c) if rev else (lambda c: c)
    combine = y_fwd is not None

    def tok(m):
        return pl.BlockSpec((1, 1, tblk, width), lambda b, p, c: (m, b, blk(c), p))

    def lora(m, half):
        return pl.BlockSpec((1, 1, tblk, LANES), lambda b, p, c: (m, b, blk(c), half))

    def mat(idx):
        return pl.BlockSpec((1, LANES, width), lambda b, p, c: (idx, 0, p))

    def vec3(idx):
        return pl.BlockSpec((1, 1, width), lambda b, p, c: (idx, 0, p))

    vec = pl.BlockSpec((1, width), lambda b, p, c: (0, p))
    state = pl.BlockSpec((1, npair, LANES, LANES), lambda b, p, c: (b, p, 0, 0))
    out_tok = pl.BlockSpec((1, tblk, width), lambda b, p, c: (b, blk(c), p))
    in_specs = [tok(0), tok(1), tok(2), lora(1, e), lora(2, e), mat(e), mat(e), vec3(e), vec3(e), vec, vec, state]
    args = [rkv, rkv, rkv, small, small, prm["w2"], prm["a2"], prm["w0"], prm["a0"], prm["k_k"], prm["k_a"], s0]
    if combine:
        in_specs += [out_tok, lora(2, 0), mat(0), vec3(0), vec, vec, vec]
        args += [y_fwd, small, prm["a2"], prm["a0"], prm["r_k"], prm["ln_g"], prm["ln_b"]]
    return pl.pallas_call(
        functools.partial(_rwkv_scan_kernel, rev=rev, nchunk=tblk // CHUNK, npair=npair, combine=combine),
        grid=(bsz, d // width, nblk),
        in_specs=in_specs,
        out_specs=[out_tok, state],
        out_shape=[
            jax.ShapeDtypeStruct((bsz, seq, d), F32),
            jax.ShapeDtypeStruct((bsz, d // LANES, LANES, LANES), F32),
        ],
        scratch_shapes=[pltpu.VMEM((npair, LANES, LANES), F32)],
        compiler_params=_params(("arbitrary", "arbitrary", "arbitrary")),
        name="rwkv7_bwd" if rev else "rwkv7_fwd",
    )(*args)


def _rk_out_kernel(z_ref, gs_ref, g2_ref, wo_ref, h_ref, gt_ref, o_ref):
    gate = _dot(_bf(_sigmoid(gs_ref[0, 0])), g2_ref[...])
    y = _dot(_bf(z_ref[0] * gate), wo_ref[...])
    o_ref[0] = h_ref[0] + gt_ref[0] * y


def _rk_out(z, small, g2, wo, h, gt, tm):
    bsz, seq, d = h.shape
    row = lambda b, i: (b, i, 0)
    glora = g2.shape[0]
    return pl.pallas_call(
        _rk_out_kernel,
        grid=(bsz, seq // tm),
        in_specs=[
            pl.BlockSpec((1, tm, d), row),
            pl.BlockSpec((1, 1, tm, glora), lambda b, i: (0, b, i, 0)),
            pl.BlockSpec((glora, d), lambda b, i: (0, 0)),
            pl.BlockSpec((d, d), lambda b, i: (0, 0)),
            pl.BlockSpec((1, tm, d), row),
            pl.BlockSpec((1, 1, d), lambda b, i: (b, 0, 0)),
        ],
        out_specs=pl.BlockSpec((1, tm, d), row),
        out_shape=jax.ShapeDtypeStruct((bsz, seq, d), F32),
        compiler_params=_params(("arbitrary", "arbitrary")),
        name="rwkv_outproj",
    )(z, small, g2, wo, h, gt)


def _rope_tables(seq):
    t = jnp.arange(seq, dtype=jnp.int32)
    rows = (t // GRID_W).astype(F32)
    cols = (t % GRID_W).astype(F32)
    half = HEAD_DIM // 2
    n_freq = half // 2
    inv = ROPE_BASE ** (-jnp.arange(n_freq, dtype=F32) / n_freq)
    lane = jnp.arange(HEAD_DIM)
    pos = jnp.where((lane < half)[None, :], rows[:, None], cols[:, None])
    ang = pos * inv[lane % n_freq][None, :]
    cos, sin = jnp.cos(ang), jnp.sin(ang)
    first = ((lane % half) < n_freq)[None, :]
    return cos, jnp.where(first, -sin, 0.0), jnp.where(first, 0.0, sin)


def _pad_lanes(w, axis):
    pad = [(0, 0)] * w.ndim
    pad[axis] = (0, LANES - w.shape[axis])
    return jnp.pad(w, pad)


def _row_tile(seq, want):
    return want if seq % want == 0 else seq


def kernel(x, c, ctx, c_ctx, mod_w, mod_b, norm_mix, norm_ffn, ffn_up, ffn_down, ab_w_in, ab_w_out, attn_sink, hgrn_lb, hgrn_onorm, rk_mix, rk_wr, rk_wk, rk_wv, rk_wo, rk_w0, rk_w1, rk_w2, rk_a0, rk_a1, rk_a2, rk_g1, rk_g2, rk_kk, rk_ka, rk_rk, rk_ln_g, rk_ln_b, final_norm):
    bsz, seq, d = x.shape
    lc = ctx.shape[1]
    depth = mod_w.shape[0]
    assert bsz + 1 <= 8 and seq % 1024 == 0 and lc % CHUNK == 0

    cond8 = jnp.zeros((8, d), F32).at[:bsz].set(c).at[bsz].set(c_ctx)
    mod = _modulation(cond8, mod_w, mod_b)
    lb_all = jnp.cumsum(jax.nn.softmax(hgrn_lb.astype(F32), axis=0), axis=0)
    rope = _rope_tables(seq)

    h, hc = x, ctx
    for layer in range(depth):
        last = layer == depth - 1
        jl = layer // 2
        m_lat = mod[layer, :bsz].reshape(bsz, 1, 6, d)
        m_ctx = jnp.broadcast_to(mod[layer, bsz].reshape(1, 1, 6, d), (bsz, 1, 6, d))
        sh1, sc1, gt1, sh2, sc2, gt2 = (m_lat[:, :, i] for i in range(6))
        csh1, csc1, cgt1, csh2, csc2, cgt2 = (m_ctx[:, :, i] for i in range(6))
        g_mix = norm_mix[layer].reshape(1, d)
        g_ffn = norm_ffn[layer].reshape(1, d)
        if layer % 2 == 0:
            w_in = _bf(ab_w_in[jl])
            w_out = _bf(ab_w_out[jl])
            sink = attn_sink[jl].astype(F32)
            lb = lb_all[jl].reshape(1, B_W)
            onorm = hgrn_onorm[jl].reshape(1, B_W).astype(F32)
            n_in = w_in.shape[1]
            pc = _inproj(hc, g_mix, csc1, csh1, w_in, None, lc, 512)
            pl_ = _inproj(h, g_mix, sc1, sh1, w_in, rope, 1024, 512)
            oa = _win_attn(sink, pl_, pc)
            zeros = jnp.zeros((bsz, B_HEADS, HEAD_DIM, HEAD_DIM), F32)
            ocf, scf = _hgrn_scan(pc, lb, zeros, False, lc)
            ocb, scb = _hgrn_scan(pc, lb, zeros, True, lc)
            olf, _ = _hgrn_scan(pl_, lb, scf, False, 512)
            olb, _ = _hgrn_scan(pl_, lb, scb, True, 512)
            h = _ab_out(oa, olf, olb, pl_, onorm, w_out, h, gt1, 512)
            if not last:
                oca = _ctx_attn(sink, pc)
                hc = _ab_out(oca, ocf, ocb, pc, onorm, w_out, hc, cgt1, lc)
        else:
            w_big = _bf(jnp.stack([rk_wr[jl], rk_wk[jl], rk_wv[jl]]))
            w_small = _bf(jnp.stack([
                rk_g1[jl],
                jnp.concatenate([_pad_lanes(rk_w1[jl, 0], 1), _pad_lanes(rk_w1[jl, 1], 1)], axis=1),
                jnp.concatenate([_pad_lanes(rk_a1[jl, 0], 1), _pad_lanes(rk_a1[jl, 1], 1)], axis=1),
            ]))
            assert w_small.shape[-1] == 2 * LANES
            mix = rk_mix[jl]
            prm = dict(
                w2=_bf(_pad_lanes(rk_w2[jl], 1)), a2=_bf(_pad_lanes(rk_a2[jl], 1)),
                w0=rk_w0[jl].reshape(2, 1, d), a0=rk_a0[jl].reshape(2, 1, d),
                k_k=rk_kk[jl].reshape(1, d), k_a=rk_ka[jl].reshape(1, d), r_k=rk_rk[jl].reshape(1, d),
                ln_g=rk_ln_g[jl].reshape(1, d), ln_b=rk_ln_b[jl].reshape(1, d),
            )
            g2 = _bf(rk_g2[jl])
            wo = _bf(rk_wo[jl])
            rkv_c = _rk_proj(hc, g_mix, csc1, csh1, mix[jnp.array([0, 2, 3])], w_big, lc, d)
            sm_c = _rk_proj(hc, g_mix, csc1, csh1, mix[jnp.array([5, 1, 4])], w_small, lc, 2 * LANES)
            rkv_l = _rk_proj(h, g_mix, sc1, sh1, mix[jnp.array([0, 2, 3])], w_big, 512, d)
            sm_l = _rk_proj(h, g_mix, sc1, sh1, mix[jnp.array([5, 1, 4])], w_small, 512, 2 * LANES)
            zeros = jnp.zeros((bsz, d // LANES, LANES, LANES), F32)
            npair = 4
            ycf, s_f = _rwkv_scan(rkv_c, sm_c, prm, zeros, False, lc, npair)
            zc, s_b = _rwkv_scan(rkv_c, sm_c, prm, zeros, True, lc, npair, y_fwd=ycf)
            ylf, _ = _rwkv_scan(rkv_l, sm_l, prm, s_f, False, 512, npair)
            zl, _ = _rwkv_scan(rkv_l, sm_l, prm, s_b, True, 512, npair, y_fwd=ylf)
            h = _rk_out(zl, sm_l, g2, wo, h, gt1, 512)
            if not last:
                hc = _rk_out(zc, sm_c, g2, wo, hc, cgt1, lc)
        w_up = _bf(ffn_up[layer])
        w_dn = _bf(ffn_down[layer])
        h = _mlp(h, g_ffn, sc2, sh2, gt2, w_up, w_dn, final_norm.reshape(1, d) if last else None, 1024, 512)
        if not last:
            hc = _mlp(hc, g_ffn, csc2, csh2, cgt2, w_up, w_dn, None, lc, 512)
    return h
```

```python
import functools

import jax
import jax.numpy as jnp
import numpy as np
from jax import lax
from jax.experimental import pallas as pl
from jax.experimental.pallas import tpu as pltpu

F32 = jnp.float32
BF16 = jnp.bfloat16
HIGHEST = lax.Precision.HIGHEST

LANES = 128
HEAD_DIM = 128
GRID_W = 64
WINDOW = 128
ROPE_BASE = 10000.0
A_Q_HEADS = 8
A_KV_HEADS = 2
A_GROUP = A_Q_HEADS // A_KV_HEADS
A_Q = A_Q_HEADS * HEAD_DIM
A_KV = A_KV_HEADS * HEAD_DIM
B_HEADS = 8
B_W = B_HEADS * HEAD_DIM
C_HEAD = 64
CHUNK = 64
SUB = 16
EPS = 1e-6
GN_EPS = 64e-5
VMEM_LIMIT = 58 * 1024 * 1024

NT_DIMS = (((1,), (1,)), ((), ()))
TN_DIMS = (((0,), (0,)), ((), ()))


def _dot(a, b, **kw):
    return jnp.dot(a, b, preferred_element_type=F32, **kw)


def _dot_nt(a, b):
    return lax.dot_general(a, b, NT_DIMS, preferred_element_type=F32)


def _dot_tn(a, b):
    return lax.dot_general(a, b, TN_DIMS, preferred_element_type=F32)


def _bf(x):
    return x.astype(BF16)


def _dot_split(x, m):
    hi = _bf(x)
    lo = _bf(x - hi.astype(F32))
    return _dot(hi, m) + _dot(lo, m)


def _sigmoid(x):
    return 1.0 / (1.0 + jnp.exp(-x))


def _silu(x):
    return x * _sigmoid(x)


def _normmod(x, g, sc, sh):
    ms = jnp.mean(x * x, axis=-1, keepdims=True)
    y = (x * lax.rsqrt(ms + EPS)) * g
    return y * (1.0 + sc) + sh


def _params(sem):
    return pltpu.CompilerParams(dimension_semantics=sem, vmem_limit_bytes=VMEM_LIMIT)


def _mod_kernel(c_ref, w_ref, b_ref, o_ref):
    s = _bf(_silu(c_ref[...]))
    o_ref[0] = _dot(s, _bf(w_ref[0])) + b_ref[0]


def _modulation(cond8, mod_w, mod_b):
    depth, d, n = mod_w.shape
    tn = 1024
    return pl.pallas_call(
        _mod_kernel,
        grid=(depth, n // tn),
        in_specs=[
            pl.BlockSpec((8, d), lambda l, j: (0, 0)),
            pl.BlockSpec((1, d, tn), lambda l, j: (l, 0, j)),
            pl.BlockSpec((1, 1, tn), lambda l, j: (l, 0, j)),
        ],
        out_specs=pl.BlockSpec((1, 8, tn), lambda l, j: (l, 0, j)),
        out_shape=jax.ShapeDtypeStruct((depth, 8, n), F32),
        compiler_params=_params(("arbitrary", "arbitrary")),
        name="modulation",
    )(cond8, mod_w, mod_b.reshape(depth, 1, n))


def _inproj_kernel(*refs, n_rope):
    if n_rope:
        x_ref, g_ref, sc_ref, sh_ref, w_ref, cos_ref, sna_ref, snb_ref, o_ref, u_ref = refs
    else:
        x_ref, g_ref, sc_ref, sh_ref, w_ref, o_ref, u_ref = refs
    j = pl.program_id(2)

    @pl.when(j == 0)
    def _():
        u_ref[...] = _bf(_normmod(x_ref[0], g_ref[...], sc_ref[0], sh_ref[0]))

    acc = _dot(u_ref[...], w_ref[...])
    if not n_rope:
        o_ref[0] = acc
        return

    per_tile = acc.shape[1] // HEAD_DIM
    n_full, n_rem = n_rope // per_tile, n_rope % per_tile

    def store(n_rot):
        cos, sna, snb = cos_ref[...], sna_ref[...], snb_ref[...]
        for hd in range(n_rot):
            sl = acc[:, hd * HEAD_DIM:(hd + 1) * HEAD_DIM]
            rot = sl * cos + pltpu.roll(sl, 96, 1) * sna + pltpu.roll(sl, 32, 1) * snb
            o_ref[0, :, hd * HEAD_DIM:(hd + 1) * HEAD_DIM] = rot
        if n_rot < per_tile:
            o_ref[0, :, n_rot * HEAD_DIM:] = acc[:, n_rot * HEAD_DIM:]

    pl.when(j < n_full)(lambda: store(per_tile))
    pl.when(j == n_full)(lambda: store(n_rem))
    pl.when(j > n_full)(lambda: store(0))


def _inproj(x, g, sc, sh, w, rope, tm, tn):
    bsz, seq, d = x.shape
    n = w.shape[1]
    n_rope = 0
    in_specs = [
        pl.BlockSpec((1, tm, d), lambda b, i, j: (b, i, 0)),
        pl.BlockSpec((1, d), lambda b, i, j: (0, 0)),
        pl.BlockSpec((1, 1, d), lambda b, i, j: (b, 0, 0)),
        pl.BlockSpec((1, 1, d), lambda b, i, j: (b, 0, 0)),
        pl.BlockSpec((d, tn), lambda b, i, j: (0, j)),
    ]
    args = [x, g, sc, sh, w]
    if rope is not None:
        n_rope = A_Q_HEADS + A_KV_HEADS
        assert tn % HEAD_DIM == 0
        in_specs += [pl.BlockSpec((tm, HEAD_DIM), lambda b, i, j: (i, 0))] * 3
        args += list(rope)
    return pl.pallas_call(
        functools.partial(_inproj_kernel, n_rope=n_rope),
        grid=(bsz, seq // tm, n // tn),
        in_specs=in_specs,
        out_specs=pl.BlockSpec((1, tm, tn), lambda b, i, j: (b, i, j)),
        out_shape=jax.ShapeDtypeStruct((bsz, seq, n), F32),
        scratch_shapes=[pltpu.VMEM((tm, d), BF16)],
        compiler_params=_params(("arbitrary", "arbitrary", "arbitrary")),
        name="ab_inproj",
    )(*args)


def _softmax_av(s_list, v_list, sink_col):
    m = sink_col
    for s in s_list:
        m = jnp.maximum(m, jnp.max(s, axis=-1, keepdims=True))
    den = jnp.exp(sink_col - m)
    out = None
    for s, v in zip(s_list, v_list):
        p = jnp.exp(s - m)
        den = den + jnp.sum(p, axis=-1, keepdims=True)
        o = _dot(_bf(p), v)
        out = o if out is None else out + o
    return out / den


def _sink_column(sink_ref, hk, rows):
    rowh = lax.broadcasted_iota(jnp.int32, (rows, 1), 0) // WINDOW
    col = jnp.full((rows, 1), sink_ref[hk * A_GROUP + A_GROUP - 1], F32)
    for g in range(A_GROUP - 1):
        col = jnp.where(rowh == g, sink_ref[hk * A_GROUP + g], col)
    return col


def _win_attn_kernel(sink_ref, q_ref, kp_ref, kc_ref, kn_ref, vp_ref, vc_ref, vn_ref,
                     ck_ref, cv_ref, o_ref):
    n = pl.program_id(1)
    nb = pl.num_programs(1)
    scale = HEAD_DIM ** -0.5
    q = q_ref[0]
    kband = jnp.concatenate([kp_ref[0], kc_ref[0], kn_ref[0]], axis=0)
    vband = jnp.concatenate([vp_ref[0], vc_ref[0], vn_ref[0]], axis=0)
    rows = A_GROUP * WINDOW
    tq = lax.broadcasted_iota(jnp.int32, (rows, 3 * WINDOW), 0) % WINDOW
    tk = lax.broadcasted_iota(jnp.int32, (rows, 3 * WINDOW), 1)
    rel = tk - WINDOW - tq
    valid = (jnp.abs(rel) <= WINDOW) & ((tk >= WINDOW) | (n > 0)) & ((tk < 2 * WINDOW) | (n < nb - 1))
    for hk in range(A_KV_HEADS):
        qs = jnp.concatenate(
            [q[:, (hk * A_GROUP + g) * HEAD_DIM:(hk * A_GROUP + g + 1) * HEAD_DIM] for g in range(A_GROUP)],
            axis=0)
        qs = _bf(qs)
        hs = slice(hk * HEAD_DIM, (hk + 1) * HEAD_DIM)
        s_win = _dot_nt(qs, _bf(kband[:, hs])) * scale
        s_win = jnp.where(valid, s_win, -jnp.inf)
        s_ctx = _dot_nt(qs, _bf(ck_ref[0][:, hs])) * scale
        o = _softmax_av([s_win, s_ctx], [_bf(vband[:, hs]), _bf(cv_ref[0][:, hs])],
                        _sink_column(sink_ref, hk, rows))
        for g in range(A_GROUP):
            h = hk * A_GROUP + g
            o_ref[0, :, h * HEAD_DIM:(h + 1) * HEAD_DIM] = _bf(o[g * WINDOW:(g + 1) * WINDOW])


def _win_attn(sink, proj, proj_ctx):
    bsz, seq, _ = proj.shape
    lc = proj_ctx.shape[1]
    nb = seq // WINDOW
    kcol, vcol = A_Q // A_KV, A_Q // A_KV + 1
    prev = lambda b, n: (b, jnp.maximum(n - 1, 0))
    nxt = lambda b, n: (b, jnp.minimum(n + 1, nb - 1))
    cur = lambda b, n: (b, n)

    def band(rowfn, col):
        return pl.BlockSpec((1, WINDOW, A_KV), lambda b, n: rowfn(b, n) + (col,))

    return pl.pallas_call(
        _win_attn_kernel,
        grid=(bsz, nb),
        in_specs=[
            pl.BlockSpec(memory_space=pltpu.SMEM),
            pl.BlockSpec((1, WINDOW, A_Q), lambda b, n: (b, n, 0)),
            band(prev, kcol), band(cur, kcol), band(nxt, kcol),
            band(prev, vcol), band(cur, vcol), band(nxt, vcol),
            pl.BlockSpec((1, lc, A_KV), lambda b, n: (b, 0, kcol)),
            pl.BlockSpec((1, lc, A_KV), lambda b, n: (b, 0, vcol)),
        ],
        out_specs=pl.BlockSpec((1, WINDOW, A_Q), lambda b, n: (b, n, 0)),
        out_shape=jax.ShapeDtypeStruct((bsz, seq, A_Q), BF16),
        compiler_params=_params(("arbitrary", "arbitrary")),
        name="window_attention",
    )(sink, proj, proj, proj, proj, proj, proj, proj, proj_ctx, proj_ctx)


def _ctx_attn_kernel(sink_ref, q_ref, k_ref, v_ref, o_ref):
    scale = HEAD_DIM ** -0.5
    q = q_ref[0]
    lc = q.shape[0]
    for h in range(A_Q_HEADS):
        hk = h // A_GROUP
        hs = slice(hk * HEAD_DIM, (hk + 1) * HEAD_DIM)
        qs = _bf(q[:, h * HEAD_DIM:(h + 1) * HEAD_DIM])
        s = _dot_nt(qs, _bf(k_ref[0][:, hs])) * scale
        sink_col = jnp.full((lc, 1), sink_ref[h], F32)
        o = _softmax_av([s], [_bf(v_ref[0][:, hs])], sink_col)
        o_ref[0, :, h * HEAD_DIM:(h + 1) * HEAD_DIM] = _bf(o)


def _ctx_attn(sink, proj_ctx):
    bsz, lc, _ = proj_ctx.shape
    kcol, vcol = A_Q // A_KV, A_Q // A_KV + 1
    return pl.pallas_call(
        _ctx_attn_kernel,
        grid=(bsz,),
        in_specs=[
            pl.BlockSpec(memory_space=pltpu.SMEM),
            pl.BlockSpec((1, lc, A_Q), lambda b: (b, 0, 0)),
            pl.BlockSpec((1, lc, A_KV), lambda b: (b, 0, kcol)),
            pl.BlockSpec((1, lc, A_KV), lambda b: (b, 0, vcol)),
        ],
        out_specs=pl.BlockSpec((1, lc, A_Q), lambda b: (b, 0, 0)),
        out_shape=jax.ShapeDtypeStruct((bsz, lc, A_Q), BF16),
        compiler_params=_params(("arbitrary",)),
        name="context_attention",
    )(sink, proj_ctx, proj_ctx, proj_ctx)


def _hgrn_chunk(bq, bi, bf, lb, st, rev):
    heads = range(len(bq))
    q = [_silu(x) for x in bq]
    v = bi
    f = [lb[h] + (1.0 - lb[h]) * _sigmoid(bf[h]) for h in heads]
    k = [1.0 - x for x in f]
    g = [jnp.log(x) for x in f]
    ti = lax.broadcasted_iota(jnp.int32, (CHUNK, CHUNK), 0)
    si = lax.broadcasted_iota(jnp.int32, (CHUNK, CHUNK), 1)
    tri = jnp.where((si >= ti) if rev else (si <= ti), 1.0, 0.0).astype(F32)
    b = [_dot(tri, x, precision=HIGHEST) for x in g]
    btot = [x[0:1] if rev else x[CHUNK - 1:CHUNK] for x in b]
    vb = [_bf(x) for x in v]
    stb = [_bf(x) for x in st]
    o = [_dot_nt(_bf(q[h] * jnp.exp(b[h])), stb[h]) for h in heads]
    khat = [_bf(k[h] * jnp.exp(btot[h] - b[h])) for h in heads]
    dst = [_dot_tn(vb[h], khat[h]) for h in heads]
    st_new = [st[h] * jnp.exp(btot[h]) + dst[h] for h in heads]
    nsub = CHUNK // SUB
    row16 = lax.broadcasted_iota(jnp.int32, (SUB, 1), 0)
    outs = [[] for _ in heads]
    for blk in range(nsub):
        r0 = blk * SUB
        rs = slice(r0, r0 + SUB)
        acc = [o[h][rs] for h in heads]
        if rev and blk < nsub - 1:
            ref_row, lo, hi = r0 + SUB, r0 + SUB, CHUNK
        elif (not rev) and blk > 0:
            ref_row, lo, hi = r0 - 1, 0, r0
        else:
            ref_row = None
        if ref_row is not None:
            bref = [b[h][ref_row:ref_row + 1] for h in heads]
            qn = [_bf(q[h][rs] * jnp.exp(b[h][rs] - bref[h])) for h in heads]
            kn = [_bf(k[h][lo:hi] * jnp.exp(bref[h] - b[h][lo:hi])) for h in heads]
            att = [_bf(_dot_nt(qn[h], kn[h])) for h in heads]
            acc = [acc[h] + _dot(att[h], vb[h][lo:hi]) for h in heads]
        for s in range(SUB):
            mask = (row16 <= s) if rev else (row16 >= s)
            for h in heads:
                bi_ = b[h][rs]
                dec = jnp.exp(jnp.where(mask, bi_ - bi_[s:s + 1], -jnp.inf))
                w = jnp.sum(q[h][rs] * dec * k[h][r0 + s:r0 + s + 1], axis=-1, keepdims=True)
                acc[h] = acc[h] + w * v[h][r0 + s:r0 + s + 1]
        for h in heads:
            outs[h].append(acc[h])
    return [jnp.concatenate(x, axis=0) for x in outs], st_new


def _hgrn_kernel(q_ref, i_ref, f_ref, lb_ref, s0_ref, o_ref, sfin_ref, st_ref, *, rev, nchunk, nhead):
    c = pl.program_id(2)

    @pl.when(c == 0)
    def _():
        st_ref[...] = s0_ref[0]

    lanes = [slice(h * HEAD_DIM, (h + 1) * HEAD_DIM) for h in range(nhead)]
    lb = [lb_ref[:, ln] for ln in lanes]

    def body(ci, carry):
        cc = (nchunk - 1 - ci) if rev else ci
        rows = pl.ds(pl.multiple_of(cc * CHUNK, CHUNK), CHUNK)
        o, st_new = _hgrn_chunk([q_ref[0, rows, ln] for ln in lanes], [i_ref[0, rows, ln] for ln in lanes],
                                [f_ref[0, rows, ln] for ln in lanes], lb,
                                [st_ref[h] for h in range(nhead)], rev)
        for h in range(nhead):
            o_ref[0, rows, lanes[h]] = o[h]
            st_ref[h] = st_new[h]
        return carry

    lax.fori_loop(0, nchunk, body, 0)

    @pl.when(c == pl.num_programs(2) - 1)
    def _():
        sfin_ref[0] = st_ref[...]


def _hgrn_scan(proj, lb, s0, rev, tblk, nhead):
    bsz, seq, _ = proj.shape
    nblk = seq // tblk
    width = nhead * HEAD_DIM
    base = A_Q + 2 * A_KV
    assert base % width == 0 and B_W % width == 0
    qc, ic = base // width, (base + B_W) // width
    fc = (base + (3 if rev else 2) * B_W) // width
    blk = (lambda c: nblk - 1 - c) if rev else (lambda c: c)

    def col(c0):
        return pl.BlockSpec((1, tblk, width), lambda b, h, c: (b, blk(c), c0 + h))

    state = pl.BlockSpec((1, nhead, HEAD_DIM, HEAD_DIM), lambda b, h, c: (b, h, 0, 0))
    return pl.pallas_call(
        functools.partial(_hgrn_kernel, rev=rev, nchunk=tblk // CHUNK, nhead=nhead),
        grid=(bsz, B_HEADS // nhead, nblk),
        in_specs=[
            col(qc), col(ic), col(fc),
            pl.BlockSpec((1, width), lambda b, h, c: (0, h)),
            state,
        ],
        out_specs=[
            pl.BlockSpec((1, tblk, width), lambda b, h, c: (b, blk(c), h)),
            state,
        ],
        out_shape=[
            jax.ShapeDtypeStruct((bsz, seq, B_W), F32),
            jax.ShapeDtypeStruct((bsz, B_HEADS, HEAD_DIM, HEAD_DIM), F32),
        ],
        scratch_shapes=[pltpu.VMEM((nhead, HEAD_DIM, HEAD_DIM), F32)],
        compiler_params=_params(("arbitrary", "arbitrary", "arbitrary")),
        name="hgrn2_bwd" if rev else "hgrn2_fwd",
    )(proj, proj, proj, lb, s0)


def _ab_out_kernel(oa_ref, of_ref, ob_ref, g0_ref, g1_ref, on_ref, w_ref, h_ref, gt_ref, o_ref, lhs_ref):
    lhs_ref[:, :A_Q] = oa_ref[0]
    half = B_W // 2
    for hd in range(B_HEADS):
        sl = slice(hd * HEAD_DIM, (hd + 1) * HEAD_DIM)
        o = of_ref[0, :, sl] + ob_ref[0, :, sl]
        o = o * lax.rsqrt(jnp.mean(o * o, axis=-1, keepdims=True) + EPS)
        o = o * on_ref[:, sl]
        gref = g0_ref if hd * HEAD_DIM < half else g1_ref
        gs = slice(hd * HEAD_DIM % half, hd * HEAD_DIM % half + HEAD_DIM)
        o = o * _silu(gref[0, :, gs])
        lhs_ref[:, A_Q + hd * HEAD_DIM:A_Q + (hd + 1) * HEAD_DIM] = _bf(o)
    y = _dot(lhs_ref[...], w_ref[...])
    o_ref[0] = h_ref[0] + gt_ref[0] * y


def _ab_out(oa, of, ob, proj, onorm, w_out, h, gt, tm):
    bsz, seq, d = h.shape
    half = B_W // 2
    gcol = (A_Q + 2 * A_KV + 4 * B_W) // half
    row = lambda b, i: (b, i, 0)
    return pl.pallas_call(
        _ab_out_kernel,
        grid=(bsz, seq // tm),
        in_specs=[
            pl.BlockSpec((1, tm, A_Q), row),
            pl.BlockSpec((1, tm, B_W), row),
            pl.BlockSpec((1, tm, B_W), row),
            pl.BlockSpec((1, tm, half), lambda b, i: (b, i, gcol)),
            pl.BlockSpec((1, tm, half), lambda b, i: (b, i, gcol + 1)),
            pl.BlockSpec((1, B_W), lambda b, i: (0, 0)),
            pl.BlockSpec((A_Q + B_W, d), lambda b, i: (0, 0)),
            pl.BlockSpec((1, tm, d), row),
            pl.BlockSpec((1, 1, d), lambda b, i: (b, 0, 0)),
        ],
        out_specs=pl.BlockSpec((1, tm, d), row),
        out_shape=jax.ShapeDtypeStruct((bsz, seq, d), F32),
        scratch_shapes=[pltpu.VMEM((tm, A_Q + B_W), BF16)],
        compiler_params=_params(("arbitrary", "arbitrary")),
        name="ab_outproj",
    )(oa, of, ob, proj, proj, onorm, w_out, h, gt)


def _mlp_kernel(*refs, final):
    if final:
        x_ref, g_ref, sc_ref, sh_ref, gt_ref, wu_ref, wd_ref, fn_ref, o_ref, u_ref = refs
    else:
        x_ref, g_ref, sc_ref, sh_ref, gt_ref, wu_ref, wd_ref, o_ref, u_ref = refs
    j = pl.program_id(2)

    @pl.when(j == 0)
    def _():
        u_ref[...] = _bf(_normmod(x_ref[0], g_ref[...], sc_ref[0], sh_ref[0]))

    hid = jnp.maximum(_dot(u_ref[...], wu_ref[...]), 0.0)
    hid = _bf(hid * hid)
    ncol = 512

    @pl.when(j == 0)
    def _():
        for n0 in range(0, o_ref.shape[2], ncol):
            o_ref[0, :, n0:n0 + ncol] = _dot(hid, wd_ref[:, n0:n0 + ncol])

    @pl.when(j != 0)
    def _():
        for n0 in range(0, o_ref.shape[2], ncol):
            o_ref[0, :, n0:n0 + ncol] += _dot(hid, wd_ref[:, n0:n0 + ncol])

    @pl.when(j == pl.num_programs(2) - 1)
    def _():
        y = x_ref[0] + gt_ref[0] * o_ref[0]
        if final:
            y = (y * lax.rsqrt(jnp.mean(y * y, axis=-1, keepdims=True) + EPS)) * fn_ref[...]
        o_ref[0] = y


def _mlp(x, g, sc, sh, gt, w_up, w_down, final_gain, tm, tf):
    bsz, seq, d = x.shape
    dff = w_up.shape[1]
    vec = pl.BlockSpec((1, 1, d), lambda b, i, j: (b, 0, 0))
    in_specs = [
        pl.BlockSpec((1, tm, d), lambda b, i, j: (b, i, 0)),
        pl.BlockSpec((1, d), lambda b, i, j: (0, 0)),
        vec, vec, vec,
        pl.BlockSpec((d, tf), lambda b, i, j: (0, j)),
        pl.BlockSpec((tf, d), lambda b, i, j: (j, 0)),
    ]
    args = [x, g, sc, sh, gt, w_up, w_down]
    if final_gain is not None:
        in_specs.append(pl.BlockSpec((1, d), lambda b, i, j: (0, 0)))
        args.append(final_gain)
    return pl.pallas_call(
        functools.partial(_mlp_kernel, final=final_gain is not None),
        grid=(bsz, seq // tm, dff // tf),
        in_specs=in_specs,
        out_specs=pl.BlockSpec((1, tm, d), lambda b, i, j: (b, i, 0)),
        out_shape=jax.ShapeDtypeStruct((bsz, seq, d), F32),
        scratch_shapes=[pltpu.VMEM((tm, d), BF16)],
        compiler_params=_params(("arbitrary", "arbitrary", "arbitrary")),
        name="sqrelu_mlp",
    )(*args)


def _rk_proj_kernel(x_ref, xp_ref, xn_ref, g_ref, sc_ref, sh_ref, mix_ref, w_ref, o_ref,
                    u_ref, lhs_ref, *, nmix, nt):
    i = pl.program_id(1)
    j = pl.program_id(2)
    tm = x_ref.shape[1]

    @pl.when(j == 0)
    def _():
        g, sc, sh = g_ref[...], sc_ref[0], sh_ref[0]
        u = _normmod(x_ref[0], g, sc, sh)
        up = _normmod(xp_ref[0], g, sc, sh)[7:8]
        un = _normmod(xn_ref[0], g, sc, sh)[0:1]
        u_ref[8:8 + tm] = u
        u_ref[7:8] = jnp.where(i == 0, 0.0, up)
        u_ref[8 + tm:9 + tm] = jnp.where(i == pl.num_programs(1) - 1, 0.0, un)
        xx = 0.5 * (u_ref[7:7 + tm] + u_ref[9:9 + tm]) - u
        for m in range(nmix):
            lhs_ref[m] = _bf(u + xx * mix_ref[m:m + 1])

    o_ref[0, 0] = _dot(lhs_ref[j // nt], w_ref[0])


def _rk_proj(x, g, sc, sh, mix, w, tm, tn):
    bsz, seq, d = x.shape
    nmix, _, n = w.shape
    nt = n // tn
    r8 = tm // 8
    last8 = seq // 8 - 1
    vec = pl.BlockSpec((1, 1, d), lambda b, i, j: (b, 0, 0))
    return pl.pallas_call(
        functools.partial(_rk_proj_kernel, nmix=nmix, nt=nt),
        grid=(bsz, seq // tm, nmix * nt),
        in_specs=[
            pl.BlockSpec((1, tm, d), lambda b, i, j: (b, i, 0)),
            pl.BlockSpec((1, 8, d), lambda b, i, j: (b, jnp.maximum(i * r8 - 1, 0), 0)),
            pl.BlockSpec((1, 8, d), lambda b, i, j: (b, jnp.minimum((i + 1) * r8, last8), 0)),
            pl.BlockSpec((1, d), lambda b, i, j: (0, 0)),
            vec, vec,
            pl.BlockSpec((nmix, d), lambda b, i, j: (0, 0)),
            pl.BlockSpec((1, d, tn), lambda b, i, j: (j // nt, 0, j % nt)),
        ],
        out_specs=pl.BlockSpec((1, 1, tm, tn), lambda b, i, j: (j // nt, b, i, j % nt)),
        out_shape=jax.ShapeDtypeStruct((nmix, bsz, seq, n), F32),
        scratch_shapes=[pltpu.VMEM((tm + 16, d), F32), pltpu.VMEM((nmix, tm, d), BF16)],
        compiler_params=_params(("arbitrary", "arbitrary", "arbitrary")),
        name="rwkv_proj",
    )(x, x, x, g, sc, sh, mix, w)


def _softplus(x):
    return jnp.maximum(x, 0.0) + jnp.log(1.0 + jnp.exp(-jnp.abs(x)))


def _pair_consts(rev):
    n = 2 * CHUNK
    ri = lax.broadcasted_iota(jnp.int32, (n, n), 0)
    ci = lax.broadcasted_iota(jnp.int32, (n, n), 1)
    same = (ri // CHUNK) == (ci // CHUNK)
    rt, ct = ri % CHUNK, ci % CHUNK
    strict = same & ((ct > rt) if rev else (ct < rt))
    incl = same & ((ct >= rt) if rev else (ct <= rt))
    top = ri < CHUNK
    eye = jnp.where(ri == ci, 1.0, 0.0).astype(F32)
    ones_bd = jnp.where(same, 1.0, 0.0).astype(BF16)
    ti = lax.broadcasted_iota(jnp.int32, (CHUNK, CHUNK), 0)
    si = lax.broadcasted_iota(jnp.int32, (CHUNK, CHUNK), 1)
    tri = jnp.where((si >= ti) if rev else (si <= ti), 1.0, 0.0).astype(F32)
    lane = lax.broadcasted_iota(jnp.int32, (1, LANES), 1)
    m0 = jnp.where(lane < C_HEAD, 1.0, 0.0).astype(F32)
    m1 = 1.0 - m0
    return dict(strict=strict, incl=incl, top=top, eye=eye, ones_bd=ones_bd, tri=tri, m0=m0, m1=m1, ri=ri, ci=ci)


def _unit_tri_inverse(n, cs):
    ri, ci = cs["ri"], cs["ci"]
    blk2 = (ri // 2) == (ci // 2)
    t = [cs["eye"] + jnp.where(blk2, x, 0.0) for x in n]
    m = 2
    while m < CHUNK:
        sel = ((ri // (2 * m)) == (ci // (2 * m))) & ((ri // m) != (ci // m))
        off = [_bf(jnp.where(sel, x, 0.0)) for x in n]
        tb = [_bf(x) for x in t]
        q = [_bf(_dot(a, b)) for a, b in zip(tb, off)]
        q = [_dot(a, b) for a, b in zip(q, tb)]
        t = [a + b for a, b in zip(t, q)]
        m *= 2
    return t


def _stack2(x, cs):
    return jnp.concatenate([x * cs["m0"], x * cs["m1"]], axis=0)


def _rwkv_chunk(r, k, v, lw, la, w2, a2, w0, a0, kkv, kav, ht, cs, rev):
    pairs = range(len(r))
    ones = cs["ones_bd"]
    tl, lab = _bf(jnp.tanh(lw)), _bf(la)
    z = [w0[p] + _dot(tl, w2[p]) for p in pairs]
    za = [_dot(lab, a2[p]) for p in pairs]
    kkr = [k[p] * kkv[p] for p in pairs]
    sq = [x * x for x in kkr]
    sq_hi = [_bf(x) for x in sq]
    sq_lo = [_bf(x - h.astype(F32)) for x, h in zip(sq, sq_hi)]
    ss = [_dot(h, ones) for h in sq_hi]
    ss = [s + _dot(l, ones) for s, l in zip(ss, sq_lo)]
    ld = [-jnp.exp(-_softplus(-x) - 0.5) for x in z]
    cl = [_dot(cs["tri"], x, precision=HIGHEST) for x in ld]
    a = [_sigmoid(a0[p] + za[p]) for p in pairs]
    kk = [x / jnp.maximum(jnp.sqrt(s), 1e-12) for x, s in zip(kkr, ss)]
    kd = [k[p] * (1.0 + (a[p] - 1.0) * kav[p]) for p in pairs]
    bb = [x * y for x, y in zip(kk, a)]
    ctot = [x[0:1] if rev else x[CHUNK - 1:CHUNK] for x in cl]
    e_neg = [jnp.exp(-x) for x in cl]
    e_tail = [jnp.exp(c - x) for c, x in zip(ctot, cl)]
    at = [_bf(_stack2(-kk[p] * jnp.exp(cl[p] - ld[p]), cs)) for p in pairs]
    rt = [_bf(_stack2(r[p] * jnp.exp(cl[p]), cs)) for p in pairs]
    rhs = [_bf(jnp.concatenate([bb[p] * e_neg[p], kd[p] * e_neg[p]], axis=0)) for p in pairs]
    g1 = [_dot_nt(x, y) for x, y in zip(at, rhs)]
    g2 = [_dot_nt(x, y) for x, y in zip(rt, rhs)]
    top, strict, incl = cs["top"], cs["strict"], cs["incl"]
    g1r = [pltpu.roll(x, C_HEAD, 1) for x in g1]
    n_ab = [jnp.where(strict, jnp.where(top, x, y), 0.0) for x, y in zip(g1, g1r)]
    n_ak = [_bf(jnp.where(strict, jnp.where(top, y, x), 0.0)) for x, y in zip(g1, g1r)]
    t = _unit_tri_inverse(n_ab, cs)
    vs = [_bf(_stack2(x, cs)) for x in v]
    htb = [_bf(x) for x in ht]
    x1 = [_dot_nt(x, h) for x, h in zip(at, htb)]
    x2 = [_dot(x, y) for x, y in zip(n_ak, vs)]
    xx = [_bf(a_ + b_) for a_, b_ in zip(x1, x2)]
    u = [_bf(_dot(_bf(a_), b_)) for a_, b_ in zip(t, xx)]
    bk = [_bf(jnp.concatenate([_stack2(bb[p] * e_tail[p], cs), _stack2(kd[p] * e_tail[p], cs)], axis=0))
          for p in pairs]
    uv = [jnp.concatenate([a_, b_], axis=0) for a_, b_ in zip(u, vs)]
    dh = [_dot_tn(a_, b_) for a_, b_ in zip(uv, bk)]
    ht_new = [ht[p] * jnp.exp(ctot[p]) + dh[p] for p in pairs]
    g2r = [pltpu.roll(x, C_HEAD, 1) for x in g2]
    n_rb = [_bf(jnp.where(incl, jnp.where(top, x, y), 0.0)) for x, y in zip(g2, g2r)]
    n_rk = [_bf(jnp.where(incl, jnp.where(top, y, x), 0.0)) for x, y in zip(g2, g2r)]
    y1 = [_dot_nt(x, h) for x, h in zip(rt, htb)]
    y2 = [_dot(x, y) for x, y in zip(n_rb, u)]
    y3 = [_dot(x, y) for x, y in zip(n_rk, vs)]
    ysum = [a_ + b_ + c_ for a_, b_, c_ in zip(y1, y2, y3)]
    y = [x[:CHUNK] + x[CHUNK:] for x in ysum]
    return y, ht_new, kd


def _rwkv_combine(y, yf, r, k, v, kd, laf, a2f, a0f, kav, rkv, lng, lnb, cs):
    pairs = range(len(y))
    ones = cs["ones_bd"]
    inv_n = 1.0 / C_HEAD

    def head_sum(xs):
        hi = [_bf(x) for x in xs]
        lo = [_bf(x - h.astype(F32)) for x, h in zip(xs, hi)]
        s = [_dot(h, ones) for h in hi]
        return [a + _dot(l, ones) for a, l in zip(s, lo)]

    lafb = _bf(laf)
    zf = [_dot(lafb, a2f[p]) for p in pairs]
    ysum = [a + b for a, b in zip(y, yf)]
    mu = [s * inv_n for s in head_sum(ysum)]
    dev = [a - b for a, b in zip(ysum, mu)]
    var = [s * inv_n for s in head_sum([x * x for x in dev])]
    a_f = [_sigmoid(a0f[p] + zf[p]) for p in pairs]
    kd_f = [k[p] * (1.0 + (a_f[p] - 1.0) * kav[p]) for p in pairs]
    bsum = head_sum([r[p] * (kd_f[p] + kd[p]) * rkv[p] for p in pairs])
    return [dev[p] * lax.rsqrt(var[p] + GN_EPS) * lng[p] + lnb[p] + bsum[p] * v[p] for p in pairs]


def _rwkv_scan_kernel(*refs, rev, nchunk, npair, combine):
    if combine:
        (r_ref, k_ref, v_ref, lw_ref, la_ref, w2_ref, a2_ref, w0_ref, a0_ref, kk_ref, ka_ref, s0_ref,
         yf_ref, laf_ref, a2f_ref, a0f_ref, rk_ref, lng_ref, lnb_ref, o_ref, sfin_ref, ht_ref) = refs
    else:
        (r_ref, k_ref, v_ref, lw_ref, la_ref, w2_ref, a2_ref, w0_ref, a0_ref, kk_ref, ka_ref, s0_ref,
         o_ref, sfin_ref, ht_ref) = refs
    c = pl.program_id(2)

    @pl.when(c == 0)
    def _():
        ht_ref[...] = s0_ref[0]

    cs = _pair_consts(rev)

    def body(ci, carry):
        cc = (nchunk - 1 - ci) if rev else ci
        rows = pl.ds(pl.multiple_of(cc * CHUNK, CHUNK), CHUNK)
        lw = lw_ref[0, 0, rows, :]
        la = la_ref[0, 0, rows, :]
        lanes = [slice(pr * LANES, (pr + 1) * LANES) for pr in range(npair)]
        r = [r_ref[0, 0, rows, ln] for ln in lanes]
        k = [k_ref[0, 0, rows, ln] for ln in lanes]
        v = [v_ref[0, 0, rows, ln] for ln in lanes]
        kav = [ka_ref[:, ln] for ln in lanes]
        y, ht_new, kd = _rwkv_chunk(
            r, k, v, lw, la, [w2_ref[0, :, ln] for ln in lanes], [a2_ref[0, :, ln] for ln in lanes],
            [w0_ref[0, :, ln] for ln in lanes], [a0_ref[0, :, ln] for ln in lanes],
            [kk_ref[:, ln] for ln in lanes], kav, [ht_ref[pr] for pr in range(npair)], cs, rev)
        for pr in range(npair):
            ht_ref[pr] = ht_new[pr]
        if combine:
            y = _rwkv_combine(
                y, [yf_ref[0, rows, ln] for ln in lanes], r, k, v, kd, laf_ref[0, 0, rows, :],
                [a2f_ref[0, :, ln] for ln in lanes], [a0f_ref[0, :, ln] for ln in lanes], kav,
                [rk_ref[:, ln] for ln in lanes], [lng_ref[:, ln] for ln in lanes],
                [lnb_ref[:, ln] for ln in lanes], cs)
        for pr in range(npair):
            o_ref[0, rows, lanes[pr]] = y[pr]
        return carry

    lax.fori_loop(0, nchunk, body, 0)

    @pl.when(c == pl.num_programs(2) - 1)
    def _():
        sfin_ref[0] = ht_ref[...]


def _rwkv_scan(rkv, small, prm, s0, rev, tblk, npair, y_fwd=None):
    _, bsz, seq, d = rkv.shape
    nblk = seq // tblk
    width = npair * LANES
    e = 1 if rev else 0
    blk = (lambda c: nblk - 1 - c) if rev else (lambda c: c)
    combine = y_fwd is not None

    def tok(m):
        return pl.BlockSpec((1, 1, tblk, width), lambda b, p, c: (m, b, blk(c), p))

    def lora(m, half):
        return pl.BlockSpec((1, 1, tblk, LANES), lambda b, p, c: (m, b, blk(c), half))

    def mat(idx):
        return pl.BlockSpec((1, LANES, width), lambda b, p, c: (idx, 0, p))

    def vec3(idx):
        return pl.BlockSpec((1, 1, width), lambda b, p, c: (idx, 0, p))

    vec = pl.BlockSpec((1, width), lambda b, p, c: (0, p))
    state = pl.BlockSpec((1, npair, LANES, LANES), lambda b, p, c: (b, p, 0, 0))
    out_tok = pl.BlockSpec((1, tblk, width), lambda b, p, c: (b, blk(c), p))
    in_specs = [tok(0), tok(1), tok(2), lora(1, e), lora(2, e), mat(e), mat(e), vec3(e), vec3(e), vec, vec, state]
    args = [rkv, rkv, rkv, small, small, prm["w2"], prm["a2"], prm["w0"], prm["a0"], prm["k_k"], prm["k_a"], s0]
    if combine:
        in_specs += [out_tok, lora(2, 0), mat(0), vec3(0), vec, vec, vec]
        args += [y_fwd, small, prm["a2"], prm["a0"], prm["r_k"], prm["ln_g"], prm["ln_b"]]
    return pl.pallas_call(
        functools.partial(_rwkv_scan_kernel, rev=rev, nchunk=tblk // CHUNK, npair=npair, combine=combine),
        grid=(bsz, d // width, nblk),
        in_specs=in_specs,
        out_specs=[out_tok, state],
        out_shape=[
            jax.ShapeDtypeStruct((bsz, seq, d), F32),
            jax.ShapeDtypeStruct((bsz, d // LANES, LANES, LANES), F32),
        ],
        scratch_shapes=[pltpu.VMEM((npair, LANES, LANES), F32)],
        compiler_params=_params(("arbitrary", "arbitrary", "arbitrary")),
        name="rwkv7_bwd" if rev else "rwkv7_fwd",
    )(*args)


def _rk_out_kernel(z_ref, gs_ref, g2_ref, wo_ref, h_ref, gt_ref, o_ref):
    gate = _dot(_bf(_sigmoid(gs_ref[0, 0])), g2_ref[...])
    y = _dot(_bf(z_ref[0] * gate), wo_ref[...])
    o_ref[0] = h_ref[0] + gt_ref[0] * y


def _rk_out(z, small, g2, wo, h, gt, tm):
    bsz, seq, d = h.shape
    row = lambda b, i: (b, i, 0)
    glora = g2.shape[0]
    return pl.pallas_call(
        _rk_out_kernel,
        grid=(bsz, seq // tm),
        in_specs=[
            pl.BlockSpec((1, tm, d), row),
            pl.BlockSpec((1, 1, tm, glora), lambda b, i: (0, b, i, 0)),
            pl.BlockSpec((glora, d), lambda b, i: (0, 0)),
            pl.BlockSpec((d, d), lambda b, i: (0, 0)),
            pl.BlockSpec((1, tm, d), row),
            pl.BlockSpec((1, 1, d), lambda b, i: (b, 0, 0)),
        ],
        out_specs=pl.BlockSpec((1, tm, d), row),
        out_shape=jax.ShapeDtypeStruct((bsz, seq, d), F32),
        compiler_params=_params(("arbitrary", "arbitrary")),
        name="rwkv_outproj",
    )(z, small, g2, wo, h, gt)


def _rope_tables(seq):
    t = jnp.arange(seq, dtype=jnp.int32)
    rows = (t // GRID_W).astype(F32)
    cols = (t % GRID_W).astype(F32)
    half = HEAD_DIM // 2
    n_freq = half // 2
    inv = ROPE_BASE ** (-jnp.arange(n_freq, dtype=F32) / n_freq)
    lane = jnp.arange(HEAD_DIM)
    pos = jnp.where((lane < half)[None, :], rows[:, None], cols[:, None])
    ang = pos * inv[lane % n_freq][None, :]
    cos, sin = jnp.cos(ang), jnp.sin(ang)
    first = ((lane % half) < n_freq)[None, :]
    return cos, jnp.where(first, -sin, 0.0), jnp.where(first, 0.0, sin)


def _pad_lanes(w, axis):
    pad = [(0, 0)] * w.ndim
    pad[axis] = (0, LANES - w.shape[axis])
    return jnp.pad(w, pad)


def _row_tile(seq, want):
    return want if seq % want == 0 else seq


def kernel(x, c, ctx, c_ctx, mod_w, mod_b, norm_mix, norm_ffn, ffn_up, ffn_down, ab_w_in, ab_w_out, attn_sink, hgrn_lb, hgrn_onorm, rk_mix, rk_wr, rk_wk, rk_wv, rk_wo, rk_w0, rk_w1, rk_w2, rk_a0, rk_a1, rk_a2, rk_g1, rk_g2, rk_kk, rk_ka, rk_rk, rk_ln_g, rk_ln_b, final_norm):
    bsz, seq, d = x.shape
    lc = ctx.shape[1]
    depth = mod_w.shape[0]
    assert bsz + 1 <= 8 and seq % 1024 == 0 and lc % CHUNK == 0

    cond8 = jnp.zeros((8, d), F32).at[:bsz].set(c).at[bsz].set(c_ctx)
    mod = _modulation(cond8, mod_w, mod_b)
    lb_all = jnp.cumsum(jax.nn.softmax(hgrn_lb.astype(F32), axis=0), axis=0)
    rope = _rope_tables(seq)

    h, hc = x, ctx
    for layer in range(depth):
        last = layer == depth - 1
        jl = layer // 2
        m_lat = mod[layer, :bsz].reshape(bsz, 1, 6, d)
        m_ctx = jnp.broadcast_to(mod[layer, bsz].reshape(1, 1, 6, d), (bsz, 1, 6, d))
        sh1, sc1, gt1, sh2, sc2, gt2 = (m_lat[:, :, i] for i in range(6))
        csh1, csc1, cgt1, csh2, csc2, cgt2 = (m_ctx[:, :, i] for i in range(6))
        g_mix = norm_mix[layer].reshape(1, d)
        g_ffn = norm_ffn[layer].reshape(1, d)
        if layer % 2 == 0:
            w_in = _bf(ab_w_in[jl])
            w_out = _bf(ab_w_out[jl])
            sink = attn_sink[jl].astype(F32)
            lb = lb_all[jl].reshape(1, B_W)
            onorm = hgrn_onorm[jl].reshape(1, B_W).astype(F32)
            n_in = w_in.shape[1]
            pc = _inproj(hc, g_mix, csc1, csh1, w_in, None, lc, 512)
            pl_ = _inproj(h, g_mix, sc1, sh1, w_in, rope, 1024, 512)
            oa = _win_attn(sink, pl_, pc)
            zeros = jnp.zeros((bsz, B_HEADS, HEAD_DIM, HEAD_DIM), F32)
            nhead = 4
            ocf, scf = _hgrn_scan(pc, lb, zeros, False, lc, nhead)
            ocb, scb = _hgrn_scan(pc, lb, zeros, True, lc, nhead)
            olf, _ = _hgrn_scan(pl_, lb, scf, False, 512, nhead)
            olb, _ = _hgrn_scan(pl_, lb, scb, True, 512, nhead)
            h = _ab_out(oa, olf, olb, pl_, onorm, w_out, h, gt1, 512)
            if not last:
                oca = _ctx_attn(sink, pc)
                hc = _ab_out(oca, ocf, ocb, pc, onorm, w_out, hc, cgt1, lc)
        else:
            w_big = _bf(jnp.stack([rk_wr[jl], rk_wk[jl], rk_wv[jl]]))
            w_small = _bf(jnp.stack([
                rk_g1[jl],
                jnp.concatenate([_pad_lanes(rk_w1[jl, 0], 1), _pad_lanes(rk_w1[jl, 1], 1)], axis=1),
                jnp.concatenate([_pad_lanes(rk_a1[jl, 0], 1), _pad_lanes(rk_a1[jl, 1], 1)], axis=1),
            ]))
            assert w_small.shape[-1] == 2 * LANES
            mix = rk_mix[jl]
            prm = dict(
                w2=_bf(_pad_lanes(rk_w2[jl], 1)), a2=_bf(_pad_lanes(rk_a2[jl], 1)),
                w0=rk_w0[jl].reshape(2, 1, d), a0=rk_a0[jl].reshape(2, 1, d),
                k_k=rk_kk[jl].reshape(1, d), k_a=rk_ka[jl].reshape(1, d), r_k=rk_rk[jl].reshape(1, d),
                ln_g=rk_ln_g[jl].reshape(1, d), ln_b=rk_ln_b[jl].reshape(1, d),
            )
            g2 = _bf(rk_g2[jl])
            wo = _bf(rk_wo[jl])
            rkv_c = _rk_proj(hc, g_mix, csc1, csh1, mix[jnp.array([0, 2, 3])], w_big, lc, d)
            sm_c = _rk_proj(hc, g_mix, csc1, csh1, mix[jnp.array([5, 1, 4])], w_small, lc, 2 * LANES)
            rkv_l = _rk_proj(h, g_mix, sc1, sh1, mix[jnp.array([0, 2, 3])], w_big, 512, d)
            sm_l = _rk_proj(h, g_mix, sc1, sh1, mix[jnp.array([5, 1, 4])], w_small, 512, 2 * LANES)
            zeros = jnp.zeros((bsz, d // LANES, LANES, LANES), F32)
            npair = 8
            ycf, s_f = _rwkv_scan(rkv_c, sm_c, prm, zeros, False, lc, npair)
            zc, s_b = _rwkv_scan(rkv_c, sm_c, prm, zeros, True, lc, npair, y_fwd=ycf)
            ylf, _ = _rwkv_scan(rkv_l, sm_l, prm, s_f, False, 512, npair)
            zl, _ = _rwkv_scan(rkv_l, sm_l, prm, s_b, True, 512, npair, y_fwd=ylf)
            h = _rk_out(zl, sm_l, g2, wo, h, gt1, 512)
            if not last:
                hc = _rk_out(zc, sm_c, g2, wo, hc, cgt1, lc)
        w_up = _bf(ffn_up[layer])
        w_dn = _bf(ffn_down[layer])
        h = _mlp(h, g_ffn, sc2, sh2, gt2, w_up, w_dn, final_norm.reshape(1, d) if last else None, 1024, 512)
        if not last:
            hc = _mlp(hc, g_ffn, csc2, csh2, cgt2, w_up, w_dn, None, lc, 512)
    return h
```

```python
import functools

import jax
import jax.numpy as jnp
import numpy as np
from jax import lax
from jax.experimental import pallas as pl
from jax.experimental.pallas import tpu as pltpu

F32 = jnp.float32
BF16 = jnp.bfloat16
HIGHEST = lax.Precision.HIGHEST

LANES = 128
HEAD_DIM = 128
GRID_W = 64
WINDOW = 128
ROPE_BASE = 10000.0
A_Q_HEADS = 8
A_KV_HEADS = 2
A_GROUP = A_Q_HEADS // A_KV_HEADS
A_Q = A_Q_HEADS * HEAD_DIM
A_KV = A_KV_HEADS * HEAD_DIM
B_HEADS = 8
B_W = B_HEADS * HEAD_DIM
C_HEAD = 64
CHUNK = 64
SUB = 16
EPS = 1e-6
GN_EPS = 64e-5
VMEM_LIMIT = 58 * 1024 * 1024

NT_DIMS = (((1,), (1,)), ((), ()))
TN_DIMS = (((0,), (0,)), ((), ()))


def _dot(a, b, **kw):
    return jnp.dot(a, b, preferred_element_type=F32, **kw)


def _dot_nt(a, b):
    return lax.dot_general(a, b, NT_DIMS, preferred_element_type=F32)


def _dot_tn(a, b):
    return lax.dot_general(a, b, TN_DIMS, preferred_element_type=F32)


def _bf(x):
    return x.astype(BF16)


def _dot_split(x, m):
    hi = _bf(x)
    lo = _bf(x - hi.astype(F32))
    return _dot(hi, m) + _dot(lo, m)


def _sigmoid(x):
    return 1.0 / (1.0 + jnp.exp(-x))


def _silu(x):
    return x * _sigmoid(x)


def _normmod(x, g, sc, sh):
    ms = jnp.mean(x * x, axis=-1, keepdims=True)
    y = (x * lax.rsqrt(ms + EPS)) * g
    return y * (1.0 + sc) + sh


def _cumsum_rows(x, rev):
    n = x.shape[0]
    row = lax.broadcasted_iota(jnp.int32, (n, 1), 0)
    s = 1
    while s < n:
        if rev:
            x = x + jnp.where(row < n - s, pltpu.roll(x, n - s, 0), 0.0)
        else:
            x = x + jnp.where(row >= s, pltpu.roll(x, s, 0), 0.0)
        s *= 2
    return x


def _params(sem):
    return pltpu.CompilerParams(dimension_semantics=sem, vmem_limit_bytes=VMEM_LIMIT)


def _mod_kernel(c_ref, w_ref, b_ref, o_ref):
    s = _bf(_silu(c_ref[...]))
    o_ref[0] = _dot(s, _bf(w_ref[0])) + b_ref[0]


def _modulation(cond8, mod_w, mod_b):
    depth, d, n = mod_w.shape
    tn = 1024
    return pl.pallas_call(
        _mod_kernel,
        grid=(depth, n // tn),
        in_specs=[
            pl.BlockSpec((8, d), lambda l, j: (0, 0)),
            pl.BlockSpec((1, d, tn), lambda l, j: (l, 0, j)),
            pl.BlockSpec((1, 1, tn), lambda l, j: (l, 0, j)),
        ],
        out_specs=pl.BlockSpec((1, 8, tn), lambda l, j: (l, 0, j)),
        out_shape=jax.ShapeDtypeStruct((depth, 8, n), F32),
        compiler_params=_params(("arbitrary", "arbitrary")),
        name="modulation",
    )(cond8, mod_w, mod_b.reshape(depth, 1, n))


def _inproj_kernel(*refs, n_rope):
    if n_rope:
        x_ref, g_ref, sc_ref, sh_ref, w_ref, cos_ref, sna_ref, snb_ref, o_ref, u_ref = refs
    else:
        x_ref, g_ref, sc_ref, sh_ref, w_ref, o_ref, u_ref = refs
    j = pl.program_id(2)

    @pl.when(j == 0)
    def _():
        u_ref[...] = _bf(_normmod(x_ref[0], g_ref[...], sc_ref[0], sh_ref[0]))

    acc = _dot(u_ref[...], w_ref[...])
    if not n_rope:
        o_ref[0] = acc
        return

    per_tile = acc.shape[1] // HEAD_DIM
    n_full, n_rem = n_rope // per_tile, n_rope % per_tile

    def store(n_rot):
        cos, sna, snb = cos_ref[...], sna_ref[...], snb_ref[...]
        for hd in range(n_rot):
            sl = acc[:, hd * HEAD_DIM:(hd + 1) * HEAD_DIM]
            rot = sl * cos + pltpu.roll(sl, 96, 1) * sna + pltpu.roll(sl, 32, 1) * snb
            o_ref[0, :, hd * HEAD_DIM:(hd + 1) * HEAD_DIM] = rot
        if n_rot < per_tile:
            o_ref[0, :, n_rot * HEAD_DIM:] = acc[:, n_rot * HEAD_DIM:]

    pl.when(j < n_full)(lambda: store(per_tile))
    pl.when(j == n_full)(lambda: store(n_rem))
    pl.when(j > n_full)(lambda: store(0))


def _inproj(x, g, sc, sh, w, rope, tm, tn):
    bsz, seq, d = x.shape
    n = w.shape[1]
    n_rope = 0
    in_specs = [
        pl.BlockSpec((1, tm, d), lambda b, i, j: (b, i, 0)),
        pl.BlockSpec((1, d), lambda b, i, j: (0, 0)),
        pl.BlockSpec((1, 1, d), lambda b, i, j: (b, 0, 0)),
        pl.BlockSpec((1, 1, d), lambda b, i, j: (b, 0, 0)),
        pl.BlockSpec((d, tn), lambda b, i, j: (0, j)),
    ]
    args = [x, g, sc, sh, w]
    if rope is not None:
        n_rope = A_Q_HEADS + A_KV_HEADS
        assert tn % HEAD_DIM == 0
        in_specs += [pl.BlockSpec((tm, HEAD_DIM), lambda b, i, j: (i, 0))] * 3
        args += list(rope)
    return pl.pallas_call(
        functools.partial(_inproj_kernel, n_rope=n_rope),
        grid=(bsz, seq // tm, n // tn),
        in_specs=in_specs,
        out_specs=pl.BlockSpec((1, tm, tn), lambda b, i, j: (b, i, j)),
        out_shape=jax.ShapeDtypeStruct((bsz, seq, n), F32),
        scratch_shapes=[pltpu.VMEM((tm, d), BF16)],
        compiler_params=_params(("arbitrary", "arbitrary", "arbitrary")),
        name="ab_inproj",
    )(*args)


def _softmax_av(s_list, v_list, sink_col):
    m = sink_col
    for s in s_list:
        m = jnp.maximum(m, jnp.max(s, axis=-1, keepdims=True))
    den = jnp.exp(sink_col - m)
    out = None
    for s, v in zip(s_list, v_list):
        p = jnp.exp(s - m)
        den = den + jnp.sum(p, axis=-1, keepdims=True)
        o = _dot(_bf(p), v)
        out = o if out is None else out + o
    return out / den


def _sink_column(sink_ref, hk, rows):
    rowh = lax.broadcasted_iota(jnp.int32, (rows, 1), 0) // WINDOW
    col = jnp.full((rows, 1), sink_ref[hk * A_GROUP + A_GROUP - 1], F32)
    for g in range(A_GROUP - 1):
        col = jnp.where(rowh == g, sink_ref[hk * A_GROUP + g], col)
    return col


def _win_attn_kernel(sink_ref, q_ref, kp_ref, kc_ref, kn_ref, vp_ref, vc_ref, vn_ref,
                     ck_ref, cv_ref, o_ref):
    n = pl.program_id(1)
    nb = pl.num_programs(1)
    scale = HEAD_DIM ** -0.5
    q = q_ref[0]
    kband = jnp.concatenate([kp_ref[0], kc_ref[0], kn_ref[0]], axis=0)
    vband = jnp.concatenate([vp_ref[0], vc_ref[0], vn_ref[0]], axis=0)
    rows = A_GROUP * WINDOW
    tq = lax.broadcasted_iota(jnp.int32, (rows, 3 * WINDOW), 0) % WINDOW
    tk = lax.broadcasted_iota(jnp.int32, (rows, 3 * WINDOW), 1)
    rel = tk - WINDOW - tq
    valid = (jnp.abs(rel) <= WINDOW) & ((tk >= WINDOW) | (n > 0)) & ((tk < 2 * WINDOW) | (n < nb - 1))
    for hk in range(A_KV_HEADS):
        qs = jnp.concatenate(
            [q[:, (hk * A_GROUP + g) * HEAD_DIM:(hk * A_GROUP + g + 1) * HEAD_DIM] for g in range(A_GROUP)],
            axis=0)
        qs = _bf(qs)
        hs = slice(hk * HEAD_DIM, (hk + 1) * HEAD_DIM)
        s_win = _dot_nt(qs, _bf(kband[:, hs])) * scale
        s_win = jnp.where(valid, s_win, -jnp.inf)
        s_ctx = _dot_nt(qs, _bf(ck_ref[0][:, hs])) * scale
        o = _softmax_av([s_win, s_ctx], [_bf(vband[:, hs]), _bf(cv_ref[0][:, hs])],
                        _sink_column(sink_ref, hk, rows))
        for g in range(A_GROUP):
            h = hk * A_GROUP + g
            o_ref[0, :, h * HEAD_DIM:(h + 1) * HEAD_DIM] = _bf(o[g * WINDOW:(g + 1) * WINDOW])


def _win_attn(sink, proj, proj_ctx):
    bsz, seq, _ = proj.shape
    lc = proj_ctx.shape[1]
    nb = seq // WINDOW
    kcol, vcol = A_Q // A_KV, A_Q // A_KV + 1
    prev = lambda b, n: (b, jnp.maximum(n - 1, 0))
    nxt = lambda b, n: (b, jnp.minimum(n + 1, nb - 1))
    cur = lambda b, n: (b, n)

    def band(rowfn, col):
        return pl.BlockSpec((1, WINDOW, A_KV), lambda b, n: rowfn(b, n) + (col,))

    return pl.pallas_call(
        _win_attn_kernel,
        grid=(bsz, nb),
        in_specs=[
            pl.BlockSpec(memory_space=pltpu.SMEM),
            pl.BlockSpec((1, WINDOW, A_Q), lambda b, n: (b, n, 0)),
            band(prev, kcol), band(cur, kcol), band(nxt, kcol),
            band(prev, vcol), band(cur, vcol), band(nxt, vcol),
            pl.BlockSpec((1, lc, A_KV), lambda b, n: (b, 0, kcol)),
            pl.BlockSpec((1, lc, A_KV), lambda b, n: (b, 0, vcol)),
        ],
        out_specs=pl.BlockSpec((1, WINDOW, A_Q), lambda b, n: (b, n, 0)),
        out_shape=jax.ShapeDtypeStruct((bsz, seq, A_Q), BF16),
        compiler_params=_params(("arbitrary", "arbitrary")),
        name="window_attention",
    )(sink, proj, proj, proj, proj, proj, proj, proj, proj_ctx, proj_ctx)


def _ctx_attn_kernel(sink_ref, q_ref, k_ref, v_ref, o_ref):
    scale = HEAD_DIM ** -0.5
    q = q_ref[0]
    lc = q.shape[0]
    for h in range(A_Q_HEADS):
        hk = h // A_GROUP
        hs = slice(hk * HEAD_DIM, (hk + 1) * HEAD_DIM)
        qs = _bf(q[:, h * HEAD_DIM:(h + 1) * HEAD_DIM])
        s = _dot_nt(qs, _bf(k_ref[0][:, hs])) * scale
        sink_col = jnp.full((lc, 1), sink_ref[h], F32)
        o = _softmax_av([s], [_bf(v_ref[0][:, hs])], sink_col)
        o_ref[0, :, h * HEAD_DIM:(h + 1) * HEAD_DIM] = _bf(o)


def _ctx_attn(sink, proj_ctx):
    bsz, lc, _ = proj_ctx.shape
    kcol, vcol = A_Q // A_KV, A_Q // A_KV + 1
    return pl.pallas_call(
        _ctx_attn_kernel,
        grid=(bsz,),
        in_specs=[
            pl.BlockSpec(memory_space=pltpu.SMEM),
            pl.BlockSpec((1, lc, A_Q), lambda b: (b, 0, 0)),
            pl.BlockSpec((1, lc, A_KV), lambda b: (b, 0, kcol)),
            pl.BlockSpec((1, lc, A_KV), lambda b: (b, 0, vcol)),
        ],
        out_specs=pl.BlockSpec((1, lc, A_Q), lambda b: (b, 0, 0)),
        out_shape=jax.ShapeDtypeStruct((bsz, lc, A_Q), BF16),
        compiler_params=_params(("arbitrary",)),
        name="context_attention",
    )(sink, proj_ctx, proj_ctx, proj_ctx)


def _hgrn_chunk(bq, bi, bf, lb, st, rev):
    heads = range(len(bq))
    q = [_silu(x) for x in bq]
    v = bi
    f = [lb[h] + (1.0 - lb[h]) * _sigmoid(bf[h]) for h in heads]
    k = [1.0 - x for x in f]
    g = [jnp.log(x) for x in f]
    b = [_cumsum_rows(x, rev) for x in g]
    btot = [x[0:1] if rev else x[CHUNK - 1:CHUNK] for x in b]
    vb = [_bf(x) for x in v]
    stb = [_bf(x) for x in st]
    o = [_dot_nt(_bf(q[h] * jnp.exp(b[h])), stb[h]) for h in heads]
    khat = [_bf(k[h] * jnp.exp(btot[h] - b[h])) for h in heads]
    dst = [_dot_tn(vb[h], khat[h]) for h in heads]
    st_new = [st[h] * jnp.exp(btot[h]) + dst[h] for h in heads]
    nsub = CHUNK // SUB
    row16 = lax.broadcasted_iota(jnp.int32, (SUB, 1), 0)
    outs = [[] for _ in heads]
    for blk in range(nsub):
        r0 = blk * SUB
        rs = slice(r0, r0 + SUB)
        acc = [o[h][rs] for h in heads]
        if rev and blk < nsub - 1:
            ref_row, lo, hi = r0 + SUB, r0 + SUB, CHUNK
        elif (not rev) and blk > 0:
            ref_row, lo, hi = r0 - 1, 0, r0
        else:
            ref_row = None
        if ref_row is not None:
            bref = [b[h][ref_row:ref_row + 1] for h in heads]
            qn = [_bf(q[h][rs] * jnp.exp(b[h][rs] - bref[h])) for h in heads]
            kn = [_bf(k[h][lo:hi] * jnp.exp(bref[h] - b[h][lo:hi])) for h in heads]
            att = [_bf(_dot_nt(qn[h], kn[h])) for h in heads]
            acc = [acc[h] + _dot(att[h], vb[h][lo:hi]) for h in heads]
        for s in range(SUB):
            mask = (row16 <= s) if rev else (row16 >= s)
            for h in heads:
                bi_ = b[h][rs]
                dec = jnp.exp(jnp.where(mask, bi_ - bi_[s:s + 1], -jnp.inf))
                w = jnp.sum(q[h][rs] * dec * k[h][r0 + s:r0 + s + 1], axis=-1, keepdims=True)
                acc[h] = acc[h] + w * v[h][r0 + s:r0 + s + 1]
        for h in heads:
            outs[h].append(acc[h])
    return [jnp.concatenate(x, axis=0) for x in outs], st_new


def _hgrn_kernel(q_ref, i_ref, f_ref, lb_ref, s0_ref, o_ref, sfin_ref, st_ref, *, rev, nchunk, nhead):
    c = pl.program_id(2)

    @pl.when(c == 0)
    def _():
        st_ref[...] = s0_ref[0]

    lanes = [slice(h * HEAD_DIM, (h + 1) * HEAD_DIM) for h in range(nhead)]
    lb = [lb_ref[:, ln] for ln in lanes]

    def body(ci, carry):
        cc = (nchunk - 1 - ci) if rev else ci
        rows = pl.ds(pl.multiple_of(cc * CHUNK, CHUNK), CHUNK)
        o, st_new = _hgrn_chunk([q_ref[0, rows, ln] for ln in lanes], [i_ref[0, rows, ln] for ln in lanes],
                                [f_ref[0, rows, ln] for ln in lanes], lb,
                                [st_ref[h] for h in range(nhead)], rev)
        for h in range(nhead):
            o_ref[0, rows, lanes[h]] = o[h]
            st_ref[h] = st_new[h]
        return carry

    lax.fori_loop(0, nchunk, body, 0)

    @pl.when(c == pl.num_programs(2) - 1)
    def _():
        sfin_ref[0] = st_ref[...]


def _hgrn_scan(proj, lb, s0, rev, tblk, nhead):
    bsz, seq, _ = proj.shape
    nblk = seq // tblk
    width = nhead * HEAD_DIM
    base = A_Q + 2 * A_KV
    assert base % width == 0 and B_W % width == 0
    qc, ic = base // width, (base + B_W) // width
    fc = (base + (3 if rev else 2) * B_W) // width
    blk = (lambda c: nblk - 1 - c) if rev else (lambda c: c)

    def col(c0):
        return pl.BlockSpec((1, tblk, width), lambda b, h, c: (b, blk(c), c0 + h))

    state = pl.BlockSpec((1, nhead, HEAD_DIM, HEAD_DIM), lambda b, h, c: (b, h, 0, 0))
    return pl.pallas_call(
        functools.partial(_hgrn_kernel, rev=rev, nchunk=tblk // CHUNK, nhead=nhead),
        grid=(bsz, B_HEADS // nhead, nblk),
        in_specs=[
            col(qc), col(ic), col(fc),
            pl.BlockSpec((1, width), lambda b, h, c: (0, h)),
            state,
        ],
        out_specs=[
            pl.BlockSpec((1, tblk, width), lambda b, h, c: (b, blk(c), h)),
            state,
        ],
        out_shape=[
            jax.ShapeDtypeStruct((bsz, seq, B_W), F32),
            jax.ShapeDtypeStruct((bsz, B_HEADS, HEAD_DIM, HEAD_DIM), F32),
        ],
        scratch_shapes=[pltpu.VMEM((nhead, HEAD_DIM, HEAD_DIM), F32)],
        compiler_params=_params(("arbitrary", "arbitrary", "arbitrary")),
        name="hgrn2_bwd" if rev else "hgrn2_fwd",
    )(proj, proj, proj, lb, s0)


def _ab_out_kernel(oa_ref, of_ref, ob_ref, g0_ref, g1_ref, on_ref, w_ref, h_ref, gt_ref, o_ref, lhs_ref):
    lhs_ref[:, :A_Q] = oa_ref[0]
    half = B_W // 2
    for hd in range(B_HEADS):
        sl = slice(hd * HEAD_DIM, (hd + 1) * HEAD_DIM)
        o = of_ref[0, :, sl] + ob_ref[0, :, sl]
        o = o * lax.rsqrt(jnp.mean(o * o, axis=-1, keepdims=True) + EPS)
        o = o * on_ref[:, sl]
        gref = g0_ref if hd * HEAD_DIM < half else g1_ref
        gs = slice(hd * HEAD_DIM % half, hd * HEAD_DIM % half + HEAD_DIM)
        o = o * _silu(gref[0, :, gs])
        lhs_ref[:, A_Q + hd * HEAD_DIM:A_Q + (hd + 1) * HEAD_DIM] = _bf(o)
    y = _dot(lhs_ref[...], w_ref[...])
    o_ref[0] = h_ref[0] + gt_ref[0] * y


def _ab_out(oa, of, ob, proj, onorm, w_out, h, gt, tm):
    bsz, seq, d = h.shape
    half = B_W // 2
    gcol = (A_Q + 2 * A_KV + 4 * B_W) // half
    row = lambda b, i: (b, i, 0)
    return pl.pallas_call(
        _ab_out_kernel,
        grid=(bsz, seq // tm),
        in_specs=[
            pl.BlockSpec((1, tm, A_Q), row),
            pl.BlockSpec((1, tm, B_W), row),
            pl.BlockSpec((1, tm, B_W), row),
            pl.BlockSpec((1, tm, half), lambda b, i: (b, i, gcol)),
            pl.BlockSpec((1, tm, half), lambda b, i: (b, i, gcol + 1)),
            pl.BlockSpec((1, B_W), lambda b, i: (0, 0)),
            pl.BlockSpec((A_Q + B_W, d), lambda b, i: (0, 0)),
            pl.BlockSpec((1, tm, d), row),
            pl.BlockSpec((1, 1, d), lambda b, i: (b, 0, 0)),
        ],
        out_specs=pl.BlockSpec((1, tm, d), row),
        out_shape=jax.ShapeDtypeStruct((bsz, seq, d), F32),
        scratch_shapes=[pltpu.VMEM((tm, A_Q + B_W), BF16)],
        compiler_params=_params(("arbitrary", "arbitrary")),
        name="ab_outproj",
    )(oa, of, ob, proj, proj, onorm, w_out, h, gt)


def _mlp_kernel(*refs, final):
    if final:
        x_ref, g_ref, sc_ref, sh_ref, gt_ref, wu_ref, wd_ref, fn_ref, o_ref, u_ref = refs
    else:
        x_ref, g_ref, sc_ref, sh_ref, gt_ref, wu_ref, wd_ref, o_ref, u_ref = refs
    j = pl.program_id(2)

    @pl.when(j == 0)
    def _():
        u_ref[...] = _bf(_normmod(x_ref[0], g_ref[...], sc_ref[0], sh_ref[0]))

    hid = jnp.maximum(_dot(u_ref[...], wu_ref[...]), 0.0)
    hid = _bf(hid * hid)
    ncol = 512

    @pl.when(j == 0)
    def _():
        for n0 in range(0, o_ref.shape[2], ncol):
            o_ref[0, :, n0:n0 + ncol] = _dot(hid, wd_ref[:, n0:n0 + ncol])

    @pl.when(j != 0)
    def _():
        for n0 in range(0, o_ref.shape[2], ncol):
            o_ref[0, :, n0:n0 + ncol] += _dot(hid, wd_ref[:, n0:n0 + ncol])

    @pl.when(j == pl.num_programs(2) - 1)
    def _():
        y = x_ref[0] + gt_ref[0] * o_ref[0]
        if final:
            y = (y * lax.rsqrt(jnp.mean(y * y, axis=-1, keepdims=True) + EPS)) * fn_ref[...]
        o_ref[0] = y


def _mlp(x, g, sc, sh, gt, w_up, w_down, final_gain, tm, tf):
    bsz, seq, d = x.shape
    dff = w_up.shape[1]
    vec = pl.BlockSpec((1, 1, d), lambda b, i, j: (b, 0, 0))
    in_specs = [
        pl.BlockSpec((1, tm, d), lambda b, i, j: (b, i, 0)),
        pl.BlockSpec((1, d), lambda b, i, j: (0, 0)),
        vec, vec, vec,
        pl.BlockSpec((d, tf), lambda b, i, j: (0, j)),
        pl.BlockSpec((tf, d), lambda b, i, j: (j, 0)),
    ]
    args = [x, g, sc, sh, gt, w_up, w_down]
    if final_gain is not None:
        in_specs.append(pl.BlockSpec((1, d), lambda b, i, j: (0, 0)))
        args.append(final_gain)
    return pl.pallas_call(
        functools.partial(_mlp_kernel, final=final_gain is not None),
        grid=(bsz, seq // tm, dff // tf),
        in_specs=in_specs,
        out_specs=pl.BlockSpec((1, tm, d), lambda b, i, j: (b, i, 0)),
        out_shape=jax.ShapeDtypeStruct((bsz, seq, d), F32),
        scratch_shapes=[pltpu.VMEM((tm, d), BF16)],
        compiler_params=_params(("arbitrary", "arbitrary", "arbitrary")),
        name="sqrelu_mlp",
    )(*args)


def _rk_proj_kernel(x_ref, xp_ref, xn_ref, g_ref, sc_ref, sh_ref, mix_ref, wb_ref, ws_ref, ob_ref, os_ref,
                    u_ref, lhs_ref, *, nbig, nsmall):
    i = pl.program_id(1)
    j = pl.program_id(2)
    tm = x_ref.shape[1]
    rb = 128

    def build(m, slot):
        mixrow = mix_ref[pl.ds(m, 1), :]
        for r0 in range(0, tm, rb):
            u = u_ref[8 + r0:8 + r0 + rb]
            xx = 0.5 * (u_ref[7 + r0:7 + r0 + rb] + u_ref[9 + r0:9 + r0 + rb]) - u
            lhs_ref[slot, r0:r0 + rb] = _bf(u + xx * mixrow)

    @pl.when(j == 0)
    def _():
        g, sc, sh = g_ref[...], sc_ref[0], sh_ref[0]
        for r0 in range(0, tm, rb):
            u_ref[8 + r0:8 + r0 + rb] = _normmod(x_ref[0, r0:r0 + rb], g, sc, sh)
        up = _normmod(xp_ref[0], g, sc, sh)[7:8]
        un = _normmod(xn_ref[0], g, sc, sh)[0:1]
        u_ref[7:8] = jnp.where(i == 0, 0.0, up)
        u_ref[8 + tm:9 + tm] = jnp.where(i == pl.num_programs(1) - 1, 0.0, un)
        build(0, 0)

    nxt = jnp.minimum(j + 1, nbig + nsmall - 1)

    @pl.when(j < nbig)
    def _():
        build(nxt, (j + 1) % 2)
        ob_ref[0, 0] = _dot(lhs_ref[j % 2], wb_ref[0])

    for s in range(nsmall):
        @pl.when(j == nbig + s)
        def _():
            if s + 1 < nsmall:
                build(nbig + s + 1, (nbig + s + 1) % 2)
            os_ref[s, 0] = _dot(lhs_ref[(nbig + s) % 2], ws_ref[s])


def _rk_proj(x, g, sc, sh, mix, w_big, w_small, tm):
    bsz, seq, d = x.shape
    nbig, _, n = w_big.shape
    nsmall, _, ns = w_small.shape
    r8 = tm // 8
    last8 = seq // 8 - 1
    vec = pl.BlockSpec((1, 1, d), lambda b, i, j: (b, 0, 0))
    return pl.pallas_call(
        functools.partial(_rk_proj_kernel, nbig=nbig, nsmall=nsmall),
        grid=(bsz, seq // tm, nbig + nsmall),
        in_specs=[
            pl.BlockSpec((1, tm, d), lambda b, i, j: (b, i, 0)),
            pl.BlockSpec((1, 8, d), lambda b, i, j: (b, jnp.maximum(i * r8 - 1, 0), 0)),
            pl.BlockSpec((1, 8, d), lambda b, i, j: (b, jnp.minimum((i + 1) * r8, last8), 0)),
            pl.BlockSpec((1, d), lambda b, i, j: (0, 0)),
            vec, vec,
            pl.BlockSpec((nbig + nsmall, d), lambda b, i, j: (0, 0)),
            pl.BlockSpec((1, d, n), lambda b, i, j: (jnp.minimum(j, nbig - 1), 0, 0)),
            pl.BlockSpec((nsmall, d, ns), lambda b, i, j: (0, 0, 0)),
        ],
        out_specs=[
            pl.BlockSpec((1, 1, tm, n), lambda b, i, j: (jnp.minimum(j, nbig - 1), b, i, 0)),
            pl.BlockSpec((nsmall, 1, tm, ns), lambda b, i, j: (0, b, i, 0)),
        ],
        out_shape=[
            jax.ShapeDtypeStruct((nbig, bsz, seq, n), F32),
            jax.ShapeDtypeStruct((nsmall, bsz, seq, ns), F32),
        ],
        scratch_shapes=[pltpu.VMEM((tm + 16, d), F32), pltpu.VMEM((2, tm, d), BF16)],
        compiler_params=_params(("arbitrary", "arbitrary", "arbitrary")),
        name="rwkv_proj",
    )(x, x, x, g, sc, sh, mix, w_big, w_small)


def _softplus(x):
    return jnp.maximum(x, 0.0) + jnp.log(1.0 + jnp.exp(-jnp.abs(x)))


def _pair_consts(rev):
    n = 2 * CHUNK
    ri = lax.broadcasted_iota(jnp.int32, (n, n), 0)
    ci = lax.broadcasted_iota(jnp.int32, (n, n), 1)
    same = (ri // CHUNK) == (ci // CHUNK)
    rt, ct = ri % CHUNK, ci % CHUNK
    strict = same & ((ct > rt) if rev else (ct < rt))
    incl = same & ((ct >= rt) if rev else (ct <= rt))
    top = ri < CHUNK
    eye = jnp.where(ri == ci, 1.0, 0.0).astype(F32)
    lane = lax.broadcasted_iota(jnp.int32, (1, LANES), 1)
    lane_lo = lane < C_HEAD
    m0 = jnp.where(lane_lo, 1.0, 0.0).astype(F32)
    m1 = 1.0 - m0
    return dict(strict=strict, incl=incl, top=top, eye=eye, m0=m0, m1=m1, lane_lo=lane_lo, ri=ri, ci=ci)


def _head_sums(x, cs):
    s0 = jnp.sum(x * cs["m0"], axis=-1, keepdims=True)
    s1 = jnp.sum(x * cs["m1"], axis=-1, keepdims=True)
    return jnp.where(cs["lane_lo"], s0, s1)


def _unit_tri_inverse(n, cs):
    ri, ci = cs["ri"], cs["ci"]
    blk2 = (ri // 2) == (ci // 2)
    t = [cs["eye"] + jnp.where(blk2, x, 0.0) for x in n]
    m = 2
    while m < CHUNK:
        sel = ((ri // (2 * m)) == (ci // (2 * m))) & ((ri // m) != (ci // m))
        off = [_bf(jnp.where(sel, x, 0.0)) for x in n]
        tb = [_bf(x) for x in t]
        q = [_bf(_dot(a, b)) for a, b in zip(tb, off)]
        q = [_dot(a, b) for a, b in zip(q, tb)]
        t = [a + b for a, b in zip(t, q)]
        m *= 2
    return t


def _stack2(x, cs):
    return jnp.concatenate([x * cs["m0"], x * cs["m1"]], axis=0)


def _rwkv_chunk(r, k, v, lw, la, w2, a2, w0, a0, kkv, kav, ht, cs, rev):
    pairs = range(len(r))
    tl, lab = _bf(jnp.tanh(lw)), _bf(la)
    z = [w0[p] + _dot(tl, w2[p]) for p in pairs]
    za = [_dot(lab, a2[p]) for p in pairs]
    kkr = [k[p] * kkv[p] for p in pairs]
    ss = [_head_sums(x * x, cs) for x in kkr]
    ld = [-jnp.exp(-_softplus(-x) - 0.5) for x in z]
    cl = [_cumsum_rows(x, rev) for x in ld]
    a = [_sigmoid(a0[p] + za[p]) for p in pairs]
    kk = [x / jnp.maximum(jnp.sqrt(s), 1e-12) for x, s in zip(kkr, ss)]
    kd = [k[p] * (1.0 + (a[p] - 1.0) * kav[p]) for p in pairs]
    bb = [x * y for x, y in zip(kk, a)]
    ctot = [x[0:1] if rev else x[CHUNK - 1:CHUNK] for x in cl]
    e_neg = [jnp.exp(-x) for x in cl]
    e_tail = [jnp.exp(c - x) for c, x in zip(ctot, cl)]
    at = [_bf(_stack2(-kk[p] * jnp.exp(cl[p] - ld[p]), cs)) for p in pairs]
    rt = [_bf(_stack2(r[p] * jnp.exp(cl[p]), cs)) for p in pairs]
    rhs = [_bf(jnp.concatenate([bb[p] * e_neg[p], kd[p] * e_neg[p]], axis=0)) for p in pairs]
    g = [_dot_nt(jnp.concatenate([x, y], axis=0), w) for x, y, w in zip(at, rt, rhs)]
    g1 = [x[:2 * CHUNK] for x in g]
    g2 = [x[2 * CHUNK:] for x in g]
    top, strict, incl = cs["top"], cs["strict"], cs["incl"]
    g1r = [pltpu.roll(x, C_HEAD, 1) for x in g1]
    n_ab = [jnp.where(strict, jnp.where(top, x, y), 0.0) for x, y in zip(g1, g1r)]
    n_ak = [_bf(jnp.where(strict, jnp.where(top, y, x), 0.0)) for x, y in zip(g1, g1r)]
    t = _unit_tri_inverse(n_ab, cs)
    vs = [_bf(_stack2(x, cs)) for x in v]
    hkv = [_bf(x.T) for x in ht]
    xx = [_bf(_dot(jnp.concatenate([at[p], n_ak[p]], axis=1), jnp.concatenate([hkv[p], vs[p]], axis=0)))
          for p in pairs]
    u =[_bf(_dot(_bf(a_), b_)) for a_, b_ in zip(t, xx)]
    bk = [_bf(jnp.concatenate([_stack2(bb[p] * e_tail[p], cs), _stack2(kd[p] * e_tail[p], cs)], axis=0))
          for p in pairs]
    uv = [jnp.concatenate([a_, b_], axis=0) for a_, b_ in zip(u, vs)]
    dh = [_dot_tn(a_, b_) for a_, b_ in zip(uv, bk)]
    ht_new = [ht[p] * jnp.exp(ctot[p]) + dh[p] for p in pairs]
    g2r = [pltpu.roll(x, C_HEAD, 1) for x in g2]
    n_rb = [_bf(jnp.where(incl, jnp.where(top, x, y), 0.0)) for x, y in zip(g2, g2r)]
    n_rk = [_bf(jnp.where(incl, jnp.where(top, y, x), 0.0)) for x, y in zip(g2, g2r)]
    ysum = [_dot(jnp.concatenate([rt[p], n_rb[p], n_rk[p]], axis=1),
                 jnp.concatenate([hkv[p], u[p], vs[p]], axis=0)) for p in pairs]
    y = [x[:CHUNK] + x[CHUNK:] for x in ysum]
    return y, ht_new, kd


def _rwkv_combine(y, yf, r, k, v, kd, laf, a2f, a0f, kav, rkv, lng, lnb, cs):
    pairs = range(len(y))
    inv_n = 1.0 / C_HEAD
    lafb = _bf(laf)
    zf = [_dot(lafb, a2f[p]) for p in pairs]
    ysum = [a + b for a, b in zip(y, yf)]
    mu = [_head_sums(x, cs) * inv_n for x in ysum]
    dev = [a - b for a, b in zip(ysum, mu)]
    var = [_head_sums(x * x, cs) * inv_n for x in dev]
    a_f = [_sigmoid(a0f[p] + zf[p]) for p in pairs]
    kd_f = [k[p] * (1.0 + (a_f[p] - 1.0) * kav[p]) for p in pairs]
    bsum = [_head_sums(r[p] * (kd_f[p] + kd[p]) * rkv[p], cs) for p in pairs]
    return [dev[p] * lax.rsqrt(var[p] + GN_EPS) * lng[p] + lnb[p] + bsum[p] * v[p] for p in pairs]


def _rwkv_scan_kernel(*refs, rev, nchunk, npair, combine):
    if combine:
        (r_ref, k_ref, v_ref, lw_ref, la_ref, w2_ref, a2_ref, w0_ref, a0_ref, kk_ref, ka_ref, s0_ref,
         yf_ref, laf_ref, a2f_ref, a0f_ref, rk_ref, lng_ref, lnb_ref, o_ref, sfin_ref, ht_ref) = refs
    else:
        (r_ref, k_ref, v_ref, lw_ref, la_ref, w2_ref, a2_ref, w0_ref, a0_ref, kk_ref, ka_ref, s0_ref,
         o_ref, sfin_ref, ht_ref) = refs
    c = pl.program_id(2)

    @pl.when(c == 0)
    def _():
        ht_ref[...] = s0_ref[0]

    cs = _pair_consts(rev)

    def body(ci, carry):
        cc = (nchunk - 1 - ci) if rev else ci
        rows = pl.ds(pl.multiple_of(cc * CHUNK, CHUNK), CHUNK)
        lw = lw_ref[0, 0, rows, :]
        la = la_ref[0, 0, rows, :]
        lanes = [slice(pr * LANES, (pr + 1) * LANES) for pr in range(npair)]
        r = [r_ref[0, 0, rows, ln] for ln in lanes]
        k = [k_ref[0, 0, rows, ln] for ln in lanes]
        v = [v_ref[0, 0, rows, ln] for ln in lanes]
        kav = [ka_ref[:, ln] for ln in lanes]
        y, ht_new, kd = _rwkv_chunk(
            r, k, v, lw, la, [w2_ref[0, :, ln] for ln in lanes], [a2_ref[0, :, ln] for ln in lanes],
            [w0_ref[0, :, ln] for ln in lanes], [a0_ref[0, :, ln] for ln in lanes],
            [kk_ref[:, ln] for ln in lanes], kav, [ht_ref[pr] for pr in range(npair)], cs, rev)
        for pr in range(npair):
            ht_ref[pr] = ht_new[pr]
        if combine:
            y = _rwkv_combine(
                y, [yf_ref[0, rows, ln] for ln in lanes], r, k, v, kd, laf_ref[0, 0, rows, :],
                [a2f_ref[0, :, ln] for ln in lanes], [a0f_ref[0, :, ln] for ln in lanes], kav,
                [rk_ref[:, ln] for ln in lanes], [lng_ref[:, ln] for ln in lanes],
                [lnb_ref[:, ln] for ln in lanes], cs)
        for pr in range(npair):
            o_ref[0, rows, lanes[pr]] = y[pr]
        return carry

    lax.fori_loop(0, nchunk, body, 0)

    @pl.when(c == pl.num_programs(2) - 1)
    def _():
        sfin_ref[0] = ht_ref[...]


def _rwkv_scan(rkv, small, prm, s0, rev, tblk, npair, y_fwd=None):
    _, bsz, seq, d = rkv.shape
    nblk = seq // tblk
    width = npair * LANES
    e = 1 if rev else 0
    blk = (lambda c: nblk - 1 - c) if rev else (lambda c: c)
    combine = y_fwd is not None

    def tok(m):
        return pl.BlockSpec((1, 1, tblk, width), lambda b, p, c: (m, b, blk(c), p))

    def lora(m, half):
        return pl.BlockSpec((1, 1, tblk, LANES), lambda b, p, c: (m, b, blk(c), half))

    def mat(idx):
        return pl.BlockSpec((1, LANES, width), lambda b, p, c: (idx, 0, p))

    def vec3(idx):
        return pl.BlockSpec((1, 1, width), lambda b, p, c: (idx, 0, p))

    vec = pl.BlockSpec((1, width), lambda b, p, c: (0, p))
    state = pl.BlockSpec((1, npair, LANES, LANES), lambda b, p, c: (b, p, 0, 0))
    out_tok = pl.BlockSpec((1, tblk, width), lambda b, p, c: (b, blk(c), p))
    in_specs = [tok(0), tok(1), tok(2), lora(1, e), lora(2, e), mat(e), mat(e), vec3(e), vec3(e), vec, vec, state]
    args = [rkv, rkv, rkv, small, small, prm["w2"], prm["a2"], prm["w0"], prm["a0"], prm["k_k"], prm["k_a"], s0]
    if combine:
        in_specs += [out_tok, lora(2, 0), mat(0), vec3(0), vec, vec, vec]
        args += [y_fwd, small, prm["a2"], prm["a0"], prm["r_k"], prm["ln_g"], prm["ln_b"]]
    return pl.pallas_call(
        functools.partial(_rwkv_scan_kernel, rev=rev, nchunk=tblk // CHUNK, npair=npair, combine=combine),
        grid=(bsz, d // width, nblk),
        in_specs=in_specs,
        out_specs=[out_tok, state],
        out_shape=[
            jax.ShapeDtypeStruct((bsz, seq, d), F32),
            jax.ShapeDtypeStruct((bsz, d // LANES, LANES, LANES), F32),
        ],
        scratch_shapes=[pltpu.VMEM((npair, LANES, LANES), F32)],
        compiler_params=_params(("arbitrary", "arbitrary", "arbitrary")),
        name="rwkv7_bwd" if rev else "rwkv7_fwd",
    )(*args)


def _rk_out_kernel(z_ref, gs_ref, g2_ref, wo_ref, h_ref, gt_ref, o_ref):
    gate = _dot(_bf(_sigmoid(gs_ref[0, 0])), g2_ref[...])
    y = _dot(_bf(z_ref[0] * gate), wo_ref[...])
    o_ref[0] = h_ref[0] + gt_ref[0] * y


def _rk_out(z, small, g2, wo, h, gt, tm):
    bsz, seq, d = h.shape
    row = lambda b, i: (b, i, 0)
    glora = g2.shape[0]
    return pl.pallas_call(
        _rk_out_kernel,
        grid=(bsz, seq // tm),
        in_specs=[
            pl.BlockSpec((1, tm, d), row),
            pl.BlockSpec((1, 1, tm, glora), lambda b, i: (0, b, i, 0)),
            pl.BlockSpec((glora, d), lambda b, i: (0, 0)),
            pl.BlockSpec((d, d), lambda b, i: (0, 0)),
            pl.BlockSpec((1, tm, d), row),
            pl.BlockSpec((1, 1, d), lambda b, i: (b, 0, 0)),
        ],
        out_specs=pl.BlockSpec((1, tm, d), row),
        out_shape=jax.ShapeDtypeStruct((bsz, seq, d), F32),
        compiler_params=_params(("arbitrary", "arbitrary")),
        name="rwkv_outproj",
    )(z, small, g2, wo, h, gt)


def _rope_tables(seq):
    t = jnp.arange(seq, dtype=jnp.int32)
    rows = (t // GRID_W).astype(F32)
    cols = (t % GRID_W).astype(F32)
    half = HEAD_DIM // 2
    n_freq = half // 2
    inv = ROPE_BASE ** (-jnp.arange(n_freq, dtype=F32) / n_freq)
    lane = jnp.arange(HEAD_DIM)
    pos = jnp.where((lane < half)[None, :], rows[:, None], cols[:, None])
    ang = pos * inv[lane % n_freq][None, :]
    cos, sin = jnp.cos(ang), jnp.sin(ang)
    first = ((lane % half) < n_freq)[None, :]
    return cos, jnp.where(first, -sin, 0.0), jnp.where(first, 0.0, sin)


def _pad_lanes(w, axis):
    pad = [(0, 0)] * w.ndim
    pad[axis] = (0, LANES - w.shape[axis])
    return jnp.pad(w, pad)


def _row_tile(seq, want):
    return want if seq % want == 0 else seq


def kernel(x, c, ctx, c_ctx, mod_w, mod_b, norm_mix, norm_ffn, ffn_up, ffn_down, ab_w_in, ab_w_out, attn_sink, hgrn_lb, hgrn_onorm, rk_mix, rk_wr, rk_wk, rk_wv, rk_wo, rk_w0, rk_w1, rk_w2, rk_a0, rk_a1, rk_a2, rk_g1, rk_g2, rk_kk, rk_ka, rk_rk, rk_ln_g, rk_ln_b, final_norm):
    bsz, seq, d = x.shape
    lc = ctx.shape[1]
    depth = mod_w.shape[0]
    assert bsz + 1 <= 8 and seq % 1024 == 0 and lc % CHUNK == 0

    cond8 = jnp.zeros((8, d), F32).at[:bsz].set(c).at[bsz].set(c_ctx)
    mod = _modulation(cond8, mod_w, mod_b)
    lb_all = jnp.cumsum(jax.nn.softmax(hgrn_lb.astype(F32), axis=0), axis=0)
    rope = _rope_tables(seq)

    h, hc = x, ctx
    for layer in range(depth):
        last = layer == depth - 1
        jl = layer // 2
        m_lat = mod[layer, :bsz].reshape(bsz, 1, 6, d)
        m_ctx = jnp.broadcast_to(mod[layer, bsz].reshape(1, 1, 6, d), (bsz, 1, 6, d))
        sh1, sc1, gt1, sh2, sc2, gt2 = (m_lat[:, :, i] for i in range(6))
        csh1, csc1, cgt1, csh2, csc2, cgt2 = (m_ctx[:, :, i] for i in range(6))
        g_mix = norm_mix[layer].reshape(1, d)
        g_ffn = norm_ffn[layer].reshape(1, d)
        if layer % 2 == 0:
            w_in = _bf(ab_w_in[jl])
            w_out = _bf(ab_w_out[jl])
            sink = attn_sink[jl].astype(F32)
            lb = lb_all[jl].reshape(1, B_W)
            onorm = hgrn_onorm[jl].reshape(1, B_W).astype(F32)
            n_in = w_in.shape[1]
            pc = _inproj(hc, g_mix, csc1, csh1, w_in, None, lc, 512)
            pl_ = _inproj(h, g_mix, sc1, sh1, w_in, rope, 1024, 512)
            oa = _win_attn(sink, pl_, pc)
            zeros = jnp.zeros((bsz, B_HEADS, HEAD_DIM, HEAD_DIM), F32)
            nhead = 4
            ocf, scf = _hgrn_scan(pc, lb, zeros, False, lc, nhead)
            ocb, scb = _hgrn_scan(pc, lb, zeros, True, lc, nhead)
            olf, _ = _hgrn_scan(pl_, lb, scf, False, 512, nhead)
            olb, _ = _hgrn_scan(pl_, lb, scb, True, 512, nhead)
            h = _ab_out(oa, olf, olb, pl_, onorm, w_out, h, gt1, 512)
            if not last:
                oca = _ctx_attn(sink, pc)
                hc = _ab_out(oca, ocf, ocb, pc, onorm, w_out, hc, cgt1, lc)
        else:
            w_big = _bf(jnp.stack([rk_wr[jl], rk_wk[jl], rk_wv[jl]]))
            w_small = _bf(jnp.stack([
                rk_g1[jl],
                jnp.concatenate([_pad_lanes(rk_w1[jl, 0], 1), _pad_lanes(rk_w1[jl, 1], 1)], axis=1),
                jnp.concatenate([_pad_lanes(rk_a1[jl, 0], 1), _pad_lanes(rk_a1[jl, 1], 1)], axis=1),
            ]))
            assert w_small.shape[-1] == 2 * LANES
            mix = rk_mix[jl]
            prm = dict(
                w2=_bf(_pad_lanes(rk_w2[jl], 1)), a2=_bf(_pad_lanes(rk_a2[jl], 1)),
                w0=rk_w0[jl].reshape(2, 1, d), a0=rk_a0[jl].reshape(2, 1, d),
                k_k=rk_kk[jl].reshape(1, d), k_a=rk_ka[jl].reshape(1, d), r_k=rk_rk[jl].reshape(1, d),
                ln_g=rk_ln_g[jl].reshape(1, d), ln_b=rk_ln_b[jl].reshape(1, d),
            )
            g2 = _bf(rk_g2[jl])
            wo = _bf(rk_wo[jl])
            mix = mix[jnp.array([0, 2, 3, 5, 1, 4])]
            rkv_c, sm_c = _rk_proj(hc, g_mix, csc1, csh1, mix, w_big, w_small, lc)
            rkv_l, sm_l = _rk_proj(h, g_mix, sc1, sh1, mix, w_big, w_small, 512)
            zeros = jnp.zeros((bsz, d // LANES, LANES, LANES), F32)
            npair = 8
            ycf, s_f = _rwkv_scan(rkv_c, sm_c, prm, zeros, False, lc, npair)
            zc, s_b = _rwkv_scan(rkv_c, sm_c, prm, zeros, True, lc, npair, y_fwd=ycf)
            ylf, _ = _rwkv_scan(rkv_l, sm_l, prm, s_f, False, 512, npair)
            zl, _ = _rwkv_scan(rkv_l, sm_l, prm, s_b, True, 512, npair, y_fwd=ylf)
            h = _rk_out(zl, sm_l, g2, wo, h, gt1, 512)
            if not last:
                hc = _rk_out(zc, sm_c, g2, wo, hc, cgt1, lc)
        w_up = _bf(ffn_up[layer])
        w_dn = _bf(ffn_down[layer])
        h = _mlp(h, g_ffn, sc2, sh2, gt2, w_up, w_dn, final_norm.reshape(1, d) if last else None, 1024, 512)
        if not last:
            hc = _mlp(hc, g_ffn, csc2, csh2, cgt2, w_up, w_dn, None, lc, 512)
    return h
```

```python
import functools

import jax
import jax.numpy as jnp
import numpy as np
from jax import lax
from jax.experimental import pallas as pl
from jax.experimental.pallas import tpu as pltpu

F32 = jnp.float32
BF16 = jnp.bfloat16
HIGHEST = lax.Precision.HIGHEST

LANES = 128
HEAD_DIM = 128
GRID_W = 64
WINDOW = 128
ROPE_BASE = 10000.0
A_Q_HEADS = 8
A_KV_HEADS = 2
A_GROUP = A_Q_HEADS // A_KV_HEADS
A_Q = A_Q_HEADS * HEAD_DIM
A_KV = A_KV_HEADS * HEAD_DIM
B_HEADS = 8
B_W = B_HEADS * HEAD_DIM
C_HEAD = 64
CHUNK = 64
SUB = 16
EPS = 1e-6
GN_EPS = 64e-5
VMEM_LIMIT = 58 * 1024 * 1024

NT_DIMS = (((1,), (1,)), ((), ()))
TN_DIMS = (((0,), (0,)), ((), ()))


def _dot(a, b, **kw):
    return jnp.dot(a, b, preferred_element_type=F32, **kw)


def _dot_nt(a, b):
    return lax.dot_general(a, b, NT_DIMS, preferred_element_type=F32)


def _dot_tn(a, b):
    return lax.dot_general(a, b, TN_DIMS, preferred_element_type=F32)


def _bf(x):
    return x.astype(BF16)


def _dot_split(x, m):
    hi = _bf(x)
    lo = _bf(x - hi.astype(F32))
    return _dot(hi, m) + _dot(lo, m)


def _sigmoid(x):
    return 1.0 / (1.0 + jnp.exp(-x))


def _silu(x):
    return x * _sigmoid(x)


def _normmod(x, g, sc, sh):
    ms = jnp.mean(x * x, axis=-1, keepdims=True)
    y = (x * lax.rsqrt(ms + EPS)) * g
    return y * (1.0 + sc) + sh


def _cumsum_rows(x, rev):
    n = x.shape[0]
    row = lax.broadcasted_iota(jnp.int32, (n, 1), 0)
    s = 1
    while s < n:
        if rev:
            x = x + jnp.where(row < n - s, pltpu.roll(x, n - s, 0), 0.0)
        else:
            x = x + jnp.where(row >= s, pltpu.roll(x, s, 0), 0.0)
        s *= 2
    return x


def _params(sem):
    return pltpu.CompilerParams(dimension_semantics=sem, vmem_limit_bytes=VMEM_LIMIT)


def _mod_kernel(c_ref, w_ref, b_ref, o_ref):
    s = _bf(_silu(c_ref[...]))
    o_ref[0] = _dot(s, _bf(w_ref[0])) + b_ref[0]


def _modulation(cond8, mod_w, mod_b):
    depth, d, n = mod_w.shape
    tn = 1024
    return pl.pallas_call(
        _mod_kernel,
        grid=(depth, n // tn),
        in_specs=[
            pl.BlockSpec((8, d), lambda l, j: (0, 0)),
            pl.BlockSpec((1, d, tn), lambda l, j: (l, 0, j)),
            pl.BlockSpec((1, 1, tn), lambda l, j: (l, 0, j)),
        ],
        out_specs=pl.BlockSpec((1, 8, tn), lambda l, j: (l, 0, j)),
        out_shape=jax.ShapeDtypeStruct((depth, 8, n), F32),
        compiler_params=_params(("arbitrary", "arbitrary")),
        name="modulation",
    )(cond8, mod_w, mod_b.reshape(depth, 1, n))


def _inproj_kernel(*refs, n_rope):
    if n_rope:
        x_ref, g_ref, sc_ref, sh_ref, w_ref, cos_ref, sna_ref, snb_ref, o_ref, u_ref = refs
    else:
        x_ref, g_ref, sc_ref, sh_ref, w_ref, o_ref, u_ref = refs
    j = pl.program_id(2)

    @pl.when(j == 0)
    def _():
        u_ref[...] = _bf(_normmod(x_ref[0], g_ref[...], sc_ref[0], sh_ref[0]))

    acc = _dot(u_ref[...], w_ref[...])
    if not n_rope:
        o_ref[0] = acc
        return

    per_tile = acc.shape[1] // HEAD_DIM
    n_full, n_rem = n_rope // per_tile, n_rope % per_tile

    def store(n_rot):
        cos, sna, snb = cos_ref[...], sna_ref[...], snb_ref[...]
        for hd in range(n_rot):
            sl = acc[:, hd * HEAD_DIM:(hd + 1) * HEAD_DIM]
            rot = sl * cos + pltpu.roll(sl, 96, 1) * sna + pltpu.roll(sl, 32, 1) * snb
            o_ref[0, :, hd * HEAD_DIM:(hd + 1) * HEAD_DIM] = rot
        if n_rot < per_tile:
            o_ref[0, :, n_rot * HEAD_DIM:] = acc[:, n_rot * HEAD_DIM:]

    pl.when(j < n_full)(lambda: store(per_tile))
    pl.when(j == n_full)(lambda: store(n_rem))
    pl.when(j > n_full)(lambda: store(0))


def _inproj(x, g, sc, sh, w, rope, tm, tn):
    bsz, seq, d = x.shape
    n = w.shape[1]
    n_rope = 0
    in_specs = [
        pl.BlockSpec((1, tm, d), lambda b, i, j: (b, i, 0)),
        pl.BlockSpec((1, d), lambda b, i, j: (0, 0)),
        pl.BlockSpec((1, 1, d), lambda b, i, j: (b, 0, 0)),
        pl.BlockSpec((1, 1, d), lambda b, i, j: (b, 0, 0)),
        pl.BlockSpec((d, tn), lambda b, i, j: (0, j)),
    ]
    args = [x, g, sc, sh, w]
    if rope is not None:
        n_rope = A_Q_HEADS + A_KV_HEADS
        assert tn % HEAD_DIM == 0
        in_specs += [pl.BlockSpec((tm, HEAD_DIM), lambda b, i, j: (i, 0))] * 3
        args += list(rope)
    return pl.pallas_call(
        functools.partial(_inproj_kernel, n_rope=n_rope),
        grid=(bsz, seq // tm, n // tn),
        in_specs=in_specs,
        out_specs=pl.BlockSpec((1, tm, tn), lambda b, i, j: (b, i, j)),
        out_shape=jax.ShapeDtypeStruct((bsz, seq, n), F32),
        scratch_shapes=[pltpu.VMEM((tm, d), BF16)],
        compiler_params=_params(("arbitrary", "arbitrary", "arbitrary")),
        name="ab_inproj",
    )(*args)


def _softmax_av(s_list, v_list, sink_col):
    m = sink_col
    for s in s_list:
        m = jnp.maximum(m, jnp.max(s, axis=-1, keepdims=True))
    den = jnp.exp(sink_col - m)
    out = None
    for s, v in zip(s_list, v_list):
        p = jnp.exp(s - m)
        den = den + jnp.sum(p, axis=-1, keepdims=True)
        o = _dot(_bf(p), v)
        out = o if out is None else out + o
    return out / den


def _sink_column(sink_ref, hk, rows):
    rowh = lax.broadcasted_iota(jnp.int32, (rows, 1), 0) // WINDOW
    col = jnp.full((rows, 1), sink_ref[hk * A_GROUP + A_GROUP - 1], F32)
    for g in range(A_GROUP - 1):
        col = jnp.where(rowh == g, sink_ref[hk * A_GROUP + g], col)
    return col


def _win_attn_kernel(sink_ref, q_ref, kp_ref, kc_ref, kn_ref, vp_ref, vc_ref, vn_ref,
                     ck_ref, cv_ref, o_ref):
    n = pl.program_id(1)
    nb = pl.num_programs(1)
    scale = HEAD_DIM ** -0.5
    q = q_ref[0]
    kband = jnp.concatenate([kp_ref[0], kc_ref[0], kn_ref[0]], axis=0)
    vband = jnp.concatenate([vp_ref[0], vc_ref[0], vn_ref[0]], axis=0)
    rows = A_GROUP * WINDOW
    tq = lax.broadcasted_iota(jnp.int32, (rows, 3 * WINDOW), 0) % WINDOW
    tk = lax.broadcasted_iota(jnp.int32, (rows, 3 * WINDOW), 1)
    rel = tk - WINDOW - tq
    valid = (jnp.abs(rel) <= WINDOW) & ((tk >= WINDOW) | (n > 0)) & ((tk < 2 * WINDOW) | (n < nb - 1))
    for hk in range(A_KV_HEADS):
        qs = jnp.concatenate(
            [q[:, (hk * A_GROUP + g) * HEAD_DIM:(hk * A_GROUP + g + 1) * HEAD_DIM] for g in range(A_GROUP)],
            axis=0)
        qs = _bf(qs)
        hs = slice(hk * HEAD_DIM, (hk + 1) * HEAD_DIM)
        s_win = _dot_nt(qs, _bf(kband[:, hs])) * scale
        s_win = jnp.where(valid, s_win, -jnp.inf)
        s_ctx = _dot_nt(qs, _bf(ck_ref[0][:, hs])) * scale
        o = _softmax_av([s_win, s_ctx], [_bf(vband[:, hs]), _bf(cv_ref[0][:, hs])],
                        _sink_column(sink_ref, hk, rows))
        for g in range(A_GROUP):
            h = hk * A_GROUP + g
            o_ref[0, :, h * HEAD_DIM:(h + 1) * HEAD_DIM] = _bf(o[g * WINDOW:(g + 1) * WINDOW])


def _win_attn(sink, proj, proj_ctx):
    bsz, seq, _ = proj.shape
    lc = proj_ctx.shape[1]
    nb = seq // WINDOW
    kcol, vcol = A_Q // A_KV, A_Q // A_KV + 1
    prev = lambda b, n: (b, jnp.maximum(n - 1, 0))
    nxt = lambda b, n: (b, jnp.minimum(n + 1, nb - 1))
    cur = lambda b, n: (b, n)

    def band(rowfn, col):
        return pl.BlockSpec((1, WINDOW, A_KV), lambda b, n: rowfn(b, n) + (col,))

    return pl.pallas_call(
        _win_attn_kernel,
        grid=(bsz, nb),
        in_specs=[
            pl.BlockSpec(memory_space=pltpu.SMEM),
            pl.BlockSpec((1, WINDOW, A_Q), lambda b, n: (b, n, 0)),
            band(prev, kcol), band(cur, kcol), band(nxt, kcol),
            band(prev, vcol), band(cur, vcol), band(nxt, vcol),
            pl.BlockSpec((1, lc, A_KV), lambda b, n: (b, 0, kcol)),
            pl.BlockSpec((1, lc, A_KV), lambda b, n: (b, 0, vcol)),
        ],
        out_specs=pl.BlockSpec((1, WINDOW, A_Q), lambda b, n: (b, n, 0)),
        out_shape=jax.ShapeDtypeStruct((bsz, seq, A_Q), BF16),
        compiler_params=_params(("arbitrary", "arbitrary")),
        name="window_attention",
    )(sink, proj, proj, proj, proj, proj, proj, proj, proj_ctx, proj_ctx)


def _ctx_attn_kernel(sink_ref, q_ref, k_ref, v_ref, o_ref):
    scale = HEAD_DIM ** -0.5
    q = q_ref[0]
    lc = q.shape[0]
    for h in range(A_Q_HEADS):
        hk = h // A_GROUP
        hs = slice(hk * HEAD_DIM, (hk + 1) * HEAD_DIM)
        qs = _bf(q[:, h * HEAD_DIM:(h + 1) * HEAD_DIM])
        s = _dot_nt(qs, _bf(k_ref[0][:, hs])) * scale
        sink_col = jnp.full((lc, 1), sink_ref[h], F32)
        o = _softmax_av([s], [_bf(v_ref[0][:, hs])], sink_col)
        o_ref[0, :, h * HEAD_DIM:(h + 1) * HEAD_DIM] = _bf(o)


def _ctx_attn(sink, proj_ctx):
    bsz, lc, _ = proj_ctx.shape
    kcol, vcol = A_Q // A_KV, A_Q // A_KV + 1
    return pl.pallas_call(
        _ctx_attn_kernel,
        grid=(bsz,),
        in_specs=[
            pl.BlockSpec(memory_space=pltpu.SMEM),
            pl.BlockSpec((1, lc, A_Q), lambda b: (b, 0, 0)),
            pl.BlockSpec((1, lc, A_KV), lambda b: (b, 0, kcol)),
            pl.BlockSpec((1, lc, A_KV), lambda b: (b, 0, vcol)),
        ],
        out_specs=pl.BlockSpec((1, lc, A_Q), lambda b: (b, 0, 0)),
        out_shape=jax.ShapeDtypeStruct((bsz, lc, A_Q), BF16),
        compiler_params=_params(("arbitrary",)),
        name="context_attention",
    )(sink, proj_ctx, proj_ctx, proj_ctx)


def _hgrn_chunk(bq, bi, bf, lb, st, rev):
    heads = range(len(bq))
    q = [_silu(x) for x in bq]
    v = bi
    f = [lb[h] + (1.0 - lb[h]) * _sigmoid(bf[h]) for h in heads]
    k = [1.0 - x for x in f]
    g = [jnp.log(x) for x in f]
    b = [_cumsum_rows(x, rev) for x in g]
    btot = [x[0:1] if rev else x[CHUNK - 1:CHUNK] for x in b]
    vb = [_bf(x) for x in v]
    stb = [_bf(x) for x in st]
    o = [_dot_nt(_bf(q[h] * jnp.exp(b[h])), stb[h]) for h in heads]
    khat = [_bf(k[h] * jnp.exp(btot[h] - b[h])) for h in heads]
    dst = [_dot_tn(vb[h], khat[h]) for h in heads]
    st_new = [st[h] * jnp.exp(btot[h]) + dst[h] for h in heads]
    nsub = CHUNK // SUB
    row16 = lax.broadcasted_iota(jnp.int32, (SUB, 1), 0)
    outs = [[] for _ in heads]
    for blk in range(nsub):
        r0 = blk * SUB
        rs = slice(r0, r0 + SUB)
        acc = [o[h][rs] for h in heads]
        if rev and blk < nsub - 1:
            ref_row, lo, hi = r0 + SUB, r0 + SUB, CHUNK
        elif (not rev) and blk > 0:
            ref_row, lo, hi = r0 - 1, 0, r0
        else:
            ref_row = None
        if ref_row is not None:
            bref = [b[h][ref_row:ref_row + 1] for h in heads]
            qn = [_bf(q[h][rs] * jnp.exp(b[h][rs] - bref[h])) for h in heads]
            kn = [_bf(k[h][lo:hi] * jnp.exp(bref[h] - b[h][lo:hi])) for h in heads]
            att = [_bf(_dot_nt(qn[h], kn[h])) for h in heads]
            acc = [acc[h] + _dot(att[h], vb[h][lo:hi]) for h in heads]
        for s in range(SUB):
            mask = (row16 <= s) if rev else (row16 >= s)
            for h in heads:
                bi_ = b[h][rs]
                dec = jnp.exp(jnp.where(mask, bi_ - bi_[s:s + 1], -jnp.inf))
                w = jnp.sum(q[h][rs] * dec * k[h][r0 + s:r0 + s + 1], axis=-1, keepdims=True)
                acc[h] = acc[h] + w * v[h][r0 + s:r0 + s + 1]
        for h in heads:
            outs[h].append(acc[h])
    return [jnp.concatenate(x, axis=0) for x in outs], st_new


def _hgrn_kernel(q_ref, i_ref, f_ref, lb_ref, s0_ref, o_ref, sfin_ref, st_ref, *, rev, nchunk, nhead):
    c = pl.program_id(2)

    @pl.when(c == 0)
    def _():
        st_ref[...] = s0_ref[0]

    lanes = [slice(h * HEAD_DIM, (h + 1) * HEAD_DIM) for h in range(nhead)]
    lb = [lb_ref[:, ln] for ln in lanes]

    def body(ci, carry):
        cc = (nchunk - 1 - ci) if rev else ci
        rows = pl.ds(pl.multiple_of(cc * CHUNK, CHUNK), CHUNK)
        o, st_new = _hgrn_chunk([q_ref[0, rows, ln] for ln in lanes], [i_ref[0, rows, ln] for ln in lanes],
                                [f_ref[0, rows, ln] for ln in lanes], lb,
                                [st_ref[h] for h in range(nhead)], rev)
        for h in range(nhead):
            o_ref[0, rows, lanes[h]] = o[h]
            st_ref[h] = st_new[h]
        return carry

    lax.fori_loop(0, nchunk, body, 0)

    @pl.when(c == pl.num_programs(2) - 1)
    def _():
        sfin_ref[0] = st_ref[...]


def _hgrn_scan(proj, lb, s0, rev, tblk, nhead):
    bsz, seq, _ = proj.shape
    nblk = seq // tblk
    width = nhead * HEAD_DIM
    base = A_Q + 2 * A_KV
    assert base % width == 0 and B_W % width == 0
    qc, ic = base // width, (base + B_W) // width
    fc = (base + (3 if rev else 2) * B_W) // width
    blk = (lambda c: nblk - 1 - c) if rev else (lambda c: c)

    def col(c0):
        return pl.BlockSpec((1, tblk, width), lambda b, h, c: (b, blk(c), c0 + h))

    state = pl.BlockSpec((1, nhead, HEAD_DIM, HEAD_DIM), lambda b, h, c: (b, h, 0, 0))
    return pl.pallas_call(
        functools.partial(_hgrn_kernel, rev=rev, nchunk=tblk // CHUNK, nhead=nhead),
        grid=(bsz, B_HEADS // nhead, nblk),
        in_specs=[
            col(qc), col(ic), col(fc),
            pl.BlockSpec((1, width), lambda b, h, c: (0, h)),
            state,
        ],
        out_specs=[
            pl.BlockSpec((1, tblk, width), lambda b, h, c: (b, blk(c), h)),
            state,
        ],
        out_shape=[
            jax.ShapeDtypeStruct((bsz, seq, B_W), F32),
            jax.ShapeDtypeStruct((bsz, B_HEADS, HEAD_DIM, HEAD_DIM), F32),
        ],
        scratch_shapes=[pltpu.VMEM((nhead, HEAD_DIM, HEAD_DIM), F32)],
        compiler_params=_params(("arbitrary", "arbitrary", "arbitrary")),
        name="hgrn2_bwd" if rev else "hgrn2_fwd",
    )(proj, proj, proj, lb, s0)


def _ab_out_kernel(oa_ref, of_ref, ob_ref, g0_ref, g1_ref, on_ref, w_ref, h_ref, gt_ref, o_ref, lhs_ref):
    lhs_ref[:, :A_Q] = oa_ref[0]
    half = B_W // 2
    for hd in range(B_HEADS):
        sl = slice(hd * HEAD_DIM, (hd + 1) * HEAD_DIM)
        o = of_ref[0, :, sl] + ob_ref[0, :, sl]
        o = o * lax.rsqrt(jnp.mean(o * o, axis=-1, keepdims=True) + EPS)
        o = o * on_ref[:, sl]
        gref = g0_ref if hd * HEAD_DIM < half else g1_ref
        gs = slice(hd * HEAD_DIM % half, hd * HEAD_DIM % half + HEAD_DIM)
        o = o * _silu(gref[0, :, gs])
        lhs_ref[:, A_Q + hd * HEAD_DIM:A_Q + (hd + 1) * HEAD_DIM] = _bf(o)
    y = _dot(lhs_ref[...], w_ref[...])
    o_ref[0] = h_ref[0] + gt_ref[0] * y


def _ab_out(oa, of, ob, proj, onorm, w_out, h, gt, tm):
    bsz, seq, d = h.shape
    half = B_W // 2
    gcol = (A_Q + 2 * A_KV + 4 * B_W) // half
    row = lambda b, i: (b, i, 0)
    return pl.pallas_call(
        _ab_out_kernel,
        grid=(bsz, seq // tm),
        in_specs=[
            pl.BlockSpec((1, tm, A_Q), row),
            pl.BlockSpec((1, tm, B_W), row),
            pl.BlockSpec((1, tm, B_W), row),
            pl.BlockSpec((1, tm, half), lambda b, i: (b, i, gcol)),
            pl.BlockSpec((1, tm, half), lambda b, i: (b, i, gcol + 1)),
            pl.BlockSpec((1, B_W), lambda b, i: (0, 0)),
            pl.BlockSpec((A_Q + B_W, d), lambda b, i: (0, 0)),
            pl.BlockSpec((1, tm, d), row),
            pl.BlockSpec((1, 1, d), lambda b, i: (b, 0, 0)),
        ],
        out_specs=pl.BlockSpec((1, tm, d), row),
        out_shape=jax.ShapeDtypeStruct((bsz, seq, d), F32),
        scratch_shapes=[pltpu.VMEM((tm, A_Q + B_W), BF16)],
        compiler_params=_params(("arbitrary", "arbitrary")),
        name="ab_outproj",
    )(oa, of, ob, proj, proj, onorm, w_out, h, gt)


def _mlp_kernel(*refs, final):
    if final:
        x_ref, g_ref, sc_ref, sh_ref, gt_ref, wu_ref, wd_ref, fn_ref, o_ref, u_ref = refs
    else:
        x_ref, g_ref, sc_ref, sh_ref, gt_ref, wu_ref, wd_ref, o_ref, u_ref = refs
    j = pl.program_id(2)

    @pl.when(j == 0)
    def _():
        u_ref[...] = _bf(_normmod(x_ref[0], g_ref[...], sc_ref[0], sh_ref[0]))

    hid = jnp.maximum(_dot(u_ref[...], wu_ref[...]), 0.0)
    hid = _bf(hid * hid)
    ncol = 512

    @pl.when(j == 0)
    def _():
        for n0 in range(0, o_ref.shape[2], ncol):
            o_ref[0, :, n0:n0 + ncol] = _dot(hid, wd_ref[:, n0:n0 + ncol])

    @pl.when(j != 0)
    def _():
        for n0 in range(0, o_ref.shape[2], ncol):
            o_ref[0, :, n0:n0 + ncol] += _dot(hid, wd_ref[:, n0:n0 + ncol])

    @pl.when(j == pl.num_programs(2) - 1)
    def _():
        y = x_ref[0] + gt_ref[0] * o_ref[0]
        if final:
            y = (y * lax.rsqrt(jnp.mean(y * y, axis=-1, keepdims=True) + EPS)) * fn_ref[...]
        o_ref[0] = y


def _mlp(x, g, sc, sh, gt, w_up, w_down, final_gain, tm, tf):
    bsz, seq, d = x.shape
    dff = w_up.shape[1]
    vec = pl.BlockSpec((1, 1, d), lambda b, i, j: (b, 0, 0))
    in_specs = [
        pl.BlockSpec((1, tm, d), lambda b, i, j: (b, i, 0)),
        pl.BlockSpec((1, d), lambda b, i, j: (0, 0)),
        vec, vec, vec,
        pl.BlockSpec((d, tf), lambda b, i, j: (0, j)),
        pl.BlockSpec((tf, d), lambda b, i, j: (j, 0)),
    ]
    args = [x, g, sc, sh, gt, w_up, w_down]
    if final_gain is not None:
        in_specs.append(pl.BlockSpec((1, d), lambda b, i, j: (0, 0)))
        args.append(final_gain)
    return pl.pallas_call(
        functools.partial(_mlp_kernel, final=final_gain is not None),
        grid=(bsz, seq // tm, dff // tf),
        in_specs=in_specs,
        out_specs=pl.BlockSpec((1, tm, d), lambda b, i, j: (b, i, 0)),
        out_shape=jax.ShapeDtypeStruct((bsz, seq, d), F32),
        scratch_shapes=[pltpu.VMEM((tm, d), BF16)],
        compiler_params=_params(("arbitrary", "arbitrary", "arbitrary")),
        name="sqrelu_mlp",
    )(*args)


def _rk_proj_kernel(x_ref, xp_ref, xn_ref, g_ref, sc_ref, sh_ref, mix_ref, wb_ref, ws_ref, ob_ref, os_ref,
                    u_ref, xx_ref, lhs_ref, *, nbig, nsmall):
    i = pl.program_id(1)
    j = pl.program_id(2)
    tm = x_ref.shape[1]
    rb = 128

    def build(m, slot):
        mixrow = mix_ref[pl.ds(m, 1), :]
        for r0 in range(0, tm, rb):
            lhs_ref[slot, r0:r0 + rb] = _bf(u_ref[8 + r0:8 + r0 + rb] + xx_ref[r0:r0 + rb] * mixrow)

    @pl.when(j == 0)
    def _():
        g, sc, sh = g_ref[...], sc_ref[0], sh_ref[0]
        for r0 in range(0, tm, rb):
            u_ref[8 + r0:8 + r0 + rb] = _normmod(x_ref[0, r0:r0 + rb], g, sc, sh)
        up = _normmod(xp_ref[0], g, sc, sh)[7:8]
        un = _normmod(xn_ref[0], g, sc, sh)[0:1]
        u_ref[7:8] = jnp.where(i == 0, 0.0, up)
        u_ref[8 + tm:9 + tm] = jnp.where(i == pl.num_programs(1) - 1, 0.0, un)
        for r0 in range(0, tm, rb):
            nb = u_ref[7 + r0:7 + r0 + rb] + u_ref[9 + r0:9 + r0 + rb]
            xx_ref[r0:r0 + rb] = 0.5 * nb - u_ref[8 + r0:8 + r0 + rb]
        build(0, 0)

    @pl.when(j < nbig)
    def _():
        build(j + 1, (j + 1) % 2)
        ob_ref[0, 0] = _dot(lhs_ref[j % 2], wb_ref[0])

    @pl.when(j == nbig)
    def _():
        for s in range(nsmall):
            os_ref[s, 0] = _dot(lhs_ref[(nbig + s) % 2], ws_ref[s])
            if s + 1 < nsmall:
                build(nbig + s + 1, (nbig + s + 1) % 2)


def _rk_proj(x, g, sc, sh, mix, w_big, w_small, tm):
    bsz, seq, d = x.shape
    nbig, _, n = w_big.shape
    nsmall, _, ns = w_small.shape
    r8 = tm // 8
    last8 = seq // 8 - 1
    vec = pl.BlockSpec((1, 1, d), lambda b, i, j: (b, 0, 0))
    return pl.pallas_call(
        functools.partial(_rk_proj_kernel, nbig=nbig, nsmall=nsmall),
        grid=(bsz, seq // tm, nbig + 1),
        in_specs=[
            pl.BlockSpec((1, tm, d), lambda b, i, j: (b, i, 0)),
            pl.BlockSpec((1, 8, d), lambda b, i, j: (b, jnp.maximum(i * r8 - 1, 0), 0)),
            pl.BlockSpec((1, 8, d), lambda b, i, j: (b, jnp.minimum((i + 1) * r8, last8), 0)),
            pl.BlockSpec((1, d), lambda b, i, j: (0, 0)),
            vec, vec,
            pl.BlockSpec((nbig + nsmall, d), lambda b, i, j: (0, 0)),
            pl.BlockSpec((1, d, n), lambda b, i, j: (jnp.minimum(j, nbig - 1), 0, 0)),
            pl.BlockSpec((nsmall, d, ns), lambda b, i, j: (0, 0, 0)),
        ],
        out_specs=[
            pl.BlockSpec((1, 1, tm, n), lambda b, i, j: (jnp.minimum(j, nbig - 1), b, i, 0)),
            pl.BlockSpec((nsmall, 1, tm, ns), lambda b, i, j: (0, b, i, 0)),
        ],
        out_shape=[
            jax.ShapeDtypeStruct((nbig, bsz, seq, n), F32),
            jax.ShapeDtypeStruct((nsmall, bsz, seq, ns), F32),
        ],
        scratch_shapes=[pltpu.VMEM((tm + 16, d), F32), pltpu.VMEM((tm, d), F32), pltpu.VMEM((2, tm, d), BF16)],
        compiler_params=_params(("arbitrary", "arbitrary", "arbitrary")),
        name="rwkv_proj",
    )(x, x, x, g, sc, sh, mix, w_big, w_small)


def _softplus(x):
    return jnp.maximum(x, 0.0) + jnp.log(1.0 + jnp.exp(-jnp.abs(x)))


def _pair_consts(rev):
    n = 2 * CHUNK
    ri = lax.broadcasted_iota(jnp.int32, (n, n), 0)
    ci = lax.broadcasted_iota(jnp.int32, (n, n), 1)
    same = (ri // CHUNK) == (ci // CHUNK)
    rt, ct = ri % CHUNK, ci % CHUNK
    strict = same & ((ct > rt) if rev else (ct < rt))
    incl = same & ((ct >= rt) if rev else (ct <= rt))
    top = ri < CHUNK
    eye = jnp.where(ri == ci, 1.0, 0.0).astype(F32)
    lane = lax.broadcasted_iota(jnp.int32, (1, LANES), 1)
    lane_lo = lane < C_HEAD
    m0 = jnp.where(lane_lo, 1.0, 0.0).astype(F32)
    m1 = 1.0 - m0
    return dict(strict=strict, incl=incl, top=top, eye=eye, m0=m0, m1=m1, lane_lo=lane_lo, ri=ri, ci=ci)


def _head_sums(x, cs):
    s0 = jnp.sum(x * cs["m0"], axis=-1, keepdims=True)
    s1 = jnp.sum(x * cs["m1"], axis=-1, keepdims=True)
    return jnp.where(cs["lane_lo"], s0, s1)


def _unit_tri_inverse(n, cs):
    ri, ci = cs["ri"], cs["ci"]
    blk2 = (ri // 2) == (ci // 2)
    t = [cs["eye"] + jnp.where(blk2, x, 0.0) for x in n]
    m = 2
    while m < CHUNK:
        sel = ((ri // (2 * m)) == (ci // (2 * m))) & ((ri // m) != (ci // m))
        off = [_bf(jnp.where(sel, x, 0.0)) for x in n]
        tb = [_bf(x) for x in t]
        q = [_bf(_dot(a, b)) for a, b in zip(tb, off)]
        q = [_dot(a, b) for a, b in zip(q, tb)]
        t = [a + b for a, b in zip(t, q)]
        m *= 2
    return t


def _stack2(x, cs):
    return jnp.concatenate([x * cs["m0"], x * cs["m1"]], axis=0)


def _rwkv_chunk(r, k, v, lw, la, w2, a2, w0, a0, kkv, kav, ht, cs, rev):
    pairs = range(len(r))
    tl, lab = _bf(jnp.tanh(lw)), _bf(la)
    z = [w0[p] + _dot(tl, w2[p]) for p in pairs]
    za = [_dot(lab, a2[p]) for p in pairs]
    kkr = [k[p] * kkv[p] for p in pairs]
    ss = [_head_sums(x * x, cs) for x in kkr]
    ld = [-jnp.exp(-_softplus(-x) - 0.5) for x in z]
    cl = [_cumsum_rows(x, rev) for x in ld]
    a = [_sigmoid(a0[p] + za[p]) for p in pairs]
    kk = [x / jnp.maximum(jnp.sqrt(s), 1e-12) for x, s in zip(kkr, ss)]
    kd = [k[p] * (1.0 + (a[p] - 1.0) * kav[p]) for p in pairs]
    bb = [x * y for x, y in zip(kk, a)]
    ctot = [x[0:1] if rev else x[CHUNK - 1:CHUNK] for x in cl]
    e_neg = [jnp.exp(-x) for x in cl]
    e_tail = [jnp.exp(c - x) for c, x in zip(ctot, cl)]
    at = [_bf(_stack2(-kk[p] * jnp.exp(cl[p] - ld[p]), cs)) for p in pairs]
    rt = [_bf(_stack2(r[p] * jnp.exp(cl[p]), cs)) for p in pairs]
    rhs = [_bf(jnp.concatenate([bb[p] * e_neg[p], kd[p] * e_neg[p]], axis=0)) for p in pairs]
    g = [_dot_nt(jnp.concatenate([x, y], axis=0), w) for x, y, w in zip(at, rt, rhs)]
    g1 = [x[:2 * CHUNK] for x in g]
    g2 = [x[2 * CHUNK:] for x in g]
    top, strict, incl = cs["top"], cs["strict"], cs["incl"]
    g1r = [pltpu.roll(x, C_HEAD, 1) for x in g1]
    n_ab = [jnp.where(strict, jnp.where(top, x, y), 0.0) for x, y in zip(g1, g1r)]
    n_ak = [_bf(jnp.where(strict, jnp.where(top, y, x), 0.0)) for x, y in zip(g1, g1r)]
    t = _unit_tri_inverse(n_ab, cs)
    vs = [_bf(_stack2(x, cs)) for x in v]
    hkv = [_bf(x.T) for x in ht]
    xx = [_bf(_dot(jnp.concatenate([at[p], n_ak[p]], axis=1), jnp.concatenate([hkv[p], vs[p]], axis=0)))
          for p in pairs]
    u =[_bf(_dot(_bf(a_), b_)) for a_, b_ in zip(t, xx)]
    bk = [_bf(jnp.concatenate([_stack2(bb[p] * e_tail[p], cs), _stack2(kd[p] * e_tail[p], cs)], axis=0))
          for p in pairs]
    uv = [jnp.concatenate([a_, b_], axis=0) for a_, b_ in zip(u, vs)]
    dh = [_dot_tn(a_, b_) for a_, b_ in zip(uv, bk)]
    ht_new = [ht[p] * jnp.exp(ctot[p]) + dh[p] for p in pairs]
    g2r = [pltpu.roll(x, C_HEAD, 1) for x in g2]
    n_rb = [_bf(jnp.where(incl, jnp.where(top, x, y), 0.0)) for x, y in zip(g2, g2r)]
    n_rk = [_bf(jnp.where(incl, jnp.where(top, y, x), 0.0)) for x, y in zip(g2, g2r)]
    ysum = [_dot(jnp.concatenate([rt[p], n_rb[p], n_rk[p]], axis=1),
                 jnp.concatenate([hkv[p], u[p], vs[p]], axis=0)) for p in pairs]
    y = [x[:CHUNK] + x[CHUNK:] for x in ysum]
    return y, ht_new, kd


def _rwkv_combine(y, yf, r, k, v, kd, laf, a2f, a0f, kav, rkv, lng, lnb, cs):
    pairs = range(len(y))
    inv_n = 1.0 / C_HEAD
    lafb = _bf(laf)
    zf = [_dot(lafb, a2f[p]) for p in pairs]
    ysum = [a + b for a, b in zip(y, yf)]
    mu = [_head_sums(x, cs) * inv_n for x in ysum]
    dev = [a - b for a, b in zip(ysum, mu)]
    var = [_head_sums(x * x, cs) * inv_n for x in dev]
    a_f = [_sigmoid(a0f[p] + zf[p]) for p in pairs]
    kd_f = [k[p] * (1.0 + (a_f[p] - 1.0) * kav[p]) for p in pairs]
    bsum = [_head_sums(r[p] * (kd_f[p] + kd[p]) * rkv[p], cs) for p in pairs]
    return [dev[p] * lax.rsqrt(var[p] + GN_EPS) * lng[p] + lnb[p] + bsum[p] * v[p] for p in pairs]


def _rwkv_scan_kernel(*refs, rev, nchunk, npair, combine):
    if combine:
        (r_ref, k_ref, v_ref, lw_ref, la_ref, w2_ref, a2_ref, w0_ref, a0_ref, kk_ref, ka_ref, s0_ref,
         yf_ref, laf_ref, a2f_ref, a0f_ref, rk_ref, lng_ref, lnb_ref, o_ref, sfin_ref, ht_ref) = refs
    else:
        (r_ref, k_ref, v_ref, lw_ref, la_ref, w2_ref, a2_ref, w0_ref, a0_ref, kk_ref, ka_ref, s0_ref,
         o_ref, sfin_ref, ht_ref) = refs
    c = pl.program_id(2)

    @pl.when(c == 0)
    def _():
        ht_ref[...] = s0_ref[0]

    cs = _pair_consts(rev)

    def body(ci, carry):
        cc = (nchunk - 1 - ci) if rev else ci
        rows = pl.ds(pl.multiple_of(cc * CHUNK, CHUNK), CHUNK)
        lw = lw_ref[0, 0, rows, :]
        la = la_ref[0, 0, rows, :]
        lanes = [slice(pr * LANES, (pr + 1) * LANES) for pr in range(npair)]
        r = [r_ref[0, 0, rows, ln] for ln in lanes]
        k = [k_ref[0, 0, rows, ln] for ln in lanes]
        v = [v_ref[0, 0, rows, ln] for ln in lanes]
        kav = [ka_ref[:, ln] for ln in lanes]
        y, ht_new, kd = _rwkv_chunk(
            r, k, v, lw, la, [w2_ref[0, :, ln] for ln in lanes], [a2_ref[0, :, ln] for ln in lanes],
            [w0_ref[0, :, ln] for ln in lanes], [a0_ref[0, :, ln] for ln in lanes],
            [kk_ref[:, ln] for ln in lanes], kav, [ht_ref[pr] for pr in range(npair)], cs, rev)
        for pr in range(npair):
            ht_ref[pr] = ht_new[pr]
        if combine:
            y = _rwkv_combine(
                y, [yf_ref[0, rows, ln] for ln in lanes], r, k, v, kd, laf_ref[0, 0, rows, :],
                [a2f_ref[0, :, ln] for ln in lanes], [a0f_ref[0, :, ln] for ln in lanes], kav,
                [rk_ref[:, ln] for ln in lanes], [lng_ref[:, ln] for ln in lanes],
                [lnb_ref[:, ln] for ln in lanes], cs)
        for pr in range(npair):
            o_ref[0, rows, lanes[pr]] = y[pr]
        return carry

    lax.fori_loop(0, nchunk, body, 0)

    @pl.when(c == pl.num_programs(2) - 1)
    def _():
        sfin_ref[0] = ht_ref[...]


def _rwkv_scan(rkv, small, prm, s0, rev, tblk, npair, y_fwd=None):
    _, bsz, seq, d = rkv.shape
    nblk = seq // tblk
    width = npair * LANES
    e = 1 if rev else 0
    blk = (lambda c: nblk - 1 - c) if rev else (lambda c: c)
    combine = y_fwd is not None

    def tok(m):
        return pl.BlockSpec((1, 1, tblk, width), lambda b, p, c: (m, b, blk(c), p))

    def lora(m, half):
        return pl.BlockSpec((1, 1, tblk, LANES), lambda b, p, c: (m, b, blk(c), half))

    def mat(idx):
        return pl.BlockSpec((1, LANES, width), lambda b, p, c: (idx, 0, p))

    def vec3(idx):
        return pl.BlockSpec((1, 1, width), lambda b, p, c: (idx, 0, p))

    vec = pl.BlockSpec((1, width), lambda b, p, c: (0, p))
    state = pl.BlockSpec((1, npair, LANES, LANES), lambda b, p, c: (b, p, 0, 0))
    out_tok = pl.BlockSpec((1, tblk, width), lambda b, p, c: (b, blk(c), p))
    in_specs = [tok(0), tok(1), tok(2), lora(1, e), lora(2, e), mat(e), mat(e), vec3(e), vec3(e), vec, vec, state]
    args = [rkv, rkv, rkv, small, small, prm["w2"], prm["a2"], prm["w0"], prm["a0"], prm["k_k"], prm["k_a"], s0]
    if combine:
        in_specs += [out_tok, lora(2, 0), mat(0), vec3(0), vec, vec, vec]
        args += [y_fwd, small, prm["a2"], prm["a0"], prm["r_k"], prm["ln_g"], prm["ln_b"]]
    return pl.pallas_call(
        functools.partial(_rwkv_scan_kernel, rev=rev, nchunk=tblk // CHUNK, npair=npair, combine=combine),
        grid=(bsz, d // width, nblk),
        in_specs=in_specs,
        out_specs=[out_tok, state],
        out_shape=[
            jax.ShapeDtypeStruct((bsz, seq, d), F32),
            jax.ShapeDtypeStruct((bsz, d // LANES, LANES, LANES), F32),
        ],
        scratch_shapes=[pltpu.VMEM((npair, LANES, LANES), F32)],
        compiler_params=_params(("arbitrary", "arbitrary", "arbitrary")),
        name="rwkv7_bwd" if rev else "rwkv7_fwd",
    )(*args)


def _rk_out_kernel(z_ref, gs_ref, g2_ref, wo_ref, h_ref, gt_ref, o_ref):
    gate = _dot(_bf(_sigmoid(gs_ref[0, 0])), g2_ref[...])
    y = _dot(_bf(z_ref[0] * gate), wo_ref[...])
    o_ref[0] = h_ref[0] + gt_ref[0] * y


def _rk_out(z, small, g2, wo, h, gt, tm):
    bsz, seq, d = h.shape
    row = lambda b, i: (b, i, 0)
    glora = g2.shape[0]
    return pl.pallas_call(
        _rk_out_kernel,
        grid=(bsz, seq // tm),
        in_specs=[
            pl.BlockSpec((1, tm, d), row),
            pl.BlockSpec((1, 1, tm, glora), lambda b, i: (0, b, i, 0)),
            pl.BlockSpec((glora, d), lambda b, i: (0, 0)),
            pl.BlockSpec((d, d), lambda b, i: (0, 0)),
            pl.BlockSpec((1, tm, d), row),
            pl.BlockSpec((1, 1, d), lambda b, i: (b, 0, 0)),
        ],
        out_specs=pl.BlockSpec((1, tm, d), row),
        out_shape=jax.ShapeDtypeStruct((bsz, seq, d), F32),
        compiler_params=_params(("arbitrary", "arbitrary")),
        name="rwkv_outproj",
    )(z, small, g2, wo, h, gt)


def _rope_tables(seq):
    t = jnp.arange(seq, dtype=jnp.int32)
    rows = (t // GRID_W).astype(F32)
    cols = (t % GRID_W).astype(F32)
    half = HEAD_DIM // 2
    n_freq = half // 2
    inv = ROPE_BASE ** (-jnp.arange(n_freq, dtype=F32) / n_freq)
    lane = jnp.arange(HEAD_DIM)
    pos = jnp.where((lane < half)[None, :], rows[:, None], cols[:, None])
    ang = pos * inv[lane % n_freq][None, :]
    cos, sin = jnp.cos(ang), jnp.sin(ang)
    first = ((lane % half) < n_freq)[None, :]
    return cos, jnp.where(first, -sin, 0.0), jnp.where(first, 0.0, sin)


def _pad_lanes(w, axis):
    pad = [(0, 0)] * w.ndim
    pad[axis] = (0, LANES - w.shape[axis])
    return jnp.pad(w, pad)


def _row_tile(seq, want):
    return want if seq % want == 0 else seq


def kernel(x, c, ctx, c_ctx, mod_w, mod_b, norm_mix, norm_ffn, ffn_up, ffn_down, ab_w_in, ab_w_out, attn_sink, hgrn_lb, hgrn_onorm, rk_mix, rk_wr, rk_wk, rk_wv, rk_wo, rk_w0, rk_w1, rk_w2, rk_a0, rk_a1, rk_a2, rk_g1, rk_g2, rk_kk, rk_ka, rk_rk, rk_ln_g, rk_ln_b, final_norm):
    bsz, seq, d = x.shape
    lc = ctx.shape[1]
    depth = mod_w.shape[0]
    assert bsz + 1 <= 8 and seq % 1024 == 0 and lc % CHUNK == 0

    cond8 = jnp.zeros((8, d), F32).at[:bsz].set(c).at[bsz].set(c_ctx)
    mod = _modulation(cond8, mod_w, mod_b)
    lb_all = jnp.cumsum(jax.nn.softmax(hgrn_lb.astype(F32), axis=0), axis=0)
    rope = _rope_tables(seq)

    h, hc = x, ctx
    for layer in range(depth):
        last = layer == depth - 1
        jl = layer // 2
        m_lat = mod[layer, :bsz].reshape(bsz, 1, 6, d)
        m_ctx = jnp.broadcast_to(mod[layer, bsz].reshape(1, 1, 6, d), (bsz, 1, 6, d))
        sh1, sc1, gt1, sh2, sc2, gt2 = (m_lat[:, :, i] for i in range(6))
        csh1, csc1, cgt1, csh2, csc2, cgt2 = (m_ctx[:, :, i] for i in range(6))
        g_mix = norm_mix[layer].reshape(1, d)
        g_ffn = norm_ffn[layer].reshape(1, d)
        if layer % 2 == 0:
            w_in = _bf(ab_w_in[jl])
            w_out = _bf(ab_w_out[jl])
            sink = attn_sink[jl].astype(F32)
            lb = lb_all[jl].reshape(1, B_W)
            onorm = hgrn_onorm[jl].reshape(1, B_W).astype(F32)
            n_in = w_in.shape[1]
            pc = _inproj(hc, g_mix, csc1, csh1, w_in, None, lc, 512)
            pl_ = _inproj(h, g_mix, sc1, sh1, w_in, rope, 1024, 512)
            oa = _win_attn(sink, pl_, pc)
            zeros = jnp.zeros((bsz, B_HEADS, HEAD_DIM, HEAD_DIM), F32)
            nhead = 4
            ocf, scf = _hgrn_scan(pc, lb, zeros, False, lc, nhead)
            ocb, scb = _hgrn_scan(pc, lb, zeros, True, lc, nhead)
            olf, _ = _hgrn_scan(pl_, lb, scf, False, 512, nhead)
            olb, _ = _hgrn_scan(pl_, lb, scb, True, 512, nhead)
            h = _ab_out(oa, olf, olb, pl_, onorm, w_out, h, gt1, 512)
            if not last:
                oca = _ctx_attn(sink, pc)
                hc = _ab_out(oca, ocf, ocb, pc, onorm, w_out, hc, cgt1, lc)
        else:
            w_big = _bf(jnp.stack([rk_wr[jl], rk_wk[jl], rk_wv[jl]]))
            w_small = _bf(jnp.stack([
                rk_g1[jl],
                jnp.concatenate([_pad_lanes(rk_w1[jl, 0], 1), _pad_lanes(rk_w1[jl, 1], 1)], axis=1),
                jnp.concatenate([_pad_lanes(rk_a1[jl, 0], 1), _pad_lanes(rk_a1[jl, 1], 1)], axis=1),
            ]))
            assert w_small.shape[-1] == 2 * LANES
            mix = rk_mix[jl]
            prm = dict(
                w2=_bf(_pad_lanes(rk_w2[jl], 1)), a2=_bf(_pad_lanes(rk_a2[jl], 1)),
                w0=rk_w0[jl].reshape(2, 1, d), a0=rk_a0[jl].reshape(2, 1, d),
                k_k=rk_kk[jl].reshape(1, d), k_a=rk_ka[jl].reshape(1, d), r_k=rk_rk[jl].reshape(1, d),
                ln_g=rk_ln_g[jl].reshape(1, d), ln_b=rk_ln_b[jl].reshape(1, d),
            )
            g2 = _bf(rk_g2[jl])
            wo = _bf(rk_wo[jl])
            mix = mix[jnp.array([0, 2, 3, 5, 1, 4])]
            rkv_c, sm_c = _rk_proj(hc, g_mix, csc1, csh1, mix, w_big, w_small, lc)
            rkv_l, sm_l = _rk_proj(h, g_mix, sc1, sh1, mix, w_big, w_small, 512)
            zeros = jnp.zeros((bsz, d // LANES, LANES, LANES), F32)
            npair = 16
            ycf, s_f = _rwkv_scan(rkv_c, sm_c, prm, zeros, False, lc, npair)
            zc, s_b = _rwkv_scan(rkv_c, sm_c, prm, zeros, True, lc, npair, y_fwd=ycf)
            ylf, _ = _rwkv_scan(rkv_l, sm_l, prm, s_f, False, 256, npair)
            zl, _ = _rwkv_scan(rkv_l, sm_l, prm, s_b, True, 256, npair, y_fwd=ylf)
            h = _rk_out(zl, sm_l, g2, wo, h, gt1, 512)
            if not last:
                hc = _rk_out(zc, sm_c, g2, wo, hc, cgt1, lc)
        w_up = _bf(ffn_up[layer])
        w_dn = _bf(ffn_down[layer])
        h = _mlp(h, g_ffn, sc2, sh2, gt2, w_up, w_dn, final_norm.reshape(1, d) if last else None, 1024, 512)
        if not last:
            hc = _mlp(hc, g_ffn, csc2, csh2, cgt2, w_up, w_dn, None, lc, 512)
    return h
```

```python
import functools

import jax
import jax.numpy as jnp
import numpy as np
from jax import lax
from jax.experimental import pallas as pl
from jax.experimental.pallas import tpu as pltpu

F32 = jnp.float32
BF16 = jnp.bfloat16
HIGHEST = lax.Precision.HIGHEST

LANES = 128
HEAD_DIM = 128
GRID_W = 64
WINDOW = 128
ROPE_BASE = 10000.0
A_Q_HEADS = 8
A_KV_HEADS = 2
A_GROUP = A_Q_HEADS // A_KV_HEADS
A_Q = A_Q_HEADS * HEAD_DIM
A_KV = A_KV_HEADS * HEAD_DIM
B_HEADS = 8
B_W = B_HEADS * HEAD_DIM
C_HEAD = 64
CHUNK = 64
SUB = 16
EPS = 1e-6
GN_EPS = 64e-5
VMEM_LIMIT = 58 * 1024 * 1024

NT_DIMS = (((1,), (1,)), ((), ()))
TN_DIMS = (((0,), (0,)), ((), ()))


def _dot(a, b, **kw):
    return jnp.dot(a, b, preferred_element_type=F32, **kw)


def _dot_nt(a, b):
    return lax.dot_general(a, b, NT_DIMS, preferred_element_type=F32)


def _dot_tn(a, b):
    return lax.dot_general(a, b, TN_DIMS, preferred_element_type=F32)


def _bf(x):
    return x.astype(BF16)


def _dot_split(x, m):
    hi = _bf(x)
    lo = _bf(x - hi.astype(F32))
    return _dot(hi, m) + _dot(lo, m)


def _sigmoid(x):
    return 1.0 / (1.0 + jnp.exp(-x))


def _silu(x):
    return x * _sigmoid(x)


def _normmod(x, g, sc, sh):
    ms = jnp.mean(x * x, axis=-1, keepdims=True)
    y = (x * lax.rsqrt(ms + EPS)) * g
    return y * (1.0 + sc) + sh


ROW_PIECE = 128


def _normmod_rows(x_ref, g_ref, sc_ref, sh_ref, u_ref):
    g, sc, sh = g_ref[...], sc_ref[0], sh_ref[0]
    rows = u_ref.shape[0]
    piece = min(ROW_PIECE, rows)
    for r0 in range(0, rows, piece):
        u_ref[r0:r0 + piece] = _bf(_normmod(x_ref[0, r0:r0 + piece], g, sc, sh))


def _cumsum_rows(x, rev):
    n = x.shape[0]
    row = lax.broadcasted_iota(jnp.int32, (n, 1), 0)
    s = 1
    while s < n:
        if rev:
            x = x + jnp.where(row < n - s, pltpu.roll(x, n - s, 0), 0.0)
        else:
            x = x + jnp.where(row >= s, pltpu.roll(x, s, 0), 0.0)
        s *= 2
    return x


def _tile_major(w, tn):
    k, n = w.shape
    return w.reshape(k, n // tn, tn).transpose(1, 0, 2)


def _params(sem):
    return pltpu.CompilerParams(dimension_semantics=sem, vmem_limit_bytes=VMEM_LIMIT)


def _mod_kernel(c_ref, w_ref, b_ref, o_ref):
    s = _bf(_silu(c_ref[...]))
    o_ref[0] = _dot(s, _bf(w_ref[0])) + b_ref[0]


def _modulation(cond8, mod_w, mod_b):
    depth, d, n = mod_w.shape
    tn = 1024
    return pl.pallas_call(
        _mod_kernel,
        grid=(depth, n // tn),
        in_specs=[
            pl.BlockSpec((8, d), lambda l, j: (0, 0)),
            pl.BlockSpec((1, d, tn), lambda l, j: (l, 0, j)),
            pl.BlockSpec((1, 1, tn), lambda l, j: (l, 0, j)),
        ],
        out_specs=pl.BlockSpec((1, 8, tn), lambda l, j: (l, 0, j)),
        out_shape=jax.ShapeDtypeStruct((depth, 8, n), F32),
        compiler_params=_params(("arbitrary", "arbitrary")),
        name="modulation",
    )(cond8, mod_w, mod_b.reshape(depth, 1, n))


def _inproj_kernel(*refs, n_rope):
    if n_rope:
        x_ref, g_ref, sc_ref, sh_ref, w_ref, cos_ref, sna_ref, snb_ref, o_ref, u_ref = refs
    else:
        x_ref, g_ref, sc_ref, sh_ref, w_ref, o_ref, u_ref = refs
    j = pl.program_id(2)

    @pl.when(j == 0)
    def _():
        _normmod_rows(x_ref, g_ref, sc_ref, sh_ref, u_ref)

    acc = _dot(u_ref[...], w_ref[0])
    if not n_rope:
        o_ref[0] = acc
        return

    per_tile = acc.shape[1] // HEAD_DIM
    n_full, n_rem = n_rope // per_tile, n_rope % per_tile

    def store(n_rot):
        cos, sna, snb = cos_ref[...], sna_ref[...], snb_ref[...]
        for hd in range(n_rot):
            sl = acc[:, hd * HEAD_DIM:(hd + 1) * HEAD_DIM]
            rot = sl * cos + pltpu.roll(sl, 96, 1) * sna + pltpu.roll(sl, 32, 1) * snb
            o_ref[0, :, hd * HEAD_DIM:(hd + 1) * HEAD_DIM] = rot
        if n_rot < per_tile:
            o_ref[0, :, n_rot * HEAD_DIM:] = acc[:, n_rot * HEAD_DIM:]

    pl.when(j < n_full)(lambda: store(per_tile))
    pl.when(j == n_full)(lambda: store(n_rem))
    pl.when(j > n_full)(lambda: store(0))


def _inproj(x, g, sc, sh, w, rope, tm, tn):
    bsz, seq, d = x.shape
    n = w.shape[1]
    n_rope = 0
    in_specs = [
        pl.BlockSpec((1, tm, d), lambda b, i, j: (b, i, 0)),
        pl.BlockSpec((1, d), lambda b, i, j: (0, 0)),
        pl.BlockSpec((1, 1, d), lambda b, i, j: (b, 0, 0)),
        pl.BlockSpec((1, 1, d), lambda b, i, j: (b, 0, 0)),
        pl.BlockSpec((1, d, tn), lambda b, i, j: (j, 0, 0)),
    ]
    args = [x, g, sc, sh, _tile_major(w, tn)]
    if rope is not None:
        n_rope = A_Q_HEADS + A_KV_HEADS
        assert tn % HEAD_DIM == 0
        in_specs += [pl.BlockSpec((tm, HEAD_DIM), lambda b, i, j: (i, 0))] * 3
        args += list(rope)
    return pl.pallas_call(
        functools.partial(_inproj_kernel, n_rope=n_rope),
        grid=(bsz, seq // tm, n // tn),
        in_specs=in_specs,
        out_specs=pl.BlockSpec((1, tm, tn), lambda b, i, j: (b, i, j)),
        out_shape=jax.ShapeDtypeStruct((bsz, seq, n), F32),
        scratch_shapes=[pltpu.VMEM((tm, d), BF16)],
        compiler_params=_params(("arbitrary", "arbitrary", "arbitrary")),
        name="ab_inproj",
    )(*args)


def _softmax_av(s_list, v_list, sink_col):
    m = sink_col
    for s in s_list:
        m = jnp.maximum(m, jnp.max(s, axis=-1, keepdims=True))
    den = jnp.exp(sink_col - m)
    out = None
    for s, v in zip(s_list, v_list):
        p = jnp.exp(s - m)
        den = den + jnp.sum(p, axis=-1, keepdims=True)
        o = _dot(_bf(p), v)
        out = o if out is None else out + o
    return out / den


def _sink_column(sink_ref, hk, rows):
    rowh = lax.broadcasted_iota(jnp.int32, (rows, 1), 0) // WINDOW
    col = jnp.full((rows, 1), sink_ref[hk * A_GROUP + A_GROUP - 1], F32)
    for g in range(A_GROUP - 1):
        col = jnp.where(rowh == g, sink_ref[hk * A_GROUP + g], col)
    return col


def _win_attn_kernel(sink_ref, q_ref, kp_ref, kc_ref, kn_ref, vp_ref, vc_ref, vn_ref,
                     ck_ref, cv_ref, o_ref):
    n = pl.program_id(1)
    nb = pl.num_programs(1)
    scale = HEAD_DIM ** -0.5
    q = q_ref[0]
    kband = jnp.concatenate([kp_ref[0], kc_ref[0], kn_ref[0]], axis=0)
    vband = jnp.concatenate([vp_ref[0], vc_ref[0], vn_ref[0]], axis=0)
    rows = A_GROUP * WINDOW
    tq = lax.broadcasted_iota(jnp.int32, (rows, 3 * WINDOW), 0) % WINDOW
    tk = lax.broadcasted_iota(jnp.int32, (rows, 3 * WINDOW), 1)
    rel = tk - WINDOW - tq
    valid = (jnp.abs(rel) <= WINDOW) & ((tk >= WINDOW) | (n > 0)) & ((tk < 2 * WINDOW) | (n < nb - 1))
    for hk in range(A_KV_HEADS):
        qs = jnp.concatenate(
            [q[:, (hk * A_GROUP + g) * HEAD_DIM:(hk * A_GROUP + g + 1) * HEAD_DIM] for g in range(A_GROUP)],
            axis=0)
        qs = _bf(qs)
        hs = slice(hk * HEAD_DIM, (hk + 1) * HEAD_DIM)
        s_win = _dot_nt(qs, _bf(kband[:, hs])) * scale
        s_win = jnp.where(valid, s_win, -jnp.inf)
        s_ctx = _dot_nt(qs, _bf(ck_ref[0][:, hs])) * scale
        o = _softmax_av([s_win, s_ctx], [_bf(vband[:, hs]), _bf(cv_ref[0][:, hs])],
                        _sink_column(sink_ref, hk, rows))
        for g in range(A_GROUP):
            h = hk * A_GROUP + g
            o_ref[0, :, h * HEAD_DIM:(h + 1) * HEAD_DIM] = _bf(o[g * WINDOW:(g + 1) * WINDOW])


def _win_attn(sink, proj, proj_ctx):
    bsz, seq, _ = proj.shape
    lc = proj_ctx.shape[1]
    nb = seq // WINDOW
    kcol, vcol = A_Q // A_KV, A_Q // A_KV + 1
    prev = lambda b, n: (b, jnp.maximum(n - 1, 0))
    nxt = lambda b, n: (b, jnp.minimum(n + 1, nb - 1))
    cur = lambda b, n: (b, n)

    def band(rowfn, col):
        return pl.BlockSpec((1, WINDOW, A_KV), lambda b, n: rowfn(b, n) + (col,))

    return pl.pallas_call(
        _win_attn_kernel,
        grid=(bsz, nb),
        in_specs=[
            pl.BlockSpec(memory_space=pltpu.SMEM),
            pl.BlockSpec((1, WINDOW, A_Q), lambda b, n: (b, n, 0)),
            band(prev, kcol), band(cur, kcol), band(nxt, kcol),
            band(prev, vcol), band(cur, vcol), band(nxt, vcol),
            pl.BlockSpec((1, lc, A_KV), lambda b, n: (b, 0, kcol)),
            pl.BlockSpec((1, lc, A_KV), lambda b, n: (b, 0, vcol)),
        ],
        out_specs=pl.BlockSpec((1, WINDOW, A_Q), lambda b, n: (b, n, 0)),
        out_shape=jax.ShapeDtypeStruct((bsz, seq, A_Q), BF16),
        compiler_params=_params(("arbitrary", "arbitrary")),
        name="window_attention",
    )(sink, proj, proj, proj, proj, proj, proj, proj, proj_ctx, proj_ctx)


def _ctx_attn_kernel(sink_ref, q_ref, k_ref, v_ref, o_ref):
    scale = HEAD_DIM ** -0.5
    q = q_ref[0]
    lc = q.shape[0]
    for h in range(A_Q_HEADS):
        hk = h // A_GROUP
        hs = slice(hk * HEAD_DIM, (hk + 1) * HEAD_DIM)
        qs = _bf(q[:, h * HEAD_DIM:(h + 1) * HEAD_DIM])
        s = _dot_nt(qs, _bf(k_ref[0][:, hs])) * scale
        sink_col = jnp.full((lc, 1), sink_ref[h], F32)
        o = _softmax_av([s], [_bf(v_ref[0][:, hs])], sink_col)
        o_ref[0, :, h * HEAD_DIM:(h + 1) * HEAD_DIM] = _bf(o)


def _ctx_attn(sink, proj_ctx):
    bsz, lc, _ = proj_ctx.shape
    kcol, vcol = A_Q // A_KV, A_Q // A_KV + 1
    return pl.pallas_call(
        _ctx_attn_kernel,
        grid=(bsz,),
        in_specs=[
            pl.BlockSpec(memory_space=pltpu.SMEM),
            pl.BlockSpec((1, lc, A_Q), lambda b: (b, 0, 0)),
            pl.BlockSpec((1, lc, A_KV), lambda b: (b, 0, kcol)),
            pl.BlockSpec((1, lc, A_KV), lambda b: (b, 0, vcol)),
        ],
        out_specs=pl.BlockSpec((1, lc, A_Q), lambda b: (b, 0, 0)),
        out_shape=jax.ShapeDtypeStruct((bsz, lc, A_Q), BF16),
        compiler_params=_params(("arbitrary",)),
        name="context_attention",
    )(sink, proj_ctx, proj_ctx, proj_ctx)


def _hgrn_chunk(bq, bi, bf, lb, st, rev):
    heads = range(len(bq))
    q = [_silu(x) for x in bq]
    v = bi
    f = [lb[h] + (1.0 - lb[h]) * _sigmoid(bf[h]) for h in heads]
    k = [1.0 - x for x in f]
    g = [jnp.log(x) for x in f]
    b = [_cumsum_rows(x, rev) for x in g]
    btot = [x[0:1] if rev else x[CHUNK - 1:CHUNK] for x in b]
    vb = [_bf(x) for x in v]
    stb = [_bf(x) for x in st]
    o = [_dot_nt(_bf(q[h] * jnp.exp(b[h])), stb[h]) for h in heads]
    khat = [_bf(k[h] * jnp.exp(btot[h] - b[h])) for h in heads]
    dst = [_dot_tn(vb[h], khat[h]) for h in heads]
    st_new = [st[h] * jnp.exp(btot[h]) + dst[h] for h in heads]
    nsub = CHUNK // SUB
    row16 = lax.broadcasted_iota(jnp.int32, (SUB, 1), 0)
    outs = [[] for _ in heads]
    for blk in range(nsub):
        r0 = blk * SUB
        rs = slice(r0, r0 + SUB)
        acc = [o[h][rs] for h in heads]
        if rev and blk < nsub - 1:
            ref_row, lo, hi = r0 + SUB, r0 + SUB, CHUNK
        elif (not rev) and blk > 0:
            ref_row, lo, hi = r0 - 1, 0, r0
        else:
            ref_row = None
        if ref_row is not None:
            bref = [b[h][ref_row:ref_row + 1] for h in heads]
            qn = [_bf(q[h][rs] * jnp.exp(b[h][rs] - bref[h])) for h in heads]
            kn = [_bf(k[h][lo:hi] * jnp.exp(bref[h] - b[h][lo:hi])) for h in heads]
            att = [_bf(_dot_nt(qn[h], kn[h])) for h in heads]
            acc = [acc[h] + _dot(att[h], vb[h][lo:hi]) for h in heads]
        for s in range(SUB):
            mask = (row16 <= s) if rev else (row16 >= s)
            for h in heads:
                bi_ = b[h][rs]
                dec = jnp.exp(jnp.where(mask, bi_ - bi_[s:s + 1], -jnp.inf))
                w = jnp.sum(q[h][rs] * dec * k[h][r0 + s:r0 + s + 1], axis=-1, keepdims=True)
                acc[h] = acc[h] + w * v[h][r0 + s:r0 + s + 1]
        for h in heads:
            outs[h].append(acc[h])
    return [jnp.concatenate(x, axis=0) for x in outs], st_new


def _hgrn_kernel(q_ref, i_ref, f_ref, lb_ref, s0_ref, o_ref, sfin_ref, st_ref, *, rev, nchunk, nhead):
    c = pl.program_id(2)

    @pl.when(c == 0)
    def _():
        st_ref[...] = s0_ref[0]

    lanes = [slice(h * HEAD_DIM, (h + 1) * HEAD_DIM) for h in range(nhead)]
    lb = [lb_ref[:, ln] for ln in lanes]

    def body(ci, carry):
        cc = (nchunk - 1 - ci) if rev else ci
        rows = pl.ds(pl.multiple_of(cc * CHUNK, CHUNK), CHUNK)
        o, st_new = _hgrn_chunk([q_ref[0, rows, ln] for ln in lanes], [i_ref[0, rows, ln] for ln in lanes],
                                [f_ref[0, rows, ln] for ln in lanes], lb,
                                [st_ref[h] for h in range(nhead)], rev)
        for h in range(nhead):
            o_ref[0, rows, lanes[h]] = o[h]
            st_ref[h] = st_new[h]
        return carry

    lax.fori_loop(0, nchunk, body, 0)

    @pl.when(c == pl.num_programs(2) - 1)
    def _():
        sfin_ref[0] = st_ref[...]


def _hgrn_scan(proj, lb, s0, rev, tblk, nhead):
    bsz, seq, _ = proj.shape
    nblk = seq // tblk
    width = nhead * HEAD_DIM
    base = A_Q + 2 * A_KV
    assert base % width == 0 and B_W % width == 0
    qc, ic = base // width, (base + B_W) // width
    fc = (base + (3 if rev else 2) * B_W) // width
    blk = (lambda c: nblk - 1 - c) if rev else (lambda c: c)

    def col(c0):
        return pl.BlockSpec((1, tblk, width), lambda b, h, c: (b, blk(c), c0 + h))

    state = pl.BlockSpec((1, nhead, HEAD_DIM, HEAD_DIM), lambda b, h, c: (b, h, 0, 0))
    return pl.pallas_call(
        functools.partial(_hgrn_kernel, rev=rev, nchunk=tblk // CHUNK, nhead=nhead),
        grid=(bsz, B_HEADS // nhead, nblk),
        in_specs=[
            col(qc), col(ic), col(fc),
            pl.BlockSpec((1, width), lambda b, h, c: (0, h)),
            state,
        ],
        out_specs=[
            pl.BlockSpec((1, tblk, width), lambda b, h, c: (b, blk(c), h)),
            state,
        ],
        out_shape=[
            jax.ShapeDtypeStruct((bsz, seq, B_W), F32),
            jax.ShapeDtypeStruct((bsz, B_HEADS, HEAD_DIM, HEAD_DIM), F32),
        ],
        scratch_shapes=[pltpu.VMEM((nhead, HEAD_DIM, HEAD_DIM), F32)],
        compiler_params=_params(("arbitrary", "arbitrary", "arbitrary")),
        name="hgrn2_bwd" if rev else "hgrn2_fwd",
    )(proj, proj, proj, lb, s0)


def _ab_out_kernel(oa_ref, of_ref, ob_ref, g0_ref, g1_ref, on_ref, w_ref, h_ref, gt_ref, o_ref, lhs_ref):
    lhs_ref[:, :A_Q] = oa_ref[0]
    half = B_W // 2
    for hd in range(B_HEADS):
        sl = slice(hd * HEAD_DIM, (hd + 1) * HEAD_DIM)
        o = of_ref[0, :, sl] + ob_ref[0, :, sl]
        o = o * lax.rsqrt(jnp.mean(o * o, axis=-1, keepdims=True) + EPS)
        o = o * on_ref[:, sl]
        gref = g0_ref if hd * HEAD_DIM < half else g1_ref
        gs = slice(hd * HEAD_DIM % half, hd * HEAD_DIM % half + HEAD_DIM)
        o = o * _silu(gref[0, :, gs])
        lhs_ref[:, A_Q + hd * HEAD_DIM:A_Q + (hd + 1) * HEAD_DIM] = _bf(o)
    y = _dot(lhs_ref[...], w_ref[...])
    o_ref[0] = h_ref[0] + gt_ref[0] * y


def _ab_out(oa, of, ob, proj, onorm, w_out, h, gt, tm):
    bsz, seq, d = h.shape
    half = B_W // 2
    gcol = (A_Q + 2 * A_KV + 4 * B_W) // half
    row = lambda b, i: (b, i, 0)
    return pl.pallas_call(
        _ab_out_kernel,
        grid=(bsz, seq // tm),
        in_specs=[
            pl.BlockSpec((1, tm, A_Q), row),
            pl.BlockSpec((1, tm, B_W), row),
            pl.BlockSpec((1, tm, B_W), row),
            pl.BlockSpec((1, tm, half), lambda b, i: (b, i, gcol)),
            pl.BlockSpec((1, tm, half), lambda b, i: (b, i, gcol + 1)),
            pl.BlockSpec((1, B_W), lambda b, i: (0, 0)),
            pl.BlockSpec((A_Q + B_W, d), lambda b, i: (0, 0)),
            pl.BlockSpec((1, tm, d), row),
            pl.BlockSpec((1, 1, d), lambda b, i: (b, 0, 0)),
        ],
        out_specs=pl.BlockSpec((1, tm, d), row),
        out_shape=jax.ShapeDtypeStruct((bsz, seq, d), F32),
        scratch_shapes=[pltpu.VMEM((tm, A_Q + B_W), BF16)],
        compiler_params=_params(("arbitrary", "arbitrary")),
        name="ab_outproj",
    )(oa, of, ob, proj, proj, onorm, w_out, h, gt)


def _mlp_kernel(*refs, final):
    if final:
        x_ref, g_ref, sc_ref, sh_ref, gt_ref, wu_ref, wd_ref, fn_ref, o_ref, u_ref = refs
    else:
        x_ref, g_ref, sc_ref, sh_ref, gt_ref, wu_ref, wd_ref, o_ref, u_ref = refs
    j = pl.program_id(2)

    tm = x_ref.shape[1]

    @pl.when(j == 0)
    def _():
        _normmod_rows(x_ref, g_ref, sc_ref, sh_ref, u_ref)

    hid = jnp.maximum(_dot(u_ref[...], wu_ref[0]), 0.0)
    hid = _bf(hid * hid)
    ncol = 512

    @pl.when(j == 0)
    def _():
        for n0 in range(0, o_ref.shape[2], ncol):
            o_ref[0, :, n0:n0 + ncol] = _dot(hid, wd_ref[:, n0:n0 + ncol])

    @pl.when(j != 0)
    def _():
        for n0 in range(0, o_ref.shape[2], ncol):
            o_ref[0, :, n0:n0 + ncol] += _dot(hid, wd_ref[:, n0:n0 + ncol])

    @pl.when(j == pl.num_programs(2) - 1)
    def _():
        for r0 in range(0, tm, ROW_PIECE):
            rs = slice(r0, r0 + ROW_PIECE)
            y = x_ref[0, rs] + gt_ref[0] * o_ref[0, rs]
            if final:
                y = (y * lax.rsqrt(jnp.mean(y * y, axis=-1, keepdims=True) + EPS)) * fn_ref[...]
            o_ref[0, rs] = y


def _mlp(x, g, sc, sh, gt, w_up, w_down, final_gain, tm, tf):
    bsz, seq, d = x.shape
    dff = w_up.shape[1]
    vec = pl.BlockSpec((1, 1, d), lambda b, i, j: (b, 0, 0))
    in_specs = [
        pl.BlockSpec((1, tm, d), lambda b, i, j: (b, i, 0)),
        pl.BlockSpec((1, d), lambda b, i, j: (0, 0)),
        vec, vec, vec,
        pl.BlockSpec((1, d, tf), lambda b, i, j: (j, 0, 0)),
        pl.BlockSpec((tf, d), lambda b, i, j: (j, 0)),
    ]
    args = [x, g, sc, sh, gt, _tile_major(w_up, tf), w_down]
    if final_gain is not None:
        in_specs.append(pl.BlockSpec((1, d), lambda b, i, j: (0, 0)))
        args.append(final_gain)
    return pl.pallas_call(
        functools.partial(_mlp_kernel, final=final_gain is not None),
        grid=(bsz, seq // tm, dff // tf),
        in_specs=in_specs,
        out_specs=pl.BlockSpec((1, tm, d), lambda b, i, j: (b, i, 0)),
        out_shape=jax.ShapeDtypeStruct((bsz, seq, d), F32),
        scratch_shapes=[pltpu.VMEM((tm, d), BF16)],
        compiler_params=_params(("arbitrary", "arbitrary", "arbitrary")),
        name="sqrelu_mlp",
    )(*args)


def _rk_proj_kernel(x_ref, xp_ref, xn_ref, g_ref, sc_ref, sh_ref, mix_ref, wb_ref, ws_ref, ob_ref, os_ref,
                    u_ref, xx_ref, lhs_ref, *, nbig, nsmall):
    i = pl.program_id(1)
    j = pl.program_id(2)
    tm = x_ref.shape[1]
    rb = 128

    def build(m, slot):
        mixrow = mix_ref[pl.ds(m, 1), :]
        for r0 in range(0, tm, rb):
            lhs_ref[slot, r0:r0 + rb] = _bf(u_ref[8 + r0:8 + r0 + rb] + xx_ref[r0:r0 + rb] * mixrow)

    @pl.when(j == 0)
    def _():
        g, sc, sh = g_ref[...], sc_ref[0], sh_ref[0]
        for r0 in range(0, tm, rb):
            u_ref[8 + r0:8 + r0 + rb] = _normmod(x_ref[0, r0:r0 + rb], g, sc, sh)
        up = _normmod(xp_ref[0], g, sc, sh)[7:8]
        un = _normmod(xn_ref[0], g, sc, sh)[0:1]
        u_ref[7:8] = jnp.where(i == 0, 0.0, up)
        u_ref[8 + tm:9 + tm] = jnp.where(i == pl.num_programs(1) - 1, 0.0, un)
        for r0 in range(0, tm, rb):
            nb = u_ref[7 + r0:7 + r0 + rb] + u_ref[9 + r0:9 + r0 + rb]
            xx_ref[r0:r0 + rb] = 0.5 * nb - u_ref[8 + r0:8 + r0 + rb]
        build(0, 0)

    @pl.when(j < nbig)
    def _():
        build(j + 1, (j + 1) % 2)
        ob_ref[0, 0] = _dot(lhs_ref[j % 2], wb_ref[0])

    @pl.when(j == nbig)
    def _():
        for s in range(nsmall):
            os_ref[s, 0] = _dot(lhs_ref[(nbig + s) % 2], ws_ref[s])
            if s + 1 < nsmall:
                build(nbig + s + 1, (nbig + s + 1) % 2)


def _rk_proj(x, g, sc, sh, mix, w_big, w_small, tm):
    bsz, seq, d = x.shape
    nbig, _, n = w_big.shape
    nsmall, _, ns = w_small.shape
    r8 = tm // 8
    last8 = seq // 8 - 1
    vec = pl.BlockSpec((1, 1, d), lambda b, i, j: (b, 0, 0))
    return pl.pallas_call(
        functools.partial(_rk_proj_kernel, nbig=nbig, nsmall=nsmall),
        grid=(bsz, seq // tm, nbig + 1),
        in_specs=[
            pl.BlockSpec((1, tm, d), lambda b, i, j: (b, i, 0)),
            pl.BlockSpec((1, 8, d), lambda b, i, j: (b, jnp.maximum(i * r8 - 1, 0), 0)),
            pl.BlockSpec((1, 8, d), lambda b, i, j: (b, jnp.minimum((i + 1) * r8, last8), 0)),
            pl.BlockSpec((1, d), lambda b, i, j: (0, 0)),
            vec, vec,
            pl.BlockSpec((nbig + nsmall, d), lambda b, i, j: (0, 0)),
            pl.BlockSpec((1, d, n), lambda b, i, j: (jnp.minimum(j, nbig - 1), 0, 0)),
            pl.BlockSpec((nsmall, d, ns), lambda b, i, j: (0, 0, 0)),
        ],
        out_specs=[
            pl.BlockSpec((1, 1, tm, n), lambda b, i, j: (jnp.minimum(j, nbig - 1), b, i, 0)),
            pl.BlockSpec((nsmall, 1, tm, ns), lambda b, i, j: (0, b, i, 0)),
        ],
        out_shape=[
            jax.ShapeDtypeStruct((nbig, bsz, seq, n), F32),
            jax.ShapeDtypeStruct((nsmall, bsz, seq, ns), F32),
        ],
        scratch_shapes=[pltpu.VMEM((tm + 16, d), F32), pltpu.VMEM((tm, d), F32), pltpu.VMEM((2, tm, d), BF16)],
        compiler_params=_params(("arbitrary", "arbitrary", "arbitrary")),
        name="rwkv_proj",
    )(x, x, x, g, sc, sh, mix, w_big, w_small)


def _softplus(x):
    return jnp.maximum(x, 0.0) + jnp.log(1.0 + jnp.exp(-jnp.abs(x)))


def _pair_consts(rev):
    n = 2 * CHUNK
    ri = lax.broadcasted_iota(jnp.int32, (n, n), 0)
    ci = lax.broadcasted_iota(jnp.int32, (n, n), 1)
    same = (ri // CHUNK) == (ci // CHUNK)
    rt, ct = ri % CHUNK, ci % CHUNK
    strict = same & ((ct > rt) if rev else (ct < rt))
    incl = same & ((ct >= rt) if rev else (ct <= rt))
    top = ri < CHUNK
    eye = jnp.where(ri == ci, 1.0, 0.0).astype(F32)
    lane = lax.broadcasted_iota(jnp.int32, (1, LANES), 1)
    lane_lo = lane < C_HEAD
    m0 = jnp.where(lane_lo, 1.0, 0.0).astype(F32)
    m1 = 1.0 - m0
    return dict(strict=strict, incl=incl, top=top, eye=eye, m0=m0, m1=m1, lane_lo=lane_lo, ri=ri, ci=ci)


def _head_sums(x, cs):
    s0 = jnp.sum(x * cs["m0"], axis=-1, keepdims=True)
    s1 = jnp.sum(x * cs["m1"], axis=-1, keepdims=True)
    return jnp.where(cs["lane_lo"], s0, s1)


def _unit_tri_inverse(n, cs):
    ri, ci = cs["ri"], cs["ci"]
    blk2 = (ri // 2) == (ci // 2)
    t = [cs["eye"] + jnp.where(blk2, x, 0.0) for x in n]
    m = 2
    while m < CHUNK:
        sel = ((ri // (2 * m)) == (ci // (2 * m))) & ((ri // m) != (ci // m))
        off = [_bf(jnp.where(sel, x, 0.0)) for x in n]
        tb = [_bf(x) for x in t]
        q = [_bf(_dot(a, b)) for a, b in zip(tb, off)]
        q = [_dot(a, b) for a, b in zip(q, tb)]
        t = [a + b for a, b in zip(t, q)]
        m *= 2
    return t


def _stack2(x, cs):
    return jnp.concatenate([x * cs["m0"], x * cs["m1"]], axis=0)


def _rwkv_chunk(r, k, v, lw, la, w2, a2, w0, a0, kkv, kav, ht, cs, rev):
    pairs = range(len(r))
    tl, lab = _bf(jnp.tanh(lw)), _bf(la)
    z = [w0[p] + _dot(tl, w2[p]) for p in pairs]
    za = [_dot(lab, a2[p]) for p in pairs]
    kkr = [k[p] * kkv[p] for p in pairs]
    ss = [_head_sums(x * x, cs) for x in kkr]
    ld = [-jnp.exp(-_softplus(-x) - 0.5) for x in z]
    cl = [_cumsum_rows(x, rev) for x in ld]
    a = [_sigmoid(a0[p] + za[p]) for p in pairs]
    kk = [x / jnp.maximum(jnp.sqrt(s), 1e-12) for x, s in zip(kkr, ss)]
    kd = [k[p] * (1.0 + (a[p] - 1.0) * kav[p]) for p in pairs]
    bb = [x * y for x, y in zip(kk, a)]
    ctot = [x[0:1] if rev else x[CHUNK - 1:CHUNK] for x in cl]
    e_neg = [jnp.exp(-x) for x in cl]
    e_tail = [jnp.exp(c - x) for c, x in zip(ctot, cl)]
    at = [_bf(_stack2(-kk[p] * jnp.exp(cl[p] - ld[p]), cs)) for p in pairs]
    rt = [_bf(_stack2(r[p] * jnp.exp(cl[p]), cs)) for p in pairs]
    rhs = [_bf(jnp.concatenate([bb[p] * e_neg[p], kd[p] * e_neg[p]], axis=0)) for p in pairs]
    g = [_dot_nt(jnp.concatenate([x, y], axis=0), w) for x, y, w in zip(at, rt, rhs)]
    g1 = [x[:2 * CHUNK] for x in g]
    g2 = [x[2 * CHUNK:] for x in g]
    top, strict, incl = cs["top"], cs["strict"], cs["incl"]
    g1r = [pltpu.roll(x, C_HEAD, 1) for x in g1]
    n_ab = [jnp.where(strict, jnp.where(top, x, y), 0.0) for x, y in zip(g1, g1r)]
    n_ak = [_bf(jnp.where(strict, jnp.where(top, y, x), 0.0)) for x, y in zip(g1, g1r)]
    t = _unit_tri_inverse(n_ab, cs)
    vs = [_bf(_stack2(x, cs)) for x in v]
    hkv = [_bf(x.T) for x in ht]
    xx = [_bf(_dot(jnp.concatenate([at[p], n_ak[p]], axis=1), jnp.concatenate([hkv[p], vs[p]], axis=0)))
          for p in pairs]
    u =[_bf(_dot(_bf(a_), b_)) for a_, b_ in zip(t, xx)]
    bk = [_bf(jnp.concatenate([_stack2(bb[p] * e_tail[p], cs), _stack2(kd[p] * e_tail[p], cs)], axis=0))
          for p in pairs]
    uv = [jnp.concatenate([a_, b_], axis=0) for a_, b_ in zip(u, vs)]
    dh = [_dot_tn(a_, b_) for a_, b_ in zip(uv, bk)]
    ht_new = [ht[p] * jnp.exp(ctot[p]) + dh[p] for p in pairs]
    g2r = [pltpu.roll(x, C_HEAD, 1) for x in g2]
    n_rb = [_bf(jnp.where(incl, jnp.where(top, x, y), 0.0)) for x, y in zip(g2, g2r)]
    n_rk = [_bf(jnp.where(incl, jnp.where(top, y, x), 0.0)) for x, y in zip(g2, g2r)]
    ysum = [_dot(jnp.concatenate([rt[p], n_rb[p], n_rk[p]], axis=1),
                 jnp.concatenate([hkv[p], u[p], vs[p]], axis=0)) for p in pairs]
    y = [x[:CHUNK] + x[CHUNK:] for x in ysum]
    return y, ht_new, kd


def _rwkv_combine(y, yf, r, k, v, kd, laf, a2f, a0f, kav, rkv, lng, lnb, cs):
    pairs = range(len(y))
    inv_n = 1.0 / C_HEAD
    lafb = _bf(laf)
    zf = [_dot(lafb, a2f[p]) for p in pairs]
    ysum = [a + b for a, b in zip(y, yf)]
    mu = [_head_sums(x, cs) * inv_n for x in ysum]
    dev = [a - b for a, b in zip(ysum, mu)]
    var = [_head_sums(x * x, cs) * inv_n for x in dev]
    a_f = [_sigmoid(a0f[p] + zf[p]) for p in pairs]
    kd_f = [k[p] * (1.0 + (a_f[p] - 1.0) * kav[p]) for p in pairs]
    bsum = [_head_sums(r[p] * (kd_f[p] + kd[p]) * rkv[p], cs) for p in pairs]
    return [dev[p] * lax.rsqrt(var[p] + GN_EPS) * lng[p] + lnb[p] + bsum[p] * v[p] for p in pairs]


def _rwkv_scan_kernel(*refs, rev, nchunk, npair, combine):
    if combine:
        (r_ref, k_ref, v_ref, lw_ref, la_ref, w2_ref, a2_ref, w0_ref, a0_ref, kk_ref, ka_ref, s0_ref,
         yf_ref, laf_ref, a2f_ref, a0f_ref, rk_ref, lng_ref, lnb_ref, o_ref, sfin_ref, ht_ref) = refs
    else:
        (r_ref, k_ref, v_ref, lw_ref, la_ref, w2_ref, a2_ref, w0_ref, a0_ref, kk_ref, ka_ref, s0_ref,
         o_ref, sfin_ref, ht_ref) = refs
    c = pl.program_id(2)

    @pl.when(c == 0)
    def _():
        ht_ref[...] = s0_ref[0]

    cs = _pair_consts(rev)

    def body(ci, carry):
        cc = (nchunk - 1 - ci) if rev else ci
        rows = pl.ds(pl.multiple_of(cc * CHUNK, CHUNK), CHUNK)
        lw = lw_ref[0, 0, rows, :]
        la = la_ref[0, 0, rows, :]
        lanes = [slice(pr * LANES, (pr + 1) * LANES) for pr in range(npair)]
        r = [r_ref[0, 0, rows, ln] for ln in lanes]
        k = [k_ref[0, 0, rows, ln] for ln in lanes]
        v = [v_ref[0, 0, rows, ln] for ln in lanes]
        kav = [ka_ref[:, ln] for ln in lanes]
        y, ht_new, kd = _rwkv_chunk(
            r, k, v, lw, la, [w2_ref[0, :, ln] for ln in lanes], [a2_ref[0, :, ln] for ln in lanes],
            [w0_ref[0, :, ln] for ln in lanes], [a0_ref[0, :, ln] for ln in lanes],
            [kk_ref[:, ln] for ln in lanes], kav, [ht_ref[pr] for pr in range(npair)], cs, rev)
        for pr in range(npair):
            ht_ref[pr] = ht_new[pr]
        if combine:
            y = _rwkv_combine(
                y, [yf_ref[0, rows, ln] for ln in lanes], r, k, v, kd, laf_ref[0, 0, rows, :],
                [a2f_ref[0, :, ln] for ln in lanes], [a0f_ref[0, :, ln] for ln in lanes], kav,
                [rk_ref[:, ln] for ln in lanes], [lng_ref[:, ln] for ln in lanes],
                [lnb_ref[:, ln] for ln in lanes], cs)
        for pr in range(npair):
            o_ref[0, rows, lanes[pr]] = y[pr]
        return carry

    lax.fori_loop(0, nchunk, body, 0)

    @pl.when(c == pl.num_programs(2) - 1)
    def _():
        sfin_ref[0] = ht_ref[...]


def _rwkv_scan(rkv, small, prm, s0, rev, tblk, npair, y_fwd=None):
    _, bsz, seq, d = rkv.shape
    nblk = seq // tblk
    width = npair * LANES
    e = 1 if rev else 0
    blk = (lambda c: nblk - 1 - c) if rev else (lambda c: c)
    combine = y_fwd is not None

    def tok(m):
        return pl.BlockSpec((1, 1, tblk, width), lambda b, p, c: (m, b, blk(c), p))

    def lora(m, half):
        return pl.BlockSpec((1, 1, tblk, LANES), lambda b, p, c: (m, b, blk(c), half))

    def mat(idx):
        return pl.BlockSpec((1, LANES, width), lambda b, p, c: (idx, 0, p))

    def vec3(idx):
        return pl.BlockSpec((1, 1, width), lambda b, p, c: (idx, 0, p))

    vec = pl.BlockSpec((1, width), lambda b, p, c: (0, p))
    state = pl.BlockSpec((1, npair, LANES, LANES), lambda b, p, c: (b, p, 0, 0))
    out_tok = pl.BlockSpec((1, tblk, width), lambda b, p, c: (b, blk(c), p))
    in_specs = [tok(0), tok(1), tok(2), lora(1, e), lora(2, e), mat(e), mat(e), vec3(e), vec3(e), vec, vec, state]
    args = [rkv, rkv, rkv, small, small, prm["w2"], prm["a2"], prm["w0"], prm["a0"], prm["k_k"], prm["k_a"], s0]
    if combine:
        in_specs += [out_tok, lora(2, 0), mat(0), vec3(0), vec, vec, vec]
        args += [y_fwd, small, prm["a2"], prm["a0"], prm["r_k"], prm["ln_g"], prm["ln_b"]]
    return pl.pallas_call(
        functools.partial(_rwkv_scan_kernel, rev=rev, nchunk=tblk // CHUNK, npair=npair, combine=combine),
        grid=(bsz, d // width, nblk),
        in_specs=in_specs,
        out_specs=[out_tok, state],
        out_shape=[
            jax.ShapeDtypeStruct((bsz, seq, d), F32),
            jax.ShapeDtypeStruct((bsz, d // LANES, LANES, LANES), F32),
        ],
        scratch_shapes=[pltpu.VMEM((npair, LANES, LANES), F32)],
        compiler_params=_params(("arbitrary", "arbitrary", "arbitrary")),
        name="rwkv7_bwd" if rev else "rwkv7_fwd",
    )(*args)


def _rk_out_kernel(z_ref, gs_ref, g2_ref, wo_ref, h_ref, gt_ref, o_ref):
    gate = _dot(_bf(_sigmoid(gs_ref[0, 0])), g2_ref[...])
    y = _dot(_bf(z_ref[0] * gate), wo_ref[...])
    o_ref[0] = h_ref[0] + gt_ref[0] * y


def _rk_out(z, small, g2, wo, h, gt, tm):
    bsz, seq, d = h.shape
    row = lambda b, i: (b, i, 0)
    glora = g2.shape[0]
    return pl.pallas_call(
        _rk_out_kernel,
        grid=(bsz, seq // tm),
        in_specs=[
            pl.BlockSpec((1, tm, d), row),
            pl.BlockSpec((1, 1, tm, glora), lambda b, i: (0, b, i, 0)),
            pl.BlockSpec((glora, d), lambda b, i: (0, 0)),
            pl.BlockSpec((d, d), lambda b, i: (0, 0)),
            pl.BlockSpec((1, tm, d), row),
            pl.BlockSpec((1, 1, d), lambda b, i: (b, 0, 0)),
        ],
        out_specs=pl.BlockSpec((1, tm, d), row),
        out_shape=jax.ShapeDtypeStruct((bsz, seq, d), F32),
        compiler_params=_params(("arbitrary", "arbitrary")),
        name="rwkv_outproj",
    )(z, small, g2, wo, h, gt)


def _rope_tables(seq):
    t = jnp.arange(seq, dtype=jnp.int32)
    rows = (t // GRID_W).astype(F32)
    cols = (t % GRID_W).astype(F32)
    half = HEAD_DIM // 2
    n_freq = half // 2
    inv = ROPE_BASE ** (-jnp.arange(n_freq, dtype=F32) / n_freq)
    lane = jnp.arange(HEAD_DIM)
    pos = jnp.where((lane < half)[None, :], rows[:, None], cols[:, None])
    ang = pos * inv[lane % n_freq][None, :]
    cos, sin = jnp.cos(ang), jnp.sin(ang)
    first = ((lane % half) < n_freq)[None, :]
    return cos, jnp.where(first, -sin, 0.0), jnp.where(first, 0.0, sin)


def _pad_lanes(w, axis):
    pad = [(0, 0)] * w.ndim
    pad[axis] = (0, LANES - w.shape[axis])
    return jnp.pad(w, pad)


def _row_tile(seq, want):
    return want if seq % want == 0 else seq


def kernel(x, c, ctx, c_ctx, mod_w, mod_b, norm_mix, norm_ffn, ffn_up, ffn_down, ab_w_in, ab_w_out, attn_sink, hgrn_lb, hgrn_onorm, rk_mix, rk_wr, rk_wk, rk_wv, rk_wo, rk_w0, rk_w1, rk_w2, rk_a0, rk_a1, rk_a2, rk_g1, rk_g2, rk_kk, rk_ka, rk_rk, rk_ln_g, rk_ln_b, final_norm):
    bsz, seq, d = x.shape
    lc = ctx.shape[1]
    depth = mod_w.shape[0]
    assert bsz + 1 <= 8 and seq % 1024 == 0 and lc % CHUNK == 0

    cond8 = jnp.zeros((8, d), F32).at[:bsz].set(c).at[bsz].set(c_ctx)
    mod = _modulation(cond8, mod_w, mod_b)
    lb_all = jnp.cumsum(jax.nn.softmax(hgrn_lb.astype(F32), axis=0), axis=0)
    rope = _rope_tables(seq)

    h, hc = x, ctx
    for layer in range(depth):
        last = layer == depth - 1
        jl = layer // 2
        m_lat = mod[layer, :bsz].reshape(bsz, 1, 6, d)
        m_ctx = jnp.broadcast_to(mod[layer, bsz].reshape(1, 1, 6, d), (bsz, 1, 6, d))
        sh1, sc1, gt1, sh2, sc2, gt2 = (m_lat[:, :, i] for i in range(6))
        csh1, csc1, cgt1, csh2, csc2, cgt2 = (m_ctx[:, :, i] for i in range(6))
        g_mix = norm_mix[layer].reshape(1, d)
        g_ffn = norm_ffn[layer].reshape(1, d)
        if layer % 2 == 0:
            w_in = _bf(ab_w_in[jl])
            w_out = _bf(ab_w_out[jl])
            sink = attn_sink[jl].astype(F32)
            lb = lb_all[jl].reshape(1, B_W)
            onorm = hgrn_onorm[jl].reshape(1, B_W).astype(F32)
            n_in = w_in.shape[1]
            pc = _inproj(hc, g_mix, csc1, csh1, w_in, None, lc, 512)
            pl_ = _inproj(h, g_mix, sc1, sh1, w_in, rope, 1024, 512)
            oa = _win_attn(sink, pl_, pc)
            zeros = jnp.zeros((bsz, B_HEADS, HEAD_DIM, HEAD_DIM), F32)
            nhead = 4
            ocf, scf = _hgrn_scan(pc, lb, zeros, False, lc, nhead)
            ocb, scb = _hgrn_scan(pc, lb, zeros, True, lc, nhead)
            olf, _ = _hgrn_scan(pl_, lb, scf, False, 512, nhead)
            olb, _ = _hgrn_scan(pl_, lb, scb, True, 512, nhead)
            h = _ab_out(oa, olf, olb, pl_, onorm, w_out, h, gt1, 512)
            if not last:
                oca = _ctx_attn(sink, pc)
                hc = _ab_out(oca, ocf, ocb, pc, onorm, w_out, hc, cgt1, lc)
        else:
            w_big = _bf(jnp.stack([rk_wr[jl], rk_wk[jl], rk_wv[jl]]))
            w_small = _bf(jnp.stack([
                rk_g1[jl],
                jnp.concatenate([_pad_lanes(rk_w1[jl, 0], 1), _pad_lanes(rk_w1[jl, 1], 1)], axis=1),
                jnp.concatenate([_pad_lanes(rk_a1[jl, 0], 1), _pad_lanes(rk_a1[jl, 1], 1)], axis=1),
            ]))
            assert w_small.shape[-1] == 2 * LANES
            mix = rk_mix[jl]
            prm = dict(
                w2=_bf(_pad_lanes(rk_w2[jl], 1)), a2=_bf(_pad_lanes(rk_a2[jl], 1)),
                w0=rk_w0[jl].reshape(2, 1, d), a0=rk_a0[jl].reshape(2, 1, d),
                k_k=rk_kk[jl].reshape(1, d), k_a=rk_ka[jl].reshape(1, d), r_k=rk_rk[jl].reshape(1, d),
                ln_g=rk_ln_g[jl].reshape(1, d), ln_b=rk_ln_b[jl].reshape(1, d),
            )
            g2 = _bf(rk_g2[jl])
            wo = _bf(rk_wo[jl])
            mix = mix[jnp.array([0, 2, 3, 5, 1, 4])]
            rkv_c, sm_c = _rk_proj(hc, g_mix, csc1, csh1, mix, w_big, w_small, lc)
            rkv_l, sm_l = _rk_proj(h, g_mix, sc1, sh1, mix, w_big, w_small, 512)
            zeros = jnp.zeros((bsz, d // LANES, LANES, LANES), F32)
            npair = 16
            ycf, s_f = _rwkv_scan(rkv_c, sm_c, prm, zeros, False, lc, npair)
            zc, s_b = _rwkv_scan(rkv_c, sm_c, prm, zeros, True, lc, npair, y_fwd=ycf)
            ylf, _ = _rwkv_scan(rkv_l, sm_l, prm, s_f, False, 256, npair)
            zl, _ = _rwkv_scan(rkv_l, sm_l, prm, s_b, True, 256, npair, y_fwd=ylf)
            h = _rk_out(zl, sm_l, g2, wo, h, gt1, 512)
            if not last:
                hc = _rk_out(zc, sm_c, g2, wo, hc, cgt1, lc)
        w_up = _bf(ffn_up[layer])
        w_dn = _bf(ffn_down[layer])
        h = _mlp(h, g_ffn, sc2, sh2, gt2, w_up, w_dn, final_norm.reshape(1, d) if last else None, 1024, 512)
        if not last:
            hc = _mlp(hc, g_ffn, csc2, csh2, cgt2, w_up, w_dn, None, lc, 512)
    return h
```

```python
import functools

import jax
import jax.numpy as jnp
import numpy as np
from jax import lax
from jax.experimental import pallas as pl
from jax.experimental.pallas import tpu as pltpu

F32 = jnp.float32
BF16 = jnp.bfloat16
HIGHEST = lax.Precision.HIGHEST

LANES = 128
HEAD_DIM = 128
GRID_W = 64
WINDOW = 128
ROPE_BASE = 10000.0
A_Q_HEADS = 8
A_KV_HEADS = 2
A_GROUP = A_Q_HEADS // A_KV_HEADS
A_Q = A_Q_HEADS * HEAD_DIM
A_KV = A_KV_HEADS * HEAD_DIM
B_HEADS = 8
B_W = B_HEADS * HEAD_DIM
C_HEAD = 64
CHUNK = 64
SUB = 16
EPS = 1e-6
GN_EPS = 64e-5
VMEM_LIMIT = 58 * 1024 * 1024

NT_DIMS = (((1,), (1,)), ((), ()))
TN_DIMS = (((0,), (0,)), ((), ()))


def _dot(a, b, **kw):
    return jnp.dot(a, b, preferred_element_type=F32, **kw)


def _dot_nt(a, b):
    return lax.dot_general(a, b, NT_DIMS, preferred_element_type=F32)


def _dot_tn(a, b):
    return lax.dot_general(a, b, TN_DIMS, preferred_element_type=F32)


def _bf(x):
    return x.astype(BF16)


def _dot_split(x, m):
    hi = _bf(x)
    lo = _bf(x - hi.astype(F32))
    return _dot(hi, m) + _dot(lo, m)


def _sigmoid(x):
    return 1.0 / (1.0 + jnp.exp(-x))


def _silu(x):
    return x * _sigmoid(x)


def _normmod(x, g, sc, sh):
    ms = jnp.mean(x * x, axis=-1, keepdims=True)
    y = (x * lax.rsqrt(ms + EPS)) * g
    return y * (1.0 + sc) + sh


ROW_PIECE = 128


def _normmod_rows(x_ref, g_ref, sc_ref, sh_ref, u_ref):
    g, sc, sh = g_ref[...], sc_ref[0], sh_ref[0]
    rows = u_ref.shape[0]
    piece = min(ROW_PIECE, rows)
    for r0 in range(0, rows, piece):
        u_ref[r0:r0 + piece] = _bf(_normmod(x_ref[0, r0:r0 + piece], g, sc, sh))


def _cumsum_rows(x, rev):
    n = x.shape[0]
    row = lax.broadcasted_iota(jnp.int32, (n, 1), 0)
    s = 1
    while s < n:
        if rev:
            x = x + jnp.where(row < n - s, pltpu.roll(x, n - s, 0), 0.0)
        else:
            x = x + jnp.where(row >= s, pltpu.roll(x, s, 0), 0.0)
        s *= 2
    return x


def _params(sem):
    return pltpu.CompilerParams(dimension_semantics=sem, vmem_limit_bytes=VMEM_LIMIT)


def _mod_kernel(c_ref, w_ref, b_ref, o_ref):
    s = _bf(_silu(c_ref[...]))
    o_ref[0] = _dot(s, _bf(w_ref[0])) + b_ref[0]


def _modulation(cond8, mod_w, mod_b):
    depth, d, n = mod_w.shape
    tn = 1024
    return pl.pallas_call(
        _mod_kernel,
        grid=(depth, n // tn),
        in_specs=[
            pl.BlockSpec((8, d), lambda l, j: (0, 0)),
            pl.BlockSpec((1, d, tn), lambda l, j: (l, 0, j)),
            pl.BlockSpec((1, 1, tn), lambda l, j: (l, 0, j)),
        ],
        out_specs=pl.BlockSpec((1, 8, tn), lambda l, j: (l, 0, j)),
        out_shape=jax.ShapeDtypeStruct((depth, 8, n), F32),
        compiler_params=_params(("arbitrary", "arbitrary")),
        name="modulation",
    )(cond8, mod_w, mod_b.reshape(depth, 1, n))


def _inproj_kernel(*refs, n_rope):
    if n_rope:
        x_ref, g_ref, sc_ref, sh_ref, w_ref, cos_ref, sna_ref, snb_ref, o_ref, u_ref = refs
    else:
        x_ref, g_ref, sc_ref, sh_ref, w_ref, o_ref, u_ref = refs
    j = pl.program_id(2)

    @pl.when(j == 0)
    def _():
        _normmod_rows(x_ref, g_ref, sc_ref, sh_ref, u_ref)

    acc = _dot(u_ref[...], w_ref[...])
    if not n_rope:
        o_ref[0] = acc
        return

    per_tile = acc.shape[1] // HEAD_DIM
    n_full, n_rem = n_rope // per_tile, n_rope % per_tile

    def store(n_rot):
        cos, sna, snb = cos_ref[...], sna_ref[...], snb_ref[...]
        for hd in range(n_rot):
            sl = acc[:, hd * HEAD_DIM:(hd + 1) * HEAD_DIM]
            rot = sl * cos + pltpu.roll(sl, 96, 1) * sna + pltpu.roll(sl, 32, 1) * snb
            o_ref[0, :, hd * HEAD_DIM:(hd + 1) * HEAD_DIM] = rot
        if n_rot < per_tile:
            o_ref[0, :, n_rot * HEAD_DIM:] = acc[:, n_rot * HEAD_DIM:]

    pl.when(j < n_full)(lambda: store(per_tile))
    pl.when(j == n_full)(lambda: store(n_rem))
    pl.when(j > n_full)(lambda: store(0))


def _inproj(x, g, sc, sh, w, rope, tm, tn):
    bsz, seq, d = x.shape
    n = w.shape[1]
    n_rope = 0
    in_specs = [
        pl.BlockSpec((1, tm, d), lambda b, i, j: (b, i, 0)),
        pl.BlockSpec((1, d), lambda b, i, j: (0, 0)),
        pl.BlockSpec((1, 1, d), lambda b, i, j: (b, 0, 0)),
        pl.BlockSpec((1, 1, d), lambda b, i, j: (b, 0, 0)),
        pl.BlockSpec((d, tn), lambda b, i, j: (0, j)),
    ]
    args = [x, g, sc, sh, w]
    if rope is not None:
        n_rope = A_Q_HEADS + A_KV_HEADS
        assert tn % HEAD_DIM == 0
        in_specs += [pl.BlockSpec((tm, HEAD_DIM), lambda b, i, j: (i, 0))] * 3
        args += list(rope)
    return pl.pallas_call(
        functools.partial(_inproj_kernel, n_rope=n_rope),
        grid=(bsz, seq // tm, n // tn),
        in_specs=in_specs,
        out_specs=pl.BlockSpec((1, tm, tn), lambda b, i, j: (b, i, j)),
        out_shape=jax.ShapeDtypeStruct((bsz, seq, n), F32),
        scratch_shapes=[pltpu.VMEM((tm, d), BF16)],
        compiler_params=_params(("arbitrary", "arbitrary", "arbitrary")),
        name="ab_inproj",
    )(*args)


def _softmax_av(s_list, v_list, sink_col):
    m = sink_col
    for s in s_list:
        m = jnp.maximum(m, jnp.max(s, axis=-1, keepdims=True))
    den = jnp.exp(sink_col - m)
    out = None
    for s, v in zip(s_list, v_list):
        p = jnp.exp(s - m)
        den = den + jnp.sum(p, axis=-1, keepdims=True)
        o = _dot(_bf(p), v)
        out = o if out is None else out + o
    return out / den


def _sink_column(sink_ref, hk, rows):
    rowh = lax.broadcasted_iota(jnp.int32, (rows, 1), 0) // WINDOW
    col = jnp.full((rows, 1), sink_ref[hk * A_GROUP + A_GROUP - 1], F32)
    for g in range(A_GROUP - 1):
        col = jnp.where(rowh == g, sink_ref[hk * A_GROUP + g], col)
    return col


def _win_attn_kernel(sink_ref, q_ref, kp_ref, kc_ref, kn_ref, vp_ref, vc_ref, vn_ref,
                     ck_ref, cv_ref, o_ref):
    n = pl.program_id(1)
    nb = pl.num_programs(1)
    scale = HEAD_DIM ** -0.5
    q = q_ref[0]
    kband = jnp.concatenate([kp_ref[0], kc_ref[0], kn_ref[0]], axis=0)
    vband = jnp.concatenate([vp_ref[0], vc_ref[0], vn_ref[0]], axis=0)
    rows = A_GROUP * WINDOW
    tq = lax.broadcasted_iota(jnp.int32, (rows, 3 * WINDOW), 0) % WINDOW
    tk = lax.broadcasted_iota(jnp.int32, (rows, 3 * WINDOW), 1)
    rel = tk - WINDOW - tq
    valid = (jnp.abs(rel) <= WINDOW) & ((tk >= WINDOW) | (n > 0)) & ((tk < 2 * WINDOW) | (n < nb - 1))
    for hk in range(A_KV_HEADS):
        qs = jnp.concatenate(
            [q[:, (hk * A_GROUP + g) * HEAD_DIM:(hk * A_GROUP + g + 1) * HEAD_DIM] for g in range(A_GROUP)],
            axis=0)
        qs = _bf(qs)
        hs = slice(hk * HEAD_DIM, (hk + 1) * HEAD_DIM)
        s_win = _dot_nt(qs, _bf(kband[:, hs])) * scale
        s_win = jnp.where(valid, s_win, -jnp.inf)
        s_ctx = _dot_nt(qs, _bf(ck_ref[0][:, hs])) * scale
        o = _softmax_av([s_win, s_ctx], [_bf(vband[:, hs]), _bf(cv_ref[0][:, hs])],
                        _sink_column(sink_ref, hk, rows))
        for g in range(A_GROUP):
            h = hk * A_GROUP + g
            o_ref[0, :, h * HEAD_DIM:(h + 1) * HEAD_DIM] = _bf(o[g * WINDOW:(g + 1) * WINDOW])


def _win_attn(sink, proj, proj_ctx):
    bsz, seq, _ = proj.shape
    lc = proj_ctx.shape[1]
    nb = seq // WINDOW
    kcol, vcol = A_Q // A_KV, A_Q // A_KV + 1
    prev = lambda b, n: (b, jnp.maximum(n - 1, 0))
    nxt = lambda b, n: (b, jnp.minimum(n + 1, nb - 1))
    cur = lambda b, n: (b, n)

    def band(rowfn, col):
        return pl.BlockSpec((1, WINDOW, A_KV), lambda b, n: rowfn(b, n) + (col,))

    return pl.pallas_call(
        _win_attn_kernel,
        grid=(bsz, nb),
        in_specs=[
            pl.BlockSpec(memory_space=pltpu.SMEM),
            pl.BlockSpec((1, WINDOW, A_Q), lambda b, n: (b, n, 0)),
            band(prev, kcol), band(cur, kcol), band(nxt, kcol),
            band(prev, vcol), band(cur, vcol), band(nxt, vcol),
            pl.BlockSpec((1, lc, A_KV), lambda b, n: (b, 0, kcol)),
            pl.BlockSpec((1, lc, A_KV), lambda b, n: (b, 0, vcol)),
        ],
        out_specs=pl.BlockSpec((1, WINDOW, A_Q), lambda b, n: (b, n, 0)),
        out_shape=jax.ShapeDtypeStruct((bsz, seq, A_Q), BF16),
        compiler_params=_params(("arbitrary", "arbitrary")),
        name="window_attention",
    )(sink, proj, proj, proj, proj, proj, proj, proj, proj_ctx, proj_ctx)


def _ctx_attn_kernel(sink_ref, q_ref, k_ref, v_ref, o_ref):
    scale = HEAD_DIM ** -0.5
    q = q_ref[0]
    lc = q.shape[0]
    for h in range(A_Q_HEADS):
        hk = h // A_GROUP
        hs = slice(hk * HEAD_DIM, (hk + 1) * HEAD_DIM)
        qs = _bf(q[:, h * HEAD_DIM:(h + 1) * HEAD_DIM])
        s = _dot_nt(qs, _bf(k_ref[0][:, hs])) * scale
        sink_col = jnp.full((lc, 1), sink_ref[h], F32)
        o = _softmax_av([s], [_bf(v_ref[0][:, hs])], sink_col)
        o_ref[0, :, h * HEAD_DIM:(h + 1) * HEAD_DIM] = _bf(o)


def _ctx_attn(sink, proj_ctx):
    bsz, lc, _ = proj_ctx.shape
    kcol, vcol = A_Q // A_KV, A_Q // A_KV + 1
    return pl.pallas_call(
        _ctx_attn_kernel,
        grid=(bsz,),
        in_specs=[
            pl.BlockSpec(memory_space=pltpu.SMEM),
            pl.BlockSpec((1, lc, A_Q), lambda b: (b, 0, 0)),
            pl.BlockSpec((1, lc, A_KV), lambda b: (b, 0, kcol)),
            pl.BlockSpec((1, lc, A_KV), lambda b: (b, 0, vcol)),
        ],
        out_specs=pl.BlockSpec((1, lc, A_Q), lambda b: (b, 0, 0)),
        out_shape=jax.ShapeDtypeStruct((bsz, lc, A_Q), BF16),
        compiler_params=_params(("arbitrary",)),
        name="context_attention",
    )(sink, proj_ctx, proj_ctx, proj_ctx)


def _hgrn_chunk(bq, bi, bf, lb, st, rev):
    heads = range(len(bq))
    q = [_silu(x) for x in bq]
    v = bi
    f = [lb[h] + (1.0 - lb[h]) * _sigmoid(bf[h]) for h in heads]
    k = [1.0 - x for x in f]
    g = [jnp.log(x) for x in f]
    b = [_cumsum_rows(x, rev) for x in g]
    btot = [x[0:1] if rev else x[CHUNK - 1:CHUNK] for x in b]
    vb = [_bf(x) for x in v]
    stb = [_bf(x) for x in st]
    o = [_dot_nt(_bf(q[h] * jnp.exp(b[h])), stb[h]) for h in heads]
    khat = [_bf(k[h] * jnp.exp(btot[h] - b[h])) for h in heads]
    dst = [_dot_tn(vb[h], khat[h]) for h in heads]
    st_new = [st[h] * jnp.exp(btot[h]) + dst[h] for h in heads]
    nsub = CHUNK // SUB
    row8 = lax.broadcasted_iota(jnp.int32, (8, 1), 0)
    outs = [[] for _ in heads]
    for blk in range(nsub):
        r0 = blk * SUB
        rs = slice(r0, r0 + SUB)
        acc = [o[h][rs] for h in heads]
        if rev and blk < nsub - 1:
            ref_row, lo, hi = r0 + SUB, r0 + SUB, CHUNK
        elif (not rev) and blk > 0:
            ref_row, lo, hi = r0 - 1, 0, r0
        else:
            ref_row = None
        if ref_row is not None:
            bref = [b[h][ref_row:ref_row + 1] for h in heads]
            qn = [_bf(q[h][rs] * jnp.exp(b[h][rs] - bref[h])) for h in heads]
            kn = [_bf(k[h][lo:hi] * jnp.exp(bref[h] - b[h][lo:hi])) for h in heads]
            att = [_bf(_dot_nt(qn[h], kn[h])) for h in heads]
            acc = [acc[h] + _dot(att[h], vb[h][lo:hi]) for h in heads]
        piece = 8
        acc = [[a[p0:p0 + piece] for p0 in range(0, SUB, piece)] for a in acc]
        for s in range(SUB):
            reached = range(0, s // piece + 1) if rev else range(s // piece, SUB // piece)
            for pc in reached:
                t0 = r0 + pc * piece
                mask = (row8 + pc * piece <= s) if rev else (row8 + pc * piece >= s)
                for h in heads:
                    rel = b[h][t0:t0 + piece] - b[h][r0 + s:r0 + s + 1]
                    dec = jnp.exp(jnp.where(mask, rel, -jnp.inf))
                    w = jnp.sum(q[h][t0:t0 + piece] * dec * k[h][r0 + s:r0 + s + 1], axis=-1, keepdims=True)
                    acc[h][pc] = acc[h][pc] + w * v[h][r0 + s:r0 + s + 1]
        for h in heads:
            outs[h].extend(acc[h])
    return [jnp.concatenate(x, axis=0) for x in outs], st_new


def _hgrn_kernel(q_ref, i_ref, f_ref, lb_ref, s0_ref, o_ref, sfin_ref, st_ref, *, rev, nchunk, nhead):
    c = pl.program_id(2)

    @pl.when(c == 0)
    def _():
        st_ref[...] = s0_ref[0]

    lanes = [slice(h * HEAD_DIM, (h + 1) * HEAD_DIM) for h in range(nhead)]
    lb = [lb_ref[:, ln] for ln in lanes]

    def body(ci, carry):
        cc = (nchunk - 1 - ci) if rev else ci
        rows = pl.ds(pl.multiple_of(cc * CHUNK, CHUNK), CHUNK)
        o, st_new = _hgrn_chunk([q_ref[0, rows, ln] for ln in lanes], [i_ref[0, rows, ln] for ln in lanes],
                                [f_ref[0, rows, ln] for ln in lanes], lb,
                                [st_ref[h] for h in range(nhead)], rev)
        for h in range(nhead):
            o_ref[0, rows, lanes[h]] = o[h]
            st_ref[h] = st_new[h]
        return carry

    lax.fori_loop(0, nchunk, body, 0)

    @pl.when(c == pl.num_programs(2) - 1)
    def _():
        sfin_ref[0] = st_ref[...]


def _hgrn_scan(proj, lb, s0, rev, tblk, nhead):
    bsz, seq, _ = proj.shape
    nblk = seq // tblk
    width = nhead * HEAD_DIM
    base = A_Q + 2 * A_KV
    assert base % width == 0 and B_W % width == 0
    qc, ic = base // width, (base + B_W) // width
    fc = (base + (3 if rev else 2) * B_W) // width
    blk = (lambda c: nblk - 1 - c) if rev else (lambda c: c)

    def col(c0):
        return pl.BlockSpec((1, tblk, width), lambda b, h, c: (b, blk(c), c0 + h))

    state = pl.BlockSpec((1, nhead, HEAD_DIM, HEAD_DIM), lambda b, h, c: (b, h, 0, 0))
    return pl.pallas_call(
        functools.partial(_hgrn_kernel, rev=rev, nchunk=tblk // CHUNK, nhead=nhead),
        grid=(bsz, B_HEADS // nhead, nblk),
        in_specs=[
            col(qc), col(ic), col(fc),
            pl.BlockSpec((1, width), lambda b, h, c: (0, h)),
            state,
        ],
        out_specs=[
            pl.BlockSpec((1, tblk, width), lambda b, h, c: (b, blk(c), h)),
            state,
        ],
        out_shape=[
            jax.ShapeDtypeStruct((bsz, seq, B_W), F32),
            jax.ShapeDtypeStruct((bsz, B_HEADS, HEAD_DIM, HEAD_DIM), F32),
        ],
        scratch_shapes=[pltpu.VMEM((nhead, HEAD_DIM, HEAD_DIM), F32)],
        compiler_params=_params(("arbitrary", "arbitrary", "arbitrary")),
        name="hgrn2_bwd" if rev else "hgrn2_fwd",
    )(proj, proj, proj, lb, s0)


def _ab_out_kernel(oa_ref, of_ref, ob_ref, g0_ref, g1_ref, on_ref, w_ref, h_ref, gt_ref, o_ref, lhs_ref):
    lhs_ref[:, :A_Q] = oa_ref[0]
    half = B_W // 2
    for hd in range(B_HEADS):
        sl = slice(hd * HEAD_DIM, (hd + 1) * HEAD_DIM)
        o = of_ref[0, :, sl] + ob_ref[0, :, sl]
        o = o * lax.rsqrt(jnp.mean(o * o, axis=-1, keepdims=True) + EPS)
        o = o * on_ref[:, sl]
        gref = g0_ref if hd * HEAD_DIM < half else g1_ref
        gs = slice(hd * HEAD_DIM % half, hd * HEAD_DIM % half + HEAD_DIM)
        o = o * _silu(gref[0, :, gs])
        lhs_ref[:, A_Q + hd * HEAD_DIM:A_Q + (hd + 1) * HEAD_DIM] = _bf(o)
    y = _dot(lhs_ref[...], w_ref[...])
    o_ref[0] = h_ref[0] + gt_ref[0] * y


def _ab_out(oa, of, ob, proj, onorm, w_out, h, gt, tm):
    bsz, seq, d = h.shape
    half = B_W // 2
    gcol = (A_Q + 2 * A_KV + 4 * B_W) // half
    row = lambda b, i: (b, i, 0)
    return pl.pallas_call(
        _ab_out_kernel,
        grid=(bsz, seq // tm),
        in_specs=[
            pl.BlockSpec((1, tm, A_Q), row),
            pl.BlockSpec((1, tm, B_W), row),
            pl.BlockSpec((1, tm, B_W), row),
            pl.BlockSpec((1, tm, half), lambda b, i: (b, i, gcol)),
            pl.BlockSpec((1, tm, half), lambda b, i: (b, i, gcol + 1)),
            pl.BlockSpec((1, B_W), lambda b, i: (0, 0)),
            pl.BlockSpec((A_Q + B_W, d), lambda b, i: (0, 0)),
            pl.BlockSpec((1, tm, d), row),
            pl.BlockSpec((1, 1, d), lambda b, i: (b, 0, 0)),
        ],
        out_specs=pl.BlockSpec((1, tm, d), row),
        out_shape=jax.ShapeDtypeStruct((bsz, seq, d), F32),
        scratch_shapes=[pltpu.VMEM((tm, A_Q + B_W), BF16)],
        compiler_params=_params(("arbitrary", "arbitrary")),
        name="ab_outproj",
    )(oa, of, ob, proj, proj, onorm, w_out, h, gt)


def _mlp_kernel(*refs, final):
    if final:
        x_ref, g_ref, sc_ref, sh_ref, gt_ref, wu_ref, wd_ref, fn_ref, o_ref, u_ref = refs
    else:
        x_ref, g_ref, sc_ref, sh_ref, gt_ref, wu_ref, wd_ref, o_ref, u_ref = refs
    j = pl.program_id(2)

    tm = x_ref.shape[1]

    @pl.when(j == 0)
    def _():
        _normmod_rows(x_ref, g_ref, sc_ref, sh_ref, u_ref)

    hid = jnp.maximum(_dot(u_ref[...], wu_ref[...]), 0.0)
    hid = _bf(hid * hid)
    ncol = 512

    @pl.when(j == 0)
    def _():
        for n0 in range(0, o_ref.shape[2], ncol):
            o_ref[0, :, n0:n0 + ncol] = _dot(hid, wd_ref[:, n0:n0 + ncol])

    @pl.when(j != 0)
    def _():
        for n0 in range(0, o_ref.shape[2], ncol):
            o_ref[0, :, n0:n0 + ncol] += _dot(hid, wd_ref[:, n0:n0 + ncol])

    @pl.when(j == pl.num_programs(2) - 1)
    def _():
        for r0 in range(0, tm, ROW_PIECE):
            rs = slice(r0, r0 + ROW_PIECE)
            y = x_ref[0, rs] + gt_ref[0] * o_ref[0, rs]
            if final:
                y = (y * lax.rsqrt(jnp.mean(y * y, axis=-1, keepdims=True) + EPS)) * fn_ref[...]
            o_ref[0, rs] = y


def _mlp(x, g, sc, sh, gt, w_up, w_down, final_gain, tm, tf):
    bsz, seq, d = x.shape
    dff = w_up.shape[1]
    vec = pl.BlockSpec((1, 1, d), lambda b, i, j: (b, 0, 0))
    in_specs = [
        pl.BlockSpec((1, tm, d), lambda b, i, j: (b, i, 0)),
        pl.BlockSpec((1, d), lambda b, i, j: (0, 0)),
        vec, vec, vec,
        pl.BlockSpec((d, tf), lambda b, i, j: (0, j)),
        pl.BlockSpec((tf, d), lambda b, i, j: (j, 0)),
    ]
    args = [x, g, sc, sh, gt, w_up, w_down]
    if final_gain is not None:
        in_specs.append(pl.BlockSpec((1, d), lambda b, i, j: (0, 0)))
        args.append(final_gain)
    return pl.pallas_call(
        functools.partial(_mlp_kernel, final=final_gain is not None),
        grid=(bsz, seq // tm, dff // tf),
        in_specs=in_specs,
        out_specs=pl.BlockSpec((1, tm, d), lambda b, i, j: (b, i, 0)),
        out_shape=jax.ShapeDtypeStruct((bsz, seq, d), F32),
        scratch_shapes=[pltpu.VMEM((tm, d), BF16)],
        compiler_params=_params(("arbitrary", "arbitrary", "arbitrary")),
        name="sqrelu_mlp",
    )(*args)


def _rk_proj_kernel(x_ref, xp_ref, xn_ref, g_ref, sc_ref, sh_ref, mix_ref, wb_ref, ws_ref, ob_ref, os_ref,
                    u_ref, xx_ref, lhs_ref, *, nbig, nsmall):
    i = pl.program_id(1)
    j = pl.program_id(2)
    tm = x_ref.shape[1]
    rb = 128

    def build(m, slot):
        mixrow = mix_ref[pl.ds(m, 1), :]
        for r0 in range(0, tm, rb):
            lhs_ref[slot, r0:r0 + rb] = _bf(u_ref[8 + r0:8 + r0 + rb] + xx_ref[r0:r0 + rb] * mixrow)

    @pl.when(j == 0)
    def _():
        g, sc, sh = g_ref[...], sc_ref[0], sh_ref[0]
        for r0 in range(0, tm, rb):
            u_ref[8 + r0:8 + r0 + rb] = _normmod(x_ref[0, r0:r0 + rb], g, sc, sh)
        up = _normmod(xp_ref[0], g, sc, sh)[7:8]
        un = _normmod(xn_ref[0], g, sc, sh)[0:1]
        u_ref[7:8] = jnp.where(i == 0, 0.0, up)
        u_ref[8 + tm:9 + tm] = jnp.where(i == pl.num_programs(1) - 1, 0.0, un)
        for r0 in range(0, tm, rb):
            nb = u_ref[7 + r0:7 + r0 + rb] + u_ref[9 + r0:9 + r0 + rb]
            xx_ref[r0:r0 + rb] = 0.5 * nb - u_ref[8 + r0:8 + r0 + rb]
        build(0, 0)

    @pl.when(j < nbig)
    def _():
        build(j + 1, (j + 1) % 2)
        ob_ref[0, 0] = _dot(lhs_ref[j % 2], wb_ref[0])

    @pl.when(j == nbig)
    def _():
        for s in range(nsmall):
            os_ref[s, 0] = _dot(lhs_ref[(nbig + s) % 2], ws_ref[s])
            if s + 1 < nsmall:
                build(nbig + s + 1, (nbig + s + 1) % 2)


def _rk_proj(x, g, sc, sh, mix, w_big, w_small, tm):
    bsz, seq, d = x.shape
    nbig, _, n = w_big.shape
    nsmall, _, ns = w_small.shape
    r8 = tm // 8
    last8 = seq // 8 - 1
    vec = pl.BlockSpec((1, 1, d), lambda b, i, j: (b, 0, 0))
    return pl.pallas_call(
        functools.partial(_rk_proj_kernel, nbig=nbig, nsmall=nsmall),
        grid=(bsz, seq // tm, nbig + 1),
        in_specs=[
            pl.BlockSpec((1, tm, d), lambda b, i, j: (b, i, 0)),
            pl.BlockSpec((1, 8, d), lambda b, i, j: (b, jnp.maximum(i * r8 - 1, 0), 0)),
            pl.BlockSpec((1, 8, d), lambda b, i, j: (b, jnp.minimum((i + 1) * r8, last8), 0)),
            pl.BlockSpec((1, d), lambda b, i, j: (0, 0)),
            vec, vec,
            pl.BlockSpec((nbig + nsmall, d), lambda b, i, j: (0, 0)),
            pl.BlockSpec((1, d, n), lambda b, i, j: (jnp.minimum(j, nbig - 1), 0, 0)),
            pl.BlockSpec((nsmall, d, ns), lambda b, i, j: (0, 0, 0)),
        ],
        out_specs=[
            pl.BlockSpec((1, 1, tm, n), lambda b, i, j: (jnp.minimum(j, nbig - 1), b, i, 0)),
            pl.BlockSpec((nsmall, 1, tm, ns), lambda b, i, j: (0, b, i, 0)),
        ],
        out_shape=[
            jax.ShapeDtypeStruct((nbig, bsz, seq, n), F32),
            jax.ShapeDtypeStruct((nsmall, bsz, seq, ns), F32),
        ],
        scratch_shapes=[pltpu.VMEM((tm + 16, d), F32), pltpu.VMEM((tm, d), F32), pltpu.VMEM((2, tm, d), BF16)],
        compiler_params=_params(("arbitrary", "arbitrary", "arbitrary")),
        name="rwkv_proj",
    )(x, x, x, g, sc, sh, mix, w_big, w_small)


DECAY_SCALE = float(np.exp(-0.5))


def _pair_consts(rev):
    n = 2 * CHUNK
    ri = lax.broadcasted_iota(jnp.int32, (n, n), 0)
    ci = lax.broadcasted_iota(jnp.int32, (n, n), 1)
    same = (ri // CHUNK) == (ci // CHUNK)
    rt, ct = ri % CHUNK, ci % CHUNK
    strict = same & ((ct > rt) if rev else (ct < rt))
    incl = same & ((ct >= rt) if rev else (ct <= rt))
    eye = jnp.where(ri == ci, 1.0, 0.0).astype(F32)
    lane = lax.broadcasted_iota(jnp.int32, (1, LANES), 1)
    lane_lo = lane < C_HEAD
    m0 = jnp.where(lane_lo, 1.0, 0.0).astype(F32)
    m1 = 1.0 - m0
    blk2 = (ri // 2) == (ci // 2)
    level = {}
    m = 2
    while m < CHUNK:
        if m % 8:
            rsel, csel = ri, ci
        else:
            rc = lax.broadcasted_iota(jnp.int32, (n // 2, n), 0)
            csel = lax.broadcasted_iota(jnp.int32, (n // 2, n), 1)
            rsel = (rc // m) * (2 * m) + (0 if rev else m) + rc % m
        level[m] = ((rsel // (2 * m)) == (csel // (2 * m))) & ((rsel // m) != (csel // m))
        m *= 2
    return dict(strict=strict, incl=incl, eye=eye, m0=m0, m1=m1, lane_lo=lane_lo, blk2=blk2, level=level)


def _head_sums(x, cs):
    s0 = jnp.sum(x * cs["m0"], axis=-1, keepdims=True)
    s1 = jnp.sum(x * cs["m1"], axis=-1, keepdims=True)
    return jnp.where(cs["lane_lo"], s0, s1)


def _unit_tri_inverse(n, cs, rev):
    size = 2 * CHUNK
    t = [cs["eye"] + jnp.where(cs["blk2"], x, 0.0) for x in n]
    nb = [_bf(x) for x in n]
    m = 2
    while m < CHUNK:
        tb = [_bf(x) for x in t]
        sel = cs["level"][m]
        if m % 8:
            q = [_bf(_dot(a, b)) for a, b in zip(tb, nb)]
            q = [_dot(a, b) for a, b in zip(q, tb)]
            t = [a + jnp.where(sel, b, 0.0) for a, b in zip(t, q)]
        else:
            starts = range(0 if rev else m, size, 2 * m)
            pick = lambda x: jnp.concatenate([x[r0:r0 + m] for r0 in starts], axis=0)
            q = [_bf(_dot(_bf(pick(x)), b)) for x, b in zip(t, nb)]
            q = [_dot(a, b) for a, b in zip(q, tb)]
            t = [_add_rows(x, jnp.where(sel, y, 0.0), starts, m) for x, y in zip(t, q)]
        m *= 2
    return t


def _add_rows(x, upd, starts, m):
    pieces, pos = [], 0
    for k, r0 in enumerate(starts):
        if r0 > pos:
            pieces.append(x[pos:r0])
        pieces.append(x[r0:r0 + m] + upd[k * m:(k + 1) * m])
        pos = r0 + m
    if pos < x.shape[0]:
        pieces.append(x[pos:])
    return jnp.concatenate(pieces, axis=0)


def _halves(top, bottom):
    return jnp.concatenate([top[:CHUNK], bottom[CHUNK:]], axis=0)


def _stack2(x, cs):
    return jnp.concatenate([x * cs["m0"], x * cs["m1"]], axis=0)


def _rwkv_chunk(r, k, v, lw, la, w2, a2, w0, a0, kkv, kav, ht, cs, rev):
    pairs = range(len(r))
    tl, lab = _bf(jnp.tanh(lw)), _bf(la)
    z = [w0[p] + _dot(tl, w2[p]) for p in pairs]
    za = [_dot(lab, a2[p]) for p in pairs]
    kkr = [k[p] * kkv[p] for p in pairs]
    ss = [_head_sums(x * x, cs) for x in kkr]
    ld = [-DECAY_SCALE * _sigmoid(x) for x in z]
    cl = [_cumsum_rows(x, rev) for x in ld]
    a = [_sigmoid(a0[p] + za[p]) for p in pairs]
    kk = [x / jnp.maximum(jnp.sqrt(s), 1e-12) for x, s in zip(kkr, ss)]
    kd = [k[p] * (1.0 + (a[p] - 1.0) * kav[p]) for p in pairs]
    bb = [x * y for x, y in zip(kk, a)]
    ctot = [x[0:1] if rev else x[CHUNK - 1:CHUNK] for x in cl]
    e_neg = [jnp.exp(-x) for x in cl]
    e_tail = [jnp.exp(c - x) for c, x in zip(ctot, cl)]
    at = [_bf(_stack2(-kk[p] * jnp.exp(cl[p] - ld[p]), cs)) for p in pairs]
    rt = [_bf(_stack2(r[p] * jnp.exp(cl[p]), cs)) for p in pairs]
    rhs = [_bf(jnp.concatenate([bb[p] * e_neg[p], kd[p] * e_neg[p]], axis=0)) for p in pairs]
    g = [_dot_nt(jnp.concatenate([x, y], axis=0), w) for x, y, w in zip(at, rt, rhs)]
    g1 = [x[:2 * CHUNK] for x in g]
    g2 = [x[2 * CHUNK:] for x in g]
    strict, incl = cs["strict"], cs["incl"]
    g1r = [pltpu.roll(x, C_HEAD, 1) for x in g1]
    n_ab = [jnp.where(strict, _halves(x, y), 0.0) for x, y in zip(g1, g1r)]
    n_ak = [_bf(jnp.where(strict, _halves(y, x), 0.0)) for x, y in zip(g1, g1r)]
    t = _unit_tri_inverse(n_ab, cs, rev)
    vs = [_bf(_stack2(x, cs)) for x in v]
    hkv = [_bf(x.T) for x in ht]
    xx = [_bf(_dot(jnp.concatenate([at[p], n_ak[p]], axis=1), jnp.concatenate([hkv[p], vs[p]], axis=0)))
          for p in pairs]
    u =[_bf(_dot(_bf(a_), b_)) for a_, b_ in zip(t, xx)]
    bk = [_bf(jnp.concatenate([_stack2(bb[p] * e_tail[p], cs), _stack2(kd[p] * e_tail[p], cs)], axis=0))
          for p in pairs]
    uv = [jnp.concatenate([a_, b_], axis=0) for a_, b_ in zip(u, vs)]
    dh = [_dot_tn(a_, b_) for a_, b_ in zip(uv, bk)]
    ht_new = [ht[p] * jnp.exp(ctot[p]) + dh[p] for p in pairs]
    g2r = [pltpu.roll(x, C_HEAD, 1) for x in g2]
    n_rb = [_bf(jnp.where(incl, _halves(x, y), 0.0)) for x, y in zip(g2, g2r)]
    n_rk = [_bf(jnp.where(incl, _halves(y, x), 0.0)) for x, y in zip(g2, g2r)]
    ysum = [_dot(jnp.concatenate([rt[p], n_rb[p], n_rk[p]], axis=1),
                 jnp.concatenate([hkv[p], u[p], vs[p]], axis=0)) for p in pairs]
    y = [x[:CHUNK] + x[CHUNK:] for x in ysum]
    return y, ht_new, kd


def _rwkv_combine(y, yf, r, k, v, kd, laf, a2f, a0f, kav, rkv, lng, lnb, cs):
    pairs = range(len(y))
    inv_n = 1.0 / C_HEAD
    lafb = _bf(laf)
    zf = [_dot(lafb, a2f[p]) for p in pairs]
    ysum = [a + b for a, b in zip(y, yf)]
    mu = [_head_sums(x, cs) * inv_n for x in ysum]
    dev = [a - b for a, b in zip(ysum, mu)]
    var = [_head_sums(x * x, cs) * inv_n for x in dev]
    a_f = [_sigmoid(a0f[p] + zf[p]) for p in pairs]
    kd_f = [k[p] * (1.0 + (a_f[p] - 1.0) * kav[p]) for p in pairs]
    bsum = [_head_sums(r[p] * (kd_f[p] + kd[p]) * rkv[p], cs) for p in pairs]
    return [dev[p] * lax.rsqrt(var[p] + GN_EPS) * lng[p] + lnb[p] + bsum[p] * v[p] for p in pairs]


def _rwkv_scan_kernel(*refs, rev, nchunk, npair, combine):
    if combine:
        (r_ref, k_ref, v_ref, lw_ref, la_ref, w2_ref, a2_ref, w0_ref, a0_ref, kk_ref, ka_ref, s0_ref,
         yf_ref, laf_ref, a2f_ref, a0f_ref, rk_ref, lng_ref, lnb_ref, o_ref, sfin_ref, ht_ref) = refs
    else:
        (r_ref, k_ref, v_ref, lw_ref, la_ref, w2_ref, a2_ref, w0_ref, a0_ref, kk_ref, ka_ref, s0_ref,
         o_ref, sfin_ref, ht_ref) = refs
    c = pl.program_id(2)

    @pl.when(c == 0)
    def _():
        ht_ref[...] = s0_ref[0]

    cs = _pair_consts(rev)

    def body(ci, carry):
        cc = (nchunk - 1 - ci) if rev else ci
        rows = pl.ds(pl.multiple_of(cc * CHUNK, CHUNK), CHUNK)
        lw = lw_ref[0, 0, rows, :]
        la = la_ref[0, 0, rows, :]
        lanes = [slice(pr * LANES, (pr + 1) * LANES) for pr in range(npair)]
        r = [r_ref[0, 0, rows, ln] for ln in lanes]
        k = [k_ref[0, 0, rows, ln] for ln in lanes]
        v = [v_ref[0, 0, rows, ln] for ln in lanes]
        kav = [ka_ref[:, ln] for ln in lanes]
        y, ht_new, kd = _rwkv_chunk(
            r, k, v, lw, la, [w2_ref[0, :, ln] for ln in lanes], [a2_ref[0, :, ln] for ln in lanes],
            [w0_ref[0, :, ln] for ln in lanes], [a0_ref[0, :, ln] for ln in lanes],
            [kk_ref[:, ln] for ln in lanes], kav, [ht_ref[pr] for pr in range(npair)], cs, rev)
        for pr in range(npair):
            ht_ref[pr] = ht_new[pr]
        if combine:
            y = _rwkv_combine(
                y, [yf_ref[0, rows, ln] for ln in lanes], r, k, v, kd, laf_ref[0, 0, rows, :],
                [a2f_ref[0, :, ln] for ln in lanes], [a0f_ref[0, :, ln] for ln in lanes], kav,
                [rk_ref[:, ln] for ln in lanes], [lng_ref[:, ln] for ln in lanes],
                [lnb_ref[:, ln] for ln in lanes], cs)
        for pr in range(npair):
            o_ref[0, rows, lanes[pr]] = y[pr]
        return carry

    lax.fori_loop(0, nchunk, body, 0)

    @pl.when(c == pl.num_programs(2) - 1)
    def _():
        sfin_ref[0] = ht_ref[...]


def _rwkv_scan(rkv, small, prm, s0, rev, tblk, npair, y_fwd=None):
    _, bsz, seq, d = rkv.shape
    nblk = seq // tblk
    width = npair * LANES
    e = 1 if rev else 0
    blk = (lambda c: nblk - 1 - c) if rev else (lambda c: c)
    combine = y_fwd is not None

    def tok(m):
        return pl.BlockSpec((1, 1, tblk, width), lambda b, p, c: (m, b, blk(c), p))

    def lora(m, half):
        return pl.BlockSpec((1, 1, tblk, LANES), lambda b, p, c: (m, b, blk(c), half))

    def mat(idx):
        return pl.BlockSpec((1, LANES, width), lambda b, p, c: (idx, 0, p))

    def vec3(idx):
        return pl.BlockSpec((1, 1, width), lambda b, p, c: (idx, 0, p))

    vec = pl.BlockSpec((1, width), lambda b, p, c: (0, p))
    state = pl.BlockSpec((1, npair, LANES, LANES), lambda b, p, c: (b, p, 0, 0))
    out_tok = pl.BlockSpec((1, tblk, width), lambda b, p, c: (b, blk(c), p))
    in_specs = [tok(0), tok(1), tok(2), lora(1, e), lora(2, e), mat(e), mat(e), vec3(e), vec3(e), vec, vec, state]
    args = [rkv, rkv, rkv, small, small, prm["w2"], prm["a2"], prm["w0"], prm["a0"], prm["k_k"], prm["k_a"], s0]
    if combine:
        in_specs += [out_tok, lora(2, 0), mat(0), vec3(0), vec, vec, vec]
        args += [y_fwd, small, prm["a2"], prm["a0"], prm["r_k"], prm["ln_g"], prm["ln_b"]]
    return pl.pallas_call(
        functools.partial(_rwkv_scan_kernel, rev=rev, nchunk=tblk // CHUNK, npair=npair, combine=combine),
        grid=(bsz, d // width, nblk),
        in_specs=in_specs,
        out_specs=[out_tok, state],
        out_shape=[
            jax.ShapeDtypeStruct((bsz, seq, d), F32),
            jax.ShapeDtypeStruct((bsz, d // LANES, LANES, LANES), F32),
        ],
        scratch_shapes=[pltpu.VMEM((npair, LANES, LANES), F32)],
        compiler_params=_params(("arbitrary", "arbitrary", "arbitrary")),
        name="rwkv7_bwd" if rev else "rwkv7_fwd",
    )(*args)


def _rk_out_kernel(z_ref, gs_ref, g2_ref, wo_ref, h_ref, gt_ref, o_ref):
    gate = _dot(_bf(_sigmoid(gs_ref[0, 0])), g2_ref[...])
    y = _dot(_bf(z_ref[0] * gate), wo_ref[...])
    o_ref[0] = h_ref[0] + gt_ref[0] * y


def _rk_out(z, small, g2, wo, h, gt, tm):
    bsz, seq, d = h.shape
    row = lambda b, i: (b, i, 0)
    glora = g2.shape[0]
    return pl.pallas_call(
        _rk_out_kernel,
        grid=(bsz, seq // tm),
        in_specs=[
            pl.BlockSpec((1, tm, d), row),
            pl.BlockSpec((1, 1, tm, glora), lambda b, i: (0, b, i, 0)),
            pl.BlockSpec((glora, d), lambda b, i: (0, 0)),
            pl.BlockSpec((d, d), lambda b, i: (0, 0)),
            pl.BlockSpec((1, tm, d), row),
            pl.BlockSpec((1, 1, d), lambda b, i: (b, 0, 0)),
        ],
        out_specs=pl.BlockSpec((1, tm, d), row),
        out_shape=jax.ShapeDtypeStruct((bsz, seq, d), F32),
        compiler_params=_params(("arbitrary", "arbitrary")),
        name="rwkv_outproj",
    )(z, small, g2, wo, h, gt)


def _rope_tables(seq):
    t = jnp.arange(seq, dtype=jnp.int32)
    rows = (t // GRID_W).astype(F32)
    cols = (t % GRID_W).astype(F32)
    half = HEAD_DIM // 2
    n_freq = half // 2
    inv = ROPE_BASE ** (-jnp.arange(n_freq, dtype=F32) / n_freq)
    lane = jnp.arange(HEAD_DIM)
    pos = jnp.where((lane < half)[None, :], rows[:, None], cols[:, None])
    ang = pos * inv[lane % n_freq][None, :]
    cos, sin = jnp.cos(ang), jnp.sin(ang)
    first = ((lane % half) < n_freq)[None, :]
    return cos, jnp.where(first, -sin, 0.0), jnp.where(first, 0.0, sin)


def _pad_lanes(w, axis):
    pad = [(0, 0)] * w.ndim
    pad[axis] = (0, LANES - w.shape[axis])
    return jnp.pad(w, pad)


def _row_tile(seq, want):
    return want if seq % want == 0 else seq


def kernel(x, c, ctx, c_ctx, mod_w, mod_b, norm_mix, norm_ffn, ffn_up, ffn_down, ab_w_in, ab_w_out, attn_sink, hgrn_lb, hgrn_onorm, rk_mix, rk_wr, rk_wk, rk_wv, rk_wo, rk_w0, rk_w1, rk_w2, rk_a0, rk_a1, rk_a2, rk_g1, rk_g2, rk_kk, rk_ka, rk_rk, rk_ln_g, rk_ln_b, final_norm):
    bsz, seq, d = x.shape
    lc = ctx.shape[1]
    depth = mod_w.shape[0]
    assert bsz + 1 <= 8 and seq % 1024 == 0 and lc % CHUNK == 0

    cond8 = jnp.zeros((8, d), F32).at[:bsz].set(c).at[bsz].set(c_ctx)
    mod = _modulation(cond8, mod_w, mod_b)
    lb_all = jnp.cumsum(jax.nn.softmax(hgrn_lb.astype(F32), axis=0), axis=0)
    rope = _rope_tables(seq)

    h, hc = x, ctx
    for layer in range(depth):
        last = layer == depth - 1
        jl = layer // 2
        m_lat = mod[layer, :bsz].reshape(bsz, 1, 6, d)
        m_ctx = jnp.broadcast_to(mod[layer, bsz].reshape(1, 1, 6, d), (bsz, 1, 6, d))
        sh1, sc1, gt1, sh2, sc2, gt2 = (m_lat[:, :, i] for i in range(6))
        csh1, csc1, cgt1, csh2, csc2, cgt2 = (m_ctx[:, :, i] for i in range(6))
        g_mix = norm_mix[layer].reshape(1, d)
        g_ffn = norm_ffn[layer].reshape(1, d)
        if layer % 2 == 0:
            w_in = _bf(ab_w_in[jl])
            w_out = _bf(ab_w_out[jl])
            sink = attn_sink[jl].astype(F32)
            lb = lb_all[jl].reshape(1, B_W)
            onorm = hgrn_onorm[jl].reshape(1, B_W).astype(F32)
            n_in = w_in.shape[1]
            pc = _inproj(hc, g_mix, csc1, csh1, w_in, None, lc, 512)
            pl_ = _inproj(h, g_mix, sc1, sh1, w_in, rope, 1024, 512)
            oa = _win_attn(sink, pl_, pc)
            zeros = jnp.zeros((bsz, B_HEADS, HEAD_DIM, HEAD_DIM), F32)
            nhead = 4
            ocf, scf = _hgrn_scan(pc, lb, zeros, False, lc, nhead)
            ocb, scb = _hgrn_scan(pc, lb, zeros, True, lc, nhead)
            olf, _ = _hgrn_scan(pl_, lb, scf, False, 512, nhead)
            olb, _ = _hgrn_scan(pl_, lb, scb, True, 512, nhead)
            h = _ab_out(oa, olf, olb, pl_, onorm, w_out, h, gt1, 512)
            if not last:
                oca = _ctx_attn(sink, pc)
                hc = _ab_out(oca, ocf, ocb, pc, onorm, w_out, hc, cgt1, lc)
        else:
            w_big = _bf(jnp.stack([rk_wr[jl], rk_wk[jl], rk_wv[jl]]))
            w_small = _bf(jnp.stack([
                rk_g1[jl],
                jnp.concatenate([_pad_lanes(rk_w1[jl, 0], 1), _pad_lanes(rk_w1[jl, 1], 1)], axis=1),
                jnp.concatenate([_pad_lanes(rk_a1[jl, 0], 1), _pad_lanes(rk_a1[jl, 1], 1)], axis=1),
            ]))
            assert w_small.shape[-1] == 2 * LANES
            mix = rk_mix[jl]
            prm = dict(
                w2=_bf(_pad_lanes(rk_w2[jl], 1)), a2=_bf(_pad_lanes(rk_a2[jl], 1)),
                w0=rk_w0[jl].reshape(2, 1, d), a0=rk_a0[jl].reshape(2, 1, d),
                k_k=rk_kk[jl].reshape(1, d), k_a=rk_ka[jl].reshape(1, d), r_k=rk_rk[jl].reshape(1, d),
                ln_g=rk_ln_g[jl].reshape(1, d), ln_b=rk_ln_b[jl].reshape(1, d),
            )
            g2 = _bf(rk_g2[jl])
            wo = _bf(rk_wo[jl])
            mix = mix[jnp.array([0, 2, 3, 5, 1, 4])]
            rkv_c, sm_c = _rk_proj(hc, g_mix, csc1, csh1, mix, w_big, w_small, lc)
            rkv_l, sm_l = _rk_proj(h, g_mix, sc1, sh1, mix, w_big, w_small, 512)
            zeros = jnp.zeros((bsz, d // LANES, LANES, LANES), F32)
            npair = 16
            ycf, s_f = _rwkv_scan(rkv_c, sm_c, prm, zeros, False, lc, npair)
            zc, s_b = _rwkv_scan(rkv_c, sm_c, prm, zeros, True, lc, npair, y_fwd=ycf)
            ylf, _ = _rwkv_scan(rkv_l, sm_l, prm, s_f, False, 256, npair)
            zl, _ = _rwkv_scan(rkv_l, sm_l, prm, s_b, True, 256, npair, y_fwd=ylf)
            h = _rk_out(zl, sm_l, g2, wo, h, gt1, 512)
            if not last:
                hc = _rk_out(zc, sm_c, g2, wo, hc, cgt1, lc)
        w_up = _bf(ffn_up[layer])
        w_dn = _bf(ffn_down[layer])
        h = _mlp(h, g_ffn, sc2, sh2, gt2, w_up, w_dn, final_norm.reshape(1, d) if last else None, 1024, 512)
        if not last:
            hc = _mlp(hc, g_ffn, csc2, csh2, cgt2, w_up, w_dn, None, lc, 512)
    return h
```

```python
import functools

import jax
import jax.numpy as jnp
import numpy as np
from jax import lax
from jax.experimental import pallas as pl
from jax.experimental.pallas import tpu as pltpu

F32 = jnp.float32
BF16 = jnp.bfloat16
HIGHEST = lax.Precision.HIGHEST

LANES = 128
HEAD_DIM = 128
GRID_W = 64
WINDOW = 128
ROPE_BASE = 10000.0
A_Q_HEADS = 8
A_KV_HEADS = 2
A_GROUP = A_Q_HEADS // A_KV_HEADS
A_Q = A_Q_HEADS * HEAD_DIM
A_KV = A_KV_HEADS * HEAD_DIM
B_HEADS = 8
B_W = B_HEADS * HEAD_DIM
C_HEAD = 64
CHUNK = 64
SUB = 16
EPS = 1e-6
GN_EPS = 64e-5
VMEM_LIMIT = 58 * 1024 * 1024

NT_DIMS = (((1,), (1,)), ((), ()))
TN_DIMS = (((0,), (0,)), ((), ()))


def _dot(a, b, **kw):
    return jnp.dot(a, b, preferred_element_type=F32, **kw)


def _dot_nt(a, b):
    return lax.dot_general(a, b, NT_DIMS, preferred_element_type=F32)


def _dot_tn(a, b):
    return lax.dot_general(a, b, TN_DIMS, preferred_element_type=F32)


def _bf(x):
    return x.astype(BF16)


def _dot_split(x, m):
    hi = _bf(x)
    lo = _bf(x - hi.astype(F32))
    return _dot(hi, m) + _dot(lo, m)


def _sigmoid(x):
    return 1.0 / (1.0 + jnp.exp(-x))


def _silu(x):
    return x * _sigmoid(x)


def _normmod(x, g, sc, sh):
    ms = jnp.mean(x * x, axis=-1, keepdims=True)
    y = (x * lax.rsqrt(ms + EPS)) * g
    return y * (1.0 + sc) + sh


ROW_PIECE = 128


def _normmod_rows(x_ref, g_ref, sc_ref, sh_ref, u_ref):
    g, sc, sh = g_ref[...], sc_ref[0], sh_ref[0]
    rows = u_ref.shape[0]
    piece = min(ROW_PIECE, rows)
    for r0 in range(0, rows, piece):
        u_ref[r0:r0 + piece] = _bf(_normmod(x_ref[0, r0:r0 + piece], g, sc, sh))


def _cumsum_rows(x, rev):
    n = x.shape[0]
    row = lax.broadcasted_iota(jnp.int32, (n, 1), 0)
    s = 1
    while s < n:
        if rev:
            x = x + jnp.where(row < n - s, pltpu.roll(x, n - s, 0), 0.0)
        else:
            x = x + jnp.where(row >= s, pltpu.roll(x, s, 0), 0.0)
        s *= 2
    return x


def _params(sem):
    return pltpu.CompilerParams(dimension_semantics=sem, vmem_limit_bytes=VMEM_LIMIT)


def _mod_kernel(c_ref, w_ref, b_ref, o_ref):
    s = _bf(_silu(c_ref[...]))
    o_ref[0] = _dot(s, _bf(w_ref[0])) + b_ref[0]


def _modulation(cond8, mod_w, mod_b):
    depth, d, n = mod_w.shape
    tn = 1024
    return pl.pallas_call(
        _mod_kernel,
        grid=(depth, n // tn),
        in_specs=[
            pl.BlockSpec((8, d), lambda l, j: (0, 0)),
            pl.BlockSpec((1, d, tn), lambda l, j: (l, 0, j)),
            pl.BlockSpec((1, 1, tn), lambda l, j: (l, 0, j)),
        ],
        out_specs=pl.BlockSpec((1, 8, tn), lambda l, j: (l, 0, j)),
        out_shape=jax.ShapeDtypeStruct((depth, 8, n), F32),
        compiler_params=_params(("arbitrary", "arbitrary")),
        name="modulation",
    )(cond8, mod_w, mod_b.reshape(depth, 1, n))


def _inproj_kernel(*refs, n_rope):
    if n_rope:
        x_ref, g_ref, sc_ref, sh_ref, w_ref, cos_ref, sna_ref, snb_ref, o_ref, u_ref = refs
    else:
        x_ref, g_ref, sc_ref, sh_ref, w_ref, o_ref, u_ref = refs
    j = pl.program_id(2)

    @pl.when(j == 0)
    def _():
        _normmod_rows(x_ref, g_ref, sc_ref, sh_ref, u_ref)

    acc = _dot(u_ref[...], w_ref[...])
    if not n_rope:
        o_ref[0] = acc
        return

    per_tile = acc.shape[1] // HEAD_DIM
    n_full, n_rem = n_rope // per_tile, n_rope % per_tile

    def store(n_rot):
        cos, sna, snb = cos_ref[...], sna_ref[...], snb_ref[...]
        for hd in range(n_rot):
            sl = acc[:, hd * HEAD_DIM:(hd + 1) * HEAD_DIM]
            rot = sl * cos + pltpu.roll(sl, 96, 1) * sna + pltpu.roll(sl, 32, 1) * snb
            o_ref[0, :, hd * HEAD_DIM:(hd + 1) * HEAD_DIM] = rot
        if n_rot < per_tile:
            o_ref[0, :, n_rot * HEAD_DIM:] = acc[:, n_rot * HEAD_DIM:]

    pl.when(j < n_full)(lambda: store(per_tile))
    pl.when(j == n_full)(lambda: store(n_rem))
    pl.when(j > n_full)(lambda: store(0))


def _inproj(x, g, sc, sh, w, rope, tm, tn):
    bsz, seq, d = x.shape
    n = w.shape[1]
    n_rope = 0
    in_specs = [
        pl.BlockSpec((1, tm, d), lambda b, i, j: (b, i, 0)),
        pl.BlockSpec((1, d), lambda b, i, j: (0, 0)),
        pl.BlockSpec((1, 1, d), lambda b, i, j: (b, 0, 0)),
        pl.BlockSpec((1, 1, d), lambda b, i, j: (b, 0, 0)),
        pl.BlockSpec((d, tn), lambda b, i, j: (0, j)),
    ]
    args = [x, g, sc, sh, w]
    if rope is not None:
        n_rope = A_Q_HEADS + A_KV_HEADS
        assert tn % HEAD_DIM == 0
        in_specs += [pl.BlockSpec((tm, HEAD_DIM), lambda b, i, j: (i, 0))] * 3
        args += list(rope)
    return pl.pallas_call(
        functools.partial(_inproj_kernel, n_rope=n_rope),
        grid=(bsz, seq // tm, n // tn),
        in_specs=in_specs,
        out_specs=pl.BlockSpec((1, tm, tn), lambda b, i, j: (b, i, j)),
        out_shape=jax.ShapeDtypeStruct((bsz, seq, n), F32),
        scratch_shapes=[pltpu.VMEM((tm, d), BF16)],
        compiler_params=_params(("arbitrary", "arbitrary", "arbitrary")),
        name="ab_inproj",
    )(*args)


def _softmax_av(s_list, v_list, sink_col):
    m = sink_col
    for s in s_list:
        m = jnp.maximum(m, jnp.max(s, axis=-1, keepdims=True))
    den = jnp.exp(sink_col - m)
    out = None
    for s, v in zip(s_list, v_list):
        p = jnp.exp(s - m)
        den = den + jnp.sum(p, axis=-1, keepdims=True)
        o = _dot(_bf(p), v)
        out = o if out is None else out + o
    return out / den


def _sink_column(sink_ref, hk, rows):
    rowh = lax.broadcasted_iota(jnp.int32, (rows, 1), 0) // WINDOW
    col = jnp.full((rows, 1), sink_ref[hk * A_GROUP + A_GROUP - 1], F32)
    for g in range(A_GROUP - 1):
        col = jnp.where(rowh == g, sink_ref[hk * A_GROUP + g], col)
    return col


def _win_attn_kernel(sink_ref, q_ref, kp_ref, kc_ref, kn_ref, vp_ref, vc_ref, vn_ref,
                     ck_ref, cv_ref, o_ref):
    n = pl.program_id(1)
    nb = pl.num_programs(1)
    scale = HEAD_DIM ** -0.5
    q = q_ref[0]
    kband = jnp.concatenate([kp_ref[0], kc_ref[0], kn_ref[0]], axis=0)
    vband = jnp.concatenate([vp_ref[0], vc_ref[0], vn_ref[0]], axis=0)
    rows = A_GROUP * WINDOW
    tq = lax.broadcasted_iota(jnp.int32, (rows, 3 * WINDOW), 0) % WINDOW
    tk = lax.broadcasted_iota(jnp.int32, (rows, 3 * WINDOW), 1)
    rel = tk - WINDOW - tq
    valid = (jnp.abs(rel) <= WINDOW) & ((tk >= WINDOW) | (n > 0)) & ((tk < 2 * WINDOW) | (n < nb - 1))
    for hk in range(A_KV_HEADS):
        qs = jnp.concatenate(
            [q[:, (hk * A_GROUP + g) * HEAD_DIM:(hk * A_GROUP + g + 1) * HEAD_DIM] for g in range(A_GROUP)],
            axis=0)
        qs = _bf(qs)
        hs = slice(hk * HEAD_DIM, (hk + 1) * HEAD_DIM)
        s_win = _dot_nt(qs, _bf(kband[:, hs])) * scale
        s_win = jnp.where(valid, s_win, -jnp.inf)
        s_ctx = _dot_nt(qs, _bf(ck_ref[0][:, hs])) * scale
        o = _softmax_av([s_win, s_ctx], [_bf(vband[:, hs]), _bf(cv_ref[0][:, hs])],
                        _sink_column(sink_ref, hk, rows))
        for g in range(A_GROUP):
            h = hk * A_GROUP + g
            o_ref[0, :, h * HEAD_DIM:(h + 1) * HEAD_DIM] = _bf(o[g * WINDOW:(g + 1) * WINDOW])


def _win_attn(sink, proj, proj_ctx):
    bsz, seq, _ = proj.shape
    lc = proj_ctx.shape[1]
    nb = seq // WINDOW
    kcol, vcol = A_Q // A_KV, A_Q // A_KV + 1
    prev = lambda b, n: (b, jnp.maximum(n - 1, 0))
    nxt = lambda b, n: (b, jnp.minimum(n + 1, nb - 1))
    cur = lambda b, n: (b, n)

    def band(rowfn, col):
        return pl.BlockSpec((1, WINDOW, A_KV), lambda b, n: rowfn(b, n) + (col,))

    return pl.pallas_call(
        _win_attn_kernel,
        grid=(bsz, nb),
        in_specs=[
            pl.BlockSpec(memory_space=pltpu.SMEM),
            pl.BlockSpec((1, WINDOW, A_Q), lambda b, n: (b, n, 0)),
            band(prev, kcol), band(cur, kcol), band(nxt, kcol),
            band(prev, vcol), band(cur, vcol), band(nxt, vcol),
            pl.BlockSpec((1, lc, A_KV), lambda b, n: (b, 0, kcol)),
            pl.BlockSpec((1, lc, A_KV), lambda b, n: (b, 0, vcol)),
        ],
        out_specs=pl.BlockSpec((1, WINDOW, A_Q), lambda b, n: (b, n, 0)),
        out_shape=jax.ShapeDtypeStruct((bsz, seq, A_Q), BF16),
        compiler_params=_params(("arbitrary", "arbitrary")),
        name="window_attention",
    )(sink, proj, proj, proj, proj, proj, proj, proj, proj_ctx, proj_ctx)


def _ctx_attn_kernel(sink_ref, q_ref, k_ref, v_ref, o_ref):
    scale = HEAD_DIM ** -0.5
    q = q_ref[0]
    lc = q.shape[0]
    for h in range(A_Q_HEADS):
        hk = h // A_GROUP
        hs = slice(hk * HEAD_DIM, (hk + 1) * HEAD_DIM)
        qs = _bf(q[:, h * HEAD_DIM:(h + 1) * HEAD_DIM])
        s = _dot_nt(qs, _bf(k_ref[0][:, hs])) * scale
        sink_col = jnp.full((lc, 1), sink_ref[h], F32)
        o = _softmax_av([s], [_bf(v_ref[0][:, hs])], sink_col)
        o_ref[0, :, h * HEAD_DIM:(h + 1) * HEAD_DIM] = _bf(o)


def _ctx_attn(sink, proj_ctx):
    bsz, lc, _ = proj_ctx.shape
    kcol, vcol = A_Q // A_KV, A_Q // A_KV + 1
    return pl.pallas_call(
        _ctx_attn_kernel,
        grid=(bsz,),
        in_specs=[
            pl.BlockSpec(memory_space=pltpu.SMEM),
            pl.BlockSpec((1, lc, A_Q), lambda b: (b, 0, 0)),
            pl.BlockSpec((1, lc, A_KV), lambda b: (b, 0, kcol)),
            pl.BlockSpec((1, lc, A_KV), lambda b: (b, 0, vcol)),
        ],
        out_specs=pl.BlockSpec((1, lc, A_Q), lambda b: (b, 0, 0)),
        out_shape=jax.ShapeDtypeStruct((bsz, lc, A_Q), BF16),
        compiler_params=_params(("arbitrary",)),
        name="context_attention",
    )(sink, proj_ctx, proj_ctx, proj_ctx)


def _hgrn_chunk(bq, bi, bf, lb, st, rev):
    heads = range(len(bq))
    q = [_silu(x) for x in bq]
    v = bi
    f = [lb[h] + (1.0 - lb[h]) * _sigmoid(bf[h]) for h in heads]
    k = [1.0 - x for x in f]
    g = [jnp.log(x) for x in f]
    b = [_cumsum_rows(x, rev) for x in g]
    btot = [x[0:1] if rev else x[CHUNK - 1:CHUNK] for x in b]
    vb = [_bf(x) for x in v]
    stb = [_bf(x) for x in st]
    o = [_dot_nt(_bf(q[h] * jnp.exp(b[h])), stb[h]) for h in heads]
    khat = [_bf(k[h] * jnp.exp(btot[h] - b[h])) for h in heads]
    dst = [_dot_tn(vb[h], khat[h]) for h in heads]
    st_new = [st[h] * jnp.exp(btot[h]) + dst[h] for h in heads]
    nsub = CHUNK // SUB
    row8 = lax.broadcasted_iota(jnp.int32, (8, 1), 0)
    outs = [[] for _ in heads]
    for blk in range(nsub):
        r0 = blk * SUB
        rs = slice(r0, r0 + SUB)
        acc = [o[h][rs] for h in heads]
        if rev and blk < nsub - 1:
            ref_row, lo, hi = r0 + SUB, r0 + SUB, CHUNK
        elif (not rev) and blk > 0:
            ref_row, lo, hi = r0 - 1, 0, r0
        else:
            ref_row = None
        if ref_row is not None:
            bref = [b[h][ref_row:ref_row + 1] for h in heads]
            qn = [_bf(q[h][rs] * jnp.exp(b[h][rs] - bref[h])) for h in heads]
            kn = [_bf(k[h][lo:hi] * jnp.exp(bref[h] - b[h][lo:hi])) for h in heads]
            att = [_bf(_dot_nt(qn[h], kn[h])) for h in heads]
            acc = [acc[h] + _dot(att[h], vb[h][lo:hi]) for h in heads]
        piece = 8
        acc = [[a[p0:p0 + piece] for p0 in range(0, SUB, piece)] for a in acc]
        for s in range(SUB):
            reached = range(0, s // piece + 1) if rev else range(s // piece, SUB // piece)
            for pc in reached:
                t0 = r0 + pc * piece
                mask = (row8 + pc * piece <= s) if rev else (row8 + pc * piece >= s)
                for h in heads:
                    rel = b[h][t0:t0 + piece] - b[h][r0 + s:r0 + s + 1]
                    dec = jnp.exp(jnp.where(mask, rel, -jnp.inf))
                    w = jnp.sum(q[h][t0:t0 + piece] * dec * k[h][r0 + s:r0 + s + 1], axis=-1, keepdims=True)
                    acc[h][pc] = acc[h][pc] + w * v[h][r0 + s:r0 + s + 1]
        for h in heads:
            outs[h].extend(acc[h])
    return [jnp.concatenate(x, axis=0) for x in outs], st_new


def _hgrn_kernel(q_ref, i_ref, f_ref, lb_ref, s0_ref, o_ref, sfin_ref, st_ref, *, rev, nchunk, nhead):
    c = pl.program_id(2)

    @pl.when(c == 0)
    def _():
        st_ref[...] = s0_ref[0]

    lanes = [slice(h * HEAD_DIM, (h + 1) * HEAD_DIM) for h in range(nhead)]
    lb = [lb_ref[:, ln] for ln in lanes]

    def body(ci, carry):
        cc = (nchunk - 1 - ci) if rev else ci
        rows = pl.ds(pl.multiple_of(cc * CHUNK, CHUNK), CHUNK)
        o, st_new = _hgrn_chunk([q_ref[0, rows, ln] for ln in lanes], [i_ref[0, rows, ln] for ln in lanes],
                                [f_ref[0, rows, ln] for ln in lanes], lb,
                                [st_ref[h] for h in range(nhead)], rev)
        for h in range(nhead):
            o_ref[0, rows, lanes[h]] = o[h]
            st_ref[h] = st_new[h]
        return carry

    lax.fori_loop(0, nchunk, body, 0, unroll=2 if nchunk % 2 == 0 else 1)

    @pl.when(c == pl.num_programs(2) - 1)
    def _():
        sfin_ref[0] = st_ref[...]


def _hgrn_scan(proj, lb, s0, rev, tblk, nhead):
    bsz, seq, _ = proj.shape
    nblk = seq // tblk
    width = nhead * HEAD_DIM
    base = A_Q + 2 * A_KV
    assert base % width == 0 and B_W % width == 0
    qc, ic = base // width, (base + B_W) // width
    fc = (base + (3 if rev else 2) * B_W) // width
    blk = (lambda c: nblk - 1 - c) if rev else (lambda c: c)

    def col(c0):
        return pl.BlockSpec((1, tblk, width), lambda b, h, c: (b, blk(c), c0 + h))

    state = pl.BlockSpec((1, nhead, HEAD_DIM, HEAD_DIM), lambda b, h, c: (b, h, 0, 0))
    return pl.pallas_call(
        functools.partial(_hgrn_kernel, rev=rev, nchunk=tblk // CHUNK, nhead=nhead),
        grid=(bsz, B_HEADS // nhead, nblk),
        in_specs=[
            col(qc), col(ic), col(fc),
            pl.BlockSpec((1, width), lambda b, h, c: (0, h)),
            state,
        ],
        out_specs=[
            pl.BlockSpec((1, tblk, width), lambda b, h, c: (b, blk(c), h)),
            state,
        ],
        out_shape=[
            jax.ShapeDtypeStruct((bsz, seq, B_W), F32),
            jax.ShapeDtypeStruct((bsz, B_HEADS, HEAD_DIM, HEAD_DIM), F32),
        ],
        scratch_shapes=[pltpu.VMEM((nhead, HEAD_DIM, HEAD_DIM), F32)],
        compiler_params=_params(("arbitrary", "arbitrary", "arbitrary")),
        name="hgrn2_bwd" if rev else "hgrn2_fwd",
    )(proj, proj, proj, lb, s0)


def _ab_out_kernel(oa_ref, of_ref, ob_ref, g0_ref, g1_ref, on_ref, w_ref, h_ref, gt_ref, o_ref, lhs_ref):
    lhs_ref[:, :A_Q] = oa_ref[0]
    half = B_W // 2
    for hd in range(B_HEADS):
        sl = slice(hd * HEAD_DIM, (hd + 1) * HEAD_DIM)
        o = of_ref[0, :, sl] + ob_ref[0, :, sl]
        o = o * lax.rsqrt(jnp.mean(o * o, axis=-1, keepdims=True) + EPS)
        o = o * on_ref[:, sl]
        gref = g0_ref if hd * HEAD_DIM < half else g1_ref
        gs = slice(hd * HEAD_DIM % half, hd * HEAD_DIM % half + HEAD_DIM)
        o = o * _silu(gref[0, :, gs])
        lhs_ref[:, A_Q + hd * HEAD_DIM:A_Q + (hd + 1) * HEAD_DIM] = _bf(o)
    y = _dot(lhs_ref[...], w_ref[...])
    o_ref[0] = h_ref[0] + gt_ref[0] * y


def _ab_out(oa, of, ob, proj, onorm, w_out, h, gt, tm):
    bsz, seq, d = h.shape
    half = B_W // 2
    gcol = (A_Q + 2 * A_KV + 4 * B_W) // half
    row = lambda b, i: (b, i, 0)
    return pl.pallas_call(
        _ab_out_kernel,
        grid=(bsz, seq // tm),
        in_specs=[
            pl.BlockSpec((1, tm, A_Q), row),
            pl.BlockSpec((1, tm, B_W), row),
            pl.BlockSpec((1, tm, B_W), row),
            pl.BlockSpec((1, tm, half), lambda b, i: (b, i, gcol)),
            pl.BlockSpec((1, tm, half), lambda b, i: (b, i, gcol + 1)),
            pl.BlockSpec((1, B_W), lambda b, i: (0, 0)),
            pl.BlockSpec((A_Q + B_W, d), lambda b, i: (0, 0)),
            pl.BlockSpec((1, tm, d), row),
            pl.BlockSpec((1, 1, d), lambda b, i: (b, 0, 0)),
        ],
        out_specs=pl.BlockSpec((1, tm, d), row),
        out_shape=jax.ShapeDtypeStruct((bsz, seq, d), F32),
        scratch_shapes=[pltpu.VMEM((tm, A_Q + B_W), BF16)],
        compiler_params=_params(("arbitrary", "arbitrary")),
        name="ab_outproj",
    )(oa, of, ob, proj, proj, onorm, w_out, h, gt)


def _mlp_kernel(*refs, final):
    if final:
        x_ref, g_ref, sc_ref, sh_ref, gt_ref, wu_ref, wd_ref, fn_ref, o_ref, u_ref = refs
    else:
        x_ref, g_ref, sc_ref, sh_ref, gt_ref, wu_ref, wd_ref, o_ref, u_ref = refs
    j = pl.program_id(2)

    tm = x_ref.shape[1]

    @pl.when(j == 0)
    def _():
        _normmod_rows(x_ref, g_ref, sc_ref, sh_ref, u_ref)

    hid = jnp.maximum(_dot(u_ref[...], wu_ref[...]), 0.0)
    hid = _bf(hid * hid)
    ncol = 512

    @pl.when(j == 0)
    def _():
        for n0 in range(0, o_ref.shape[2], ncol):
            o_ref[0, :, n0:n0 + ncol] = _dot(hid, wd_ref[:, n0:n0 + ncol])

    @pl.when(j != 0)
    def _():
        for n0 in range(0, o_ref.shape[2], ncol):
            o_ref[0, :, n0:n0 + ncol] += _dot(hid, wd_ref[:, n0:n0 + ncol])

    @pl.when(j == pl.num_programs(2) - 1)
    def _():
        for r0 in range(0, tm, ROW_PIECE):
            rs = slice(r0, r0 + ROW_PIECE)
            y = x_ref[0, rs] + gt_ref[0] * o_ref[0, rs]
            if final:
                y = (y * lax.rsqrt(jnp.mean(y * y, axis=-1, keepdims=True) + EPS)) * fn_ref[...]
            o_ref[0, rs] = y


def _mlp(x, g, sc, sh, gt, w_up, w_down, final_gain, tm, tf):
    bsz, seq, d = x.shape
    dff = w_up.shape[1]
    vec = pl.BlockSpec((1, 1, d), lambda b, i, j: (b, 0, 0))
    in_specs = [
        pl.BlockSpec((1, tm, d), lambda b, i, j: (b, i, 0)),
        pl.BlockSpec((1, d), lambda b, i, j: (0, 0)),
        vec, vec, vec,
        pl.BlockSpec((d, tf), lambda b, i, j: (0, j)),
        pl.BlockSpec((tf, d), lambda b, i, j: (j, 0)),
    ]
    args = [x, g, sc, sh, gt, w_up, w_down]
    if final_gain is not None:
        in_specs.append(pl.BlockSpec((1, d), lambda b, i, j: (0, 0)))
        args.append(final_gain)
    return pl.pallas_call(
        functools.partial(_mlp_kernel, final=final_gain is not None),
        grid=(bsz, seq // tm, dff // tf),
        in_specs=in_specs,
        out_specs=pl.BlockSpec((1, tm, d), lambda b, i, j: (b, i, 0)),
        out_shape=jax.ShapeDtypeStruct((bsz, seq, d), F32),
        scratch_shapes=[pltpu.VMEM((tm, d), BF16)],
        compiler_params=_params(("arbitrary", "arbitrary", "arbitrary")),
        name="sqrelu_mlp",
    )(*args)


def _rk_proj_kernel(x_ref, xp_ref, xn_ref, g_ref, sc_ref, sh_ref, mix_ref, wb_ref, ws_ref, ob_ref, os_ref,
                    u_ref, xx_ref, lhs_ref, *, nbig, nsmall):
    i = pl.program_id(1)
    j = pl.program_id(2)
    tm = x_ref.shape[1]
    rb = 128

    def build(m, slot):
        mixrow = mix_ref[pl.ds(m, 1), :]
        for r0 in range(0, tm, rb):
            lhs_ref[slot, r0:r0 + rb] = _bf(u_ref[8 + r0:8 + r0 + rb] + xx_ref[r0:r0 + rb] * mixrow)

    @pl.when(j == 0)
    def _():
        g, sc, sh = g_ref[...], sc_ref[0], sh_ref[0]
        for r0 in range(0, tm, rb):
            u_ref[8 + r0:8 + r0 + rb] = _normmod(x_ref[0, r0:r0 + rb], g, sc, sh)
        up = _normmod(xp_ref[0], g, sc, sh)[7:8]
        un = _normmod(xn_ref[0], g, sc, sh)[0:1]
        u_ref[7:8] = jnp.where(i == 0, 0.0, up)
        u_ref[8 + tm:9 + tm] = jnp.where(i == pl.num_programs(1) - 1, 0.0, un)
        for r0 in range(0, tm, rb):
            nb = u_ref[7 + r0:7 + r0 + rb] + u_ref[9 + r0:9 + r0 + rb]
            xx_ref[r0:r0 + rb] = 0.5 * nb - u_ref[8 + r0:8 + r0 + rb]
        build(0, 0)

    @pl.when(j < nbig)
    def _():
        build(j + 1, (j + 1) % 2)
        ob_ref[0, 0] = _dot(lhs_ref[j % 2], wb_ref[0])

    @pl.when(j == nbig)
    def _():
        for s in range(nsmall):
            os_ref[s, 0] = _dot(lhs_ref[(nbig + s) % 2], ws_ref[s])
            if s + 1 < nsmall:
                build(nbig + s + 1, (nbig + s + 1) % 2)


def _rk_proj(x, g, sc, sh, mix, w_big, w_small, tm):
    bsz, seq, d = x.shape
    nbig, _, n = w_big.shape
    nsmall, _, ns = w_small.shape
    r8 = tm // 8
    last8 = seq // 8 - 1
    vec = pl.BlockSpec((1, 1, d), lambda b, i, j: (b, 0, 0))
    return pl.pallas_call(
        functools.partial(_rk_proj_kernel, nbig=nbig, nsmall=nsmall),
        grid=(bsz, seq // tm, nbig + 1),
        in_specs=[
            pl.BlockSpec((1, tm, d), lambda b, i, j: (b, i, 0)),
            pl.BlockSpec((1, 8, d), lambda b, i, j: (b, jnp.maximum(i * r8 - 1, 0), 0)),
            pl.BlockSpec((1, 8, d), lambda b, i, j: (b, jnp.minimum((i + 1) * r8, last8), 0)),
            pl.BlockSpec((1, d), lambda b, i, j: (0, 0)),
            vec, vec,
            pl.BlockSpec((nbig + nsmall, d), lambda b, i, j: (0, 0)),
            pl.BlockSpec((1, d, n), lambda b, i, j: (jnp.minimum(j, nbig - 1), 0, 0)),
            pl.BlockSpec((nsmall, d, ns), lambda b, i, j: (0, 0, 0)),
        ],
        out_specs=[
            pl.BlockSpec((1, 1, tm, n), lambda b, i, j: (jnp.minimum(j, nbig - 1), b, i, 0)),
            pl.BlockSpec((nsmall, 1, tm, ns), lambda b, i, j: (0, b, i, 0)),
        ],
        out_shape=[
            jax.ShapeDtypeStruct((nbig, bsz, seq, n), F32),
            jax.ShapeDtypeStruct((nsmall, bsz, seq, ns), F32),
        ],
        scratch_shapes=[pltpu.VMEM((tm + 16, d), F32), pltpu.VMEM((tm, d), F32), pltpu.VMEM((2, tm, d), BF16)],
        compiler_params=_params(("arbitrary", "arbitrary", "arbitrary")),
        name="rwkv_proj",
    )(x, x, x, g, sc, sh, mix, w_big, w_small)


DECAY_SCALE = float(np.exp(-0.5))


def _pair_consts(rev):
    n = 2 * CHUNK
    ri = lax.broadcasted_iota(jnp.int32, (n, n), 0)
    ci = lax.broadcasted_iota(jnp.int32, (n, n), 1)
    same = (ri // CHUNK) == (ci // CHUNK)
    rt, ct = ri % CHUNK, ci % CHUNK
    strict = same & ((ct > rt) if rev else (ct < rt))
    incl = same & ((ct >= rt) if rev else (ct <= rt))
    eye = jnp.where(ri == ci, 1.0, 0.0).astype(F32)
    lane = lax.broadcasted_iota(jnp.int32, (1, LANES), 1)
    lane_lo = lane < C_HEAD
    m0 = jnp.where(lane_lo, 1.0, 0.0).astype(F32)
    m1 = 1.0 - m0
    blk2 = (ri // 2) == (ci // 2)
    level = {}
    m = 2
    while m < CHUNK:
        if m % 8:
            rsel, csel = ri, ci
        else:
            rc = lax.broadcasted_iota(jnp.int32, (n // 2, n), 0)
            csel = lax.broadcasted_iota(jnp.int32, (n // 2, n), 1)
            rsel = (rc // m) * (2 * m) + (0 if rev else m) + rc % m
        level[m] = ((rsel // (2 * m)) == (csel // (2 * m))) & ((rsel // m) != (csel // m))
        m *= 2
    return dict(strict=strict, incl=incl, eye=eye, m0=m0, m1=m1, lane_lo=lane_lo, blk2=blk2, level=level)


def _head_sums(x, cs):
    s0 = jnp.sum(x * cs["m0"], axis=-1, keepdims=True)
    s1 = jnp.sum(x * cs["m1"], axis=-1, keepdims=True)
    return jnp.where(cs["lane_lo"], s0, s1)


def _unit_tri_inverse(n, cs, rev):
    size = 2 * CHUNK
    t = [cs["eye"] + jnp.where(cs["blk2"], x, 0.0) for x in n]
    nb = [_bf(x) for x in n]
    m = 2
    while m < CHUNK:
        tb = [_bf(x) for x in t]
        sel = cs["level"][m]
        if m % 8:
            q = [_bf(_dot(a, b)) for a, b in zip(tb, nb)]
            q = [_dot(a, b) for a, b in zip(q, tb)]
            t = [a + jnp.where(sel, b, 0.0) for a, b in zip(t, q)]
        else:
            starts = range(0 if rev else m, size, 2 * m)
            pick = lambda x: jnp.concatenate([x[r0:r0 + m] for r0 in starts], axis=0)
            q = [_bf(_dot(_bf(pick(x)), b)) for x, b in zip(t, nb)]
            q = [_dot(a, b) for a, b in zip(q, tb)]
            t = [_add_rows(x, jnp.where(sel, y, 0.0), starts, m) for x, y in zip(t, q)]
        m *= 2
    return t


def _add_rows(x, upd, starts, m):
    pieces, pos = [], 0
    for k, r0 in enumerate(starts):
        if r0 > pos:
            pieces.append(x[pos:r0])
        pieces.append(x[r0:r0 + m] + upd[k * m:(k + 1) * m])
        pos = r0 + m
    if pos < x.shape[0]:
        pieces.append(x[pos:])
    return jnp.concatenate(pieces, axis=0)


def _halves(top, bottom):
    return jnp.concatenate([top[:CHUNK], bottom[CHUNK:]], axis=0)


def _stack2(x, cs):
    return jnp.concatenate([x * cs["m0"], x * cs["m1"]], axis=0)


def _rwkv_chunk(r, k, v, lw, la, w2, a2, w0, a0, kkv, kav, ht, cs, rev):
    pairs = range(len(r))
    tl, lab = _bf(jnp.tanh(lw)), _bf(la)
    z = [w0[p] + _dot(tl, w2[p]) for p in pairs]
    za = [_dot(lab, a2[p]) for p in pairs]
    kkr = [k[p] * kkv[p] for p in pairs]
    ss = [_head_sums(x * x, cs) for x in kkr]
    ld = [-DECAY_SCALE * _sigmoid(x) for x in z]
    cl = [_cumsum_rows(x, rev) for x in ld]
    a = [_sigmoid(a0[p] + za[p]) for p in pairs]
    kk = [x / jnp.maximum(jnp.sqrt(s), 1e-12) for x, s in zip(kkr, ss)]
    kd = [k[p] * (1.0 + (a[p] - 1.0) * kav[p]) for p in pairs]
    bb = [x * y for x, y in zip(kk, a)]
    ctot = [x[0:1] if rev else x[CHUNK - 1:CHUNK] for x in cl]
    e_neg = [jnp.exp(-x) for x in cl]
    e_tail = [jnp.exp(c - x) for c, x in zip(ctot, cl)]
    at = [_bf(_stack2(-kk[p] * jnp.exp(cl[p] - ld[p]), cs)) for p in pairs]
    rt = [_bf(_stack2(r[p] * jnp.exp(cl[p]), cs)) for p in pairs]
    rhs = [_bf(jnp.concatenate([bb[p] * e_neg[p], kd[p] * e_neg[p]], axis=0)) for p in pairs]
    g = [_dot_nt(jnp.concatenate([x, y], axis=0), w) for x, y, w in zip(at, rt, rhs)]
    g1 = [x[:2 * CHUNK] for x in g]
    g2 = [x[2 * CHUNK:] for x in g]
    strict, incl = cs["strict"], cs["incl"]
    g1r = [pltpu.roll(x, C_HEAD, 1) for x in g1]
    n_ab = [jnp.where(strict, _halves(x, y), 0.0) for x, y in zip(g1, g1r)]
    n_ak = [_bf(jnp.where(strict, _halves(y, x), 0.0)) for x, y in zip(g1, g1r)]
    t = _unit_tri_inverse(n_ab, cs, rev)
    vs = [_bf(_stack2(x, cs)) for x in v]
    hkv = [_bf(x.T) for x in ht]
    xx = [_bf(_dot(jnp.concatenate([at[p], n_ak[p]], axis=1), jnp.concatenate([hkv[p], vs[p]], axis=0)))
          for p in pairs]
    u =[_bf(_dot(_bf(a_), b_)) for a_, b_ in zip(t, xx)]
    bk = [_bf(jnp.concatenate([_stack2(bb[p] * e_tail[p], cs), _stack2(kd[p] * e_tail[p], cs)], axis=0))
          for p in pairs]
    uv = [jnp.concatenate([a_, b_], axis=0) for a_, b_ in zip(u, vs)]
    dh = [_dot_tn(a_, b_) for a_, b_ in zip(uv, bk)]
    ht_new = [ht[p] * jnp.exp(ctot[p]) + dh[p] for p in pairs]
    g2r = [pltpu.roll(x, C_HEAD, 1) for x in g2]
    n_rb = [_bf(jnp.where(incl, _halves(x, y), 0.0)) for x, y in zip(g2, g2r)]
    n_rk = [_bf(jnp.where(incl, _halves(y, x), 0.0)) for x, y in zip(g2, g2r)]
    ysum = [_dot(jnp.concatenate([rt[p], n_rb[p], n_rk[p]], axis=1),
                 jnp.concatenate([hkv[p], u[p], vs[p]], axis=0)) for p in pairs]
    y = [x[:CHUNK] + x[CHUNK:] for x in ysum]
    return y, ht_new, kd


def _rwkv_combine(y, yf, r, k, v, kd, laf, a2f, a0f, kav, rkv, lng, lnb, cs):
    pairs = range(len(y))
    inv_n = 1.0 / C_HEAD
    lafb = _bf(laf)
    zf = [_dot(lafb, a2f[p]) for p in pairs]
    ysum = [a + b for a, b in zip(y, yf)]
    mu = [_head_sums(x, cs) * inv_n for x in ysum]
    dev = [a - b for a, b in zip(ysum, mu)]
    var = [_head_sums(x * x, cs) * inv_n for x in dev]
    a_f = [_sigmoid(a0f[p] + zf[p]) for p in pairs]
    kd_f = [k[p] * (1.0 + (a_f[p] - 1.0) * kav[p]) for p in pairs]
    bsum = [_head_sums(r[p] * (kd_f[p] + kd[p]) * rkv[p], cs) for p in pairs]
    return [dev[p] * lax.rsqrt(var[p] + GN_EPS) * lng[p] + lnb[p] + bsum[p] * v[p] for p in pairs]


def _rwkv_scan_kernel(*refs, rev, nchunk, npair, combine):
    if combine:
        (r_ref, k_ref, v_ref, lw_ref, la_ref, w2_ref, a2_ref, w0_ref, a0_ref, kk_ref, ka_ref, s0_ref,
         yf_ref, laf_ref, a2f_ref, a0f_ref, rk_ref, lng_ref, lnb_ref, o_ref, sfin_ref, ht_ref) = refs
    else:
        (r_ref, k_ref, v_ref, lw_ref, la_ref, w2_ref, a2_ref, w0_ref, a0_ref, kk_ref, ka_ref, s0_ref,
         o_ref, sfin_ref, ht_ref) = refs
    c = pl.program_id(2)

    @pl.when(c == 0)
    def _():
        ht_ref[...] = s0_ref[0]

    cs = _pair_consts(rev)

    def body(ci, carry):
        cc = (nchunk - 1 - ci) if rev else ci
        rows = pl.ds(pl.multiple_of(cc * CHUNK, CHUNK), CHUNK)
        lw = lw_ref[0, 0, rows, :]
        la = la_ref[0, 0, rows, :]
        lanes = [slice(pr * LANES, (pr + 1) * LANES) for pr in range(npair)]
        r = [r_ref[0, 0, rows, ln] for ln in lanes]
        k = [k_ref[0, 0, rows, ln] for ln in lanes]
        v = [v_ref[0, 0, rows, ln] for ln in lanes]
        kav = [ka_ref[:, ln] for ln in lanes]
        y, ht_new, kd = _rwkv_chunk(
            r, k, v, lw, la, [w2_ref[0, :, ln] for ln in lanes], [a2_ref[0, :, ln] for ln in lanes],
            [w0_ref[0, :, ln] for ln in lanes], [a0_ref[0, :, ln] for ln in lanes],
            [kk_ref[:, ln] for ln in lanes], kav, [ht_ref[pr] for pr in range(npair)], cs, rev)
        for pr in range(npair):
            ht_ref[pr] = ht_new[pr]
        if combine:
            y = _rwkv_combine(
                y, [yf_ref[0, rows, ln] for ln in lanes], r, k, v, kd, laf_ref[0, 0, rows, :],
                [a2f_ref[0, :, ln] for ln in lanes], [a0f_ref[0, :, ln] for ln in lanes], kav,
                [rk_ref[:, ln] for ln in lanes], [lng_ref[:, ln] for ln in lanes],
                [lnb_ref[:, ln] for ln in lanes], cs)
        for pr in range(npair):
            o_ref[0, rows, lanes[pr]] = y[pr]
        return carry

    lax.fori_loop(0, nchunk, body, 0, unroll=2 if nchunk % 2 == 0 else 1)

    @pl.when(c == pl.num_programs(2) - 1)
    def _():
        sfin_ref[0] = ht_ref[...]


def _rwkv_scan(rkv, small, prm, s0, rev, tblk, npair, y_fwd=None):
    _, bsz, seq, d = rkv.shape
    nblk = seq // tblk
    width = npair * LANES
    e = 1 if rev else 0
    blk = (lambda c: nblk - 1 - c) if rev else (lambda c: c)
    combine = y_fwd is not None

    def tok(m):
        return pl.BlockSpec((1, 1, tblk, width), lambda b, p, c: (m, b, blk(c), p))

    def lora(m, half):
        return pl.BlockSpec((1, 1, tblk, LANES), lambda b, p, c: (m, b, blk(c), half))

    def mat(idx):
        return pl.BlockSpec((1, LANES, width), lambda b, p, c: (idx, 0, p))

    def vec3(idx):
        return pl.BlockSpec((1, 1, width), lambda b, p, c: (idx, 0, p))

    vec = pl.BlockSpec((1, width), lambda b, p, c: (0, p))
    state = pl.BlockSpec((1, npair, LANES, LANES), lambda b, p, c: (b, p, 0, 0))
    out_tok = pl.BlockSpec((1, tblk, width), lambda b, p, c: (b, blk(c), p))
    in_specs = [tok(0), tok(1), tok(2), lora(1, e), lora(2, e), mat(e), mat(e), vec3(e), vec3(e), vec, vec, state]
    args = [rkv, rkv, rkv, small, small, prm["w2"], prm["a2"], prm["w0"], prm["a0"], prm["k_k"], prm["k_a"], s0]
    if combine:
        in_specs += [out_tok, lora(2, 0), mat(0), vec3(0), vec, vec, vec]
        args += [y_fwd, small, prm["a2"], prm["a0"], prm["r_k"], prm["ln_g"], prm["ln_b"]]
    return pl.pallas_call(
        functools.partial(_rwkv_scan_kernel, rev=rev, nchunk=tblk // CHUNK, npair=npair, combine=combine),
        grid=(bsz, d // width, nblk),
        in_specs=in_specs,
        out_specs=[out_tok, state],
        out_shape=[
            jax.ShapeDtypeStruct((bsz, seq, d), F32),
            jax.ShapeDtypeStruct((bsz, d // LANES, LANES, LANES), F32),
        ],
        scratch_shapes=[pltpu.VMEM((npair, LANES, LANES), F32)],
        compiler_params=_params(("arbitrary", "arbitrary", "arbitrary")),
        name="rwkv7_bwd" if rev else "rwkv7_fwd",
    )(*args)


def _rk_out_kernel(z_ref, gs_ref, g2_ref, wo_ref, h_ref, gt_ref, o_ref):
    gate = _dot(_bf(_sigmoid(gs_ref[0, 0])), g2_ref[...])
    y = _dot(_bf(z_ref[0] * gate), wo_ref[...])
    o_ref[0] = h_ref[0] + gt_ref[0] * y


def _rk_out(z, small, g2, wo, h, gt, tm):
    bsz, seq, d = h.shape
    row = lambda b, i: (b, i, 0)
    glora = g2.shape[0]
    return pl.pallas_call(
        _rk_out_kernel,
        grid=(bsz, seq // tm),
        in_specs=[
            pl.BlockSpec((1, tm, d), row),
            pl.BlockSpec((1, 1, tm, glora), lambda b, i: (0, b, i, 0)),
            pl.BlockSpec((glora, d), lambda b, i: (0, 0)),
            pl.BlockSpec((d, d), lambda b, i: (0, 0)),
            pl.BlockSpec((1, tm, d), row),
            pl.BlockSpec((1, 1, d), lambda b, i: (b, 0, 0)),
        ],
        out_specs=pl.BlockSpec((1, tm, d), row),
        out_shape=jax.ShapeDtypeStruct((bsz, seq, d), F32),
        compiler_params=_params(("arbitrary", "arbitrary")),
        name="rwkv_outproj",
    )(z, small, g2, wo, h, gt)


def _rope_tables(seq):
    t = jnp.arange(seq, dtype=jnp.int32)
    rows = (t // GRID_W).astype(F32)
    cols = (t % GRID_W).astype(F32)
    half = HEAD_DIM // 2
    n_freq = half // 2
    inv = ROPE_BASE ** (-jnp.arange(n_freq, dtype=F32) / n_freq)
    lane = jnp.arange(HEAD_DIM)
    pos = jnp.where((lane < half)[None, :], rows[:, None], cols[:, None])
    ang = pos * inv[lane % n_freq][None, :]
    cos, sin = jnp.cos(ang), jnp.sin(ang)
    first = ((lane % half) < n_freq)[None, :]
    return cos, jnp.where(first, -sin, 0.0), jnp.where(first, 0.0, sin)


def _pad_lanes(w, axis):
    pad = [(0, 0)] * w.ndim
    pad[axis] = (0, LANES - w.shape[axis])
    return jnp.pad(w, pad)


def _row_tile(seq, want):
    return want if seq % want == 0 else seq


def kernel(x, c, ctx, c_ctx, mod_w, mod_b, norm_mix, norm_ffn, ffn_up, ffn_down, ab_w_in, ab_w_out, attn_sink, hgrn_lb, hgrn_onorm, rk_mix, rk_wr, rk_wk, rk_wv, rk_wo, rk_w0, rk_w1, rk_w2, rk_a0, rk_a1, rk_a2, rk_g1, rk_g2, rk_kk, rk_ka, rk_rk, rk_ln_g, rk_ln_b, final_norm):
    bsz, seq, d = x.shape
    lc = ctx.shape[1]
    depth = mod_w.shape[0]
    assert bsz + 1 <= 8 and seq % 1024 == 0 and lc % CHUNK == 0

    cond8 = jnp.zeros((8, d), F32).at[:bsz].set(c).at[bsz].set(c_ctx)
    mod = _modulation(cond8, mod_w, mod_b)
    lb_all = jnp.cumsum(jax.nn.softmax(hgrn_lb.astype(F32), axis=0), axis=0)
    rope = _rope_tables(seq)

    h, hc = x, ctx
    for layer in range(depth):
        last = layer == depth - 1
        jl = layer // 2
        m_lat = mod[layer, :bsz].reshape(bsz, 1, 6, d)
        m_ctx = jnp.broadcast_to(mod[layer, bsz].reshape(1, 1, 6, d), (bsz, 1, 6, d))
        sh1, sc1, gt1, sh2, sc2, gt2 = (m_lat[:, :, i] for i in range(6))
        csh1, csc1, cgt1, csh2, csc2, cgt2 = (m_ctx[:, :, i] for i in range(6))
        g_mix = norm_mix[layer].reshape(1, d)
        g_ffn = norm_ffn[layer].reshape(1, d)
        if layer % 2 == 0:
            w_in = _bf(ab_w_in[jl])
            w_out = _bf(ab_w_out[jl])
            sink = attn_sink[jl].astype(F32)
            lb = lb_all[jl].reshape(1, B_W)
            onorm = hgrn_onorm[jl].reshape(1, B_W).astype(F32)
            n_in = w_in.shape[1]
            pc = _inproj(hc, g_mix, csc1, csh1, w_in, None, lc, 512)
            pl_ = _inproj(h, g_mix, sc1, sh1, w_in, rope, 1024, 512)
            oa = _win_attn(sink, pl_, pc)
            zeros = jnp.zeros((bsz, B_HEADS, HEAD_DIM, HEAD_DIM), F32)
            nhead = 4
            ocf, scf = _hgrn_scan(pc, lb, zeros, False, lc, nhead)
            ocb, scb = _hgrn_scan(pc, lb, zeros, True, lc, nhead)
            olf, _ = _hgrn_scan(pl_, lb, scf, False, 512, nhead)
            olb, _ = _hgrn_scan(pl_, lb, scb, True, 512, nhead)
            h = _ab_out(oa, olf, olb, pl_, onorm, w_out, h, gt1, 512)
            if not last:
                oca = _ctx_attn(sink, pc)
                hc = _ab_out(oca, ocf, ocb, pc, onorm, w_out, hc, cgt1, lc)
        else:
            w_big = _bf(jnp.stack([rk_wr[jl], rk_wk[jl], rk_wv[jl]]))
            w_small = _bf(jnp.stack([
                rk_g1[jl],
                jnp.concatenate([_pad_lanes(rk_w1[jl, 0], 1), _pad_lanes(rk_w1[jl, 1], 1)], axis=1),
                jnp.concatenate([_pad_lanes(rk_a1[jl, 0], 1), _pad_lanes(rk_a1[jl, 1], 1)], axis=1),
            ]))
            assert w_small.shape[-1] == 2 * LANES
            mix = rk_mix[jl]
            prm = dict(
                w2=_bf(_pad_lanes(rk_w2[jl], 1)), a2=_bf(_pad_lanes(rk_a2[jl], 1)),
                w0=rk_w0[jl].reshape(2, 1, d), a0=rk_a0[jl].reshape(2, 1, d),
                k_k=rk_kk[jl].reshape(1, d), k_a=rk_ka[jl].reshape(1, d), r_k=rk_rk[jl].reshape(1, d),
                ln_g=rk_ln_g[jl].reshape(1, d), ln_b=rk_ln_b[jl].reshape(1, d),
            )
            g2 = _bf(rk_g2[jl])
            wo = _bf(rk_wo[jl])
            mix = mix[jnp.array([0, 2, 3, 5, 1, 4])]
            rkv_c, sm_c = _rk_proj(hc, g_mix, csc1, csh1, mix, w_big, w_small, lc)
            rkv_l, sm_l = _rk_proj(h, g_mix, sc1, sh1, mix, w_big, w_small, 512)
            zeros = jnp.zeros((bsz, d // LANES, LANES, LANES), F32)
            npair = 16
            ycf, s_f = _rwkv_scan(rkv_c, sm_c, prm, zeros, False, lc, npair)
            zc, s_b = _rwkv_scan(rkv_c, sm_c, prm, zeros, True, lc, npair, y_fwd=ycf)
            ylf, _ = _rwkv_scan(rkv_l, sm_l, prm, s_f, False, 256, npair)
            zl, _ = _rwkv_scan(rkv_l, sm_l, prm, s_b, True, 256, npair, y_fwd=ylf)
            h = _rk_out(zl, sm_l, g2, wo, h, gt1, 512)
            if not last:
                hc = _rk_out(zc, sm_c, g2, wo, hc, cgt1, lc)
        w_up = _bf(ffn_up[layer])
        w_dn = _bf(ffn_down[layer])
        h = _mlp(h, g_ffn, sc2, sh2, gt2, w_up, w_dn, final_norm.reshape(1, d) if last else None, 1024, 512)
        if not last:
            hc = _mlp(hc, g_ffn, csc2, csh2, cgt2, w_up, w_dn, None, lc, 512)
    return h
```

```python
import functools

import jax
import jax.numpy as jnp
import numpy as np
from jax import lax
from jax.experimental import pallas as pl
from jax.experimental.pallas import tpu as pltpu

F32 = jnp.float32
BF16 = jnp.bfloat16
HIGHEST = lax.Precision.HIGHEST

LANES = 128
HEAD_DIM = 128
GRID_W = 64
WINDOW = 128
ROPE_BASE = 10000.0
A_Q_HEADS = 8
A_KV_HEADS = 2
A_GROUP = A_Q_HEADS // A_KV_HEADS
A_Q = A_Q_HEADS * HEAD_DIM
A_KV = A_KV_HEADS * HEAD_DIM
B_HEADS = 8
B_W = B_HEADS * HEAD_DIM
C_HEAD = 64
CHUNK = 64
SUB = 16
EPS = 1e-6
GN_EPS = 64e-5
VMEM_LIMIT = 58 * 1024 * 1024

NT_DIMS = (((1,), (1,)), ((), ()))
TN_DIMS = (((0,), (0,)), ((), ()))


def _dot(a, b, **kw):
    return jnp.dot(a, b, preferred_element_type=F32, **kw)


def _dot_nt(a, b):
    return lax.dot_general(a, b, NT_DIMS, preferred_element_type=F32)


def _dot_tn(a, b):
    return lax.dot_general(a, b, TN_DIMS, preferred_element_type=F32)


def _bf(x):
    return x.astype(BF16)


def _dot_split(x, m):
    hi = _bf(x)
    lo = _bf(x - hi.astype(F32))
    return _dot(hi, m) + _dot(lo, m)


def _sigmoid(x):
    return 1.0 / (1.0 + jnp.exp(-x))


def _silu(x):
    return x * _sigmoid(x)


def _normmod(x, g, sc, sh):
    return _normscale(x, g * (1.0 + sc), sh)


def _normscale(x, gain, shift):
    ms = jnp.mean(x * x, axis=-1, keepdims=True)
    return (x * lax.rsqrt(ms + EPS)) * gain + shift


ROW_PIECE = 16


def _for_row_pieces(rows, fn):
    def body(p, carry):
        fn(pl.ds(pl.multiple_of(p * ROW_PIECE, ROW_PIECE), ROW_PIECE))
        return carry

    lax.fori_loop(0, rows // ROW_PIECE, body, 0, unroll=8)


def _normmod_rows(x_ref, g_ref, sc_ref, sh_ref, u_ref, row0=0):
    gain, shift = g_ref[...] * (1.0 + sc_ref[0]), sh_ref[0]

    def piece(rs):
        dst = rs if row0 == 0 else pl.ds(rs.start + row0, ROW_PIECE)
        u_ref[dst] = _normscale(x_ref[0, rs], gain, shift).astype(u_ref.dtype)

    _for_row_pieces(x_ref.shape[1], piece)


def _cumsum_rows(x, rev):
    n = x.shape[0]
    row = lax.broadcasted_iota(jnp.int32, (n, 1), 0)
    s = 1
    while s < n:
        if rev:
            x = x + jnp.where(row < n - s, pltpu.roll(x, n - s, 0), 0.0)
        else:
            x = x + jnp.where(row >= s, pltpu.roll(x, s, 0), 0.0)
        s *= 2
    return x


def _params(sem):
    return pltpu.CompilerParams(dimension_semantics=sem, vmem_limit_bytes=VMEM_LIMIT)


def _mod_kernel(c_ref, w_ref, b_ref, o_ref):
    s = _bf(_silu(c_ref[...]))
    o_ref[0] = _dot(s, _bf(w_ref[0])) + b_ref[0]


def _modulation(cond8, mod_w, mod_b):
    depth, d, n = mod_w.shape
    tn = 1024
    return pl.pallas_call(
        _mod_kernel,
        grid=(depth, n // tn),
        in_specs=[
            pl.BlockSpec((8, d), lambda l, j: (0, 0)),
            pl.BlockSpec((1, d, tn), lambda l, j: (l, 0, j)),
            pl.BlockSpec((1, 1, tn), lambda l, j: (l, 0, j)),
        ],
        out_specs=pl.BlockSpec((1, 8, tn), lambda l, j: (l, 0, j)),
        out_shape=jax.ShapeDtypeStruct((depth, 8, n), F32),
        compiler_params=_params(("arbitrary", "arbitrary")),
        name="modulation",
    )(cond8, mod_w, mod_b.reshape(depth, 1, n))


def _inproj_kernel(*refs, n_rope):
    if n_rope:
        x_ref, g_ref, sc_ref, sh_ref, w_ref, cos_ref, sna_ref, snb_ref, o_ref, u_ref = refs
    else:
        x_ref, g_ref, sc_ref, sh_ref, w_ref, o_ref, u_ref = refs
    j = pl.program_id(2)

    @pl.when(j == 0)
    def _():
        _normmod_rows(x_ref, g_ref, sc_ref, sh_ref, u_ref)

    acc = _dot(u_ref[...], w_ref[...])
    if not n_rope:
        o_ref[0] = acc
        return

    per_tile = acc.shape[1] // HEAD_DIM
    n_full, n_rem = n_rope // per_tile, n_rope % per_tile

    def store(n_rot):
        cos, sna, snb = cos_ref[...], sna_ref[...], snb_ref[...]
        for hd in range(n_rot):
            sl = acc[:, hd * HEAD_DIM:(hd + 1) * HEAD_DIM]
            rot = sl * cos + pltpu.roll(sl, 96, 1) * sna + pltpu.roll(sl, 32, 1) * snb
            o_ref[0, :, hd * HEAD_DIM:(hd + 1) * HEAD_DIM] = rot
        if n_rot < per_tile:
            o_ref[0, :, n_rot * HEAD_DIM:] = acc[:, n_rot * HEAD_DIM:]

    pl.when(j < n_full)(lambda: store(per_tile))
    pl.when(j == n_full)(lambda: store(n_rem))
    pl.when(j > n_full)(lambda: store(0))


def _inproj(x, g, sc, sh, w, rope, tm, tn):
    bsz, seq, d = x.shape
    n = w.shape[1]
    n_rope = 0
    in_specs = [
        pl.BlockSpec((1, tm, d), lambda b, i, j: (b, i, 0)),
        pl.BlockSpec((1, d), lambda b, i, j: (0, 0)),
        pl.BlockSpec((1, 1, d), lambda b, i, j: (b, 0, 0)),
        pl.BlockSpec((1, 1, d), lambda b, i, j: (b, 0, 0)),
        pl.BlockSpec((d, tn), lambda b, i, j: (0, j)),
    ]
    args = [x, g, sc, sh, w]
    if rope is not None:
        n_rope = A_Q_HEADS + A_KV_HEADS
        assert tn % HEAD_DIM == 0
        in_specs += [pl.BlockSpec((tm, HEAD_DIM), lambda b, i, j: (i, 0))] * 3
        args += list(rope)
    return pl.pallas_call(
        functools.partial(_inproj_kernel, n_rope=n_rope),
        grid=(bsz, seq // tm, n // tn),
        in_specs=in_specs,
        out_specs=pl.BlockSpec((1, tm, tn), lambda b, i, j: (b, i, j)),
        out_shape=jax.ShapeDtypeStruct((bsz, seq, n), F32),
        scratch_shapes=[pltpu.VMEM((tm, d), BF16)],
        compiler_params=_params(("arbitrary", "arbitrary", "arbitrary")),
        name="ab_inproj",
    )(*args)


def _softmax_av(s_list, v_list, sink_col):
    m = sink_col
    for s in s_list:
        m = jnp.maximum(m, jnp.max(s, axis=-1, keepdims=True))
    den = jnp.exp(sink_col - m)
    out = None
    for s, v in zip(s_list, v_list):
        p = jnp.exp(s - m)
        den = den + jnp.sum(p, axis=-1, keepdims=True)
        o = _dot(_bf(p), v)
        out = o if out is None else out + o
    return out / den


def _sink_column(sink_ref, hk, rows):
    rowh = lax.broadcasted_iota(jnp.int32, (rows, 1), 0) // WINDOW
    col = jnp.full((rows, 1), sink_ref[hk * A_GROUP + A_GROUP - 1], F32)
    for g in range(A_GROUP - 1):
        col = jnp.where(rowh == g, sink_ref[hk * A_GROUP + g], col)
    return col


def _win_attn_kernel(sink_ref, q_ref, kp_ref, kc_ref, kn_ref, vp_ref, vc_ref, vn_ref,
                     ck_ref, cv_ref, o_ref):
    n = pl.program_id(1)
    nb = pl.num_programs(1)
    scale = HEAD_DIM ** -0.5
    q = q_ref[0]
    kband = jnp.concatenate([kp_ref[0], kc_ref[0], kn_ref[0]], axis=0)
    vband = jnp.concatenate([vp_ref[0], vc_ref[0], vn_ref[0]], axis=0)
    rows = A_GROUP * WINDOW
    tq = lax.broadcasted_iota(jnp.int32, (rows, 3 * WINDOW), 0) % WINDOW
    tk = lax.broadcasted_iota(jnp.int32, (rows, 3 * WINDOW), 1)
    rel = tk - WINDOW - tq
    valid = (jnp.abs(rel) <= WINDOW) & ((tk >= WINDOW) | (n > 0)) & ((tk < 2 * WINDOW) | (n < nb - 1))
    for hk in range(A_KV_HEADS):
        qs = jnp.concatenate(
            [q[:, (hk * A_GROUP + g) * HEAD_DIM:(hk * A_GROUP + g + 1) * HEAD_DIM] for g in range(A_GROUP)],
            axis=0)
        qs = _bf(qs)
        hs = slice(hk * HEAD_DIM, (hk + 1) * HEAD_DIM)
        s_win = _dot_nt(qs, _bf(kband[:, hs])) * scale
        s_win = jnp.where(valid, s_win, -jnp.inf)
        s_ctx = _dot_nt(qs, _bf(ck_ref[0][:, hs])) * scale
        o = _softmax_av([s_win, s_ctx], [_bf(vband[:, hs]), _bf(cv_ref[0][:, hs])],
                        _sink_column(sink_ref, hk, rows))
        for g in range(A_GROUP):
            h = hk * A_GROUP + g
            o_ref[0, :, h * HEAD_DIM:(h + 1) * HEAD_DIM] = _bf(o[g * WINDOW:(g + 1) * WINDOW])


def _win_attn(sink, proj, proj_ctx):
    bsz, seq, _ = proj.shape
    lc = proj_ctx.shape[1]
    nb = seq // WINDOW
    kcol, vcol = A_Q // A_KV, A_Q // A_KV + 1
    prev = lambda b, n: (b, jnp.maximum(n - 1, 0))
    nxt = lambda b, n: (b, jnp.minimum(n + 1, nb - 1))
    cur = lambda b, n: (b, n)

    def band(rowfn, col):
        return pl.BlockSpec((1, WINDOW, A_KV), lambda b, n: rowfn(b, n) + (col,))

    return pl.pallas_call(
        _win_attn_kernel,
        grid=(bsz, nb),
        in_specs=[
            pl.BlockSpec(memory_space=pltpu.SMEM),
            pl.BlockSpec((1, WINDOW, A_Q), lambda b, n: (b, n, 0)),
            band(prev, kcol), band(cur, kcol), band(nxt, kcol),
            band(prev, vcol), band(cur, vcol), band(nxt, vcol),
            pl.BlockSpec((1, lc, A_KV), lambda b, n: (b, 0, kcol)),
            pl.BlockSpec((1, lc, A_KV), lambda b, n: (b, 0, vcol)),
        ],
        out_specs=pl.BlockSpec((1, WINDOW, A_Q), lambda b, n: (b, n, 0)),
        out_shape=jax.ShapeDtypeStruct((bsz, seq, A_Q), BF16),
        compiler_params=_params(("arbitrary", "arbitrary")),
        name="window_attention",
    )(sink, proj, proj, proj, proj, proj, proj, proj, proj_ctx, proj_ctx)


def _ctx_attn_kernel(sink_ref, q_ref, k_ref, v_ref, o_ref):
    scale = HEAD_DIM ** -0.5
    q = q_ref[0]
    lc = q.shape[0]
    for h in range(A_Q_HEADS):
        hk = h // A_GROUP
        hs = slice(hk * HEAD_DIM, (hk + 1) * HEAD_DIM)
        qs = _bf(q[:, h * HEAD_DIM:(h + 1) * HEAD_DIM])
        s = _dot_nt(qs, _bf(k_ref[0][:, hs])) * scale
        sink_col = jnp.full((lc, 1), sink_ref[h], F32)
        o = _softmax_av([s], [_bf(v_ref[0][:, hs])], sink_col)
        o_ref[0, :, h * HEAD_DIM:(h + 1) * HEAD_DIM] = _bf(o)


def _ctx_attn(sink, proj_ctx):
    bsz, lc, _ = proj_ctx.shape
    kcol, vcol = A_Q // A_KV, A_Q // A_KV + 1
    return pl.pallas_call(
        _ctx_attn_kernel,
        grid=(bsz,),
        in_specs=[
            pl.BlockSpec(memory_space=pltpu.SMEM),
            pl.BlockSpec((1, lc, A_Q), lambda b: (b, 0, 0)),
            pl.BlockSpec((1, lc, A_KV), lambda b: (b, 0, kcol)),
            pl.BlockSpec((1, lc, A_KV), lambda b: (b, 0, vcol)),
        ],
        out_specs=pl.BlockSpec((1, lc, A_Q), lambda b: (b, 0, 0)),
        out_shape=jax.ShapeDtypeStruct((bsz, lc, A_Q), BF16),
        compiler_params=_params(("arbitrary",)),
        name="context_attention",
    )(sink, proj_ctx, proj_ctx, proj_ctx)


def _hgrn_chunk(bq, bi, bf, lb, st, rev):
    heads = range(len(bq))
    q = [_silu(x) for x in bq]
    v = bi
    f = [lb[h] + (1.0 - lb[h]) * _sigmoid(bf[h]) for h in heads]
    k = [1.0 - x for x in f]
    g = [jnp.log(x) for x in f]
    b = [_cumsum_rows(x, rev) for x in g]
    btot = [x[0:1] if rev else x[CHUNK - 1:CHUNK] for x in b]
    vb = [_bf(x) for x in v]
    stb = [_bf(x) for x in st]
    o = [_dot_nt(_bf(q[h] * jnp.exp(b[h])), stb[h]) for h in heads]
    khat = [_bf(k[h] * jnp.exp(btot[h] - b[h])) for h in heads]
    dst = [_dot_tn(vb[h], khat[h]) for h in heads]
    st_new = [st[h] * jnp.exp(btot[h]) + dst[h] for h in heads]
    nsub = CHUNK // SUB
    row8 = lax.broadcasted_iota(jnp.int32, (8, 1), 0)
    outs = [[] for _ in heads]
    for blk in range(nsub):
        r0 = blk * SUB
        rs = slice(r0, r0 + SUB)
        acc = [o[h][rs] for h in heads]
        if rev and blk < nsub - 1:
            ref_row, lo, hi = r0 + SUB, r0 + SUB, CHUNK
        elif (not rev) and blk > 0:
            ref_row, lo, hi = r0 - 1, 0, r0
        else:
            ref_row = None
        if ref_row is not None:
            bref = [b[h][ref_row:ref_row + 1] for h in heads]
            qn = [_bf(q[h][rs] * jnp.exp(b[h][rs] - bref[h])) for h in heads]
            kn = [_bf(k[h][lo:hi] * jnp.exp(bref[h] - b[h][lo:hi])) for h in heads]
            att = [_bf(_dot_nt(qn[h], kn[h])) for h in heads]
            acc = [acc[h] + _dot(att[h], vb[h][lo:hi]) for h in heads]
        piece = 8
        acc = [[a[p0:p0 + piece] for p0 in range(0, SUB, piece)] for a in acc]
        for s in range(SUB):
            reached = range(0, s // piece + 1) if rev else range(s // piece, SUB // piece)
            for pc in reached:
                t0 = r0 + pc * piece
                mask = (row8 + pc * piece <= s) if rev else (row8 + pc * piece >= s)
                for h in heads:
                    rel = b[h][t0:t0 + piece] - b[h][r0 + s:r0 + s + 1]
                    dec = jnp.exp(jnp.where(mask, rel, -jnp.inf))
                    w = jnp.sum(q[h][t0:t0 + piece] * dec * k[h][r0 + s:r0 + s + 1], axis=-1, keepdims=True)
                    acc[h][pc] = acc[h][pc] + w * v[h][r0 + s:r0 + s + 1]
        for h in heads:
            outs[h].extend(acc[h])
    return [jnp.concatenate(x, axis=0) for x in outs], st_new


def _hgrn_kernel(q_ref, i_ref, f_ref, lb_ref, s0_ref, o_ref, sfin_ref, st_ref, *, rev, nchunk, nhead):
    c = pl.program_id(2)

    @pl.when(c == 0)
    def _():
        st_ref[...] = s0_ref[0]

    lanes = [slice(h * HEAD_DIM, (h + 1) * HEAD_DIM) for h in range(nhead)]
    lb = [lb_ref[:, ln] for ln in lanes]

    def body(ci, carry):
        cc = (nchunk - 1 - ci) if rev else ci
        rows = pl.ds(pl.multiple_of(cc * CHUNK, CHUNK), CHUNK)
        o, st_new = _hgrn_chunk([q_ref[0, rows, ln] for ln in lanes], [i_ref[0, rows, ln] for ln in lanes],
                                [f_ref[0, rows, ln] for ln in lanes], lb,
                                [st_ref[h] for h in range(nhead)], rev)
        for h in range(nhead):
            o_ref[0, rows, lanes[h]] = o[h]
            st_ref[h] = st_new[h]
        return carry

    lax.fori_loop(0, nchunk, body, 0, unroll=2 if nchunk % 2 == 0 else 1)

    @pl.when(c == pl.num_programs(2) - 1)
    def _():
        sfin_ref[0] = st_ref[...]


def _hgrn_scan(proj, lb, s0, rev, tblk, nhead):
    bsz, seq, _ = proj.shape
    nblk = seq // tblk
    width = nhead * HEAD_DIM
    base = A_Q + 2 * A_KV
    assert base % width == 0 and B_W % width == 0
    qc, ic = base // width, (base + B_W) // width
    fc = (base + (3 if rev else 2) * B_W) // width
    blk = (lambda c: nblk - 1 - c) if rev else (lambda c: c)

    def col(c0):
        return pl.BlockSpec((1, tblk, width), lambda b, h, c: (b, blk(c), c0 + h))

    state = pl.BlockSpec((1, nhead, HEAD_DIM, HEAD_DIM), lambda b, h, c: (b, h, 0, 0))
    return pl.pallas_call(
        functools.partial(_hgrn_kernel, rev=rev, nchunk=tblk // CHUNK, nhead=nhead),
        grid=(bsz, B_HEADS // nhead, nblk),
        in_specs=[
            col(qc), col(ic), col(fc),
            pl.BlockSpec((1, width), lambda b, h, c: (0, h)),
            state,
        ],
        out_specs=[
            pl.BlockSpec((1, tblk, width), lambda b, h, c: (b, blk(c), h)),
            state,
        ],
        out_shape=[
            jax.ShapeDtypeStruct((bsz, seq, B_W), F32),
            jax.ShapeDtypeStruct((bsz, B_HEADS, HEAD_DIM, HEAD_DIM), F32),
        ],
        scratch_shapes=[pltpu.VMEM((nhead, HEAD_DIM, HEAD_DIM), F32)],
        compiler_params=_params(("arbitrary", "arbitrary", "arbitrary")),
        name="hgrn2_bwd" if rev else "hgrn2_fwd",
    )(proj, proj, proj, lb, s0)


def _ab_out_kernel(oa_ref, of_ref, ob_ref, g0_ref, g1_ref, on_ref, w_ref, h_ref, gt_ref, o_ref, lhs_ref):
    lhs_ref[:, :A_Q] = oa_ref[0]
    half = B_W // 2
    for hd in range(B_HEADS):
        sl = slice(hd * HEAD_DIM, (hd + 1) * HEAD_DIM)
        o = of_ref[0, :, sl] + ob_ref[0, :, sl]
        o = o * lax.rsqrt(jnp.mean(o * o, axis=-1, keepdims=True) + EPS)
        o = o * on_ref[:, sl]
        gref = g0_ref if hd * HEAD_DIM < half else g1_ref
        gs = slice(hd * HEAD_DIM % half, hd * HEAD_DIM % half + HEAD_DIM)
        o = o * _silu(gref[0, :, gs])
        lhs_ref[:, A_Q + hd * HEAD_DIM:A_Q + (hd + 1) * HEAD_DIM] = _bf(o)
    y = _dot(lhs_ref[...], w_ref[...])
    o_ref[0] = h_ref[0] + gt_ref[0] * y


def _ab_out(oa, of, ob, proj, onorm, w_out, h, gt, tm):
    bsz, seq, d = h.shape
    half = B_W // 2
    gcol = (A_Q + 2 * A_KV + 4 * B_W) // half
    row = lambda b, i: (b, i, 0)
    return pl.pallas_call(
        _ab_out_kernel,
        grid=(bsz, seq // tm),
        in_specs=[
            pl.BlockSpec((1, tm, A_Q), row),
            pl.BlockSpec((1, tm, B_W), row),
            pl.BlockSpec((1, tm, B_W), row),
            pl.BlockSpec((1, tm, half), lambda b, i: (b, i, gcol)),
            pl.BlockSpec((1, tm, half), lambda b, i: (b, i, gcol + 1)),
            pl.BlockSpec((1, B_W), lambda b, i: (0, 0)),
            pl.BlockSpec((A_Q + B_W, d), lambda b, i: (0, 0)),
            pl.BlockSpec((1, tm, d), row),
            pl.BlockSpec((1, 1, d), lambda b, i: (b, 0, 0)),
        ],
        out_specs=pl.BlockSpec((1, tm, d), row),
        out_shape=jax.ShapeDtypeStruct((bsz, seq, d), F32),
        scratch_shapes=[pltpu.VMEM((tm, A_Q + B_W), BF16)],
        compiler_params=_params(("arbitrary", "arbitrary")),
        name="ab_outproj",
    )(oa, of, ob, proj, proj, onorm, w_out, h, gt)


def _mlp_kernel(*refs, final):
    if final:
        x_ref, g_ref, sc_ref, sh_ref, gt_ref, wu_ref, wd_ref, fn_ref, o_ref, u_ref = refs
    else:
        x_ref, g_ref, sc_ref, sh_ref, gt_ref, wu_ref, wd_ref, o_ref, u_ref = refs
    j = pl.program_id(2)

    tm = x_ref.shape[1]

    @pl.when(j == 0)
    def _():
        _normmod_rows(x_ref, g_ref, sc_ref, sh_ref, u_ref)

    hid = jnp.maximum(_dot(u_ref[...], wu_ref[...]), 0.0)
    hid = _bf(hid * hid)
    ncol = 512

    @pl.when(j == 0)
    def _():
        for n0 in range(0, o_ref.shape[2], ncol):
            o_ref[0, :, n0:n0 + ncol] = _dot(hid, wd_ref[:, n0:n0 + ncol])

    @pl.when(j != 0)
    def _():
        for n0 in range(0, o_ref.shape[2], ncol):
            o_ref[0, :, n0:n0 + ncol] += _dot(hid, wd_ref[:, n0:n0 + ncol])

    @pl.when(j == pl.num_programs(2) - 1)
    def _():
        piece = 128
        for r0 in range(0, tm, piece):
            rs = slice(r0, r0 + piece)
            y = x_ref[0, rs] + gt_ref[0] * o_ref[0, rs]
            if final:
                y = (y * lax.rsqrt(jnp.mean(y * y, axis=-1, keepdims=True) + EPS)) * fn_ref[...]
            o_ref[0, rs] = y


def _mlp(x, g, sc, sh, gt, w_up, w_down, final_gain, tm, tf):
    bsz, seq, d = x.shape
    dff = w_up.shape[1]
    vec = pl.BlockSpec((1, 1, d), lambda b, i, j: (b, 0, 0))
    in_specs = [
        pl.BlockSpec((1, tm, d), lambda b, i, j: (b, i, 0)),
        pl.BlockSpec((1, d), lambda b, i, j: (0, 0)),
        vec, vec, vec,
        pl.BlockSpec((d, tf), lambda b, i, j: (0, j)),
        pl.BlockSpec((tf, d), lambda b, i, j: (j, 0)),
    ]
    args = [x, g, sc, sh, gt, w_up, w_down]
    if final_gain is not None:
        in_specs.append(pl.BlockSpec((1, d), lambda b, i, j: (0, 0)))
        args.append(final_gain)
    return pl.pallas_call(
        functools.partial(_mlp_kernel, final=final_gain is not None),
        grid=(bsz, seq // tm, dff // tf),
        in_specs=in_specs,
        out_specs=pl.BlockSpec((1, tm, d), lambda b, i, j: (b, i, 0)),
        out_shape=jax.ShapeDtypeStruct((bsz, seq, d), F32),
        scratch_shapes=[pltpu.VMEM((tm, d), BF16)],
        compiler_params=_params(("arbitrary", "arbitrary", "arbitrary")),
        name="sqrelu_mlp",
    )(*args)


def _rk_proj_kernel(x_ref, xp_ref, xn_ref, g_ref, sc_ref, sh_ref, mix_ref, wb_ref, ws_ref, ob_ref, os_ref,
                    u_ref, xx_ref, lhs_ref, *, nbig, nsmall):
    i = pl.program_id(1)
    j = pl.program_id(2)
    tm = x_ref.shape[1]
    rb = 32

    def build(m, slot):
        mixrow = mix_ref[pl.ds(m, 1), :]
        for r0 in range(0, tm, rb):
            lhs_ref[slot, r0:r0 + rb] = _bf(u_ref[8 + r0:8 + r0 + rb] + xx_ref[r0:r0 + rb] * mixrow)

    @pl.when(j == 0)
    def _():
        g, sc, sh = g_ref[...], sc_ref[0], sh_ref[0]
        _normmod_rows(x_ref, g_ref, sc_ref, sh_ref, u_ref, row0=8)
        up = _normmod(xp_ref[0], g, sc, sh)[7:8]
        un = _normmod(xn_ref[0], g, sc, sh)[0:1]
        u_ref[7:8] = jnp.where(i == 0, 0.0, up)
        u_ref[8 + tm:9 + tm] = jnp.where(i == pl.num_programs(1) - 1, 0.0, un)
        mix0 = mix_ref[0:1, :]
        for r0 in range(0, tm, ROW_PIECE):
            u = u_ref[8 + r0:8 + r0 + ROW_PIECE]
            xx = 0.5 * (u_ref[7 + r0:7 + r0 + ROW_PIECE] + u_ref[9 + r0:9 + r0 + ROW_PIECE]) - u
            xx_ref[r0:r0 + ROW_PIECE] = xx
            lhs_ref[0, r0:r0 + ROW_PIECE] = _bf(u + xx * mix0)

    @pl.when(j < nbig)
    def _():
        build(j + 1, (j + 1) % 2)
        ob_ref[0, 0] = _dot(lhs_ref[j % 2], wb_ref[0])

    @pl.when(j == nbig)
    def _():
        for s in range(nsmall):
            os_ref[s, 0] = _dot(lhs_ref[(nbig + s) % 2], ws_ref[s])
            if s + 1 < nsmall:
                build(nbig + s + 1, (nbig + s + 1) % 2)


def _rk_proj(x, g, sc, sh, mix, w_big, w_small, tm):
    bsz, seq, d = x.shape
    nbig, _, n = w_big.shape
    nsmall, _, ns = w_small.shape
    r8 = tm // 8
    last8 = seq // 8 - 1
    vec = pl.BlockSpec((1, 1, d), lambda b, i, j: (b, 0, 0))
    return pl.pallas_call(
        functools.partial(_rk_proj_kernel, nbig=nbig, nsmall=nsmall),
        grid=(bsz, seq // tm, nbig + 1),
        in_specs=[
            pl.BlockSpec((1, tm, d), lambda b, i, j: (b, i, 0)),
            pl.BlockSpec((1, 8, d), lambda b, i, j: (b, jnp.maximum(i * r8 - 1, 0), 0)),
            pl.BlockSpec((1, 8, d), lambda b, i, j: (b, jnp.minimum((i + 1) * r8, last8), 0)),
            pl.BlockSpec((1, d), lambda b, i, j: (0, 0)),
            vec, vec,
            pl.BlockSpec((nbig + nsmall, d), lambda b, i, j: (0, 0)),
            pl.BlockSpec((1, d, n), lambda b, i, j: (jnp.minimum(j, nbig - 1), 0, 0)),
            pl.BlockSpec((nsmall, d, ns), lambda b, i, j: (0, 0, 0)),
        ],
        out_specs=[
            pl.BlockSpec((1, 1, tm, n), lambda b, i, j: (jnp.minimum(j, nbig - 1), b, i, 0)),
            pl.BlockSpec((nsmall, 1, tm, ns), lambda b, i, j: (0, b, i, 0)),
        ],
        out_shape=[
            jax.ShapeDtypeStruct((nbig, bsz, seq, n), F32),
            jax.ShapeDtypeStruct((nsmall, bsz, seq, ns), F32),
        ],
        scratch_shapes=[pltpu.VMEM((tm + 16, d), F32), pltpu.VMEM((tm, d), F32), pltpu.VMEM((2, tm, d), BF16)],
        compiler_params=_params(("arbitrary", "arbitrary", "arbitrary")),
        name="rwkv_proj",
    )(x, x, x, g, sc, sh, mix, w_big, w_small)


DECAY_SCALE = float(np.exp(-0.5))


def _pair_consts(rev):
    n = 2 * CHUNK
    ri = lax.broadcasted_iota(jnp.int32, (n, n), 0)
    ci = lax.broadcasted_iota(jnp.int32, (n, n), 1)
    same = (ri // CHUNK) == (ci // CHUNK)
    rt, ct = ri % CHUNK, ci % CHUNK
    strict = same & ((ct > rt) if rev else (ct < rt))
    incl = same & ((ct >= rt) if rev else (ct <= rt))
    eye = jnp.where(ri == ci, 1.0, 0.0).astype(F32)
    lane = lax.broadcasted_iota(jnp.int32, (1, LANES), 1)
    lane_lo = lane < C_HEAD
    m0 = jnp.where(lane_lo, 1.0, 0.0).astype(F32)
    m1 = 1.0 - m0
    blk2 = (ri // 2) == (ci // 2)
    level = {}
    m = 2
    while m < CHUNK:
        if m % 8:
            rsel, csel = ri, ci
        else:
            rc = lax.broadcasted_iota(jnp.int32, (n // 2, n), 0)
            csel = lax.broadcasted_iota(jnp.int32, (n // 2, n), 1)
            rsel = (rc // m) * (2 * m) + (0 if rev else m) + rc % m
        level[m] = ((rsel // (2 * m)) == (csel // (2 * m))) & ((rsel // m) != (csel // m))
        m *= 2
    return dict(strict=strict, incl=incl, eye=eye, m0=m0, m1=m1, lane_lo=lane_lo, blk2=blk2, level=level)


def _head_sums(x, cs):
    s0 = jnp.sum(x * cs["m0"], axis=-1, keepdims=True)
    s1 = jnp.sum(x * cs["m1"], axis=-1, keepdims=True)
    return jnp.where(cs["lane_lo"], s0, s1)


def _unit_tri_inverse(n, cs, rev):
    size = 2 * CHUNK
    t = [cs["eye"] + jnp.where(cs["blk2"], x, 0.0) for x in n]
    nb = [_bf(x) for x in n]
    m = 2
    while m < CHUNK:
        tb = [_bf(x) for x in t]
        sel = cs["level"][m]
        if m % 8:
            q = [_bf(_dot(a, b)) for a, b in zip(tb, nb)]
            q = [_dot(a, b) for a, b in zip(q, tb)]
            t = [a + jnp.where(sel, b, 0.0) for a, b in zip(t, q)]
        else:
            starts = range(0 if rev else m, size, 2 * m)
            pick = lambda x: jnp.concatenate([x[r0:r0 + m] for r0 in starts], axis=0)
            q = [_bf(_dot(_bf(pick(x)), b)) for x, b in zip(t, nb)]
            q = [_dot(a, b) for a, b in zip(q, tb)]
            t = [_add_rows(x, jnp.where(sel, y, 0.0), starts, m) for x, y in zip(t, q)]
        m *= 2
    return t


def _add_rows(x, upd, starts, m):
    pieces, pos = [], 0
    for k, r0 in enumerate(starts):
        if r0 > pos:
            pieces.append(x[pos:r0])
        pieces.append(x[r0:r0 + m] + upd[k * m:(k + 1) * m])
        pos = r0 + m
    if pos < x.shape[0]:
        pieces.append(x[pos:])
    return jnp.concatenate(pieces, axis=0)


def _halves(top, bottom):
    return jnp.concatenate([top[:CHUNK], bottom[CHUNK:]], axis=0)


def _stack2(x, cs):
    return jnp.concatenate([x * cs["m0"], x * cs["m1"]], axis=0)


def _rwkv_chunk(r, k, v, lw, la, w2, a2, w0, a0, kkv, kav, ht, cs, rev):
    pairs = range(len(r))
    tl, lab = _bf(jnp.tanh(lw)), _bf(la)
    z = [w0[p] + _dot(tl, w2[p]) for p in pairs]
    za = [_dot(lab, a2[p]) for p in pairs]
    kkr = [k[p] * kkv[p] for p in pairs]
    ss = [_head_sums(x * x, cs) for x in kkr]
    ld = [-DECAY_SCALE * _sigmoid(x) for x in z]
    cl = [_cumsum_rows(x, rev) for x in ld]
    a = [_sigmoid(a0[p] + za[p]) for p in pairs]
    kk = [x / jnp.maximum(jnp.sqrt(s), 1e-12) for x, s in zip(kkr, ss)]
    kd = [k[p] * (1.0 + (a[p] - 1.0) * kav[p]) for p in pairs]
    bb = [x * y for x, y in zip(kk, a)]
    ctot = [x[0:1] if rev else x[CHUNK - 1:CHUNK] for x in cl]
    e_neg = [jnp.exp(-x) for x in cl]
    e_tail = [jnp.exp(c - x) for c, x in zip(ctot, cl)]
    at = [_bf(_stack2(-kk[p] * jnp.exp(cl[p] - ld[p]), cs)) for p in pairs]
    rt = [_bf(_stack2(r[p] * jnp.exp(cl[p]), cs)) for p in pairs]
    rhs = [_bf(jnp.concatenate([bb[p] * e_neg[p], kd[p] * e_neg[p]], axis=0)) for p in pairs]
    g = [_dot_nt(jnp.concatenate([x, y], axis=0), w) for x, y, w in zip(at, rt, rhs)]
    g1 = [x[:2 * CHUNK] for x in g]
    g2 = [x[2 * CHUNK:] for x in g]
    strict, incl = cs["strict"], cs["incl"]
    g1r = [pltpu.roll(x, C_HEAD, 1) for x in g1]
    n_ab = [jnp.where(strict, _halves(x, y), 0.0) for x, y in zip(g1, g1r)]
    n_ak = [_bf(jnp.where(strict, _halves(y, x), 0.0)) for x, y in zip(g1, g1r)]
    t = _unit_tri_inverse(n_ab, cs, rev)
    vs = [_bf(_stack2(x, cs)) for x in v]
    hkv = [_bf(x.T) for x in ht]
    xx = [_bf(_dot(jnp.concatenate([at[p], n_ak[p]], axis=1), jnp.concatenate([hkv[p], vs[p]], axis=0)))
          for p in pairs]
    u =[_bf(_dot(_bf(a_), b_)) for a_, b_ in zip(t, xx)]
    bk = [_bf(jnp.concatenate([_stack2(bb[p] * e_tail[p], cs), _stack2(kd[p] * e_tail[p], cs)], axis=0))
          for p in pairs]
    uv = [jnp.concatenate([a_, b_], axis=0) for a_, b_ in zip(u, vs)]
    dh = [_dot_tn(a_, b_) for a_, b_ in zip(uv, bk)]
    ht_new = [ht[p] * jnp.exp(ctot[p]) + dh[p] for p in pairs]
    g2r = [pltpu.roll(x, C_HEAD, 1) for x in g2]
    n_rb = [_bf(jnp.where(incl, _halves(x, y), 0.0)) for x, y in zip(g2, g2r)]
    n_rk = [_bf(jnp.where(incl, _halves(y, x), 0.0)) for x, y in zip(g2, g2r)]
    ysum = [_dot(jnp.concatenate([rt[p], n_rb[p], n_rk[p]], axis=1),
                 jnp.concatenate([hkv[p], u[p], vs[p]], axis=0)) for p in pairs]
    y = [x[:CHUNK] + x[CHUNK:] for x in ysum]
    return y, ht_new, kd


def _rwkv_combine(y, yf, r, k, v, kd, laf, a2f, a0f, kav, rkv, lng, lnb, cs):
    pairs = range(len(y))
    inv_n = 1.0 / C_HEAD
    lafb = _bf(laf)
    zf = [_dot(lafb, a2f[p]) for p in pairs]
    ysum = [a + b for a, b in zip(y, yf)]
    mu = [_head_sums(x, cs) * inv_n for x in ysum]
    dev = [a - b for a, b in zip(ysum, mu)]
    var = [_head_sums(x * x, cs) * inv_n for x in dev]
    a_f = [_sigmoid(a0f[p] + zf[p]) for p in pairs]
    kd_f = [k[p] * (1.0 + (a_f[p] - 1.0) * kav[p]) for p in pairs]
    bsum = [_head_sums(r[p] * (kd_f[p] + kd[p]) * rkv[p], cs) for p in pairs]
    return [dev[p] * lax.rsqrt(var[p] + GN_EPS) * lng[p] + lnb[p] + bsum[p] * v[p] for p in pairs]


def _rwkv_scan_kernel(*refs, rev, nchunk, npair, combine):
    if combine:
        (r_ref, k_ref, v_ref, lw_ref, la_ref, w2_ref, a2_ref, w0_ref, a0_ref, kk_ref, ka_ref, s0_ref,
         yf_ref, laf_ref, a2f_ref, a0f_ref, rk_ref, lng_ref, lnb_ref, o_ref, sfin_ref, ht_ref) = refs
    else:
        (r_ref, k_ref, v_ref, lw_ref, la_ref, w2_ref, a2_ref, w0_ref, a0_ref, kk_ref, ka_ref, s0_ref,
         o_ref, sfin_ref, ht_ref) = refs
    c = pl.program_id(2)

    @pl.when(c == 0)
    def _():
        ht_ref[...] = s0_ref[0]

    cs = _pair_consts(rev)

    def body(ci, carry):
        cc = (nchunk - 1 - ci) if rev else ci
        rows = pl.ds(pl.multiple_of(cc * CHUNK, CHUNK), CHUNK)
        lw = lw_ref[0, 0, rows, :]
        la = la_ref[0, 0, rows, :]
        lanes = [slice(pr * LANES, (pr + 1) * LANES) for pr in range(npair)]
        r = [r_ref[0, 0, rows, ln] for ln in lanes]
        k = [k_ref[0, 0, rows, ln] for ln in lanes]
        v = [v_ref[0, 0, rows, ln] for ln in lanes]
        kav = [ka_ref[:, ln] for ln in lanes]
        y, ht_new, kd = _rwkv_chunk(
            r, k, v, lw, la, [w2_ref[0, :, ln] for ln in lanes], [a2_ref[0, :, ln] for ln in lanes],
            [w0_ref[0, :, ln] for ln in lanes], [a0_ref[0, :, ln] for ln in lanes],
            [kk_ref[:, ln] for ln in lanes], kav, [ht_ref[pr] for pr in range(npair)], cs, rev)
        for pr in range(npair):
            ht_ref[pr] = ht_new[pr]
        if combine:
            y = _rwkv_combine(
                y, [yf_ref[0, rows, ln] for ln in lanes], r, k, v, kd, laf_ref[0, 0, rows, :],
                [a2f_ref[0, :, ln] for ln in lanes], [a0f_ref[0, :, ln] for ln in lanes], kav,
                [rk_ref[:, ln] for ln in lanes], [lng_ref[:, ln] for ln in lanes],
                [lnb_ref[:, ln] for ln in lanes], cs)
        for pr in range(npair):
            o_ref[0, rows, lanes[pr]] = y[pr]
        return carry

    lax.fori_loop(0, nchunk, body, 0, unroll=2 if nchunk % 2 == 0 else 1)

    @pl.when(c == pl.num_programs(2) - 1)
    def _():
        sfin_ref[0] = ht_ref[...]


def _rwkv_scan(rkv, small, prm, s0, rev, tblk, npair, y_fwd=None):
    _, bsz, seq, d = rkv.shape
    nblk = seq // tblk
    width = npair * LANES
    e = 1 if rev else 0
    blk = (lambda c: nblk - 1 - c) if rev else (lambda c: c)
    combine = y_fwd is not None

    def tok(m):
        return pl.BlockSpec((1, 1, tblk, width), lambda b, p, c: (m, b, blk(c), p))

    def lora(m, half):
        return pl.BlockSpec((1, 1, tblk, LANES), lambda b, p, c: (m, b, blk(c), half))

    def mat(idx):
        return pl.BlockSpec((1, LANES, width), lambda b, p, c: (idx, 0, p))

    def vec3(idx):
        return pl.BlockSpec((1, 1, width), lambda b, p, c: (idx, 0, p))

    vec = pl.BlockSpec((1, width), lambda b, p, c: (0, p))
    state = pl.BlockSpec((1, npair, LANES, LANES), lambda b, p, c: (b, p, 0, 0))
    out_tok = pl.BlockSpec((1, tblk, width), lambda b, p, c: (b, blk(c), p))
    in_specs = [tok(0), tok(1), tok(2), lora(1, e), lora(2, e), mat(e), mat(e), vec3(e), vec3(e), vec, vec, state]
    args = [rkv, rkv, rkv, small, small, prm["w2"], prm["a2"], prm["w0"], prm["a0"], prm["k_k"], prm["k_a"], s0]
    if combine:
        in_specs += [out_tok, lora(2, 0), mat(0), vec3(0), vec, vec, vec]
        args += [y_fwd, small, prm["a2"], prm["a0"], prm["r_k"], prm["ln_g"], prm["ln_b"]]
    return pl.pallas_call(
        functools.partial(_rwkv_scan_kernel, rev=rev, nchunk=tblk // CHUNK, npair=npair, combine=combine),
        grid=(bsz, d // width, nblk),
        in_specs=in_specs,
        out_specs=[out_tok, state],
        out_shape=[
            jax.ShapeDtypeStruct((bsz, seq, d), F32),
            jax.ShapeDtypeStruct((bsz, d // LANES, LANES, LANES), F32),
        ],
        scratch_shapes=[pltpu.VMEM((npair, LANES, LANES), F32)],
        compiler_params=_params(("arbitrary", "arbitrary", "arbitrary")),
        name="rwkv7_bwd" if rev else "rwkv7_fwd",
    )(*args)


def _rk_out_kernel(z_ref, gs_ref, g2_ref, wo_ref, h_ref, gt_ref, o_ref):
    gate = _dot(_bf(_sigmoid(gs_ref[0, 0])), g2_ref[...])
    y = _dot(_bf(z_ref[0] * gate), wo_ref[...])
    o_ref[0] = h_ref[0] + gt_ref[0] * y


def _rk_out(z, small, g2, wo, h, gt, tm):
    bsz, seq, d = h.shape
    row = lambda b, i: (b, i, 0)
    glora = g2.shape[0]
    return pl.pallas_call(
        _rk_out_kernel,
        grid=(bsz, seq // tm),
        in_specs=[
            pl.BlockSpec((1, tm, d), row),
            pl.BlockSpec((1, 1, tm, glora), lambda b, i: (0, b, i, 0)),
            pl.BlockSpec((glora, d), lambda b, i: (0, 0)),
            pl.BlockSpec((d, d), lambda b, i: (0, 0)),
            pl.BlockSpec((1, tm, d), row),
            pl.BlockSpec((1, 1, d), lambda b, i: (b, 0, 0)),
        ],
        out_specs=pl.BlockSpec((1, tm, d), row),
        out_shape=jax.ShapeDtypeStruct((bsz, seq, d), F32),
        compiler_params=_params(("arbitrary", "arbitrary")),
        name="rwkv_outproj",
    )(z, small, g2, wo, h, gt)


def _rope_tables(seq):
    t = jnp.arange(seq, dtype=jnp.int32)
    rows = (t // GRID_W).astype(F32)
    cols = (t % GRID_W).astype(F32)
    half = HEAD_DIM // 2
    n_freq = half // 2
    inv = ROPE_BASE ** (-jnp.arange(n_freq, dtype=F32) / n_freq)
    lane = jnp.arange(HEAD_DIM)
    pos = jnp.where((lane < half)[None, :], rows[:, None], cols[:, None])
    ang = pos * inv[lane % n_freq][None, :]
    cos, sin = jnp.cos(ang), jnp.sin(ang)
    first = ((lane % half) < n_freq)[None, :]
    return cos, jnp.where(first, -sin, 0.0), jnp.where(first, 0.0, sin)


def _pad_lanes(w, axis):
    pad = [(0, 0)] * w.ndim
    pad[axis] = (0, LANES - w.shape[axis])
    return jnp.pad(w, pad)


def _row_tile(seq, want):
    return want if seq % want == 0 else seq


def kernel(x, c, ctx, c_ctx, mod_w, mod_b, norm_mix, norm_ffn, ffn_up, ffn_down, ab_w_in, ab_w_out, attn_sink, hgrn_lb, hgrn_onorm, rk_mix, rk_wr, rk_wk, rk_wv, rk_wo, rk_w0, rk_w1, rk_w2, rk_a0, rk_a1, rk_a2, rk_g1, rk_g2, rk_kk, rk_ka, rk_rk, rk_ln_g, rk_ln_b, final_norm):
    bsz, seq, d = x.shape
    lc = ctx.shape[1]
    depth = mod_w.shape[0]
    assert bsz + 1 <= 8 and seq % 1024 == 0 and lc % CHUNK == 0

    cond8 = jnp.zeros((8, d), F32).at[:bsz].set(c).at[bsz].set(c_ctx)
    mod = _modulation(cond8, mod_w, mod_b)
    lb_all = jnp.cumsum(jax.nn.softmax(hgrn_lb.astype(F32), axis=0), axis=0)
    rope = _rope_tables(seq)

    h, hc = x, ctx
    for layer in range(depth):
        last = layer == depth - 1
        jl = layer // 2
        m_lat = mod[layer, :bsz].reshape(bsz, 1, 6, d)
        m_ctx = jnp.broadcast_to(mod[layer, bsz].reshape(1, 1, 6, d), (bsz, 1, 6, d))
        sh1, sc1, gt1, sh2, sc2, gt2 = (m_lat[:, :, i] for i in range(6))
        csh1, csc1, cgt1, csh2, csc2, cgt2 = (m_ctx[:, :, i] for i in range(6))
        g_mix = norm_mix[layer].reshape(1, d)
        g_ffn = norm_ffn[layer].reshape(1, d)
        if layer % 2 == 0:
            w_in = _bf(ab_w_in[jl])
            w_out = _bf(ab_w_out[jl])
            sink = attn_sink[jl].astype(F32)
            lb = lb_all[jl].reshape(1, B_W)
            onorm = hgrn_onorm[jl].reshape(1, B_W).astype(F32)
            n_in = w_in.shape[1]
            pc = _inproj(hc, g_mix, csc1, csh1, w_in, None, lc, 512)
            pl_ = _inproj(h, g_mix, sc1, sh1, w_in, rope, 1024, 512)
            oa = _win_attn(sink, pl_, pc)
            zeros = jnp.zeros((bsz, B_HEADS, HEAD_DIM, HEAD_DIM), F32)
            nhead = 4
            ocf, scf = _hgrn_scan(pc, lb, zeros, False, lc, nhead)
            ocb, scb = _hgrn_scan(pc, lb, zeros, True, lc, nhead)
            olf, _ = _hgrn_scan(pl_, lb, scf, False, 512, nhead)
            olb, _ = _hgrn_scan(pl_, lb, scb, True, 512, nhead)
            h = _ab_out(oa, olf, olb, pl_, onorm, w_out, h, gt1, 512)
            if not last:
                oca = _ctx_attn(sink, pc)
                hc = _ab_out(oca, ocf, ocb, pc, onorm, w_out, hc, cgt1, lc)
        else:
            w_big = _bf(jnp.stack([rk_wr[jl], rk_wk[jl], rk_wv[jl]]))
            w_small = _bf(jnp.stack([
                rk_g1[jl],
                jnp.concatenate([_pad_lanes(rk_w1[jl, 0], 1), _pad_lanes(rk_w1[jl, 1], 1)], axis=1),
                jnp.concatenate([_pad_lanes(rk_a1[jl, 0], 1), _pad_lanes(rk_a1[jl, 1], 1)], axis=1),
            ]))
            assert w_small.shape[-1] == 2 * LANES
            mix = rk_mix[jl]
            prm = dict(
                w2=_bf(_pad_lanes(rk_w2[jl], 1)), a2=_bf(_pad_lanes(rk_a2[jl], 1)),
                w0=rk_w0[jl].reshape(2, 1, d), a0=rk_a0[jl].reshape(2, 1, d),
                k_k=rk_kk[jl].reshape(1, d), k_a=rk_ka[jl].reshape(1, d), r_k=rk_rk[jl].reshape(1, d),
                ln_g=rk_ln_g[jl].reshape(1, d), ln_b=rk_ln_b[jl].reshape(1, d),
            )
            g2 = _bf(rk_g2[jl])
            wo = _bf(rk_wo[jl])
            mix = mix[jnp.array([0, 2, 3, 5, 1, 4])]
            rkv_c, sm_c = _rk_proj(hc, g_mix, csc1, csh1, mix, w_big, w_small, lc)
            rkv_l, sm_l = _rk_proj(h, g_mix, sc1, sh1, mix, w_big, w_small, 512)
            zeros = jnp.zeros((bsz, d // LANES, LANES, LANES), F32)
            npair = 16
            ycf, s_f = _rwkv_scan(rkv_c, sm_c, prm, zeros, False, lc, npair)
            zc, s_b = _rwkv_scan(rkv_c, sm_c, prm, zeros, True, lc, npair, y_fwd=ycf)
            ylf, _ = _rwkv_scan(rkv_l, sm_l, prm, s_f, False, 256, npair)
            zl, _ = _rwkv_scan(rkv_l, sm_l, prm, s_b, True, 256, npair, y_fwd=ylf)
            h = _rk_out(zl, sm_l, g2, wo, h, gt1, 512)
            if not last:
                hc = _rk_out(zc, sm_c, g2, wo, hc, cgt1, lc)
        w_up = _bf(ffn_up[layer])
        w_dn = _bf(ffn_down[layer])
        h = _mlp(h, g_ffn, sc2, sh2, gt2, w_up, w_dn, final_norm.reshape(1, d) if last else None, 1024, 512)
        if not last:
            hc = _mlp(hc, g_ffn, csc2, csh2, cgt2, w_up, w_dn, None, lc, 512)
    return h
```

```python
import functools

import jax
import jax.numpy as jnp
import numpy as np
from jax import lax
from jax.experimental import pallas as pl
from jax.experimental.pallas import tpu as pltpu

F32 = jnp.float32
BF16 = jnp.bfloat16
HIGHEST = lax.Precision.HIGHEST

LANES = 128
HEAD_DIM = 128
GRID_W = 64
WINDOW = 128
ROPE_BASE = 10000.0
A_Q_HEADS = 8
A_KV_HEADS = 2
A_GROUP = A_Q_HEADS // A_KV_HEADS
A_Q = A_Q_HEADS * HEAD_DIM
A_KV = A_KV_HEADS * HEAD_DIM
B_HEADS = 8
B_W = B_HEADS * HEAD_DIM
C_HEAD = 64
CHUNK = 64
SUB = 16
EPS = 1e-6
GN_EPS = 64e-5
VMEM_LIMIT = 58 * 1024 * 1024
LOG2E = float(np.log2(np.e))

NT_DIMS = (((1,), (1,)), ((), ()))
TN_DIMS = (((0,), (0,)), ((), ()))


def _dot(a, b, **kw):
    return jnp.dot(a, b, preferred_element_type=F32, **kw)


def _dot_nt(a, b):
    return lax.dot_general(a, b, NT_DIMS, preferred_element_type=F32)


def _dot_tn(a, b):
    return lax.dot_general(a, b, TN_DIMS, preferred_element_type=F32)


def _bf(x):
    return x.astype(BF16)


def _dot_split(x, m):
    hi = _bf(x)
    lo = _bf(x - hi.astype(F32))
    return _dot(hi, m) + _dot(lo, m)


def _sigmoid(x):
    return 1.0 / (1.0 + jnp.exp(-x))


def _silu(x):
    return x * _sigmoid(x)


def _normmod(x, g, sc, sh):
    return _normscale(x, g * (1.0 + sc), sh)


def _normscale(x, gain, shift):
    ms = jnp.mean(x * x, axis=-1, keepdims=True)
    return (x * lax.rsqrt(ms + EPS)) * gain + shift


ROW_PIECE = 16


def _for_row_pieces(rows, fn):
    def body(p, carry):
        fn(pl.ds(pl.multiple_of(p * ROW_PIECE, ROW_PIECE), ROW_PIECE))
        return carry

    lax.fori_loop(0, rows // ROW_PIECE, body, 0, unroll=8)


def _normmod_rows(x_ref, g_ref, sc_ref, sh_ref, u_ref, row0=0):
    gain, shift = g_ref[...] * (1.0 + sc_ref[0]), sh_ref[0]

    def piece(rs):
        dst = rs if row0 == 0 else pl.ds(rs.start + row0, ROW_PIECE)
        u_ref[dst] = _normscale(x_ref[0, rs], gain, shift).astype(u_ref.dtype)

    _for_row_pieces(x_ref.shape[1], piece)


def _cumsum_rows(x, rev):
    n = x.shape[0]
    row = lax.broadcasted_iota(jnp.int32, (n, 1), 0)
    s = 1
    while s < n:
        if rev:
            x = x + jnp.where(row < n - s, pltpu.roll(x, n - s, 0), 0.0)
        else:
            x = x + jnp.where(row >= s, pltpu.roll(x, s, 0), 0.0)
        s *= 2
    return x


def _params(sem):
    return pltpu.CompilerParams(dimension_semantics=sem, vmem_limit_bytes=VMEM_LIMIT)


def _mod_kernel(c_ref, w_ref, b_ref, o_ref):
    s = _bf(_silu(c_ref[...]))
    o_ref[0] = _dot(s, _bf(w_ref[0])) + b_ref[0]


def _modulation(cond8, mod_w, mod_b):
    depth, d, n = mod_w.shape
    tn = 1024
    return pl.pallas_call(
        _mod_kernel,
        grid=(depth, n // tn),
        in_specs=[
            pl.BlockSpec((8, d), lambda l, j: (0, 0)),
            pl.BlockSpec((1, d, tn), lambda l, j: (l, 0, j)),
            pl.BlockSpec((1, 1, tn), lambda l, j: (l, 0, j)),
        ],
        out_specs=pl.BlockSpec((1, 8, tn), lambda l, j: (l, 0, j)),
        out_shape=jax.ShapeDtypeStruct((depth, 8, n), F32),
        compiler_params=_params(("arbitrary", "arbitrary")),
        name="modulation",
    )(cond8, mod_w, mod_b.reshape(depth, 1, n))


def _inproj_kernel(*refs, n_rope):
    if n_rope:
        x_ref, g_ref, sc_ref, sh_ref, w_ref, cos_ref, sna_ref, snb_ref, o_ref, u_ref = refs
    else:
        x_ref, g_ref, sc_ref, sh_ref, w_ref, o_ref, u_ref = refs
    j = pl.program_id(2)

    @pl.when(j == 0)
    def _():
        _normmod_rows(x_ref, g_ref, sc_ref, sh_ref, u_ref)

    acc = _dot(u_ref[...], w_ref[...])
    if not n_rope:
        o_ref[0] = acc
        return

    per_tile = acc.shape[1] // HEAD_DIM
    n_full, n_rem = n_rope // per_tile, n_rope % per_tile

    def store(n_rot):
        cos, sna, snb = cos_ref[...], sna_ref[...], snb_ref[...]
        for hd in range(n_rot):
            sl = acc[:, hd * HEAD_DIM:(hd + 1) * HEAD_DIM]
            rot = sl * cos + pltpu.roll(sl, 96, 1) * sna + pltpu.roll(sl, 32, 1) * snb
            o_ref[0, :, hd * HEAD_DIM:(hd + 1) * HEAD_DIM] = rot
        if n_rot < per_tile:
            o_ref[0, :, n_rot * HEAD_DIM:] = acc[:, n_rot * HEAD_DIM:]

    pl.when(j < n_full)(lambda: store(per_tile))
    pl.when(j == n_full)(lambda: store(n_rem))
    pl.when(j > n_full)(lambda: store(0))


def _inproj(x, g, sc, sh, w, rope, tm, tn):
    bsz, seq, d = x.shape
    n = w.shape[1]
    n_rope = 0
    in_specs = [
        pl.BlockSpec((1, tm, d), lambda b, i, j: (b, i, 0)),
        pl.BlockSpec((1, d), lambda b, i, j: (0, 0)),
        pl.BlockSpec((1, 1, d), lambda b, i, j: (b, 0, 0)),
        pl.BlockSpec((1, 1, d), lambda b, i, j: (b, 0, 0)),
        pl.BlockSpec((d, tn), lambda b, i, j: (0, j)),
    ]
    args = [x, g, sc, sh, w]
    if rope is not None:
        n_rope = A_Q_HEADS + A_KV_HEADS
        assert tn % HEAD_DIM == 0
        in_specs += [pl.BlockSpec((tm, HEAD_DIM), lambda b, i, j: (i, 0))] * 3
        args += list(rope)
    return pl.pallas_call(
        functools.partial(_inproj_kernel, n_rope=n_rope),
        grid=(bsz, seq // tm, n // tn),
        in_specs=in_specs,
        out_specs=pl.BlockSpec((1, tm, tn), lambda b, i, j: (b, i, j)),
        out_shape=jax.ShapeDtypeStruct((bsz, seq, n), F32),
        scratch_shapes=[pltpu.VMEM((tm, d), BF16)],
        compiler_params=_params(("arbitrary", "arbitrary", "arbitrary")),
        name="ab_inproj",
    )(*args)


def _inproj_stream_kernel(x_ref, g_ref, sc_ref, sh_ref, w_ref, cos_ref, sna_ref, snb_ref, o_ref,
                          lhs0_ref, lhs1_ref, *, n_rope):
    j = pl.program_id(0)
    i = pl.program_id(2)
    tm = x_ref.shape[1]
    rb = 32

    @pl.when((j == 0) & (pl.program_id(1) == 0) & (i == 0))
    def _():
        lhs1_ref[...] = jnp.zeros_like(lhs1_ref)

    def step(build_ref, ready_ref):
        gain, shift = g_ref[...] * (1.0 + sc_ref[0]), sh_ref[0]
        for r0 in range(0, tm, rb):
            build_ref[r0:r0 + rb] = _bf(_normscale(x_ref[0, r0:r0 + rb], gain, shift))
        o_ref[0] = _dot(ready_ref[...], w_ref[...])

    pl.when(i % 2 == 0)(lambda: step(lhs0_ref, lhs1_ref))
    pl.when(i % 2 == 1)(lambda: step(lhs1_ref, lhs0_ref))

    @pl.when(j == 0)
    def _():
        cos, sna, snb = cos_ref[...], sna_ref[...], snb_ref[...]
        for hd in range(n_rope):
            sl = o_ref[0, :, hd * HEAD_DIM:(hd + 1) * HEAD_DIM]
            rot = sl * cos + pltpu.roll(sl, 96, 1) * sna + pltpu.roll(sl, 32, 1) * snb
            o_ref[0, :, hd * HEAD_DIM:(hd + 1) * HEAD_DIM] = rot


def _inproj_stream(x, g, sc, sh, w, rope, tm, tn):
    bsz, seq, d = x.shape
    n = w.shape[1]
    ni = seq // tm
    n_rope = A_Q_HEADS + A_KV_HEADS
    assert n_rope * HEAD_DIM <= tn and n % tn == 0
    cur = lambda j, b, i: (b, jnp.minimum(i, ni - 1), 0)
    done = lambda i: jnp.maximum(i - 1, 0)
    table = pl.BlockSpec((tm, HEAD_DIM), lambda j, b, i: (done(i), 0))
    return pl.pallas_call(
        functools.partial(_inproj_stream_kernel, n_rope=n_rope),
        grid=(n // tn, bsz, ni + 1),
        in_specs=[
            pl.BlockSpec((1, tm, d), cur),
            pl.BlockSpec((1, d), lambda j, b, i: (0, 0)),
            pl.BlockSpec((1, 1, d), lambda j, b, i: (b, 0, 0)),
            pl.BlockSpec((1, 1, d), lambda j, b, i: (b, 0, 0)),
            pl.BlockSpec((d, tn), lambda j, b, i: (0, j)),
            table, table, table,
        ],
        out_specs=pl.BlockSpec((1, tm, tn), lambda j, b, i: (b, done(i), j)),
        out_shape=jax.ShapeDtypeStruct((bsz, seq, n), F32),
        scratch_shapes=[pltpu.VMEM((tm, d), BF16), pltpu.VMEM((tm, d), BF16)],
        compiler_params=_params(("arbitrary", "arbitrary", "arbitrary")),
        name="ab_inproj_stream",
    )(x, g, sc, sh, w, *rope)


def _softmax_av(s_list, v_list, sink_col):
    m = sink_col
    for s in s_list:
        m = jnp.maximum(m, jnp.max(s, axis=-1, keepdims=True))
    den = jnp.exp2(sink_col - m)
    out = None
    for s, v in zip(s_list, v_list):
        p = jnp.exp2(s - m)
        den = den + jnp.sum(p, axis=-1, keepdims=True)
        o = _dot(_bf(p), v)
        out = o if out is None else out + o
    return out / den


def _sink_column(sink_ref, hk, rows):
    rowh = lax.broadcasted_iota(jnp.int32, (rows, 1), 0) // WINDOW
    col = jnp.full((rows, 1), sink_ref[hk * A_GROUP + A_GROUP - 1], F32)
    for g in range(A_GROUP - 1):
        col = jnp.where(rowh == g, sink_ref[hk * A_GROUP + g], col)
    return col * LOG2E


def _win_attn_kernel(sink_ref, q_ref, kp_ref, kc_ref, kn_ref, vp_ref, vc_ref, vn_ref,
                     ck_ref, cv_ref, o_ref):
    n = pl.program_id(1)
    nb = pl.num_programs(1)
    scale = HEAD_DIM ** -0.5 * LOG2E
    q = q_ref[0]
    kband = jnp.concatenate([kp_ref[0], kc_ref[0], kn_ref[0]], axis=0)
    vband = jnp.concatenate([vp_ref[0], vc_ref[0], vn_ref[0]], axis=0)
    rows = A_GROUP * WINDOW
    tq = lax.broadcasted_iota(jnp.int32, (rows, 3 * WINDOW), 0) % WINDOW
    tk = lax.broadcasted_iota(jnp.int32, (rows, 3 * WINDOW), 1)
    rel = tk - WINDOW - tq
    valid = (jnp.abs(rel) <= WINDOW) & ((tk >= WINDOW) | (n > 0)) & ((tk < 2 * WINDOW) | (n < nb - 1))
    for hk in range(A_KV_HEADS):
        qs = jnp.concatenate(
            [q[:, (hk * A_GROUP + g) * HEAD_DIM:(hk * A_GROUP + g + 1) * HEAD_DIM] for g in range(A_GROUP)],
            axis=0)
        qs = _bf(qs)
        hs = slice(hk * HEAD_DIM, (hk + 1) * HEAD_DIM)
        s_win = _dot_nt(qs, _bf(kband[:, hs])) * scale
        s_win = jnp.where(valid, s_win, -jnp.inf)
        s_ctx = _dot_nt(qs, _bf(ck_ref[0][:, hs])) * scale
        o = _softmax_av([s_win, s_ctx], [_bf(vband[:, hs]), _bf(cv_ref[0][:, hs])],
                        _sink_column(sink_ref, hk, rows))
        for g in range(A_GROUP):
            h = hk * A_GROUP + g
            o_ref[0, :, h * HEAD_DIM:(h + 1) * HEAD_DIM] = _bf(o[g * WINDOW:(g + 1) * WINDOW])


def _win_attn(sink, proj, proj_ctx):
    bsz, seq, _ = proj.shape
    lc = proj_ctx.shape[1]
    nb = seq // WINDOW
    kcol, vcol = A_Q // A_KV, A_Q // A_KV + 1
    prev = lambda b, n: (b, jnp.maximum(n - 1, 0))
    nxt = lambda b, n: (b, jnp.minimum(n + 1, nb - 1))
    cur = lambda b, n: (b, n)

    def band(rowfn, col):
        return pl.BlockSpec((1, WINDOW, A_KV), lambda b, n: rowfn(b, n) + (col,))

    return pl.pallas_call(
        _win_attn_kernel,
        grid=(bsz, nb),
        in_specs=[
            pl.BlockSpec(memory_space=pltpu.SMEM),
            pl.BlockSpec((1, WINDOW, A_Q), lambda b, n: (b, n, 0)),
            band(prev, kcol), band(cur, kcol), band(nxt, kcol),
            band(prev, vcol), band(cur, vcol), band(nxt, vcol),
            pl.BlockSpec((1, lc, A_KV), lambda b, n: (b, 0, kcol)),
            pl.BlockSpec((1, lc, A_KV), lambda b, n: (b, 0, vcol)),
        ],
        out_specs=pl.BlockSpec((1, WINDOW, A_Q), lambda b, n: (b, n, 0)),
        out_shape=jax.ShapeDtypeStruct((bsz, seq, A_Q), BF16),
        compiler_params=_params(("arbitrary", "arbitrary")),
        name="window_attention",
    )(sink, proj, proj, proj, proj, proj, proj, proj, proj_ctx, proj_ctx)


def _ctx_attn_kernel(sink_ref, q_ref, k_ref, v_ref, o_ref):
    scale = HEAD_DIM ** -0.5 * LOG2E
    q = q_ref[0]
    lc = q.shape[0]
    for h in range(A_Q_HEADS):
        hk = h // A_GROUP
        hs = slice(hk * HEAD_DIM, (hk + 1) * HEAD_DIM)
        qs = _bf(q[:, h * HEAD_DIM:(h + 1) * HEAD_DIM])
        s = _dot_nt(qs, _bf(k_ref[0][:, hs])) * scale
        sink_col = jnp.full((lc, 1), sink_ref[h], F32) * LOG2E
        o = _softmax_av([s], [_bf(v_ref[0][:, hs])], sink_col)
        o_ref[0, :, h * HEAD_DIM:(h + 1) * HEAD_DIM] = _bf(o)


def _ctx_attn(sink, proj_ctx):
    bsz, lc, _ = proj_ctx.shape
    kcol, vcol = A_Q // A_KV, A_Q // A_KV + 1
    return pl.pallas_call(
        _ctx_attn_kernel,
        grid=(bsz,),
        in_specs=[
            pl.BlockSpec(memory_space=pltpu.SMEM),
            pl.BlockSpec((1, lc, A_Q), lambda b: (b, 0, 0)),
            pl.BlockSpec((1, lc, A_KV), lambda b: (b, 0, kcol)),
            pl.BlockSpec((1, lc, A_KV), lambda b: (b, 0, vcol)),
        ],
        out_specs=pl.BlockSpec((1, lc, A_Q), lambda b: (b, 0, 0)),
        out_shape=jax.ShapeDtypeStruct((bsz, lc, A_Q), BF16),
        compiler_params=_params(("arbitrary",)),
        name="context_attention",
    )(sink, proj_ctx, proj_ctx, proj_ctx)


def _hgrn_chunk(bq, bi, bf, lb, st, rev):
    heads = range(len(bq))
    q = [_silu(x) for x in bq]
    v = bi
    f = [lb[h] + (1.0 - lb[h]) * _sigmoid(bf[h]) for h in heads]
    k = [1.0 - x for x in f]
    g = [jnp.log(x) * LOG2E for x in f]
    b = [_cumsum_rows(x, rev) for x in g]
    btot = [x[0:1] if rev else x[CHUNK - 1:CHUNK] for x in b]
    vb = [_bf(x) for x in v]
    stb = [_bf(x) for x in st]
    o = [_dot_nt(_bf(q[h] * jnp.exp2(b[h])), stb[h]) for h in heads]
    khat = [_bf(k[h] * jnp.exp2(btot[h] - b[h])) for h in heads]
    dst = [_dot_tn(vb[h], khat[h]) for h in heads]
    st_new = [st[h] * jnp.exp2(btot[h]) + dst[h] for h in heads]
    nsub = CHUNK // SUB
    row8 = lax.broadcasted_iota(jnp.int32, (8, 1), 0)
    outs = [[] for _ in heads]
    for blk in range(nsub):
        r0 = blk * SUB
        rs = slice(r0, r0 + SUB)
        acc = [o[h][rs] for h in heads]
        if rev and blk < nsub - 1:
            ref_row, lo, hi = r0 + SUB, r0 + SUB, CHUNK
        elif (not rev) and blk > 0:
            ref_row, lo, hi = r0 - 1, 0, r0
        else:
            ref_row = None
        if ref_row is not None:
            bref = [b[h][ref_row:ref_row + 1] for h in heads]
            qn = [_bf(q[h][rs] * jnp.exp2(b[h][rs] - bref[h])) for h in heads]
            kn = [_bf(k[h][lo:hi] * jnp.exp2(bref[h] - b[h][lo:hi])) for h in heads]
            att = [_bf(_dot_nt(qn[h], kn[h])) for h in heads]
            acc = [acc[h] + _dot(att[h], vb[h][lo:hi]) for h in heads]
        piece = 8
        acc = [[a[p0:p0 + piece] for p0 in range(0, SUB, piece)] for a in acc]
        for s in range(SUB):
            reached = range(0, s // piece + 1) if rev else range(s // piece, SUB // piece)
            for pc in reached:
                t0 = r0 + pc * piece
                mask = (row8 + pc * piece <= s) if rev else (row8 + pc * piece >= s)
                for h in heads:
                    rel = b[h][t0:t0 + piece] - b[h][r0 + s:r0 + s + 1]
                    dec = jnp.exp2(jnp.where(mask, rel, -jnp.inf))
                    w = jnp.sum(q[h][t0:t0 + piece] * dec * k[h][r0 + s:r0 + s + 1], axis=-1, keepdims=True)
                    acc[h][pc] = acc[h][pc] + w * v[h][r0 + s:r0 + s + 1]
        for h in heads:
            outs[h].extend(acc[h])
    return [jnp.concatenate(x, axis=0) for x in outs], st_new


def _hgrn_kernel(q_ref, i_ref, f_ref, lb_ref, s0_ref, o_ref, sfin_ref, st_ref, *, rev, nchunk, nhead):
    c = pl.program_id(2)

    @pl.when(c == 0)
    def _():
        st_ref[...] = s0_ref[0]

    lanes = [slice(h * HEAD_DIM, (h + 1) * HEAD_DIM) for h in range(nhead)]
    lb = [lb_ref[:, ln] for ln in lanes]

    def body(ci, carry):
        cc = (nchunk - 1 - ci) if rev else ci
        rows = pl.ds(pl.multiple_of(cc * CHUNK, CHUNK), CHUNK)
        o, st_new = _hgrn_chunk([q_ref[0, rows, ln] for ln in lanes], [i_ref[0, rows, ln] for ln in lanes],
                                [f_ref[0, rows, ln] for ln in lanes], lb,
                                [st_ref[h] for h in range(nhead)], rev)
        for h in range(nhead):
            o_ref[0, rows, lanes[h]] = o[h]
            st_ref[h] = st_new[h]
        return carry

    lax.fori_loop(0, nchunk, body, 0, unroll=2 if nchunk % 2 == 0 else 1)

    @pl.when(c == pl.num_programs(2) - 1)
    def _():
        sfin_ref[0] = st_ref[...]


def _hgrn_scan(proj, lb, s0, rev, tblk, nhead):
    bsz, seq, _ = proj.shape
    nblk = seq // tblk
    width = nhead * HEAD_DIM
    base = A_Q + 2 * A_KV
    assert base % width == 0 and B_W % width == 0
    qc, ic = base // width, (base + B_W) // width
    fc = (base + (3 if rev else 2) * B_W) // width
    blk = (lambda c: nblk - 1 - c) if rev else (lambda c: c)

    def col(c0):
        return pl.BlockSpec((1, tblk, width), lambda b, h, c: (b, blk(c), c0 + h))

    state = pl.BlockSpec((1, nhead, HEAD_DIM, HEAD_DIM), lambda b, h, c: (b, h, 0, 0))
    return pl.pallas_call(
        functools.partial(_hgrn_kernel, rev=rev, nchunk=tblk // CHUNK, nhead=nhead),
        grid=(bsz, B_HEADS // nhead, nblk),
        in_specs=[
            col(qc), col(ic), col(fc),
            pl.BlockSpec((1, width), lambda b, h, c: (0, h)),
            state,
        ],
        out_specs=[
            pl.BlockSpec((1, tblk, width), lambda b, h, c: (b, blk(c), h)),
            state,
        ],
        out_shape=[
            jax.ShapeDtypeStruct((bsz, seq, B_W), F32),
            jax.ShapeDtypeStruct((bsz, B_HEADS, HEAD_DIM, HEAD_DIM), F32),
        ],
        scratch_shapes=[pltpu.VMEM((nhead, HEAD_DIM, HEAD_DIM), F32)],
        compiler_params=_params(("arbitrary", "arbitrary", "arbitrary")),
        name="hgrn2_bwd" if rev else "hgrn2_fwd",
    )(proj, proj, proj, lb, s0)


def _ab_out_kernel(oa_ref, of_ref, ob_ref, g0_ref, g1_ref, on_ref, w_ref, h_ref, gt_ref, o_ref, lhs_ref):
    lhs_ref[:, :A_Q] = oa_ref[0]
    half = B_W // 2
    for hd in range(B_HEADS):
        sl = slice(hd * HEAD_DIM, (hd + 1) * HEAD_DIM)
        o = of_ref[0, :, sl] + ob_ref[0, :, sl]
        o = o * lax.rsqrt(jnp.mean(o * o, axis=-1, keepdims=True) + EPS)
        o = o * on_ref[:, sl]
        gref = g0_ref if hd * HEAD_DIM < half else g1_ref
        gs = slice(hd * HEAD_DIM % half, hd * HEAD_DIM % half + HEAD_DIM)
        o = o * _silu(gref[0, :, gs])
        lhs_ref[:, A_Q + hd * HEAD_DIM:A_Q + (hd + 1) * HEAD_DIM] = _bf(o)
    y = _dot(lhs_ref[...], w_ref[...])
    o_ref[0] = h_ref[0] + gt_ref[0] * y


def _ab_out(oa, of, ob, proj, onorm, w_out, h, gt, tm):
    bsz, seq, d = h.shape
    half = B_W // 2
    gcol = (A_Q + 2 * A_KV + 4 * B_W) // half
    row = lambda b, i: (b, i, 0)
    return pl.pallas_call(
        _ab_out_kernel,
        grid=(bsz, seq // tm),
        in_specs=[
            pl.BlockSpec((1, tm, A_Q), row),
            pl.BlockSpec((1, tm, B_W), row),
            pl.BlockSpec((1, tm, B_W), row),
            pl.BlockSpec((1, tm, half), lambda b, i: (b, i, gcol)),
            pl.BlockSpec((1, tm, half), lambda b, i: (b, i, gcol + 1)),
            pl.BlockSpec((1, B_W), lambda b, i: (0, 0)),
            pl.BlockSpec((A_Q + B_W, d), lambda b, i: (0, 0)),
            pl.BlockSpec((1, tm, d), row),
            pl.BlockSpec((1, 1, d), lambda b, i: (b, 0, 0)),
        ],
        out_specs=pl.BlockSpec((1, tm, d), row),
        out_shape=jax.ShapeDtypeStruct((bsz, seq, d), F32),
        scratch_shapes=[pltpu.VMEM((tm, A_Q + B_W), BF16)],
        compiler_params=_params(("arbitrary", "arbitrary")),
        name="ab_outproj",
    )(oa, of, ob, proj, proj, onorm, w_out, h, gt)


def _mlp_kernel(*refs, final):
    if final:
        x_ref, g_ref, sc_ref, sh_ref, gt_ref, wu_ref, wd_ref, fn_ref, o_ref, u_ref = refs
    else:
        x_ref, g_ref, sc_ref, sh_ref, gt_ref, wu_ref, wd_ref, o_ref, u_ref = refs
    j = pl.program_id(2)

    tm = x_ref.shape[1]

    @pl.when(j == 0)
    def _():
        _normmod_rows(x_ref, g_ref, sc_ref, sh_ref, u_ref)

    hid = jnp.maximum(_dot(u_ref[...], wu_ref[...]), 0.0)
    hid = _bf(hid * hid)
    ncol = 512

    @pl.when(j == 0)
    def _():
        for n0 in range(0, o_ref.shape[2], ncol):
            o_ref[0, :, n0:n0 + ncol] = _dot(hid, wd_ref[:, n0:n0 + ncol])

    @pl.when(j != 0)
    def _():
        for n0 in range(0, o_ref.shape[2], ncol):
            o_ref[0, :, n0:n0 + ncol] += _dot(hid, wd_ref[:, n0:n0 + ncol])

    @pl.when(j == pl.num_programs(2) - 1)
    def _():
        piece = 128
        for r0 in range(0, tm, piece):
            rs = slice(r0, r0 + piece)
            y = x_ref[0, rs] + gt_ref[0] * o_ref[0, rs]
            if final:
                y = (y * lax.rsqrt(jnp.mean(y * y, axis=-1, keepdims=True) + EPS)) * fn_ref[...]
            o_ref[0, rs] = y


def _mlp(x, g, sc, sh, gt, w_up, w_down, final_gain, tm, tf):
    bsz, seq, d = x.shape
    dff = w_up.shape[1]
    vec = pl.BlockSpec((1, 1, d), lambda b, i, j: (b, 0, 0))
    in_specs = [
        pl.BlockSpec((1, tm, d), lambda b, i, j: (b, i, 0)),
        pl.BlockSpec((1, d), lambda b, i, j: (0, 0)),
        vec, vec, vec,
        pl.BlockSpec((d, tf), lambda b, i, j: (0, j)),
        pl.BlockSpec((tf, d), lambda b, i, j: (j, 0)),
    ]
    args = [x, g, sc, sh, gt, w_up, w_down]
    if final_gain is not None:
        in_specs.append(pl.BlockSpec((1, d), lambda b, i, j: (0, 0)))
        args.append(final_gain)
    return pl.pallas_call(
        functools.partial(_mlp_kernel, final=final_gain is not None),
        grid=(bsz, seq // tm, dff // tf),
        in_specs=in_specs,
        out_specs=pl.BlockSpec((1, tm, d), lambda b, i, j: (b, i, 0)),
        out_shape=jax.ShapeDtypeStruct((bsz, seq, d), F32),
        scratch_shapes=[pltpu.VMEM((tm, d), BF16)],
        compiler_params=_params(("arbitrary", "arbitrary", "arbitrary")),
        name="sqrelu_mlp",
    )(*args)


def _rk_proj_kernel(x_ref, xp_ref, xn_ref, g_ref, sc_ref, sh_ref, mix_ref, wb_ref, ws_ref, ob_ref, os_ref,
                    u_ref, xx_ref, lhs_ref, *, nbig, nsmall):
    i = pl.program_id(1)
    j = pl.program_id(2)
    tm = x_ref.shape[1]
    rb = 32

    def build(m, slot):
        mixrow = mix_ref[pl.ds(m, 1), :]
        for r0 in range(0, tm, rb):
            lhs_ref[slot, r0:r0 + rb] = _bf(u_ref[8 + r0:8 + r0 + rb] + xx_ref[r0:r0 + rb] * mixrow)

    @pl.when(j == 0)
    def _():
        g, sc, sh = g_ref[...], sc_ref[0], sh_ref[0]
        _normmod_rows(x_ref, g_ref, sc_ref, sh_ref, u_ref, row0=8)
        up = _normmod(xp_ref[0], g, sc, sh)[7:8]
        un = _normmod(xn_ref[0], g, sc, sh)[0:1]
        u_ref[7:8] = jnp.where(i == 0, 0.0, up)
        u_ref[8 + tm:9 + tm] = jnp.where(i == pl.num_programs(1) - 1, 0.0, un)
        mix0 = mix_ref[0:1, :]
        for r0 in range(0, tm, ROW_PIECE):
            u = u_ref[8 + r0:8 + r0 + ROW_PIECE]
            xx = 0.5 * (u_ref[7 + r0:7 + r0 + ROW_PIECE] + u_ref[9 + r0:9 + r0 + ROW_PIECE]) - u
            xx_ref[r0:r0 + ROW_PIECE] = xx
            lhs_ref[0, r0:r0 + ROW_PIECE] = _bf(u + xx * mix0)

    @pl.when(j < nbig)
    def _():
        build(j + 1, (j + 1) % 2)
        ob_ref[0, 0] = _dot(lhs_ref[j % 2], wb_ref[0])

    @pl.when(j == nbig)
    def _():
        for s in range(nsmall):
            os_ref[s, 0] = _dot(lhs_ref[(nbig + s) % 2], ws_ref[s])
            if s + 1 < nsmall:
                build(nbig + s + 1, (nbig + s + 1) % 2)


def _rk_proj(x, g, sc, sh, mix, w_big, w_small, tm):
    bsz, seq, d = x.shape
    nbig, _, n = w_big.shape
    nsmall, _, ns = w_small.shape
    r8 = tm // 8
    last8 = seq // 8 - 1
    vec = pl.BlockSpec((1, 1, d), lambda b, i, j: (b, 0, 0))
    return pl.pallas_call(
        functools.partial(_rk_proj_kernel, nbig=nbig, nsmall=nsmall),
        grid=(bsz, seq // tm, nbig + 1),
        in_specs=[
            pl.BlockSpec((1, tm, d), lambda b, i, j: (b, i, 0)),
            pl.BlockSpec((1, 8, d), lambda b, i, j: (b, jnp.maximum(i * r8 - 1, 0), 0)),
            pl.BlockSpec((1, 8, d), lambda b, i, j: (b, jnp.minimum((i + 1) * r8, last8), 0)),
            pl.BlockSpec((1, d), lambda b, i, j: (0, 0)),
            vec, vec,
            pl.BlockSpec((nbig + nsmall, d), lambda b, i, j: (0, 0)),
            pl.BlockSpec((1, d, n), lambda b, i, j: (jnp.minimum(j, nbig - 1), 0, 0)),
            pl.BlockSpec((nsmall, d, ns), lambda b, i, j: (0, 0, 0)),
        ],
        out_specs=[
            pl.BlockSpec((1, 1, tm, n), lambda b, i, j: (jnp.minimum(j, nbig - 1), b, i, 0)),
            pl.BlockSpec((nsmall, 1, tm, ns), lambda b, i, j: (0, b, i, 0)),
        ],
        out_shape=[
            jax.ShapeDtypeStruct((nbig, bsz, seq, n), F32),
            jax.ShapeDtypeStruct((nsmall, bsz, seq, ns), F32),
        ],
        scratch_shapes=[pltpu.VMEM((tm + 16, d), F32), pltpu.VMEM((tm, d), F32), pltpu.VMEM((2, tm, d), BF16)],
        compiler_params=_params(("arbitrary", "arbitrary", "arbitrary")),
        name="rwkv_proj",
    )(x, x, x, g, sc, sh, mix, w_big, w_small)


DECAY_SCALE = float(np.exp(-0.5))


def _pair_consts(rev):
    n = 2 * CHUNK
    ri = lax.broadcasted_iota(jnp.int32, (n, n), 0)
    ci = lax.broadcasted_iota(jnp.int32, (n, n), 1)
    same = (ri // CHUNK) == (ci // CHUNK)
    rt, ct = ri % CHUNK, ci % CHUNK
    strict = same & ((ct > rt) if rev else (ct < rt))
    incl = same & ((ct >= rt) if rev else (ct <= rt))
    eye = jnp.where(ri == ci, 1.0, 0.0).astype(F32)
    lane = lax.broadcasted_iota(jnp.int32, (1, LANES), 1)
    lane_lo = lane < C_HEAD
    m0 = jnp.where(lane_lo, 1.0, 0.0).astype(F32)
    m1 = 1.0 - m0
    blk2 = (ri // 2) == (ci // 2)
    level = {}
    m = 2
    while m < CHUNK:
        if m % 8:
            rsel, csel = ri, ci
        else:
            rc = lax.broadcasted_iota(jnp.int32, (n // 2, n), 0)
            csel = lax.broadcasted_iota(jnp.int32, (n // 2, n), 1)
            rsel = (rc // m) * (2 * m) + (0 if rev else m) + rc % m
        level[m] = ((rsel // (2 * m)) == (csel // (2 * m))) & ((rsel // m) != (csel // m))
        m *= 2
    return dict(strict=strict, incl=incl, eye=eye, m0=m0, m1=m1, lane_lo=lane_lo, blk2=blk2, level=level)


def _head_sums(x, cs):
    s0 = jnp.sum(x * cs["m0"], axis=-1, keepdims=True)
    s1 = jnp.sum(x * cs["m1"], axis=-1, keepdims=True)
    return jnp.where(cs["lane_lo"], s0, s1)


def _unit_tri_inverse(n, cs, rev):
    size = 2 * CHUNK
    t = [cs["eye"] + jnp.where(cs["blk2"], x, 0.0) for x in n]
    nb = [_bf(x) for x in n]
    m = 2
    while m < CHUNK:
        tb = [_bf(x) for x in t]
        sel = cs["level"][m]
        if m % 8:
            q = [_bf(_dot(a, b)) for a, b in zip(tb, nb)]
            q = [_dot(a, b) for a, b in zip(q, tb)]
            t = [a + jnp.where(sel, b, 0.0) for a, b in zip(t, q)]
        else:
            starts = range(0 if rev else m, size, 2 * m)
            pick = lambda x: jnp.concatenate([x[r0:r0 + m] for r0 in starts], axis=0)
            q = [_bf(_dot(_bf(pick(x)), b)) for x, b in zip(t, nb)]
            q = [_dot(a, b) for a, b in zip(q, tb)]
            t = [_add_rows(x, jnp.where(sel, y, 0.0), starts, m) for x, y in zip(t, q)]
        m *= 2
    return t


def _add_rows(x, upd, starts, m):
    pieces, pos = [], 0
    for k, r0 in enumerate(starts):
        if r0 > pos:
            pieces.append(x[pos:r0])
        pieces.append(x[r0:r0 + m] + upd[k * m:(k + 1) * m])
        pos = r0 + m
    if pos < x.shape[0]:
        pieces.append(x[pos:])
    return jnp.concatenate(pieces, axis=0)


def _halves(top, bottom):
    return jnp.concatenate([top[:CHUNK], bottom[CHUNK:]], axis=0)


def _stack2(x, cs):
    return jnp.concatenate([x * cs["m0"], x * cs["m1"]], axis=0)


def _rwkv_chunk(r, k, v, lw, la, w2, a2, w0, a0, kkv, kav, ht, cs, rev):
    pairs = range(len(r))
    tl, lab = _bf(jnp.tanh(lw)), _bf(la)
    z = [w0[p] + _dot(tl, w2[p]) for p in pairs]
    za = [_dot(lab, a2[p]) for p in pairs]
    kkr = [k[p] * kkv[p] for p in pairs]
    ss = [_head_sums(x * x, cs) for x in kkr]
    ld = [-(DECAY_SCALE * LOG2E) * _sigmoid(x) for x in z]
    cl = [_cumsum_rows(x, rev) for x in ld]
    a = [_sigmoid(a0[p] + za[p]) for p in pairs]
    kk = [x / jnp.maximum(jnp.sqrt(s), 1e-12) for x, s in zip(kkr, ss)]
    kd = [k[p] * (1.0 + (a[p] - 1.0) * kav[p]) for p in pairs]
    bb = [x * y for x, y in zip(kk, a)]
    ctot = [x[0:1] if rev else x[CHUNK - 1:CHUNK] for x in cl]
    e_neg = [jnp.exp2(-x) for x in cl]
    e_tail = [jnp.exp2(c - x) for c, x in zip(ctot, cl)]
    at = [_bf(_stack2(-kk[p] * jnp.exp2(cl[p] - ld[p]), cs)) for p in pairs]
    rt = [_bf(_stack2(r[p] * jnp.exp2(cl[p]), cs)) for p in pairs]
    rhs = [_bf(jnp.concatenate([bb[p] * e_neg[p], kd[p] * e_neg[p]], axis=0)) for p in pairs]
    g = [_dot_nt(jnp.concatenate([x, y], axis=0), w) for x, y, w in zip(at, rt, rhs)]
    g1 = [x[:2 * CHUNK] for x in g]
    g2 = [x[2 * CHUNK:] for x in g]
    strict, incl = cs["strict"], cs["incl"]
    g1r = [pltpu.roll(x, C_HEAD, 1) for x in g1]
    n_ab = [jnp.where(strict, _halves(x, y), 0.0) for x, y in zip(g1, g1r)]
    n_ak = [_bf(jnp.where(strict, _halves(y, x), 0.0)) for x, y in zip(g1, g1r)]
    t = _unit_tri_inverse(n_ab, cs, rev)
    vs = [_bf(_stack2(x, cs)) for x in v]
    hkv = [_bf(x.T) for x in ht]
    xx = [_bf(_dot(jnp.concatenate([at[p], n_ak[p]], axis=1), jnp.concatenate([hkv[p], vs[p]], axis=0)))
          for p in pairs]
    u =[_bf(_dot(_bf(a_), b_)) for a_, b_ in zip(t, xx)]
    bk = [_bf(jnp.concatenate([_stack2(bb[p] * e_tail[p], cs), _stack2(kd[p] * e_tail[p], cs)], axis=0))
          for p in pairs]
    uv = [jnp.concatenate([a_, b_], axis=0) for a_, b_ in zip(u, vs)]
    dh = [_dot_tn(a_, b_) for a_, b_ in zip(uv, bk)]
    ht_new = [ht[p] * jnp.exp2(ctot[p]) + dh[p] for p in pairs]
    g2r = [pltpu.roll(x, C_HEAD, 1) for x in g2]
    n_rb = [_bf(jnp.where(incl, _halves(x, y), 0.0)) for x, y in zip(g2, g2r)]
    n_rk = [_bf(jnp.where(incl, _halves(y, x), 0.0)) for x, y in zip(g2, g2r)]
    ysum = [_dot(jnp.concatenate([rt[p], n_rb[p], n_rk[p]], axis=1),
                 jnp.concatenate([hkv[p], u[p], vs[p]], axis=0)) for p in pairs]
    y = [x[:CHUNK] + x[CHUNK:] for x in ysum]
    return y, ht_new, kd


def _rwkv_combine(y, yf, r, k, v, kd, laf, a2f, a0f, kav, rkv, lng, lnb, cs):
    pairs = range(len(y))
    inv_n = 1.0 / C_HEAD
    lafb = _bf(laf)
    zf = [_dot(lafb, a2f[p]) for p in pairs]
    ysum = [a + b for a, b in zip(y, yf)]
    mu = [_head_sums(x, cs) * inv_n for x in ysum]
    dev = [a - b for a, b in zip(ysum, mu)]
    var = [_head_sums(x * x, cs) * inv_n for x in dev]
    a_f = [_sigmoid(a0f[p] + zf[p]) for p in pairs]
    kd_f = [k[p] * (1.0 + (a_f[p] - 1.0) * kav[p]) for p in pairs]
    bsum = [_head_sums(r[p] * (kd_f[p] + kd[p]) * rkv[p], cs) for p in pairs]
    return [dev[p] * lax.rsqrt(var[p] + GN_EPS) * lng[p] + lnb[p] + bsum[p] * v[p] for p in pairs]


def _rwkv_scan_kernel(*refs, rev, nchunk, npair, combine):
    if combine:
        (r_ref, k_ref, v_ref, lw_ref, la_ref, w2_ref, a2_ref, w0_ref, a0_ref, kk_ref, ka_ref, s0_ref,
         yf_ref, laf_ref, a2f_ref, a0f_ref, rk_ref, lng_ref, lnb_ref, o_ref, sfin_ref, ht_ref) = refs
    else:
        (r_ref, k_ref, v_ref, lw_ref, la_ref, w2_ref, a2_ref, w0_ref, a0_ref, kk_ref, ka_ref, s0_ref,
         o_ref, sfin_ref, ht_ref) = refs
    c = pl.program_id(2)

    @pl.when(c == 0)
    def _():
        ht_ref[...] = s0_ref[0]

    cs = _pair_consts(rev)

    def body(ci, carry):
        cc = (nchunk - 1 - ci) if rev else ci
        rows = pl.ds(pl.multiple_of(cc * CHUNK, CHUNK), CHUNK)
        lw = lw_ref[0, 0, rows, :]
        la = la_ref[0, 0, rows, :]
        lanes = [slice(pr * LANES, (pr + 1) * LANES) for pr in range(npair)]
        r = [r_ref[0, 0, rows, ln] for ln in lanes]
        k = [k_ref[0, 0, rows, ln] for ln in lanes]
        v = [v_ref[0, 0, rows, ln] for ln in lanes]
        kav = [ka_ref[:, ln] for ln in lanes]
        y, ht_new, kd = _rwkv_chunk(
            r, k, v, lw, la, [w2_ref[0, :, ln] for ln in lanes], [a2_ref[0, :, ln] for ln in lanes],
            [w0_ref[0, :, ln] for ln in lanes], [a0_ref[0, :, ln] for ln in lanes],
            [kk_ref[:, ln] for ln in lanes], kav, [ht_ref[pr] for pr in range(npair)], cs, rev)
        for pr in range(npair):
            ht_ref[pr] = ht_new[pr]
        if combine:
            y = _rwkv_combine(
                y, [yf_ref[0, rows, ln] for ln in lanes], r, k, v, kd, laf_ref[0, 0, rows, :],
                [a2f_ref[0, :, ln] for ln in lanes], [a0f_ref[0, :, ln] for ln in lanes], kav,
                [rk_ref[:, ln] for ln in lanes], [lng_ref[:, ln] for ln in lanes],
                [lnb_ref[:, ln] for ln in lanes], cs)
        for pr in range(npair):
            o_ref[0, rows, lanes[pr]] = y[pr]
        return carry

    lax.fori_loop(0, nchunk, body, 0, unroll=2 if nchunk % 2 == 0 else 1)

    @pl.when(c == pl.num_programs(2) - 1)
    def _():
        sfin_ref[0] = ht_ref[...]


def _rwkv_scan(rkv, small, prm, s0, rev, tblk, npair, y_fwd=None):
    _, bsz, seq, d = rkv.shape
    nblk = seq // tblk
    width = npair * LANES
    e = 1 if rev else 0
    blk = (lambda c: nblk - 1 - c) if rev else (lambda c: c)
    combine = y_fwd is not None

    def tok(m):
        return pl.BlockSpec((1, 1, tblk, width), lambda b, p, c: (m, b, blk(c), p))

    def lora(m, half):
        return pl.BlockSpec((1, 1, tblk, LANES), lambda b, p, c: (m, b, blk(c), half))

    def mat(idx):
        return pl.BlockSpec((1, LANES, width), lambda b, p, c: (idx, 0, p))

    def vec3(idx):
        return pl.BlockSpec((1, 1, width), lambda b, p, c: (idx, 0, p))

    vec = pl.BlockSpec((1, width), lambda b, p, c: (0, p))
    state = pl.BlockSpec((1, npair, LANES, LANES), lambda b, p, c: (b, p, 0, 0))
    out_tok = pl.BlockSpec((1, tblk, width), lambda b, p, c: (b, blk(c), p))
    in_specs = [tok(0), tok(1), tok(2), lora(1, e), lora(2, e), mat(e), mat(e), vec3(e), vec3(e), vec, vec, state]
    args = [rkv, rkv, rkv, small, small, prm["w2"], prm["a2"], prm["w0"], prm["a0"], prm["k_k"], prm["k_a"], s0]
    if combine:
        in_specs += [out_tok, lora(2, 0), mat(0), vec3(0), vec, vec, vec]
        args += [y_fwd, small, prm["a2"], prm["a0"], prm["r_k"], prm["ln_g"], prm["ln_b"]]
    return pl.pallas_call(
        functools.partial(_rwkv_scan_kernel, rev=rev, nchunk=tblk // CHUNK, npair=npair, combine=combine),
        grid=(bsz, d // width, nblk),
        in_specs=in_specs,
        out_specs=[out_tok, state],
        out_shape=[
            jax.ShapeDtypeStruct((bsz, seq, d), F32),
            jax.ShapeDtypeStruct((bsz, d // LANES, LANES, LANES), F32),
        ],
        scratch_shapes=[pltpu.VMEM((npair, LANES, LANES), F32)],
        compiler_params=_params(("arbitrary", "arbitrary", "arbitrary")),
        name="rwkv7_bwd" if rev else "rwkv7_fwd",
    )(*args)


def _rk_out_kernel(z_ref, gs_ref, g2_ref, wo_ref, h_ref, gt_ref, o_ref):
    gate = _dot(_bf(_sigmoid(gs_ref[0, 0])), g2_ref[...])
    y = _dot(_bf(z_ref[0] * gate), wo_ref[...])
    o_ref[0] = h_ref[0] + gt_ref[0] * y


def _rk_out(z, small, g2, wo, h, gt, tm):
    bsz, seq, d = h.shape
    row = lambda b, i: (b, i, 0)
    glora = g2.shape[0]
    return pl.pallas_call(
        _rk_out_kernel,
        grid=(bsz, seq // tm),
        in_specs=[
            pl.BlockSpec((1, tm, d), row),
            pl.BlockSpec((1, 1, tm, glora), lambda b, i: (0, b, i, 0)),
            pl.BlockSpec((glora, d), lambda b, i: (0, 0)),
            pl.BlockSpec((d, d), lambda b, i: (0, 0)),
            pl.BlockSpec((1, tm, d), row),
            pl.BlockSpec((1, 1, d), lambda b, i: (b, 0, 0)),
        ],
        out_specs=pl.BlockSpec((1, tm, d), row),
        out_shape=jax.ShapeDtypeStruct((bsz, seq, d), F32),
        compiler_params=_params(("arbitrary", "arbitrary")),
        name="rwkv_outproj",
    )(z, small, g2, wo, h, gt)


def _rope_tables(seq):
    t = jnp.arange(seq, dtype=jnp.int32)
    rows = (t // GRID_W).astype(F32)
    cols = (t % GRID_W).astype(F32)
    half = HEAD_DIM // 2
    n_freq = half // 2
    inv = ROPE_BASE ** (-jnp.arange(n_freq, dtype=F32) / n_freq)
    lane = jnp.arange(HEAD_DIM)
    pos = jnp.where((lane < half)[None, :], rows[:, None], cols[:, None])
    ang = pos * inv[lane % n_freq][None, :]
    cos, sin = jnp.cos(ang), jnp.sin(ang)
    first = ((lane % half) < n_freq)[None, :]
    return cos, jnp.where(first, -sin, 0.0), jnp.where(first, 0.0, sin)


def _pad_lanes(w, axis):
    pad = [(0, 0)] * w.ndim
    pad[axis] = (0, LANES - w.shape[axis])
    return jnp.pad(w, pad)


def _row_tile(seq, want):
    return want if seq % want == 0 else seq


def kernel(x, c, ctx, c_ctx, mod_w, mod_b, norm_mix, norm_ffn, ffn_up, ffn_down, ab_w_in, ab_w_out, attn_sink, hgrn_lb, hgrn_onorm, rk_mix, rk_wr, rk_wk, rk_wv, rk_wo, rk_w0, rk_w1, rk_w2, rk_a0, rk_a1, rk_a2, rk_g1, rk_g2, rk_kk, rk_ka, rk_rk, rk_ln_g, rk_ln_b, final_norm):
    bsz, seq, d = x.shape
    lc = ctx.shape[1]
    depth = mod_w.shape[0]
    assert bsz + 1 <= 8 and seq % 1024 == 0 and lc % CHUNK == 0

    cond8 = jnp.zeros((8, d), F32).at[:bsz].set(c).at[bsz].set(c_ctx)
    mod = _modulation(cond8, mod_w, mod_b)
    lb_all = jnp.cumsum(jax.nn.softmax(hgrn_lb.astype(F32), axis=0), axis=0)
    rope = _rope_tables(seq)

    h, hc = x, ctx
    for layer in range(depth):
        last = layer == depth - 1
        jl = layer // 2
        m_lat = mod[layer, :bsz].reshape(bsz, 1, 6, d)
        m_ctx = jnp.broadcast_to(mod[layer, bsz].reshape(1, 1, 6, d), (bsz, 1, 6, d))
        sh1, sc1, gt1, sh2, sc2, gt2 = (m_lat[:, :, i] for i in range(6))
        csh1, csc1, cgt1, csh2, csc2, cgt2 = (m_ctx[:, :, i] for i in range(6))
        g_mix = norm_mix[layer].reshape(1, d)
        g_ffn = norm_ffn[layer].reshape(1, d)
        if layer % 2 == 0:
            w_in = _bf(ab_w_in[jl])
            w_out = _bf(ab_w_out[jl])
            sink = attn_sink[jl].astype(F32)
            lb = lb_all[jl].reshape(1, B_W)
            onorm = hgrn_onorm[jl].reshape(1, B_W).astype(F32)
            n_in = w_in.shape[1]
            pc = _inproj(hc, g_mix, csc1, csh1, w_in, None, lc, 512)
            pl_ = _inproj_stream(h, g_mix, sc1, sh1, w_in, rope, 512, n_in // 4)
            oa = _win_attn(sink, pl_, pc)
            zeros = jnp.zeros((bsz, B_HEADS, HEAD_DIM, HEAD_DIM), F32)
            nhead = 4
            ocf, scf = _hgrn_scan(pc, lb, zeros, False, lc, nhead)
            ocb, scb = _hgrn_scan(pc, lb, zeros, True, lc, nhead)
            olf, _ = _hgrn_scan(pl_, lb, scf, False, 512, nhead)
            olb, _ = _hgrn_scan(pl_, lb, scb, True, 512, nhead)
            h = _ab_out(oa, olf, olb, pl_, onorm, w_out, h, gt1, 512)
            if not last:
                oca = _ctx_attn(sink, pc)
                hc = _ab_out(oca, ocf, ocb, pc, onorm, w_out, hc, cgt1, lc)
        else:
            w_big = _bf(jnp.stack([rk_wr[jl], rk_wk[jl], rk_wv[jl]]))
            w_small = _bf(jnp.stack([
                rk_g1[jl],
                jnp.concatenate([_pad_lanes(rk_w1[jl, 0], 1), _pad_lanes(rk_w1[jl, 1], 1)], axis=1),
                jnp.concatenate([_pad_lanes(rk_a1[jl, 0], 1), _pad_lanes(rk_a1[jl, 1], 1)], axis=1),
            ]))
            assert w_small.shape[-1] == 2 * LANES
            mix = rk_mix[jl]
            prm = dict(
                w2=_bf(_pad_lanes(rk_w2[jl], 1)), a2=_bf(_pad_lanes(rk_a2[jl], 1)),
                w0=rk_w0[jl].reshape(2, 1, d), a0=rk_a0[jl].reshape(2, 1, d),
                k_k=rk_kk[jl].reshape(1, d), k_a=rk_ka[jl].reshape(1, d), r_k=rk_rk[jl].reshape(1, d),
                ln_g=rk_ln_g[jl].reshape(1, d), ln_b=rk_ln_b[jl].reshape(1, d),
            )
            g2 = _bf(rk_g2[jl])
            wo = _bf(rk_wo[jl])
            mix = mix[jnp.array([0, 2, 3, 5, 1, 4])]
            rkv_c, sm_c = _rk_proj(hc, g_mix, csc1, csh1, mix, w_big, w_small, lc)
            rkv_l, sm_l = _rk_proj(h, g_mix, sc1, sh1, mix, w_big, w_small, 512)
            zeros = jnp.zeros((bsz, d // LANES, LANES, LANES), F32)
            npair = 16
            ycf, s_f = _rwkv_scan(rkv_c, sm_c, prm, zeros, False, lc, npair)
            zc, s_b = _rwkv_scan(rkv_c, sm_c, prm, zeros, True, lc, npair, y_fwd=ycf)
            ylf, _ = _rwkv_scan(rkv_l, sm_l, prm, s_f, False, 256, npair)
            zl, _ = _rwkv_scan(rkv_l, sm_l, prm, s_b, True, 256, npair, y_fwd=ylf)
            h = _rk_out(zl, sm_l, g2, wo, h, gt1, 512)
            if not last:
                hc = _rk_out(zc, sm_c, g2, wo, hc, cgt1, lc)
        w_up = _bf(ffn_up[layer])
        w_dn = _bf(ffn_down[layer])
        h = _mlp(h, g_ffn, sc2, sh2, gt2, w_up, w_dn, final_norm.reshape(1, d) if last else None, 1024, 512)
        if not last:
            hc = _mlp(hc, g_ffn, csc2, csh2, cgt2, w_up, w_dn, None, lc, 512)
    return h
```

```python
import functools

import jax
import jax.numpy as jnp
import numpy as np
from jax import lax
from jax.experimental import pallas as pl
from jax.experimental.pallas import tpu as pltpu

F32 = jnp.float32
BF16 = jnp.bfloat16
HIGHEST = lax.Precision.HIGHEST

LANES = 128
HEAD_DIM = 128
GRID_W = 64
WINDOW = 128
ROPE_BASE = 10000.0
A_Q_HEADS = 8
A_KV_HEADS = 2
A_GROUP = A_Q_HEADS // A_KV_HEADS
A_Q = A_Q_HEADS * HEAD_DIM
A_KV = A_KV_HEADS * HEAD_DIM
B_HEADS = 8
B_W = B_HEADS * HEAD_DIM
C_HEAD = 64
CHUNK = 64
SUB = 16
EPS = 1e-6
GN_EPS = 64e-5
VMEM_LIMIT = 58 * 1024 * 1024
LOG2E = float(np.log2(np.e))

NT_DIMS = (((1,), (1,)), ((), ()))
TN_DIMS = (((0,), (0,)), ((), ()))


def _dot(a, b, **kw):
    return jnp.dot(a, b, preferred_element_type=F32, **kw)


def _dot_nt(a, b):
    return lax.dot_general(a, b, NT_DIMS, preferred_element_type=F32)


def _dot_tn(a, b):
    return lax.dot_general(a, b, TN_DIMS, preferred_element_type=F32)


def _bf(x):
    return x.astype(BF16)


def _dot_split(x, m):
    hi = _bf(x)
    lo = _bf(x - hi.astype(F32))
    return _dot(hi, m) + _dot(lo, m)


def _sigmoid(x):
    return 1.0 / (1.0 + jnp.exp(-x))


def _silu(x):
    return x * _sigmoid(x)


def _normmod(x, g, sc, sh):
    return _normscale(x, g * (1.0 + sc), sh)


def _normscale(x, gain, shift):
    ms = jnp.mean(x * x, axis=-1, keepdims=True)
    return (x * lax.rsqrt(ms + EPS)) * gain + shift


ROW_PIECE = 16


def _for_row_pieces(rows, fn):
    def body(p, carry):
        fn(pl.ds(pl.multiple_of(p * ROW_PIECE, ROW_PIECE), ROW_PIECE))
        return carry

    lax.fori_loop(0, rows // ROW_PIECE, body, 0, unroll=8)


def _normmod_rows(x_ref, g_ref, sc_ref, sh_ref, u_ref, row0=0):
    gain, shift = g_ref[...] * (1.0 + sc_ref[0]), sh_ref[0]

    def piece(rs):
        dst = rs if row0 == 0 else pl.ds(rs.start + row0, ROW_PIECE)
        u_ref[dst] = _normscale(x_ref[0, rs], gain, shift).astype(u_ref.dtype)

    _for_row_pieces(x_ref.shape[1], piece)


def _cumsum_rows(x, rev):
    n = x.shape[0]
    row = lax.broadcasted_iota(jnp.int32, (n, 1), 0)
    s = 1
    while s < n:
        if rev:
            x = x + jnp.where(row < n - s, pltpu.roll(x, n - s, 0), 0.0)
        else:
            x = x + jnp.where(row >= s, pltpu.roll(x, s, 0), 0.0)
        s *= 2
    return x


def _params(sem):
    return pltpu.CompilerParams(dimension_semantics=sem, vmem_limit_bytes=VMEM_LIMIT)


def _mod_kernel(c_ref, w_ref, b_ref, o_ref):
    s = _bf(_silu(c_ref[...]))
    o_ref[0] = _dot(s, _bf(w_ref[0])) + b_ref[0]


def _modulation(cond8, mod_w, mod_b):
    depth, d, n = mod_w.shape
    tn = 1024
    return pl.pallas_call(
        _mod_kernel,
        grid=(depth, n // tn),
        in_specs=[
            pl.BlockSpec((8, d), lambda l, j: (0, 0)),
            pl.BlockSpec((1, d, tn), lambda l, j: (l, 0, j)),
            pl.BlockSpec((1, 1, tn), lambda l, j: (l, 0, j)),
        ],
        out_specs=pl.BlockSpec((1, 8, tn), lambda l, j: (l, 0, j)),
        out_shape=jax.ShapeDtypeStruct((depth, 8, n), F32),
        compiler_params=_params(("arbitrary", "arbitrary")),
        name="modulation",
    )(cond8, mod_w, mod_b.reshape(depth, 1, n))


def _inproj_kernel(*refs, n_rope):
    if n_rope:
        x_ref, g_ref, sc_ref, sh_ref, w_ref, cos_ref, sna_ref, snb_ref, o_ref, u_ref = refs
    else:
        x_ref, g_ref, sc_ref, sh_ref, w_ref, o_ref, u_ref = refs
    j = pl.program_id(2)

    @pl.when(j == 0)
    def _():
        _normmod_rows(x_ref, g_ref, sc_ref, sh_ref, u_ref)

    acc = _dot(u_ref[...], w_ref[...])
    if not n_rope:
        o_ref[0] = acc
        return

    per_tile = acc.shape[1] // HEAD_DIM
    n_full, n_rem = n_rope // per_tile, n_rope % per_tile

    def store(n_rot):
        cos, sna, snb = cos_ref[...], sna_ref[...], snb_ref[...]
        for hd in range(n_rot):
            sl = acc[:, hd * HEAD_DIM:(hd + 1) * HEAD_DIM]
            rot = sl * cos + pltpu.roll(sl, 96, 1) * sna + pltpu.roll(sl, 32, 1) * snb
            o_ref[0, :, hd * HEAD_DIM:(hd + 1) * HEAD_DIM] = rot
        if n_rot < per_tile:
            o_ref[0, :, n_rot * HEAD_DIM:] = acc[:, n_rot * HEAD_DIM:]

    pl.when(j < n_full)(lambda: store(per_tile))
    pl.when(j == n_full)(lambda: store(n_rem))
    pl.when(j > n_full)(lambda: store(0))


def _inproj(x, g, sc, sh, w, rope, tm, tn):
    bsz, seq, d = x.shape
    n = w.shape[1]
    n_rope = 0
    in_specs = [
        pl.BlockSpec((1, tm, d), lambda b, i, j: (b, i, 0)),
        pl.BlockSpec((1, d), lambda b, i, j: (0, 0)),
        pl.BlockSpec((1, 1, d), lambda b, i, j: (b, 0, 0)),
        pl.BlockSpec((1, 1, d), lambda b, i, j: (b, 0, 0)),
        pl.BlockSpec((d, tn), lambda b, i, j: (0, j)),
    ]
    args = [x, g, sc, sh, w]
    if rope is not None:
        n_rope = A_Q_HEADS + A_KV_HEADS
        assert tn % HEAD_DIM == 0
        in_specs += [pl.BlockSpec((tm, HEAD_DIM), lambda b, i, j: (i, 0))] * 3
        args += list(rope)
    return pl.pallas_call(
        functools.partial(_inproj_kernel, n_rope=n_rope),
        grid=(bsz, seq // tm, n // tn),
        in_specs=in_specs,
        out_specs=pl.BlockSpec((1, tm, tn), lambda b, i, j: (b, i, j)),
        out_shape=jax.ShapeDtypeStruct((bsz, seq, n), F32),
        scratch_shapes=[pltpu.VMEM((tm, d), BF16)],
        compiler_params=_params(("arbitrary", "arbitrary", "arbitrary")),
        name="ab_inproj",
    )(*args)


def _inproj_stream_kernel(x_ref, g_ref, sc_ref, sh_ref, w_ref, cos_ref, sna_ref, snb_ref, o_ref,
                          lhs0_ref, lhs1_ref, *, n_rope):
    j = pl.program_id(0)
    i = pl.program_id(2)
    tm = x_ref.shape[1]
    rb = 32

    @pl.when((j == 0) & (pl.program_id(1) == 0) & (i == 0))
    def _():
        lhs1_ref[...] = jnp.zeros_like(lhs1_ref)

    def step(build_ref, ready_ref):
        gain, shift = g_ref[...] * (1.0 + sc_ref[0]), sh_ref[0]
        for r0 in range(0, tm, rb):
            build_ref[r0:r0 + rb] = _bf(_normscale(x_ref[0, r0:r0 + rb], gain, shift))
        o_ref[0] = _dot(ready_ref[...], w_ref[...])

    pl.when(i % 2 == 0)(lambda: step(lhs0_ref, lhs1_ref))
    pl.when(i % 2 == 1)(lambda: step(lhs1_ref, lhs0_ref))

    @pl.when(j == 0)
    def _():
        cos, sna, snb = cos_ref[...], sna_ref[...], snb_ref[...]
        for hd in range(n_rope):
            sl = o_ref[0, :, hd * HEAD_DIM:(hd + 1) * HEAD_DIM]
            rot = sl * cos + pltpu.roll(sl, 96, 1) * sna + pltpu.roll(sl, 32, 1) * snb
            o_ref[0, :, hd * HEAD_DIM:(hd + 1) * HEAD_DIM] = rot


def _inproj_stream(x, g, sc, sh, w, rope, tm, tn):
    bsz, seq, d = x.shape
    n = w.shape[1]
    ni = seq // tm
    n_rope = A_Q_HEADS + A_KV_HEADS
    assert n_rope * HEAD_DIM <= tn and n % tn == 0
    cur = lambda j, b, i: (b, jnp.minimum(i, ni - 1), 0)
    done = lambda i: jnp.maximum(i - 1, 0)
    table = pl.BlockSpec((tm, HEAD_DIM), lambda j, b, i: (done(i), 0))
    return pl.pallas_call(
        functools.partial(_inproj_stream_kernel, n_rope=n_rope),
        grid=(n // tn, bsz, ni + 1),
        in_specs=[
            pl.BlockSpec((1, tm, d), cur),
            pl.BlockSpec((1, d), lambda j, b, i: (0, 0)),
            pl.BlockSpec((1, 1, d), lambda j, b, i: (b, 0, 0)),
            pl.BlockSpec((1, 1, d), lambda j, b, i: (b, 0, 0)),
            pl.BlockSpec((d, tn), lambda j, b, i: (0, j)),
            table, table, table,
        ],
        out_specs=pl.BlockSpec((1, tm, tn), lambda j, b, i: (b, done(i), j)),
        out_shape=jax.ShapeDtypeStruct((bsz, seq, n), F32),
        scratch_shapes=[pltpu.VMEM((tm, d), BF16), pltpu.VMEM((tm, d), BF16)],
        compiler_params=_params(("arbitrary", "arbitrary", "arbitrary")),
        name="ab_inproj_stream",
    )(x, g, sc, sh, w, *rope)


def _softmax_av(s_list, v_list, sink_col):
    m = sink_col
    for s in s_list:
        m = jnp.maximum(m, jnp.max(s, axis=-1, keepdims=True))
    den = jnp.exp2(sink_col - m)
    out = None
    for s, v in zip(s_list, v_list):
        p = jnp.exp2(s - m)
        den = den + jnp.sum(p, axis=-1, keepdims=True)
        o = _dot(_bf(p), v)
        out = o if out is None else out + o
    return out / den


def _sink_column(sink_ref, hk, rows):
    rowh = lax.broadcasted_iota(jnp.int32, (rows, 1), 0) // WINDOW
    col = jnp.full((rows, 1), sink_ref[hk * A_GROUP + A_GROUP - 1], F32)
    for g in range(A_GROUP - 1):
        col = jnp.where(rowh == g, sink_ref[hk * A_GROUP + g], col)
    return col * LOG2E


def _win_attn_kernel(sink_ref, q_ref, kp_ref, kc_ref, kn_ref, vp_ref, vc_ref, vn_ref,
                     ck_ref, cv_ref, o_ref):
    n = pl.program_id(1)
    nb = pl.num_programs(1)
    scale = HEAD_DIM ** -0.5 * LOG2E
    q = q_ref[0]
    kband = jnp.concatenate([kp_ref[0], kc_ref[0], kn_ref[0]], axis=0)
    vband = jnp.concatenate([vp_ref[0], vc_ref[0], vn_ref[0]], axis=0)
    rows = A_GROUP * WINDOW
    tq = lax.broadcasted_iota(jnp.int32, (rows, 3 * WINDOW), 0) % WINDOW
    tk = lax.broadcasted_iota(jnp.int32, (rows, 3 * WINDOW), 1)
    rel = tk - WINDOW - tq
    valid = (jnp.abs(rel) <= WINDOW) & ((tk >= WINDOW) | (n > 0)) & ((tk < 2 * WINDOW) | (n < nb - 1))
    for hk in range(A_KV_HEADS):
        qs = jnp.concatenate(
            [q[:, (hk * A_GROUP + g) * HEAD_DIM:(hk * A_GROUP + g + 1) * HEAD_DIM] for g in range(A_GROUP)],
            axis=0)
        qs = _bf(qs)
        hs = slice(hk * HEAD_DIM, (hk + 1) * HEAD_DIM)
        s_win = _dot_nt(qs, _bf(kband[:, hs])) * scale
        s_win = jnp.where(valid, s_win, -jnp.inf)
        s_ctx = _dot_nt(qs, _bf(ck_ref[0][:, hs])) * scale
        o = _softmax_av([s_win, s_ctx], [_bf(vband[:, hs]), _bf(cv_ref[0][:, hs])],
                        _sink_column(sink_ref, hk, rows))
        for g in range(A_GROUP):
            h = hk * A_GROUP + g
            o_ref[0, :, h * HEAD_DIM:(h + 1) * HEAD_DIM] = _bf(o[g * WINDOW:(g + 1) * WINDOW])


def _win_attn(sink, proj, proj_ctx):
    bsz, seq, _ = proj.shape
    lc = proj_ctx.shape[1]
    nb = seq // WINDOW
    kcol, vcol = A_Q // A_KV, A_Q // A_KV + 1
    prev = lambda b, n: (b, jnp.maximum(n - 1, 0))
    nxt = lambda b, n: (b, jnp.minimum(n + 1, nb - 1))
    cur = lambda b, n: (b, n)

    def band(rowfn, col):
        return pl.BlockSpec((1, WINDOW, A_KV), lambda b, n: rowfn(b, n) + (col,))

    return pl.pallas_call(
        _win_attn_kernel,
        grid=(bsz, nb),
        in_specs=[
            pl.BlockSpec(memory_space=pltpu.SMEM),
            pl.BlockSpec((1, WINDOW, A_Q), lambda b, n: (b, n, 0)),
            band(prev, kcol), band(cur, kcol), band(nxt, kcol),
            band(prev, vcol), band(cur, vcol), band(nxt, vcol),
            pl.BlockSpec((1, lc, A_KV), lambda b, n: (b, 0, kcol)),
            pl.BlockSpec((1, lc, A_KV), lambda b, n: (b, 0, vcol)),
        ],
        out_specs=pl.BlockSpec((1, WINDOW, A_Q), lambda b, n: (b, n, 0)),
        out_shape=jax.ShapeDtypeStruct((bsz, seq, A_Q), BF16),
        compiler_params=_params(("arbitrary", "arbitrary")),
        name="window_attention",
    )(sink, proj, proj, proj, proj, proj, proj, proj, proj_ctx, proj_ctx)


def _ctx_attn_kernel(sink_ref, q_ref, k_ref, v_ref, o_ref):
    scale = HEAD_DIM ** -0.5 * LOG2E
    q = q_ref[0]
    lc = q.shape[0]
    for h in range(A_Q_HEADS):
        hk = h // A_GROUP
        hs = slice(hk * HEAD_DIM, (hk + 1) * HEAD_DIM)
        qs = _bf(q[:, h * HEAD_DIM:(h + 1) * HEAD_DIM])
        s = _dot_nt(qs, _bf(k_ref[0][:, hs])) * scale
        sink_col = jnp.full((lc, 1), sink_ref[h], F32) * LOG2E
        o = _softmax_av([s], [_bf(v_ref[0][:, hs])], sink_col)
        o_ref[0, :, h * HEAD_DIM:(h + 1) * HEAD_DIM] = _bf(o)


def _ctx_attn(sink, proj_ctx):
    bsz, lc, _ = proj_ctx.shape
    kcol, vcol = A_Q // A_KV, A_Q // A_KV + 1
    return pl.pallas_call(
        _ctx_attn_kernel,
        grid=(bsz,),
        in_specs=[
            pl.BlockSpec(memory_space=pltpu.SMEM),
            pl.BlockSpec((1, lc, A_Q), lambda b: (b, 0, 0)),
            pl.BlockSpec((1, lc, A_KV), lambda b: (b, 0, kcol)),
            pl.BlockSpec((1, lc, A_KV), lambda b: (b, 0, vcol)),
        ],
        out_specs=pl.BlockSpec((1, lc, A_Q), lambda b: (b, 0, 0)),
        out_shape=jax.ShapeDtypeStruct((bsz, lc, A_Q), BF16),
        compiler_params=_params(("arbitrary",)),
        name="context_attention",
    )(sink, proj_ctx, proj_ctx, proj_ctx)


def _hgrn_chunk(bq, bi, bf, lb, st, rev):
    heads = range(len(bq))
    q = [_silu(x) for x in bq]
    v = bi
    f = [lb[h] + (1.0 - lb[h]) * _sigmoid(bf[h]) for h in heads]
    k = [1.0 - x for x in f]
    g = [jnp.log(x) * LOG2E for x in f]
    b = [_cumsum_rows(x, rev) for x in g]
    btot = [x[0:1] if rev else x[CHUNK - 1:CHUNK] for x in b]
    vb = [_bf(x) for x in v]
    stb = [_bf(x) for x in st]
    o = [_dot_nt(_bf(q[h] * jnp.exp2(b[h])), stb[h]) for h in heads]
    khat = [_bf(k[h] * jnp.exp2(btot[h] - b[h])) for h in heads]
    dst = [_dot_tn(vb[h], khat[h]) for h in heads]
    st_new = [st[h] * jnp.exp2(btot[h]) + dst[h] for h in heads]
    nsub = CHUNK // SUB
    row8 = lax.broadcasted_iota(jnp.int32, (8, 1), 0)
    outs = [[] for _ in heads]
    for blk in range(nsub):
        r0 = blk * SUB
        rs = slice(r0, r0 + SUB)
        acc = [o[h][rs] for h in heads]
        if rev and blk < nsub - 1:
            ref_row, lo, hi = r0 + SUB, r0 + SUB, CHUNK
        elif (not rev) and blk > 0:
            ref_row, lo, hi = r0 - 1, 0, r0
        else:
            ref_row = None
        if ref_row is not None:
            bref = [b[h][ref_row:ref_row + 1] for h in heads]
            qn = [_bf(q[h][rs] * jnp.exp2(b[h][rs] - bref[h])) for h in heads]
            kn = [_bf(k[h][lo:hi] * jnp.exp2(bref[h] - b[h][lo:hi])) for h in heads]
            att = [_bf(_dot_nt(qn[h], kn[h])) for h in heads]
            acc = [acc[h] + _dot(att[h], vb[h][lo:hi]) for h in heads]
        piece = 8
        acc = [[a[p0:p0 + piece] for p0 in range(0, SUB, piece)] for a in acc]
        for s in range(SUB):
            reached = range(0, s // piece + 1) if rev else range(s // piece, SUB // piece)
            for pc in reached:
                t0 = r0 + pc * piece
                mask = (row8 + pc * piece <= s) if rev else (row8 + pc * piece >= s)
                for h in heads:
                    rel = b[h][t0:t0 + piece] - b[h][r0 + s:r0 + s + 1]
                    dec = jnp.exp2(jnp.where(mask, rel, -jnp.inf))
                    w = jnp.sum(q[h][t0:t0 + piece] * dec * k[h][r0 + s:r0 + s + 1], axis=-1, keepdims=True)
                    acc[h][pc] = acc[h][pc] + w * v[h][r0 + s:r0 + s + 1]
        for h in heads:
            outs[h].extend(acc[h])
    return [jnp.concatenate(x, axis=0) for x in outs], st_new


def _hgrn_kernel(q_ref, i_ref, f_ref, lb_ref, s0_ref, o_ref, sfin_ref, st_ref, *, rev, nchunk, nhead):
    c = pl.program_id(2)

    @pl.when(c == 0)
    def _():
        st_ref[...] = s0_ref[0]

    lanes = [slice(h * HEAD_DIM, (h + 1) * HEAD_DIM) for h in range(nhead)]
    lb = [lb_ref[:, ln] for ln in lanes]

    def body(ci, carry):
        cc = (nchunk - 1 - ci) if rev else ci
        rows = pl.ds(pl.multiple_of(cc * CHUNK, CHUNK), CHUNK)
        o, st_new = _hgrn_chunk([q_ref[0, rows, ln] for ln in lanes], [i_ref[0, rows, ln] for ln in lanes],
                                [f_ref[0, rows, ln] for ln in lanes], lb,
                                [st_ref[h] for h in range(nhead)], rev)
        for h in range(nhead):
            o_ref[0, rows, lanes[h]] = o[h]
            st_ref[h] = st_new[h]
        return carry

    lax.fori_loop(0, nchunk, body, 0, unroll=2 if nchunk % 2 == 0 else 1)

    @pl.when(c == pl.num_programs(2) - 1)
    def _():
        sfin_ref[0] = st_ref[...]


def _hgrn_scan(proj, lb, s0, rev, tblk, nhead):
    bsz, seq, _ = proj.shape
    nblk = seq // tblk
    width = nhead * HEAD_DIM
    base = A_Q + 2 * A_KV
    assert base % width == 0 and B_W % width == 0
    qc, ic = base // width, (base + B_W) // width
    fc = (base + (3 if rev else 2) * B_W) // width
    blk = (lambda c: nblk - 1 - c) if rev else (lambda c: c)

    def col(c0):
        return pl.BlockSpec((1, tblk, width), lambda b, h, c: (b, blk(c), c0 + h))

    state = pl.BlockSpec((1, nhead, HEAD_DIM, HEAD_DIM), lambda b, h, c: (b, h, 0, 0))
    return pl.pallas_call(
        functools.partial(_hgrn_kernel, rev=rev, nchunk=tblk // CHUNK, nhead=nhead),
        grid=(bsz, B_HEADS // nhead, nblk),
        in_specs=[
            col(qc), col(ic), col(fc),
            pl.BlockSpec((1, width), lambda b, h, c: (0, h)),
            state,
        ],
        out_specs=[
            pl.BlockSpec((1, tblk, width), lambda b, h, c: (b, blk(c), h)),
            state,
        ],
        out_shape=[
            jax.ShapeDtypeStruct((bsz, seq, B_W), F32),
            jax.ShapeDtypeStruct((bsz, B_HEADS, HEAD_DIM, HEAD_DIM), F32),
        ],
        scratch_shapes=[pltpu.VMEM((nhead, HEAD_DIM, HEAD_DIM), F32)],
        compiler_params=_params(("arbitrary", "arbitrary", "arbitrary")),
        name="hgrn2_bwd" if rev else "hgrn2_fwd",
    )(proj, proj, proj, lb, s0)


def _ab_out_kernel(oa_ref, of_ref, ob_ref, g0_ref, g1_ref, on_ref, w_ref, h_ref, gt_ref, o_ref, lhs_ref):
    lhs_ref[:, :A_Q] = oa_ref[0]
    half = B_W // 2
    for hd in range(B_HEADS):
        sl = slice(hd * HEAD_DIM, (hd + 1) * HEAD_DIM)
        o = of_ref[0, :, sl] + ob_ref[0, :, sl]
        o = o * lax.rsqrt(jnp.mean(o * o, axis=-1, keepdims=True) + EPS)
        o = o * on_ref[:, sl]
        gref = g0_ref if hd * HEAD_DIM < half else g1_ref
        gs = slice(hd * HEAD_DIM % half, hd * HEAD_DIM % half + HEAD_DIM)
        o = o * _silu(gref[0, :, gs])
        lhs_ref[:, A_Q + hd * HEAD_DIM:A_Q + (hd + 1) * HEAD_DIM] = _bf(o)
    y = _dot(lhs_ref[...], w_ref[...])
    o_ref[0] = h_ref[0] + gt_ref[0] * y


def _ab_out(oa, of, ob, proj, onorm, w_out, h, gt, tm):
    bsz, seq, d = h.shape
    half = B_W // 2
    gcol = (A_Q + 2 * A_KV + 4 * B_W) // half
    row = lambda b, i: (b, i, 0)
    return pl.pallas_call(
        _ab_out_kernel,
        grid=(bsz, seq // tm),
        in_specs=[
            pl.BlockSpec((1, tm, A_Q), row),
            pl.BlockSpec((1, tm, B_W), row),
            pl.BlockSpec((1, tm, B_W), row),
            pl.BlockSpec((1, tm, half), lambda b, i: (b, i, gcol)),
            pl.BlockSpec((1, tm, half), lambda b, i: (b, i, gcol + 1)),
            pl.BlockSpec((1, B_W), lambda b, i: (0, 0)),
            pl.BlockSpec((A_Q + B_W, d), lambda b, i: (0, 0)),
            pl.BlockSpec((1, tm, d), row),
            pl.BlockSpec((1, 1, d), lambda b, i: (b, 0, 0)),
        ],
        out_specs=pl.BlockSpec((1, tm, d), row),
        out_shape=jax.ShapeDtypeStruct((bsz, seq, d), F32),
        scratch_shapes=[pltpu.VMEM((tm, A_Q + B_W), BF16)],
        compiler_params=_params(("arbitrary", "arbitrary")),
        name="ab_outproj",
    )(oa, of, ob, proj, proj, onorm, w_out, h, gt)


def _mlp_kernel(*refs, final):
    if final:
        x_ref, g_ref, sc_ref, sh_ref, gt_ref, wu_ref, wd_ref, fn_ref, o_ref, u_ref = refs
    else:
        x_ref, g_ref, sc_ref, sh_ref, gt_ref, wu_ref, wd_ref, o_ref, u_ref = refs
    j = pl.program_id(2)

    tm = x_ref.shape[1]

    @pl.when(j == 0)
    def _():
        _normmod_rows(x_ref, g_ref, sc_ref, sh_ref, u_ref)

    hid = jnp.maximum(_dot(u_ref[...], wu_ref[...]), 0.0)
    hid = _bf(hid * hid)
    ncol = 512

    @pl.when(j == 0)
    def _():
        for n0 in range(0, o_ref.shape[2], ncol):
            o_ref[0, :, n0:n0 + ncol] = _dot(hid, wd_ref[:, n0:n0 + ncol])

    @pl.when(j != 0)
    def _():
        for n0 in range(0, o_ref.shape[2], ncol):
            o_ref[0, :, n0:n0 + ncol] += _dot(hid, wd_ref[:, n0:n0 + ncol])

    @pl.when(j == pl.num_programs(2) - 1)
    def _():
        piece = 128
        for r0 in range(0, tm, piece):
            rs = slice(r0, r0 + piece)
            y = x_ref[0, rs] + gt_ref[0] * o_ref[0, rs]
            if final:
                y = (y * lax.rsqrt(jnp.mean(y * y, axis=-1, keepdims=True) + EPS)) * fn_ref[...]
            o_ref[0, rs] = y


def _mlp(x, g, sc, sh, gt, w_up, w_down, final_gain, tm, tf):
    bsz, seq, d = x.shape
    dff = w_up.shape[1]
    vec = pl.BlockSpec((1, 1, d), lambda b, i, j: (b, 0, 0))
    in_specs = [
        pl.BlockSpec((1, tm, d), lambda b, i, j: (b, i, 0)),
        pl.BlockSpec((1, d), lambda b, i, j: (0, 0)),
        vec, vec, vec,
        pl.BlockSpec((d, tf), lambda b, i, j: (0, j)),
        pl.BlockSpec((tf, d), lambda b, i, j: (j, 0)),
    ]
    args = [x, g, sc, sh, gt, w_up, w_down]
    if final_gain is not None:
        in_specs.append(pl.BlockSpec((1, d), lambda b, i, j: (0, 0)))
        args.append(final_gain)
    return pl.pallas_call(
        functools.partial(_mlp_kernel, final=final_gain is not None),
        grid=(bsz, seq // tm, dff // tf),
        in_specs=in_specs,
        out_specs=pl.BlockSpec((1, tm, d), lambda b, i, j: (b, i, 0)),
        out_shape=jax.ShapeDtypeStruct((bsz, seq, d), F32),
        scratch_shapes=[pltpu.VMEM((tm, d), BF16)],
        compiler_params=_params(("arbitrary", "arbitrary", "arbitrary")),
        name="sqrelu_mlp",
    )(*args)


def _rk_proj_kernel(x_ref, xp_ref, xn_ref, g_ref, sc_ref, sh_ref, mix_ref, wb_ref, ws_ref, ob_ref, os_ref,
                    u_ref, xx_ref, lhs_ref, *, nbig, nsmall):
    i = pl.program_id(1)
    j = pl.program_id(2)
    tm = x_ref.shape[1]
    rb = 32

    def build(m, slot):
        mixrow = mix_ref[pl.ds(m, 1), :]
        for r0 in range(0, tm, rb):
            lhs_ref[slot, r0:r0 + rb] = _bf(u_ref[8 + r0:8 + r0 + rb] + xx_ref[r0:r0 + rb] * mixrow)

    @pl.when(j == 0)
    def _():
        g, sc, sh = g_ref[...], sc_ref[0], sh_ref[0]
        _normmod_rows(x_ref, g_ref, sc_ref, sh_ref, u_ref, row0=8)
        up = _normmod(xp_ref[0], g, sc, sh)[7:8]
        un = _normmod(xn_ref[0], g, sc, sh)[0:1]
        u_ref[7:8] = jnp.where(i == 0, 0.0, up)
        u_ref[8 + tm:9 + tm] = jnp.where(i == pl.num_programs(1) - 1, 0.0, un)
        mix0 = mix_ref[0:1, :]
        for r0 in range(0, tm, ROW_PIECE):
            u = u_ref[8 + r0:8 + r0 + ROW_PIECE]
            xx = 0.5 * (u_ref[7 + r0:7 + r0 + ROW_PIECE] + u_ref[9 + r0:9 + r0 + ROW_PIECE]) - u
            xx_ref[r0:r0 + ROW_PIECE] = xx
            lhs_ref[0, r0:r0 + ROW_PIECE] = _bf(u + xx * mix0)

    @pl.when(j < nbig)
    def _():
        build(j + 1, (j + 1) % 2)
        ob_ref[0, 0] = _dot(lhs_ref[j % 2], wb_ref[0])

    @pl.when(j == nbig)
    def _():
        for s in range(nsmall):
            os_ref[s, 0] = _dot(lhs_ref[(nbig + s) % 2], ws_ref[s])
            if s + 1 < nsmall:
                build(nbig + s + 1, (nbig + s + 1) % 2)


def _rk_proj(x, g, sc, sh, mix, w_big, w_small, tm):
    bsz, seq, d = x.shape
    nbig, _, n = w_big.shape
    nsmall, _, ns = w_small.shape
    r8 = tm // 8
    last8 = seq // 8 - 1
    vec = pl.BlockSpec((1, 1, d), lambda b, i, j: (b, 0, 0))
    return pl.pallas_call(
        functools.partial(_rk_proj_kernel, nbig=nbig, nsmall=nsmall),
        grid=(bsz, seq // tm, nbig + 1),
        in_specs=[
            pl.BlockSpec((1, tm, d), lambda b, i, j: (b, i, 0)),
            pl.BlockSpec((1, 8, d), lambda b, i, j: (b, jnp.maximum(i * r8 - 1, 0), 0)),
            pl.BlockSpec((1, 8, d), lambda b, i, j: (b, jnp.minimum((i + 1) * r8, last8), 0)),
            pl.BlockSpec((1, d), lambda b, i, j: (0, 0)),
            vec, vec,
            pl.BlockSpec((nbig + nsmall, d), lambda b, i, j: (0, 0)),
            pl.BlockSpec((1, d, n), lambda b, i, j: (jnp.minimum(j, nbig - 1), 0, 0)),
            pl.BlockSpec((nsmall, d, ns), lambda b, i, j: (0, 0, 0)),
        ],
        out_specs=[
            pl.BlockSpec((1, 1, tm, n), lambda b, i, j: (jnp.minimum(j, nbig - 1), b, i, 0)),
            pl.BlockSpec((nsmall, 1, tm, ns), lambda b, i, j: (0, b, i, 0)),
        ],
        out_shape=[
            jax.ShapeDtypeStruct((nbig, bsz, seq, n), F32),
            jax.ShapeDtypeStruct((nsmall, bsz, seq, ns), F32),
        ],
        scratch_shapes=[pltpu.VMEM((tm + 16, d), F32), pltpu.VMEM((tm, d), F32), pltpu.VMEM((2, tm, d), BF16)],
        compiler_params=_params(("arbitrary", "arbitrary", "arbitrary")),
        name="rwkv_proj",
    )(x, x, x, g, sc, sh, mix, w_big, w_small)


DECAY_SCALE = float(np.exp(-0.5))


def _pair_consts(rev):
    n = 2 * CHUNK
    ri = lax.broadcasted_iota(jnp.int32, (n, n), 0)
    ci = lax.broadcasted_iota(jnp.int32, (n, n), 1)
    same = (ri // CHUNK) == (ci // CHUNK)
    rt, ct = ri % CHUNK, ci % CHUNK
    strict = same & ((ct > rt) if rev else (ct < rt))
    incl = same & ((ct >= rt) if rev else (ct <= rt))
    eye = jnp.where(ri == ci, 1.0, 0.0).astype(F32)
    lane = lax.broadcasted_iota(jnp.int32, (1, LANES), 1)
    lane_lo = lane < C_HEAD
    m0 = jnp.where(lane_lo, 1.0, 0.0).astype(F32)
    m1 = 1.0 - m0
    blk2 = (ri // 2) == (ci // 2)
    level = {}
    m = 2
    while m < CHUNK:
        if m % 8:
            rsel, csel = ri, ci
        else:
            rc = lax.broadcasted_iota(jnp.int32, (n // 2, n), 0)
            csel = lax.broadcasted_iota(jnp.int32, (n // 2, n), 1)
            rsel = (rc // m) * (2 * m) + (0 if rev else m) + rc % m
        level[m] = ((rsel // (2 * m)) == (csel // (2 * m))) & ((rsel // m) != (csel // m))
        m *= 2
    return dict(strict=strict, incl=incl, eye=eye, m0=m0, m1=m1, lane_lo=lane_lo, blk2=blk2, level=level)


def _head_sums(x, cs):
    s0 = jnp.sum(x * cs["m0"], axis=-1, keepdims=True)
    s1 = jnp.sum(x * cs["m1"], axis=-1, keepdims=True)
    return jnp.where(cs["lane_lo"], s0, s1)


def _unit_tri_inverse(n, cs, rev):
    size = 2 * CHUNK
    t = [cs["eye"] + jnp.where(cs["blk2"], x, 0.0) for x in n]
    nb = [_bf(x) for x in n]
    m = 2
    while m < CHUNK:
        tb = [_bf(x) for x in t]
        sel = cs["level"][m]
        if m % 8:
            q = [_bf(_dot(a, b)) for a, b in zip(tb, nb)]
            q = [_dot(a, b) for a, b in zip(q, tb)]
            t = [a + jnp.where(sel, b, 0.0) for a, b in zip(t, q)]
        else:
            starts = range(0 if rev else m, size, 2 * m)
            pick = lambda x: jnp.concatenate([x[r0:r0 + m] for r0 in starts], axis=0)
            q = [_bf(_dot(_bf(pick(x)), b)) for x, b in zip(t, nb)]
            q = [_dot(a, b) for a, b in zip(q, tb)]
            t = [_add_rows(x, jnp.where(sel, y, 0.0), starts, m) for x, y in zip(t, q)]
        m *= 2
    return t


def _add_rows(x, upd, starts, m):
    pieces, pos = [], 0
    for k, r0 in enumerate(starts):
        if r0 > pos:
            pieces.append(x[pos:r0])
        pieces.append(x[r0:r0 + m] + upd[k * m:(k + 1) * m])
        pos = r0 + m
    if pos < x.shape[0]:
        pieces.append(x[pos:])
    return jnp.concatenate(pieces, axis=0)


def _halves(top, bottom):
    return jnp.concatenate([top[:CHUNK], bottom[CHUNK:]], axis=0)


def _stack2(x, cs):
    return jnp.concatenate([x * cs["m0"], x * cs["m1"]], axis=0)


def _rwkv_chunk(r, k, v, lw, la, w2, a2, w0, a0, kkv, kav, ht, cs, rev):
    pairs = range(len(r))
    tl, lab = _bf(jnp.tanh(lw)), _bf(la)
    z = [w0[p] + _dot(tl, w2[p]) for p in pairs]
    za = [_dot(lab, a2[p]) for p in pairs]
    kkr = [k[p] * kkv[p] for p in pairs]
    ss = [_head_sums(x * x, cs) for x in kkr]
    ld = [-(DECAY_SCALE * LOG2E) * _sigmoid(x) for x in z]
    cl = [_cumsum_rows(x, rev) for x in ld]
    a = [_sigmoid(a0[p] + za[p]) for p in pairs]
    kk = [x / jnp.maximum(jnp.sqrt(s), 1e-12) for x, s in zip(kkr, ss)]
    kd = [k[p] * (1.0 + (a[p] - 1.0) * kav[p]) for p in pairs]
    bb = [x * y for x, y in zip(kk, a)]
    ctot = [x[0:1] if rev else x[CHUNK - 1:CHUNK] for x in cl]
    e_neg = [jnp.exp2(-x) for x in cl]
    e_tail = [jnp.exp2(c - x) for c, x in zip(ctot, cl)]
    at = [_bf(_stack2(-kk[p] * jnp.exp2(cl[p] - ld[p]), cs)) for p in pairs]
    rt = [_bf(_stack2(r[p] * jnp.exp2(cl[p]), cs)) for p in pairs]
    rhs = [_bf(jnp.concatenate([bb[p] * e_neg[p], kd[p] * e_neg[p]], axis=0)) for p in pairs]
    g = [_dot_nt(jnp.concatenate([x, y], axis=0), w) for x, y, w in zip(at, rt, rhs)]
    g1 = [x[:2 * CHUNK] for x in g]
    g2 = [x[2 * CHUNK:] for x in g]
    strict, incl = cs["strict"], cs["incl"]
    g1r = [pltpu.roll(x, C_HEAD, 1) for x in g1]
    n_ab = [jnp.where(strict, _halves(x, y), 0.0) for x, y in zip(g1, g1r)]
    n_ak = [_bf(jnp.where(strict, _halves(y, x), 0.0)) for x, y in zip(g1, g1r)]
    t = _unit_tri_inverse(n_ab, cs, rev)
    vs = [_bf(_stack2(x, cs)) for x in v]
    hkv = [_bf(x.T) for x in ht]
    xx = [_bf(_dot(jnp.concatenate([at[p], n_ak[p]], axis=1), jnp.concatenate([hkv[p], vs[p]], axis=0)))
          for p in pairs]
    u =[_bf(_dot(_bf(a_), b_)) for a_, b_ in zip(t, xx)]
    bk = [_bf(jnp.concatenate([_stack2(bb[p] * e_tail[p], cs), _stack2(kd[p] * e_tail[p], cs)], axis=0))
          for p in pairs]
    uv = [jnp.concatenate([a_, b_], axis=0) for a_, b_ in zip(u, vs)]
    dh = [_dot_tn(a_, b_) for a_, b_ in zip(uv, bk)]
    ht_new = [ht[p] * jnp.exp2(ctot[p]) + dh[p] for p in pairs]
    g2r = [pltpu.roll(x, C_HEAD, 1) for x in g2]
    n_rb = [_bf(jnp.where(incl, _halves(x, y), 0.0)) for x, y in zip(g2, g2r)]
    n_rk = [_bf(jnp.where(incl, _halves(y, x), 0.0)) for x, y in zip(g2, g2r)]
    ysum = [_dot(jnp.concatenate([rt[p], n_rb[p], n_rk[p]], axis=1),
                 jnp.concatenate([hkv[p], u[p], vs[p]], axis=0)) for p in pairs]
    y = [x[:CHUNK] + x[CHUNK:] for x in ysum]
    return y, ht_new, kd


def _rwkv_combine(y, yf, r, k, v, kd, laf, a2f, a0f, kav, rkv, lng, lnb, cs):
    pairs = range(len(y))
    inv_n = 1.0 / C_HEAD
    lafb = _bf(laf)
    zf = [_dot(lafb, a2f[p]) for p in pairs]
    ysum = [a + b for a, b in zip(y, yf)]
    mu = [_head_sums(x, cs) * inv_n for x in ysum]
    dev = [a - b for a, b in zip(ysum, mu)]
    var = [_head_sums(x * x, cs) * inv_n for x in dev]
    a_f = [_sigmoid(a0f[p] + zf[p]) for p in pairs]
    kd_f = [k[p] * (1.0 + (a_f[p] - 1.0) * kav[p]) for p in pairs]
    bsum = [_head_sums(r[p] * (kd_f[p] + kd[p]) * rkv[p], cs) for p in pairs]
    return [dev[p] * lax.rsqrt(var[p] + GN_EPS) * lng[p] + lnb[p] + bsum[p] * v[p] for p in pairs]


def _rwkv_scan_kernel(*refs, rev, nchunk, npair, combine):
    if combine:
        (r_ref, k_ref, v_ref, lw_ref, la_ref, w2_ref, a2_ref, w0_ref, a0_ref, kk_ref, ka_ref, s0_ref,
         yf_ref, laf_ref, a2f_ref, a0f_ref, rk_ref, lng_ref, lnb_ref, o_ref, sfin_ref, ht_ref) = refs
    else:
        (r_ref, k_ref, v_ref, lw_ref, la_ref, w2_ref, a2_ref, w0_ref, a0_ref, kk_ref, ka_ref, s0_ref,
         o_ref, sfin_ref, ht_ref) = refs
    c = pl.program_id(2)

    @pl.when(c == 0)
    def _():
        ht_ref[...] = s0_ref[0]

    cs = _pair_consts(rev)

    def body(ci, carry):
        cc = (nchunk - 1 - ci) if rev else ci
        rows = pl.ds(pl.multiple_of(cc * CHUNK, CHUNK), CHUNK)
        lw = lw_ref[0, 0, rows, :]
        la = la_ref[0, 0, rows, :]
        lanes = [slice(pr * LANES, (pr + 1) * LANES) for pr in range(npair)]
        r = [r_ref[0, 0, rows, ln] for ln in lanes]
        k = [k_ref[0, 0, rows, ln] for ln in lanes]
        v = [v_ref[0, 0, rows, ln] for ln in lanes]
        kav = [ka_ref[:, ln] for ln in lanes]
        y, ht_new, kd = _rwkv_chunk(
            r, k, v, lw, la, [w2_ref[0, :, ln] for ln in lanes], [a2_ref[0, :, ln] for ln in lanes],
            [w0_ref[0, :, ln] for ln in lanes], [a0_ref[0, :, ln] for ln in lanes],
            [kk_ref[:, ln] for ln in lanes], kav, [ht_ref[pr] for pr in range(npair)], cs, rev)
        for pr in range(npair):
            ht_ref[pr] = ht_new[pr]
        if combine:
            y = _rwkv_combine(
                y, [yf_ref[0, rows, ln] for ln in lanes], r, k, v, kd, laf_ref[0, 0, rows, :],
                [a2f_ref[0, :, ln] for ln in lanes], [a0f_ref[0, :, ln] for ln in lanes], kav,
                [rk_ref[:, ln] for ln in lanes], [lng_ref[:, ln] for ln in lanes],
                [lnb_ref[:, ln] for ln in lanes], cs)
        for pr in range(npair):
            o_ref[0, rows, lanes[pr]] = y[pr]
        return carry

    lax.fori_loop(0, nchunk, body, 0, unroll=2 if nchunk % 2 == 0 else 1)

    @pl.when(c == pl.num_programs(2) - 1)
    def _():
        sfin_ref[0] = ht_ref[...]


def _rwkv_scan(rkv, small, prm, s0, rev, tblk, npair, y_fwd=None):
    _, bsz, seq, d = rkv.shape
    nblk = seq // tblk
    width = npair * LANES
    e = 1 if rev else 0
    blk = (lambda c: nblk - 1 - c) if rev else (lambda c: c)
    combine = y_fwd is not None

    def tok(m):
        return pl.BlockSpec((1, 1, tblk, width), lambda b, p, c: (m, b, blk(c), p))

    def lora(m, half):
        return pl.BlockSpec((1, 1, tblk, LANES), lambda b, p, c: (m, b, blk(c), half))

    def mat(idx):
        return pl.BlockSpec((1, LANES, width), lambda b, p, c: (idx, 0, p))

    def vec3(idx):
        return pl.BlockSpec((1, 1, width), lambda b, p, c: (idx, 0, p))

    vec = pl.BlockSpec((1, width), lambda b, p, c: (0, p))
    state = pl.BlockSpec((1, npair, LANES, LANES), lambda b, p, c: (b, p, 0, 0))
    out_tok = pl.BlockSpec((1, tblk, width), lambda b, p, c: (b, blk(c), p))
    in_specs = [tok(0), tok(1), tok(2), lora(1, e), lora(2, e), mat(e), mat(e), vec3(e), vec3(e), vec, vec, state]
    args = [rkv, rkv, rkv, small, small, prm["w2"], prm["a2"], prm["w0"], prm["a0"], prm["k_k"], prm["k_a"], s0]
    if combine:
        in_specs += [out_tok, lora(2, 0), mat(0), vec3(0), vec, vec, vec]
        args += [y_fwd, small, prm["a2"], prm["a0"], prm["r_k"], prm["ln_g"], prm["ln_b"]]
    return pl.pallas_call(
        functools.partial(_rwkv_scan_kernel, rev=rev, nchunk=tblk // CHUNK, npair=npair, combine=combine),
        grid=(bsz, d // width, nblk),
        in_specs=in_specs,
        out_specs=[out_tok, state],
        out_shape=[
            jax.ShapeDtypeStruct((bsz, seq, d), F32),
            jax.ShapeDtypeStruct((bsz, d // LANES, LANES, LANES), F32),
        ],
        scratch_shapes=[pltpu.VMEM((npair, LANES, LANES), F32)],
        compiler_params=_params(("arbitrary", "arbitrary", "arbitrary")),
        name="rwkv7_bwd" if rev else "rwkv7_fwd",
    )(*args)


def _rk_out_kernel(z_ref, gs_ref, g2_ref, wo_ref, h_ref, gt_ref, o_ref):
    gate = _dot(_bf(_sigmoid(gs_ref[0, 0])), g2_ref[...])
    y = _dot(_bf(z_ref[0] * gate), wo_ref[...])
    o_ref[0] = h_ref[0] + gt_ref[0] * y


def _rk_out(z, small, g2, wo, h, gt, tm):
    bsz, seq, d = h.shape
    row = lambda b, i: (b, i, 0)
    glora = g2.shape[0]
    return pl.pallas_call(
        _rk_out_kernel,
        grid=(bsz, seq // tm),
        in_specs=[
            pl.BlockSpec((1, tm, d), row),
            pl.BlockSpec((1, 1, tm, glora), lambda b, i: (0, b, i, 0)),
            pl.BlockSpec((glora, d), lambda b, i: (0, 0)),
            pl.BlockSpec((d, d), lambda b, i: (0, 0)),
            pl.BlockSpec((1, tm, d), row),
            pl.BlockSpec((1, 1, d), lambda b, i: (b, 0, 0)),
        ],
        out_specs=pl.BlockSpec((1, tm, d), row),
        out_shape=jax.ShapeDtypeStruct((bsz, seq, d), F32),
        compiler_params=_params(("arbitrary", "arbitrary")),
        name="rwkv_outproj",
    )(z, small, g2, wo, h, gt)


def _rope_tables(seq):
    t = jnp.arange(seq, dtype=jnp.int32)
    rows = (t // GRID_W).astype(F32)
    cols = (t % GRID_W).astype(F32)
    half = HEAD_DIM // 2
    n_freq = half // 2
    inv = ROPE_BASE ** (-jnp.arange(n_freq, dtype=F32) / n_freq)
    lane = jnp.arange(HEAD_DIM)
    pos = jnp.where((lane < half)[None, :], rows[:, None], cols[:, None])
    ang = pos * inv[lane % n_freq][None, :]
    cos, sin = jnp.cos(ang), jnp.sin(ang)
    first = ((lane % half) < n_freq)[None, :]
    return cos, jnp.where(first, -sin, 0.0), jnp.where(first, 0.0, sin)


def _pad_lanes(w, axis):
    pad = [(0, 0)] * w.ndim
    pad[axis] = (0, LANES - w.shape[axis])
    return jnp.pad(w, pad)


def _row_tile(seq, want):
    return want if seq % want == 0 else seq


def kernel(x, c, ctx, c_ctx, mod_w, mod_b, norm_mix, norm_ffn, ffn_up, ffn_down, ab_w_in, ab_w_out, attn_sink, hgrn_lb, hgrn_onorm, rk_mix, rk_wr, rk_wk, rk_wv, rk_wo, rk_w0, rk_w1, rk_w2, rk_a0, rk_a1, rk_a2, rk_g1, rk_g2, rk_kk, rk_ka, rk_rk, rk_ln_g, rk_ln_b, final_norm):
    bsz, seq, d = x.shape
    lc = ctx.shape[1]
    depth = mod_w.shape[0]
    assert bsz + 1 <= 8 and seq % 1024 == 0 and lc % CHUNK == 0

    cond8 = jnp.zeros((8, d), F32).at[:bsz].set(c).at[bsz].set(c_ctx)
    mod = _modulation(cond8, mod_w, mod_b)
    lb_all = jnp.cumsum(jax.nn.softmax(hgrn_lb.astype(F32), axis=0), axis=0)
    rope = _rope_tables(seq)

    h, hc = x, ctx
    for layer in range(depth):
        last = layer == depth - 1
        jl = layer // 2
        m_lat = mod[layer, :bsz].reshape(bsz, 1, 6, d)
        m_ctx = jnp.broadcast_to(mod[layer, bsz].reshape(1, 1, 6, d), (bsz, 1, 6, d))
        sh1, sc1, gt1, sh2, sc2, gt2 = (m_lat[:, :, i] for i in range(6))
        csh1, csc1, cgt1, csh2, csc2, cgt2 = (m_ctx[:, :, i] for i in range(6))
        g_mix = norm_mix[layer].reshape(1, d)
        g_ffn = norm_ffn[layer].reshape(1, d)
        if layer % 2 == 0:
            w_in = _bf(ab_w_in[jl])
            w_out = _bf(ab_w_out[jl])
            sink = attn_sink[jl].astype(F32)
            lb = lb_all[jl].reshape(1, B_W)
            onorm = hgrn_onorm[jl].reshape(1, B_W).astype(F32)
            n_in = w_in.shape[1]
            pc = _inproj(hc, g_mix, csc1, csh1, w_in, None, lc, 512)
            pl_ = _inproj_stream(h, g_mix, sc1, sh1, w_in, rope, 512, n_in // 2)
            oa = _win_attn(sink, pl_, pc)
            zeros = jnp.zeros((bsz, B_HEADS, HEAD_DIM, HEAD_DIM), F32)
            nhead = 4
            ocf, scf = _hgrn_scan(pc, lb, zeros, False, lc, nhead)
            ocb, scb = _hgrn_scan(pc, lb, zeros, True, lc, nhead)
            olf, _ = _hgrn_scan(pl_, lb, scf, False, 512, nhead)
            olb, _ = _hgrn_scan(pl_, lb, scb, True, 512, nhead)
            h = _ab_out(oa, olf, olb, pl_, onorm, w_out, h, gt1, 512)
            if not last:
                oca = _ctx_attn(sink, pc)
                hc = _ab_out(oca, ocf, ocb, pc, onorm, w_out, hc, cgt1, lc)
        else:
            w_big = _bf(jnp.stack([rk_wr[jl], rk_wk[jl], rk_wv[jl]]))
            w_small = _bf(jnp.stack([
                rk_g1[jl],
                jnp.concatenate([_pad_lanes(rk_w1[jl, 0], 1), _pad_lanes(rk_w1[jl, 1], 1)], axis=1),
                jnp.concatenate([_pad_lanes(rk_a1[jl, 0], 1), _pad_lanes(rk_a1[jl, 1], 1)], axis=1),
            ]))
            assert w_small.shape[-1] == 2 * LANES
            mix = rk_mix[jl]
            prm = dict(
                w2=_bf(_pad_lanes(rk_w2[jl], 1)), a2=_bf(_pad_lanes(rk_a2[jl], 1)),
                w0=rk_w0[jl].reshape(2, 1, d), a0=rk_a0[jl].reshape(2, 1, d),
                k_k=rk_kk[jl].reshape(1, d), k_a=rk_ka[jl].reshape(1, d), r_k=rk_rk[jl].reshape(1, d),
                ln_g=rk_ln_g[jl].reshape(1, d), ln_b=rk_ln_b[jl].reshape(1, d),
            )
            g2 = _bf(rk_g2[jl])
            wo = _bf(rk_wo[jl])
            mix = mix[jnp.array([0, 2, 3, 5, 1, 4])]
            rkv_c, sm_c = _rk_proj(hc, g_mix, csc1, csh1, mix, w_big, w_small, lc)
            rkv_l, sm_l = _rk_proj(h, g_mix, sc1, sh1, mix, w_big, w_small, 512)
            zeros = jnp.zeros((bsz, d // LANES, LANES, LANES), F32)
            npair = 16
            ycf, s_f = _rwkv_scan(rkv_c, sm_c, prm, zeros, False, lc, npair)
            zc, s_b = _rwkv_scan(rkv_c, sm_c, prm, zeros, True, lc, npair, y_fwd=ycf)
            ylf, _ = _rwkv_scan(rkv_l, sm_l, prm, s_f, False, 256, npair)
            zl, _ = _rwkv_scan(rkv_l, sm_l, prm, s_b, True, 256, npair, y_fwd=ylf)
            h = _rk_out(zl, sm_l, g2, wo, h, gt1, 512)
            if not last:
                hc = _rk_out(zc, sm_c, g2, wo, hc, cgt1, lc)
        w_up = _bf(ffn_up[layer])
        w_dn = _bf(ffn_down[layer])
        h = _mlp(h, g_ffn, sc2, sh2, gt2, w_up, w_dn, final_norm.reshape(1, d) if last else None, 1024, 512)
        if not last:
            hc = _mlp(hc, g_ffn, csc2, csh2, cgt2, w_up, w_dn, None, lc, 512)
    return h
```

```python
import functools
from typing import NamedTuple

import jax
import jax.numpy as jnp
import numpy as np
from jax import lax
from jax.experimental import pallas as pl
from jax.experimental.pallas import tpu as pltpu

F32 = jnp.float32
BF16 = jnp.bfloat16

LANES = 128
HEAD_DIM = 128
GRID_W = 64
WINDOW = 128
ROPE_BASE = 10000.0
A_Q_HEADS = 8
A_KV_HEADS = 2
A_GROUP = A_Q_HEADS // A_KV_HEADS
A_Q = A_Q_HEADS * HEAD_DIM
A_KV = A_KV_HEADS * HEAD_DIM
B_HEADS = 8
B_W = B_HEADS * HEAD_DIM
C_HEAD = 64
CHUNK = 64
SUB = 16
EPS = 1e-6
GN_EPS = 64e-5
VMEM_LIMIT = 58 * 1024 * 1024
LOG2E = float(np.log2(np.e))

NT_DIMS = (((1,), (1,)), ((), ()))
TN_DIMS = (((0,), (0,)), ((), ()))


def _dot(a, b, **kw):
    return jnp.dot(a, b, preferred_element_type=F32, **kw)


def _dot_nt(a, b):
    return lax.dot_general(a, b, NT_DIMS, preferred_element_type=F32)


def _dot_tn(a, b):
    return lax.dot_general(a, b, TN_DIMS, preferred_element_type=F32)


def _bf(x):
    return x.astype(BF16)


def _sigmoid(x):
    return 1.0 / (1.0 + jnp.exp(-x))


def _silu(x):
    return x * _sigmoid(x)


def _normmod(x, g, sc, sh):
    return _normscale(x, g * (1.0 + sc), sh)


def _normscale(x, gain, shift):
    ms = jnp.mean(x * x, axis=-1, keepdims=True)
    return (x * lax.rsqrt(ms + EPS)) * gain + shift


ROW_PIECE = 16


def _for_row_pieces(rows, fn):
    def body(p, carry):
        fn(pl.ds(pl.multiple_of(p * ROW_PIECE, ROW_PIECE), ROW_PIECE))
        return carry

    lax.fori_loop(0, rows // ROW_PIECE, body, 0, unroll=8)


def _normmod_rows(x_ref, g_ref, sc_ref, sh_ref, u_ref, row0=0):
    gain, shift = g_ref[...] * (1.0 + sc_ref[0]), sh_ref[0]

    def piece(rs):
        dst = rs if row0 == 0 else pl.ds(rs.start + row0, ROW_PIECE)
        u_ref[dst] = _normscale(x_ref[0, rs], gain, shift).astype(u_ref.dtype)

    _for_row_pieces(x_ref.shape[1], piece)


def _cumsum_rows(x, rev):
    n = x.shape[0]
    row = lax.broadcasted_iota(jnp.int32, (n, 1), 0)
    s = 1
    while s < n:
        if rev:
            x = x + jnp.where(row < n - s, pltpu.roll(x, n - s, 0), 0.0)
        else:
            x = x + jnp.where(row >= s, pltpu.roll(x, s, 0), 0.0)
        s *= 2
    return x


def _params(sem):
    return pltpu.CompilerParams(dimension_semantics=sem, vmem_limit_bytes=VMEM_LIMIT)


def _mod_kernel(c_ref, w_ref, b_ref, o_ref):
    s = _bf(_silu(c_ref[...]))
    o_ref[0] = _dot(s, _bf(w_ref[0])) + b_ref[0]


def _modulation(cond8, mod_w, mod_b):
    depth, d, n = mod_w.shape
    tn = 1024
    return pl.pallas_call(
        _mod_kernel,
        grid=(depth, n // tn),
        in_specs=[
            pl.BlockSpec((8, d), lambda l, j: (0, 0)),
            pl.BlockSpec((1, d, tn), lambda l, j: (l, 0, j)),
            pl.BlockSpec((1, 1, tn), lambda l, j: (l, 0, j)),
        ],
        out_specs=pl.BlockSpec((1, 8, tn), lambda l, j: (l, 0, j)),
        out_shape=jax.ShapeDtypeStruct((depth, 8, n), F32),
        compiler_params=_params(("arbitrary", "arbitrary")),
        name="modulation",
    )(cond8, mod_w, mod_b.reshape(depth, 1, n))


def _inproj_kernel(*refs, n_rope):
    if n_rope:
        x_ref, g_ref, sc_ref, sh_ref, w_ref, cos_ref, sna_ref, snb_ref, o_ref, u_ref = refs
    else:
        x_ref, g_ref, sc_ref, sh_ref, w_ref, o_ref, u_ref = refs
    j = pl.program_id(2)

    @pl.when(j == 0)
    def _():
        _normmod_rows(x_ref, g_ref, sc_ref, sh_ref, u_ref)

    acc = _dot(u_ref[...], w_ref[...])
    if not n_rope:
        o_ref[0] = acc
        return

    per_tile = acc.shape[1] // HEAD_DIM
    n_full, n_rem = n_rope // per_tile, n_rope % per_tile

    def store(n_rot):
        cos, sna, snb = cos_ref[...], sna_ref[...], snb_ref[...]
        for hd in range(n_rot):
            sl = acc[:, hd * HEAD_DIM:(hd + 1) * HEAD_DIM]
            rot = sl * cos + pltpu.roll(sl, 96, 1) * sna + pltpu.roll(sl, 32, 1) * snb
            o_ref[0, :, hd * HEAD_DIM:(hd + 1) * HEAD_DIM] = rot
        if n_rot < per_tile:
            o_ref[0, :, n_rot * HEAD_DIM:] = acc[:, n_rot * HEAD_DIM:]

    pl.when(j < n_full)(lambda: store(per_tile))
    pl.when(j == n_full)(lambda: store(n_rem))
    pl.when(j > n_full)(lambda: store(0))


def _inproj(x, g, sc, sh, w, rope, tm, tn):
    bsz, seq, d = x.shape
    n = w.shape[1]
    n_rope = 0
    in_specs = [
        pl.BlockSpec((1, tm, d), lambda b, i, j: (b, i, 0)),
        pl.BlockSpec((1, d), lambda b, i, j: (0, 0)),
        pl.BlockSpec((1, 1, d), lambda b, i, j: (b, 0, 0)),
        pl.BlockSpec((1, 1, d), lambda b, i, j: (b, 0, 0)),
        pl.BlockSpec((d, tn), lambda b, i, j: (0, j)),
    ]
    args = [x, g, sc, sh, w]
    if rope is not None:
        n_rope = A_Q_HEADS + A_KV_HEADS
        assert tn % HEAD_DIM == 0
        in_specs += [pl.BlockSpec((tm, HEAD_DIM), lambda b, i, j: (i, 0))] * 3
        args += list(rope)
    return pl.pallas_call(
        functools.partial(_inproj_kernel, n_rope=n_rope),
        grid=(bsz, seq // tm, n // tn),
        in_specs=in_specs,
        out_specs=pl.BlockSpec((1, tm, tn), lambda b, i, j: (b, i, j)),
        out_shape=jax.ShapeDtypeStruct((bsz, seq, n), F32),
        scratch_shapes=[pltpu.VMEM((tm, d), BF16)],
        compiler_params=_params(("arbitrary", "arbitrary", "arbitrary")),
        name="ab_inproj",
    )(*args)


def _inproj_stream_kernel(x_ref, g_ref, sc_ref, sh_ref, w_ref, cos_ref, sna_ref, snb_ref, o_ref,
                          lhs0_ref, lhs1_ref, *, n_rope):
    j = pl.program_id(0)
    i = pl.program_id(2)
    tm = x_ref.shape[1]
    rb = 32

    @pl.when((j == 0) & (pl.program_id(1) == 0) & (i == 0))
    def _():
        lhs1_ref[...] = jnp.zeros_like(lhs1_ref)

    def step(build_ref, ready_ref):
        gain, shift = g_ref[...] * (1.0 + sc_ref[0]), sh_ref[0]
        for r0 in range(0, tm, rb):
            build_ref[r0:r0 + rb] = _bf(_normscale(x_ref[0, r0:r0 + rb], gain, shift))
        o_ref[0] = _dot(ready_ref[...], w_ref[...])

    pl.when(i % 2 == 0)(lambda: step(lhs0_ref, lhs1_ref))
    pl.when(i % 2 == 1)(lambda: step(lhs1_ref, lhs0_ref))

    @pl.when(j == 0)
    def _():
        cos, sna, snb = cos_ref[...], sna_ref[...], snb_ref[...]
        for hd in range(n_rope):
            sl = o_ref[0, :, hd * HEAD_DIM:(hd + 1) * HEAD_DIM]
            rot = sl * cos + pltpu.roll(sl, 96, 1) * sna + pltpu.roll(sl, 32, 1) * snb
            o_ref[0, :, hd * HEAD_DIM:(hd + 1) * HEAD_DIM] = rot


def _inproj_stream(x, g, sc, sh, w, rope, tm, tn):
    bsz, seq, d = x.shape
    n = w.shape[1]
    ni = seq // tm
    n_rope = A_Q_HEADS + A_KV_HEADS
    assert n_rope * HEAD_DIM <= tn and n % tn == 0
    cur = lambda j, b, i: (b, jnp.minimum(i, ni - 1), 0)
    done = lambda i: jnp.maximum(i - 1, 0)
    table = pl.BlockSpec((tm, HEAD_DIM), lambda j, b, i: (done(i), 0))
    return pl.pallas_call(
        functools.partial(_inproj_stream_kernel, n_rope=n_rope),
        grid=(n // tn, bsz, ni + 1),
        in_specs=[
            pl.BlockSpec((1, tm, d), cur),
            pl.BlockSpec((1, d), lambda j, b, i: (0, 0)),
            pl.BlockSpec((1, 1, d), lambda j, b, i: (b, 0, 0)),
            pl.BlockSpec((1, 1, d), lambda j, b, i: (b, 0, 0)),
            pl.BlockSpec((d, tn), lambda j, b, i: (0, j)),
            table, table, table,
        ],
        out_specs=pl.BlockSpec((1, tm, tn), lambda j, b, i: (b, done(i), j)),
        out_shape=jax.ShapeDtypeStruct((bsz, seq, n), F32),
        scratch_shapes=[pltpu.VMEM((tm, d), BF16), pltpu.VMEM((tm, d), BF16)],
        compiler_params=_params(("arbitrary", "arbitrary", "arbitrary")),
        name="ab_inproj_stream",
    )(x, g, sc, sh, w, *rope)


def _softmax_av(s_list, v_list, sink_col):
    m = sink_col
    for s in s_list:
        m = jnp.maximum(m, jnp.max(s, axis=-1, keepdims=True))
    den = jnp.exp2(sink_col - m)
    out = None
    for s, v in zip(s_list, v_list):
        p = jnp.exp2(s - m)
        den = den + jnp.sum(p, axis=-1, keepdims=True)
        o = _dot(_bf(p), v)
        out = o if out is None else out + o
    return out / den


def _sink_column(sink_ref, hk, rows):
    rowh = lax.broadcasted_iota(jnp.int32, (rows, 1), 0) // WINDOW
    col = jnp.full((rows, 1), sink_ref[hk * A_GROUP + A_GROUP - 1], F32)
    for g in range(A_GROUP - 1):
        col = jnp.where(rowh == g, sink_ref[hk * A_GROUP + g], col)
    return col * LOG2E


def _win_attn_kernel(sink_ref, q_ref, kp_ref, kc_ref, kn_ref, vp_ref, vc_ref, vn_ref,
                     ck_ref, cv_ref, o_ref):
    n = pl.program_id(1)
    nb = pl.num_programs(1)
    scale = HEAD_DIM ** -0.5 * LOG2E
    q = q_ref[0]
    kband = jnp.concatenate([kp_ref[0], kc_ref[0], kn_ref[0]], axis=0)
    vband = jnp.concatenate([vp_ref[0], vc_ref[0], vn_ref[0]], axis=0)
    rows = A_GROUP * WINDOW
    tq = lax.broadcasted_iota(jnp.int32, (rows, 3 * WINDOW), 0) % WINDOW
    tk = lax.broadcasted_iota(jnp.int32, (rows, 3 * WINDOW), 1)
    rel = tk - WINDOW - tq
    valid = (jnp.abs(rel) <= WINDOW) & ((tk >= WINDOW) | (n > 0)) & ((tk < 2 * WINDOW) | (n < nb - 1))
    for hk in range(A_KV_HEADS):
        qs = jnp.concatenate(
            [q[:, (hk * A_GROUP + g) * HEAD_DIM:(hk * A_GROUP + g + 1) * HEAD_DIM] for g in range(A_GROUP)],
            axis=0)
        qs = _bf(qs)
        hs = slice(hk * HEAD_DIM, (hk + 1) * HEAD_DIM)
        s_win = _dot_nt(qs, _bf(kband[:, hs])) * scale
        s_win = jnp.where(valid, s_win, -jnp.inf)
        s_ctx = _dot_nt(qs, _bf(ck_ref[0][:, hs])) * scale
        o = _softmax_av([s_win, s_ctx], [_bf(vband[:, hs]), _bf(cv_ref[0][:, hs])],
                        _sink_column(sink_ref, hk, rows))
        for g in range(A_GROUP):
            h = hk * A_GROUP + g
            o_ref[0, :, h * HEAD_DIM:(h + 1) * HEAD_DIM] = _bf(o[g * WINDOW:(g + 1) * WINDOW])


def _win_attn(sink, proj, proj_ctx):
    bsz, seq, _ = proj.shape
    lc = proj_ctx.shape[1]
    nb = seq // WINDOW
    kcol, vcol = A_Q // A_KV, A_Q // A_KV + 1
    prev = lambda b, n: (b, jnp.maximum(n - 1, 0))
    nxt = lambda b, n: (b, jnp.minimum(n + 1, nb - 1))
    cur = lambda b, n: (b, n)

    def band(rowfn, col):
        return pl.BlockSpec((1, WINDOW, A_KV), lambda b, n: rowfn(b, n) + (col,))

    return pl.pallas_call(
        _win_attn_kernel,
        grid=(bsz, nb),
        in_specs=[
            pl.BlockSpec(memory_space=pltpu.SMEM),
            pl.BlockSpec((1, WINDOW, A_Q), lambda b, n: (b, n, 0)),
            band(prev, kcol), band(cur, kcol), band(nxt, kcol),
            band(prev, vcol), band(cur, vcol), band(nxt, vcol),
            pl.BlockSpec((1, lc, A_KV), lambda b, n: (b, 0, kcol)),
            pl.BlockSpec((1, lc, A_KV), lambda b, n: (b, 0, vcol)),
        ],
        out_specs=pl.BlockSpec((1, WINDOW, A_Q), lambda b, n: (b, n, 0)),
        out_shape=jax.ShapeDtypeStruct((bsz, seq, A_Q), BF16),
        compiler_params=_params(("arbitrary", "arbitrary")),
        name="window_attention",
    )(sink, proj, proj, proj, proj, proj, proj, proj, proj_ctx, proj_ctx)


def _ctx_attn_kernel(sink_ref, q_ref, k_ref, v_ref, o_ref):
    scale = HEAD_DIM ** -0.5 * LOG2E
    q = q_ref[0]
    lc = q.shape[0]
    for h in range(A_Q_HEADS):
        hk = h // A_GROUP
        hs = slice(hk * HEAD_DIM, (hk + 1) * HEAD_DIM)
        qs = _bf(q[:, h * HEAD_DIM:(h + 1) * HEAD_DIM])
        s = _dot_nt(qs, _bf(k_ref[0][:, hs])) * scale
        sink_col = jnp.full((lc, 1), sink_ref[h], F32) * LOG2E
        o = _softmax_av([s], [_bf(v_ref[0][:, hs])], sink_col)
        o_ref[0, :, h * HEAD_DIM:(h + 1) * HEAD_DIM] = _bf(o)


def _ctx_attn(sink, proj_ctx):
    bsz, lc, _ = proj_ctx.shape
    kcol, vcol = A_Q // A_KV, A_Q // A_KV + 1
    return pl.pallas_call(
        _ctx_attn_kernel,
        grid=(bsz,),
        in_specs=[
            pl.BlockSpec(memory_space=pltpu.SMEM),
            pl.BlockSpec((1, lc, A_Q), lambda b: (b, 0, 0)),
            pl.BlockSpec((1, lc, A_KV), lambda b: (b, 0, kcol)),
            pl.BlockSpec((1, lc, A_KV), lambda b: (b, 0, vcol)),
        ],
        out_specs=pl.BlockSpec((1, lc, A_Q), lambda b: (b, 0, 0)),
        out_shape=jax.ShapeDtypeStruct((bsz, lc, A_Q), BF16),
        compiler_params=_params(("arbitrary",)),
        name="context_attention",
    )(sink, proj_ctx, proj_ctx, proj_ctx)


def _hgrn_chunk(bq, bi, bf, lb, st, rev):
    heads = range(len(bq))
    q = [_silu(x) for x in bq]
    v = bi
    f = [lb[h] + (1.0 - lb[h]) * _sigmoid(bf[h]) for h in heads]
    k = [1.0 - x for x in f]
    g = [jnp.log(x) * LOG2E for x in f]
    b = [_cumsum_rows(x, rev) for x in g]
    btot = [x[0:1] if rev else x[CHUNK - 1:CHUNK] for x in b]
    vb = [_bf(x) for x in v]
    stb = [_bf(x) for x in st]
    o = [_dot_nt(_bf(q[h] * jnp.exp2(b[h])), stb[h]) for h in heads]
    khat = [_bf(k[h] * jnp.exp2(btot[h] - b[h])) for h in heads]
    dst = [_dot_tn(vb[h], khat[h]) for h in heads]
    st_new = [st[h] * jnp.exp2(btot[h]) + dst[h] for h in heads]
    nsub = CHUNK // SUB
    row8 = lax.broadcasted_iota(jnp.int32, (8, 1), 0)
    outs = [[] for _ in heads]
    for blk in range(nsub):
        r0 = blk * SUB
        rs = slice(r0, r0 + SUB)
        acc = [o[h][rs] for h in heads]
        if rev and blk < nsub - 1:
            ref_row, lo, hi = r0 + SUB, r0 + SUB, CHUNK
        elif (not rev) and blk > 0:
            ref_row, lo, hi = r0 - 1, 0, r0
        else:
            ref_row = None
        if ref_row is not None:
            bref = [b[h][ref_row:ref_row + 1] for h in heads]
            qn = [_bf(q[h][rs] * jnp.exp2(b[h][rs] - bref[h])) for h in heads]
            kn = [_bf(k[h][lo:hi] * jnp.exp2(bref[h] - b[h][lo:hi])) for h in heads]
            att = [_bf(_dot_nt(qn[h], kn[h])) for h in heads]
            acc = [acc[h] + _dot(att[h], vb[h][lo:hi]) for h in heads]
        piece = 8
        acc = [[a[p0:p0 + piece] for p0 in range(0, SUB, piece)] for a in acc]
        for s in range(SUB):
            reached = range(0, s // piece + 1) if rev else range(s // piece, SUB // piece)
            for pc in reached:
                t0 = r0 + pc * piece
                mask = (row8 + pc * piece <= s) if rev else (row8 + pc * piece >= s)
                for h in heads:
                    rel = b[h][t0:t0 + piece] - b[h][r0 + s:r0 + s + 1]
                    dec = jnp.exp2(jnp.where(mask, rel, -jnp.inf))
                    w = jnp.sum(q[h][t0:t0 + piece] * dec * k[h][r0 + s:r0 + s + 1], axis=-1, keepdims=True)
                    acc[h][pc] = acc[h][pc] + w * v[h][r0 + s:r0 + s + 1]
        for h in heads:
            outs[h].extend(acc[h])
    return [jnp.concatenate(x, axis=0) for x in outs], st_new


def _hgrn_kernel(q_ref, i_ref, f_ref, lb_ref, s0_ref, o_ref, sfin_ref, st_ref, *, rev, nchunk, nhead):
    c = pl.program_id(2)

    @pl.when(c == 0)
    def _():
        st_ref[...] = s0_ref[0]

    lanes = [slice(h * HEAD_DIM, (h + 1) * HEAD_DIM) for h in range(nhead)]
    lb = [lb_ref[:, ln] for ln in lanes]

    def body(ci, carry):
        cc = (nchunk - 1 - ci) if rev else ci
        rows = pl.ds(pl.multiple_of(cc * CHUNK, CHUNK), CHUNK)
        o, st_new = _hgrn_chunk([q_ref[0, rows, ln] for ln in lanes], [i_ref[0, rows, ln] for ln in lanes],
                                [f_ref[0, rows, ln] for ln in lanes], lb,
                                [st_ref[h] for h in range(nhead)], rev)
        for h in range(nhead):
            o_ref[0, rows, lanes[h]] = o[h]
            st_ref[h] = st_new[h]
        return carry

    lax.fori_loop(0, nchunk, body, 0, unroll=2 if nchunk % 2 == 0 else 1)

    @pl.when(c == pl.num_programs(2) - 1)
    def _():
        sfin_ref[0] = st_ref[...]


def _hgrn_scan(proj, lb, s0, rev, tblk, nhead):
    bsz, seq, _ = proj.shape
    nblk = seq // tblk
    width = nhead * HEAD_DIM
    base = A_Q + 2 * A_KV
    assert base % width == 0 and B_W % width == 0
    qc, ic = base // width, (base + B_W) // width
    fc = (base + (3 if rev else 2) * B_W) // width
    blk = (lambda c: nblk - 1 - c) if rev else (lambda c: c)

    def col(c0):
        return pl.BlockSpec((1, tblk, width), lambda b, h, c: (b, blk(c), c0 + h))

    state = pl.BlockSpec((1, nhead, HEAD_DIM, HEAD_DIM), lambda b, h, c: (b, h, 0, 0))
    return pl.pallas_call(
        functools.partial(_hgrn_kernel, rev=rev, nchunk=tblk // CHUNK, nhead=nhead),
        grid=(bsz, B_HEADS // nhead, nblk),
        in_specs=[
            col(qc), col(ic), col(fc),
            pl.BlockSpec((1, width), lambda b, h, c: (0, h)),
            state,
        ],
        out_specs=[
            pl.BlockSpec((1, tblk, width), lambda b, h, c: (b, blk(c), h)),
            state,
        ],
        out_shape=[
            jax.ShapeDtypeStruct((bsz, seq, B_W), F32),
            jax.ShapeDtypeStruct((bsz, B_HEADS, HEAD_DIM, HEAD_DIM), F32),
        ],
        scratch_shapes=[pltpu.VMEM((nhead, HEAD_DIM, HEAD_DIM), F32)],
        compiler_params=_params(("arbitrary", "arbitrary", "arbitrary")),
        name="hgrn2_bwd" if rev else "hgrn2_fwd",
    )(proj, proj, proj, lb, s0)


def _ab_out_kernel(oa_ref, of_ref, ob_ref, g0_ref, g1_ref, on_ref, w_ref, h_ref, gt_ref, o_ref, lhs_ref):
    lhs_ref[:, :A_Q] = oa_ref[0]
    half = B_W // 2
    for hd in range(B_HEADS):
        sl = slice(hd * HEAD_DIM, (hd + 1) * HEAD_DIM)
        o = of_ref[0, :, sl] + ob_ref[0, :, sl]
        o = o * lax.rsqrt(jnp.mean(o * o, axis=-1, keepdims=True) + EPS)
        o = o * on_ref[:, sl]
        gref = g0_ref if hd * HEAD_DIM < half else g1_ref
        gs = slice(hd * HEAD_DIM % half, hd * HEAD_DIM % half + HEAD_DIM)
        o = o * _silu(gref[0, :, gs])
        lhs_ref[:, A_Q + hd * HEAD_DIM:A_Q + (hd + 1) * HEAD_DIM] = _bf(o)
    y = _dot(lhs_ref[...], w_ref[...])
    o_ref[0] = h_ref[0] + gt_ref[0] * y


def _ab_out(oa, of, ob, proj, onorm, w_out, h, gt, tm):
    bsz, seq, d = h.shape
    half = B_W // 2
    gcol = (A_Q + 2 * A_KV + 4 * B_W) // half
    row = lambda b, i: (b, i, 0)
    return pl.pallas_call(
        _ab_out_kernel,
        grid=(bsz, seq // tm),
        in_specs=[
            pl.BlockSpec((1, tm, A_Q), row),
            pl.BlockSpec((1, tm, B_W), row),
            pl.BlockSpec((1, tm, B_W), row),
            pl.BlockSpec((1, tm, half), lambda b, i: (b, i, gcol)),
            pl.BlockSpec((1, tm, half), lambda b, i: (b, i, gcol + 1)),
            pl.BlockSpec((1, B_W), lambda b, i: (0, 0)),
            pl.BlockSpec((A_Q + B_W, d), lambda b, i: (0, 0)),
            pl.BlockSpec((1, tm, d), row),
            pl.BlockSpec((1, 1, d), lambda b, i: (b, 0, 0)),
        ],
        out_specs=pl.BlockSpec((1, tm, d), row),
        out_shape=jax.ShapeDtypeStruct((bsz, seq, d), F32),
        scratch_shapes=[pltpu.VMEM((tm, A_Q + B_W), BF16)],
        compiler_params=_params(("arbitrary", "arbitrary")),
        name="ab_outproj",
    )(oa, of, ob, proj, proj, onorm, w_out, h, gt)


def _mlp_kernel(*refs, final):
    if final:
        x_ref, g_ref, sc_ref, sh_ref, gt_ref, wu_ref, wd_ref, fn_ref, o_ref, u_ref = refs
    else:
        x_ref, g_ref, sc_ref, sh_ref, gt_ref, wu_ref, wd_ref, o_ref, u_ref = refs
    j = pl.program_id(2)

    tm = x_ref.shape[1]

    @pl.when(j == 0)
    def _():
        _normmod_rows(x_ref, g_ref, sc_ref, sh_ref, u_ref)

    hid = jnp.maximum(_dot(u_ref[...], wu_ref[...]), 0.0)
    hid = _bf(hid * hid)
    ncol = 512

    @pl.when(j == 0)
    def _():
        for n0 in range(0, o_ref.shape[2], ncol):
            o_ref[0, :, n0:n0 + ncol] = _dot(hid, wd_ref[:, n0:n0 + ncol])

    @pl.when(j != 0)
    def _():
        for n0 in range(0, o_ref.shape[2], ncol):
            o_ref[0, :, n0:n0 + ncol] += _dot(hid, wd_ref[:, n0:n0 + ncol])

    @pl.when(j == pl.num_programs(2) - 1)
    def _():
        piece = 128
        for r0 in range(0, tm, piece):
            rs = slice(r0, r0 + piece)
            y = x_ref[0, rs] + gt_ref[0] * o_ref[0, rs]
            if final:
                y = (y * lax.rsqrt(jnp.mean(y * y, axis=-1, keepdims=True) + EPS)) * fn_ref[...]
            o_ref[0, rs] = y


def _mlp(x, g, sc, sh, gt, w_up, w_down, final_gain, tm, tf):
    bsz, seq, d = x.shape
    dff = w_up.shape[1]
    vec = pl.BlockSpec((1, 1, d), lambda b, i, j: (b, 0, 0))
    in_specs = [
        pl.BlockSpec((1, tm, d), lambda b, i, j: (b, i, 0)),
        pl.BlockSpec((1, d), lambda b, i, j: (0, 0)),
        vec, vec, vec,
        pl.BlockSpec((d, tf), lambda b, i, j: (0, j)),
        pl.BlockSpec((tf, d), lambda b, i, j: (j, 0)),
    ]
    args = [x, g, sc, sh, gt, w_up, w_down]
    if final_gain is not None:
        in_specs.append(pl.BlockSpec((1, d), lambda b, i, j: (0, 0)))
        args.append(final_gain)
    return pl.pallas_call(
        functools.partial(_mlp_kernel, final=final_gain is not None),
        grid=(bsz, seq // tm, dff // tf),
        in_specs=in_specs,
        out_specs=pl.BlockSpec((1, tm, d), lambda b, i, j: (b, i, 0)),
        out_shape=jax.ShapeDtypeStruct((bsz, seq, d), F32),
        scratch_shapes=[pltpu.VMEM((tm, d), BF16)],
        compiler_params=_params(("arbitrary", "arbitrary", "arbitrary")),
        name="sqrelu_mlp",
    )(*args)


def _rk_proj_kernel(x_ref, xp_ref, xn_ref, g_ref, sc_ref, sh_ref, mix_ref, wb_ref, ws_ref, ob_ref, os_ref,
                    u_ref, xx_ref, lhs_ref, *, nbig, nsmall):
    i = pl.program_id(1)
    j = pl.program_id(2)
    tm = x_ref.shape[1]
    rb = 32

    def build(m, slot):
        mixrow = mix_ref[pl.ds(m, 1), :]
        for r0 in range(0, tm, rb):
            lhs_ref[slot, r0:r0 + rb] = _bf(u_ref[8 + r0:8 + r0 + rb] + xx_ref[r0:r0 + rb] * mixrow)

    @pl.when(j == 0)
    def _():
        g, sc, sh = g_ref[...], sc_ref[0], sh_ref[0]
        _normmod_rows(x_ref, g_ref, sc_ref, sh_ref, u_ref, row0=8)
        up = _normmod(xp_ref[0], g, sc, sh)[7:8]
        un = _normmod(xn_ref[0], g, sc, sh)[0:1]
        u_ref[7:8] = jnp.where(i == 0, 0.0, up)
        u_ref[8 + tm:9 + tm] = jnp.where(i == pl.num_programs(1) - 1, 0.0, un)
        mix0 = mix_ref[0:1, :]
        for r0 in range(0, tm, ROW_PIECE):
            u = u_ref[8 + r0:8 + r0 + ROW_PIECE]
            xx = 0.5 * (u_ref[7 + r0:7 + r0 + ROW_PIECE] + u_ref[9 + r0:9 + r0 + ROW_PIECE]) - u
            xx_ref[r0:r0 + ROW_PIECE] = xx
            lhs_ref[0, r0:r0 + ROW_PIECE] = _bf(u + xx * mix0)

    @pl.when(j < nbig)
    def _():
        build(j + 1, (j + 1) % 2)
        ob_ref[0, 0] = _dot(lhs_ref[j % 2], wb_ref[0])

    @pl.when(j == nbig)
    def _():
        for s in range(nsmall):
            os_ref[s, 0] = _dot(lhs_ref[(nbig + s) % 2], ws_ref[s])
            if s + 1 < nsmall:
                build(nbig + s + 1, (nbig + s + 1) % 2)


def _rk_proj(x, g, sc, sh, mix, w_big, w_small, tm):
    bsz, seq, d = x.shape
    nbig, _, n = w_big.shape
    nsmall, _, ns = w_small.shape
    r8 = tm // 8
    last8 = seq // 8 - 1
    vec = pl.BlockSpec((1, 1, d), lambda b, i, j: (b, 0, 0))
    return pl.pallas_call(
        functools.partial(_rk_proj_kernel, nbig=nbig, nsmall=nsmall),
        grid=(bsz, seq // tm, nbig + 1),
        in_specs=[
            pl.BlockSpec((1, tm, d), lambda b, i, j: (b, i, 0)),
            pl.BlockSpec((1, 8, d), lambda b, i, j: (b, jnp.maximum(i * r8 - 1, 0), 0)),
            pl.BlockSpec((1, 8, d), lambda b, i, j: (b, jnp.minimum((i + 1) * r8, last8), 0)),
            pl.BlockSpec((1, d), lambda b, i, j: (0, 0)),
            vec, vec,
            pl.BlockSpec((nbig + nsmall, d), lambda b, i, j: (0, 0)),
            pl.BlockSpec((1, d, n), lambda b, i, j: (jnp.minimum(j, nbig - 1), 0, 0)),
            pl.BlockSpec((nsmall, d, ns), lambda b, i, j: (0, 0, 0)),
        ],
        out_specs=[
            pl.BlockSpec((1, 1, tm, n), lambda b, i, j: (jnp.minimum(j, nbig - 1), b, i, 0)),
            pl.BlockSpec((nsmall, 1, tm, ns), lambda b, i, j: (0, b, i, 0)),
        ],
        out_shape=[
            jax.ShapeDtypeStruct((nbig, bsz, seq, n), F32),
            jax.ShapeDtypeStruct((nsmall, bsz, seq, ns), F32),
        ],
        scratch_shapes=[pltpu.VMEM((tm + 16, d), F32), pltpu.VMEM((tm, d), F32), pltpu.VMEM((2, tm, d), BF16)],
        compiler_params=_params(("arbitrary", "arbitrary", "arbitrary")),
        name="rwkv_proj",
    )(x, x, x, g, sc, sh, mix, w_big, w_small)


DECAY_SCALE = float(np.exp(-0.5))


def _pair_consts(rev):
    n = 2 * CHUNK
    ri = lax.broadcasted_iota(jnp.int32, (n, n), 0)
    ci = lax.broadcasted_iota(jnp.int32, (n, n), 1)
    same = (ri // CHUNK) == (ci // CHUNK)
    rt, ct = ri % CHUNK, ci % CHUNK
    strict = same & ((ct > rt) if rev else (ct < rt))
    incl = same & ((ct >= rt) if rev else (ct <= rt))
    eye = jnp.where(ri == ci, 1.0, 0.0).astype(F32)
    lane = lax.broadcasted_iota(jnp.int32, (1, LANES), 1)
    lane_lo = lane < C_HEAD
    m0 = jnp.where(lane_lo, 1.0, 0.0).astype(F32)
    m1 = 1.0 - m0
    blk2 = (ri // 2) == (ci // 2)
    level = {}
    m = 2
    while m < CHUNK:
        if m % 8:
            rsel, csel = ri, ci
        else:
            rc = lax.broadcasted_iota(jnp.int32, (n // 2, n), 0)
            csel = lax.broadcasted_iota(jnp.int32, (n // 2, n), 1)
            rsel = (rc // m) * (2 * m) + (0 if rev else m) + rc % m
        level[m] = ((rsel // (2 * m)) == (csel // (2 * m))) & ((rsel // m) != (csel // m))
        m *= 2
    return dict(strict=strict, incl=incl, eye=eye, m0=m0, m1=m1, lane_lo=lane_lo, blk2=blk2, level=level)


def _head_sums(x, cs):
    s0 = jnp.sum(x * cs["m0"], axis=-1, keepdims=True)
    s1 = jnp.sum(x * cs["m1"], axis=-1, keepdims=True)
    return jnp.where(cs["lane_lo"], s0, s1)


def _unit_tri_inverse(n, cs, rev):
    size = 2 * CHUNK
    t = [cs["eye"] + jnp.where(cs["blk2"], x, 0.0) for x in n]
    nb = [_bf(x) for x in n]
    m = 2
    while m < CHUNK:
        tb = [_bf(x) for x in t]
        sel = cs["level"][m]
        if m % 8:
            q = [_bf(_dot(a, b)) for a, b in zip(tb, nb)]
            q = [_dot(a, b) for a, b in zip(q, tb)]
            t = [a + jnp.where(sel, b, 0.0) for a, b in zip(t, q)]
        else:
            starts = range(0 if rev else m, size, 2 * m)
            pick = lambda x: jnp.concatenate([x[r0:r0 + m] for r0 in starts], axis=0)
            q = [_bf(_dot(_bf(pick(x)), b)) for x, b in zip(t, nb)]
            q = [_dot(a, b) for a, b in zip(q, tb)]
            t = [_add_rows(x, jnp.where(sel, y, 0.0), starts, m) for x, y in zip(t, q)]
        m *= 2
    return t


def _add_rows(x, upd, starts, m):
    pieces, pos = [], 0
    for k, r0 in enumerate(starts):
        if r0 > pos:
            pieces.append(x[pos:r0])
        pieces.append(x[r0:r0 + m] + upd[k * m:(k + 1) * m])
        pos = r0 + m
    if pos < x.shape[0]:
        pieces.append(x[pos:])
    return jnp.concatenate(pieces, axis=0)


def _halves(top, bottom):
    return jnp.concatenate([top[:CHUNK], bottom[CHUNK:]], axis=0)


def _stack2(x, cs):
    return jnp.concatenate([x * cs["m0"], x * cs["m1"]], axis=0)


def _rwkv_chunk(r, k, v, lw, la, w2, a2, w0, a0, kkv, kav, ht, cs, rev):
    pairs = range(len(r))
    tl, lab = _bf(jnp.tanh(lw)), _bf(la)
    z = [w0[p] + _dot(tl, w2[p]) for p in pairs]
    za = [_dot(lab, a2[p]) for p in pairs]
    kkr = [k[p] * kkv[p] for p in pairs]
    ss = [_head_sums(x * x, cs) for x in kkr]
    ld = [-(DECAY_SCALE * LOG2E) * _sigmoid(x) for x in z]
    cl = [_cumsum_rows(x, rev) for x in ld]
    a = [_sigmoid(a0[p] + za[p]) for p in pairs]
    kk = [x / jnp.maximum(jnp.sqrt(s), 1e-12) for x, s in zip(kkr, ss)]
    kd = [k[p] * (1.0 + (a[p] - 1.0) * kav[p]) for p in pairs]
    bb = [x * y for x, y in zip(kk, a)]
    ctot = [x[0:1] if rev else x[CHUNK - 1:CHUNK] for x in cl]
    e_neg = [jnp.exp2(-x) for x in cl]
    e_tail = [jnp.exp2(c - x) for c, x in zip(ctot, cl)]
    at = [_bf(_stack2(-kk[p] * jnp.exp2(cl[p] - ld[p]), cs)) for p in pairs]
    rt = [_bf(_stack2(r[p] * jnp.exp2(cl[p]), cs)) for p in pairs]
    rhs = [_bf(jnp.concatenate([bb[p] * e_neg[p], kd[p] * e_neg[p]], axis=0)) for p in pairs]
    g = [_dot_nt(jnp.concatenate([x, y], axis=0), w) for x, y, w in zip(at, rt, rhs)]
    g1 = [x[:2 * CHUNK] for x in g]
    g2 = [x[2 * CHUNK:] for x in g]
    strict, incl = cs["strict"], cs["incl"]
    g1r = [pltpu.roll(x, C_HEAD, 1) for x in g1]
    n_ab = [jnp.where(strict, _halves(x, y), 0.0) for x, y in zip(g1, g1r)]
    n_ak = [_bf(jnp.where(strict, _halves(y, x), 0.0)) for x, y in zip(g1, g1r)]
    t = _unit_tri_inverse(n_ab, cs, rev)
    vs = [_bf(_stack2(x, cs)) for x in v]
    hkv = [_bf(x.T) for x in ht]
    xx = [_bf(_dot(jnp.concatenate([at[p], n_ak[p]], axis=1), jnp.concatenate([hkv[p], vs[p]], axis=0)))
          for p in pairs]
    u =[_bf(_dot(_bf(a_), b_)) for a_, b_ in zip(t, xx)]
    bk = [_bf(jnp.concatenate([_stack2(bb[p] * e_tail[p], cs), _stack2(kd[p] * e_tail[p], cs)], axis=0))
          for p in pairs]
    uv = [jnp.concatenate([a_, b_], axis=0) for a_, b_ in zip(u, vs)]
    dh = [_dot_tn(a_, b_) for a_, b_ in zip(uv, bk)]
    ht_new = [ht[p] * jnp.exp2(ctot[p]) + dh[p] for p in pairs]
    g2r = [pltpu.roll(x, C_HEAD, 1) for x in g2]
    n_rb = [_bf(jnp.where(incl, _halves(x, y), 0.0)) for x, y in zip(g2, g2r)]
    n_rk = [_bf(jnp.where(incl, _halves(y, x), 0.0)) for x, y in zip(g2, g2r)]
    ysum = [_dot(jnp.concatenate([rt[p], n_rb[p], n_rk[p]], axis=1),
                 jnp.concatenate([hkv[p], u[p], vs[p]], axis=0)) for p in pairs]
    y = [x[:CHUNK] + x[CHUNK:] for x in ysum]
    return y, ht_new, kd


def _rwkv_combine(y, yf, r, k, v, kd, laf, a2f, a0f, kav, rkv, lng, lnb, cs):
    pairs = range(len(y))
    inv_n = 1.0 / C_HEAD
    lafb = _bf(laf)
    zf = [_dot(lafb, a2f[p]) for p in pairs]
    ysum = [a + b for a, b in zip(y, yf)]
    mu = [_head_sums(x, cs) * inv_n for x in ysum]
    dev = [a - b for a, b in zip(ysum, mu)]
    var = [_head_sums(x * x, cs) * inv_n for x in dev]
    a_f = [_sigmoid(a0f[p] + zf[p]) for p in pairs]
    kd_f = [k[p] * (1.0 + (a_f[p] - 1.0) * kav[p]) for p in pairs]
    bsum = [_head_sums(r[p] * (kd_f[p] + kd[p]) * rkv[p], cs) for p in pairs]
    return [dev[p] * lax.rsqrt(var[p] + GN_EPS) * lng[p] + lnb[p] + bsum[p] * v[p] for p in pairs]


def _rwkv_scan_kernel(*refs, rev, nchunk, npair, combine):
    if combine:
        (r_ref, k_ref, v_ref, lw_ref, la_ref, w2_ref, a2_ref, w0_ref, a0_ref, kk_ref, ka_ref, s0_ref,
         yf_ref, laf_ref, a2f_ref, a0f_ref, rk_ref, lng_ref, lnb_ref, o_ref, sfin_ref, ht_ref) = refs
    else:
        (r_ref, k_ref, v_ref, lw_ref, la_ref, w2_ref, a2_ref, w0_ref, a0_ref, kk_ref, ka_ref, s0_ref,
         o_ref, sfin_ref, ht_ref) = refs
    c = pl.program_id(2)

    @pl.when(c == 0)
    def _():
        ht_ref[...] = s0_ref[0]

    cs = _pair_consts(rev)

    def body(ci, carry):
        cc = (nchunk - 1 - ci) if rev else ci
        rows = pl.ds(pl.multiple_of(cc * CHUNK, CHUNK), CHUNK)
        lw = lw_ref[0, 0, rows, :]
        la = la_ref[0, 0, rows, :]
        lanes = [slice(pr * LANES, (pr + 1) * LANES) for pr in range(npair)]
        r = [r_ref[0, 0, rows, ln] for ln in lanes]
        k = [k_ref[0, 0, rows, ln] for ln in lanes]
        v = [v_ref[0, 0, rows, ln] for ln in lanes]
        kav = [ka_ref[:, ln] for ln in lanes]
        y, ht_new, kd = _rwkv_chunk(
            r, k, v, lw, la, [w2_ref[0, :, ln] for ln in lanes], [a2_ref[0, :, ln] for ln in lanes],
            [w0_ref[0, :, ln] for ln in lanes], [a0_ref[0, :, ln] for ln in lanes],
            [kk_ref[:, ln] for ln in lanes], kav, [ht_ref[pr] for pr in range(npair)], cs, rev)
        for pr in range(npair):
            ht_ref[pr] = ht_new[pr]
        if combine:
            y = _rwkv_combine(
                y, [yf_ref[0, rows, ln] for ln in lanes], r, k, v, kd, laf_ref[0, 0, rows, :],
                [a2f_ref[0, :, ln] for ln in lanes], [a0f_ref[0, :, ln] for ln in lanes], kav,
                [rk_ref[:, ln] for ln in lanes], [lng_ref[:, ln] for ln in lanes],
                [lnb_ref[:, ln] for ln in lanes], cs)
        for pr in range(npair):
            o_ref[0, rows, lanes[pr]] = y[pr]
        return carry

    lax.fori_loop(0, nchunk, body, 0, unroll=2 if nchunk % 2 == 0 else 1)

    @pl.when(c == pl.num_programs(2) - 1)
    def _():
        sfin_ref[0] = ht_ref[...]


def _rwkv_scan(rkv, small, prm, s0, rev, tblk, npair, y_fwd=None):
    _, bsz, seq, d = rkv.shape
    nblk = seq // tblk
    width = npair * LANES
    e = 1 if rev else 0
    blk = (lambda c: nblk - 1 - c) if rev else (lambda c: c)
    combine = y_fwd is not None

    def tok(m):
        return pl.BlockSpec((1, 1, tblk, width), lambda b, p, c: (m, b, blk(c), p))

    def lora(m, half):
        return pl.BlockSpec((1, 1, tblk, LANES), lambda b, p, c: (m, b, blk(c), half))

    def mat(idx):
        return pl.BlockSpec((1, LANES, width), lambda b, p, c: (idx, 0, p))

    def vec3(idx):
        return pl.BlockSpec((1, 1, width), lambda b, p, c: (idx, 0, p))

    vec = pl.BlockSpec((1, width), lambda b, p, c: (0, p))
    state = pl.BlockSpec((1, npair, LANES, LANES), lambda b, p, c: (b, p, 0, 0))
    out_tok = pl.BlockSpec((1, tblk, width), lambda b, p, c: (b, blk(c), p))
    in_specs = [tok(0), tok(1), tok(2), lora(1, e), lora(2, e), mat(e), mat(e), vec3(e), vec3(e), vec, vec, state]
    args = [rkv, rkv, rkv, small, small, prm["w2"], prm["a2"], prm["w0"], prm["a0"], prm["k_k"], prm["k_a"], s0]
    if combine:
        in_specs += [out_tok, lora(2, 0), mat(0), vec3(0), vec, vec, vec]
        args += [y_fwd, small, prm["a2"], prm["a0"], prm["r_k"], prm["ln_g"], prm["ln_b"]]
    return pl.pallas_call(
        functools.partial(_rwkv_scan_kernel, rev=rev, nchunk=tblk // CHUNK, npair=npair, combine=combine),
        grid=(bsz, d // width, nblk),
        in_specs=in_specs,
        out_specs=[out_tok, state],
        out_shape=[
            jax.ShapeDtypeStruct((bsz, seq, d), F32),
            jax.ShapeDtypeStruct((bsz, d // LANES, LANES, LANES), F32),
        ],
        scratch_shapes=[pltpu.VMEM((npair, LANES, LANES), F32)],
        compiler_params=_params(("arbitrary", "arbitrary", "arbitrary")),
        name="rwkv7_bwd" if rev else "rwkv7_fwd",
    )(*args)


def _rk_out_kernel(z_ref, gs_ref, g2_ref, wo_ref, h_ref, gt_ref, o_ref):
    gate = _dot(_bf(_sigmoid(gs_ref[0, 0])), g2_ref[...])
    y = _dot(_bf(z_ref[0] * gate), wo_ref[...])
    o_ref[0] = h_ref[0] + gt_ref[0] * y


def _rk_out(z, small, g2, wo, h, gt, tm):
    bsz, seq, d = h.shape
    row = lambda b, i: (b, i, 0)
    glora = g2.shape[0]
    return pl.pallas_call(
        _rk_out_kernel,
        grid=(bsz, seq // tm),
        in_specs=[
            pl.BlockSpec((1, tm, d), row),
            pl.BlockSpec((1, 1, tm, glora), lambda b, i: (0, b, i, 0)),
            pl.BlockSpec((glora, d), lambda b, i: (0, 0)),
            pl.BlockSpec((d, d), lambda b, i: (0, 0)),
            pl.BlockSpec((1, tm, d), row),
            pl.BlockSpec((1, 1, d), lambda b, i: (b, 0, 0)),
        ],
        out_specs=pl.BlockSpec((1, tm, d), row),
        out_shape=jax.ShapeDtypeStruct((bsz, seq, d), F32),
        compiler_params=_params(("arbitrary", "arbitrary")),
        name="rwkv_outproj",
    )(z, small, g2, wo, h, gt)


def _rope_tables(seq):
    t = jnp.arange(seq, dtype=jnp.int32)
    rows = (t // GRID_W).astype(F32)
    cols = (t % GRID_W).astype(F32)
    half = HEAD_DIM // 2
    n_freq = half // 2
    inv = ROPE_BASE ** (-jnp.arange(n_freq, dtype=F32) / n_freq)
    lane = jnp.arange(HEAD_DIM)
    pos = jnp.where((lane < half)[None, :], rows[:, None], cols[:, None])
    ang = pos * inv[lane % n_freq][None, :]
    cos, sin = jnp.cos(ang), jnp.sin(ang)
    first = ((lane % half) < n_freq)[None, :]
    return cos, jnp.where(first, -sin, 0.0), jnp.where(first, 0.0, sin)


def _pad_lanes(w, axis):
    pad = [(0, 0)] * w.ndim
    pad[axis] = (0, LANES - w.shape[axis])
    return jnp.pad(w, pad)


class _Tiles(NamedTuple):
    rows: int
    mlp_rows: int
    ff: int
    in_cols: int
    hgrn_tokens: int
    rwkv_tokens: int


def _tiles(seq, n_in, latent):
    if latent:
        return _Tiles(rows=512, mlp_rows=1024, ff=512, in_cols=n_in // 2, hgrn_tokens=512, rwkv_tokens=256)
    return _Tiles(rows=seq, mlp_rows=seq, ff=512, in_cols=512, hgrn_tokens=seq, rwkv_tokens=seq)


HGRN_HEADS_PER_STEP = 4
RWKV_PAIRS_PER_STEP = 16

def kernel(x, c, ctx, c_ctx, mod_w, mod_b, norm_mix, norm_ffn, ffn_up, ffn_down, ab_w_in, ab_w_out, attn_sink, hgrn_lb, hgrn_onorm, rk_mix, rk_wr, rk_wk, rk_wv, rk_wo, rk_w0, rk_w1, rk_w2, rk_a0, rk_a1, rk_a2, rk_g1, rk_g2, rk_kk, rk_ka, rk_rk, rk_ln_g, rk_ln_b, final_norm):
    bsz, seq, d = x.shape
    lc = ctx.shape[1]
    depth = mod_w.shape[0]
    n_in = ab_w_in.shape[2]
    tl, tc = _tiles(seq, n_in, latent=True), _tiles(lc, n_in, latent=False)
    assert bsz + 1 <= 8 and seq % tl.mlp_rows == 0 and lc % CHUNK == 0

    cond8 = jnp.zeros((8, d), F32).at[:bsz].set(c).at[bsz].set(c_ctx)
    mod = _modulation(cond8, mod_w, mod_b)
    lb_all = jnp.cumsum(jax.nn.softmax(hgrn_lb.astype(F32), axis=0), axis=0)
    rope = _rope_tables(seq)

    h, hc = x, ctx
    for layer in range(depth):
        last = layer == depth - 1
        jl = layer // 2
        m_lat = mod[layer, :bsz].reshape(bsz, 1, 6, d)
        m_ctx = jnp.broadcast_to(mod[layer, bsz].reshape(1, 1, 6, d), (bsz, 1, 6, d))
        sh1, sc1, gt1, sh2, sc2, gt2 = (m_lat[:, :, i] for i in range(6))
        csh1, csc1, cgt1, csh2, csc2, cgt2 = (m_ctx[:, :, i] for i in range(6))
        g_mix = norm_mix[layer].reshape(1, d)
        g_ffn = norm_ffn[layer].reshape(1, d)
        if layer % 2 == 0:
            w_in = _bf(ab_w_in[jl])
            w_out = _bf(ab_w_out[jl])
            sink = attn_sink[jl].astype(F32)
            lb = lb_all[jl].reshape(1, B_W)
            onorm = hgrn_onorm[jl].reshape(1, B_W).astype(F32)
            pc = _inproj(hc, g_mix, csc1, csh1, w_in, None, tc.rows, tc.in_cols)
            pl_ = _inproj_stream(h, g_mix, sc1, sh1, w_in, rope, tl.rows, tl.in_cols)
            oa = _win_attn(sink, pl_, pc)
            zeros = jnp.zeros((bsz, B_HEADS, HEAD_DIM, HEAD_DIM), F32)
            nhead = HGRN_HEADS_PER_STEP
            ocf, scf = _hgrn_scan(pc, lb, zeros, False, tc.hgrn_tokens, nhead)
            ocb, scb = _hgrn_scan(pc, lb, zeros, True, tc.hgrn_tokens, nhead)
            olf, _ = _hgrn_scan(pl_, lb, scf, False, tl.hgrn_tokens, nhead)
            olb, _ = _hgrn_scan(pl_, lb, scb, True, tl.hgrn_tokens, nhead)
            h = _ab_out(oa, olf, olb, pl_, onorm, w_out, h, gt1, tl.rows)
            if not last:
                oca = _ctx_attn(sink, pc)
                hc = _ab_out(oca, ocf, ocb, pc, onorm, w_out, hc, cgt1, tc.rows)
        else:
            w_big = _bf(jnp.stack([rk_wr[jl], rk_wk[jl], rk_wv[jl]]))
            w_small = _bf(jnp.stack([
                rk_g1[jl],
                jnp.concatenate([_pad_lanes(rk_w1[jl, 0], 1), _pad_lanes(rk_w1[jl, 1], 1)], axis=1),
                jnp.concatenate([_pad_lanes(rk_a1[jl, 0], 1), _pad_lanes(rk_a1[jl, 1], 1)], axis=1),
            ]))
            assert w_small.shape[-1] == 2 * LANES
            mix = rk_mix[jl]
            prm = dict(
                w2=_bf(_pad_lanes(rk_w2[jl], 1)), a2=_bf(_pad_lanes(rk_a2[jl], 1)),
                w0=rk_w0[jl].reshape(2, 1, d), a0=rk_a0[jl].reshape(2, 1, d),
                k_k=rk_kk[jl].reshape(1, d), k_a=rk_ka[jl].reshape(1, d), r_k=rk_rk[jl].reshape(1, d),
                ln_g=rk_ln_g[jl].reshape(1, d), ln_b=rk_ln_b[jl].reshape(1, d),
            )
            g2 = _bf(rk_g2[jl])
            wo = _bf(rk_wo[jl])
            mix = mix[jnp.array([0, 2, 3, 5, 1, 4])]
            rkv_c, sm_c = _rk_proj(hc, g_mix, csc1, csh1, mix, w_big, w_small, tc.rows)
            rkv_l, sm_l = _rk_proj(h, g_mix, sc1, sh1, mix, w_big, w_small, tl.rows)
            zeros = jnp.zeros((bsz, d // LANES, LANES, LANES), F32)
            npair = RWKV_PAIRS_PER_STEP
            ycf, s_f = _rwkv_scan(rkv_c, sm_c, prm, zeros, False, tc.rwkv_tokens, npair)
            zc, s_b = _rwkv_scan(rkv_c, sm_c, prm, zeros, True, tc.rwkv_tokens, npair, y_fwd=ycf)
            ylf, _ = _rwkv_scan(rkv_l, sm_l, prm, s_f, False, tl.rwkv_tokens, npair)
            zl, _ = _rwkv_scan(rkv_l, sm_l, prm, s_b, True, tl.rwkv_tokens, npair, y_fwd=ylf)
            h = _rk_out(zl, sm_l, g2, wo, h, gt1, tl.rows)
            if not last:
                hc = _rk_out(zc, sm_c, g2, wo, hc, cgt1, tc.rows)
        w_up = _bf(ffn_up[layer])
        w_dn = _bf(ffn_down[layer])
        final_gain = final_norm.reshape(1, d) if last else None
        h = _mlp(h, g_ffn, sc2, sh2, gt2, w_up, w_dn, final_gain, tl.mlp_rows, tl.ff)
        if not last:
            hc = _mlp(hc, g_ffn, csc2, csh2, cgt2, w_up, w_dn, None, tc.mlp_rows, tc.ff)
    return h
```

```python
import functools
from typing import NamedTuple

import jax
import jax.numpy as jnp
import numpy as np
from jax import lax
from jax.experimental import pallas as pl
from jax.experimental.pallas import tpu as pltpu

F32 = jnp.float32
BF16 = jnp.bfloat16

LANES = 128
HEAD_DIM = 128
GRID_W = 64
WINDOW = 128
ROPE_BASE = 10000.0
A_Q_HEADS = 8
A_KV_HEADS = 2
A_GROUP = A_Q_HEADS // A_KV_HEADS
A_Q = A_Q_HEADS * HEAD_DIM
A_KV = A_KV_HEADS * HEAD_DIM
B_HEADS = 8
B_W = B_HEADS * HEAD_DIM
C_HEAD = 64
CHUNK = 64
SUB = 16
EPS = 1e-6
GN_EPS = 64e-5
VMEM_LIMIT = 58 * 1024 * 1024
LOG2E = float(np.log2(np.e))

NT_DIMS = (((1,), (1,)), ((), ()))
TN_DIMS = (((0,), (0,)), ((), ()))


def _dot(a, b, **kw):
    return jnp.dot(a, b, preferred_element_type=F32, **kw)


def _dot_nt(a, b):
    return lax.dot_general(a, b, NT_DIMS, preferred_element_type=F32)


def _dot_tn(a, b):
    return lax.dot_general(a, b, TN_DIMS, preferred_element_type=F32)


def _bf(x):
    return x.astype(BF16)


def _sigmoid(x):
    return 1.0 / (1.0 + jnp.exp(-x))


def _silu(x):
    return x * _sigmoid(x)


def _normmod(x, g, sc, sh):
    return _normscale(x, g * (1.0 + sc), sh)


def _normscale(x, gain, shift):
    ms = jnp.mean(x * x, axis=-1, keepdims=True)
    return (x * lax.rsqrt(ms + EPS)) * gain + shift


ROW_PIECE = 16


def _for_row_pieces(rows, fn):
    def body(p, carry):
        fn(pl.ds(pl.multiple_of(p * ROW_PIECE, ROW_PIECE), ROW_PIECE))
        return carry

    lax.fori_loop(0, rows // ROW_PIECE, body, 0, unroll=8)


def _normmod_rows(x_ref, g_ref, sc_ref, sh_ref, u_ref, row0=0):
    gain, shift = g_ref[...] * (1.0 + sc_ref[0]), sh_ref[0]

    def piece(rs):
        dst = rs if row0 == 0 else pl.ds(rs.start + row0, ROW_PIECE)
        u_ref[dst] = _normscale(x_ref[0, rs], gain, shift).astype(u_ref.dtype)

    _for_row_pieces(x_ref.shape[1], piece)


def _cumsum_rows(x, rev):
    n = x.shape[0]
    row = lax.broadcasted_iota(jnp.int32, (n, 1), 0)
    s = 1
    while s < n:
        if rev:
            x = x + jnp.where(row < n - s, pltpu.roll(x, n - s, 0), 0.0)
        else:
            x = x + jnp.where(row >= s, pltpu.roll(x, s, 0), 0.0)
        s *= 2
    return x


def _params(sem):
    return pltpu.CompilerParams(dimension_semantics=sem, vmem_limit_bytes=VMEM_LIMIT)


def _mod_kernel(c_ref, w_ref, b_ref, o_ref):
    s = _bf(_silu(c_ref[...]))
    o_ref[0] = _dot(s, _bf(w_ref[0])) + b_ref[0]


def _modulation(cond8, mod_w, mod_b):
    depth, d, n = mod_w.shape
    tn = 1024
    return pl.pallas_call(
        _mod_kernel,
        grid=(depth, n // tn),
        in_specs=[
            pl.BlockSpec((8, d), lambda l, j: (0, 0)),
            pl.BlockSpec((1, d, tn), lambda l, j: (l, 0, j)),
            pl.BlockSpec((1, 1, tn), lambda l, j: (l, 0, j)),
        ],
        out_specs=pl.BlockSpec((1, 8, tn), lambda l, j: (l, 0, j)),
        out_shape=jax.ShapeDtypeStruct((depth, 8, n), F32),
        compiler_params=_params(("arbitrary", "arbitrary")),
        name="modulation",
    )(cond8, mod_w, mod_b.reshape(depth, 1, n))


def _inproj_kernel(*refs, n_rope):
    if n_rope:
        x_ref, g_ref, sc_ref, sh_ref, w_ref, cos_ref, sna_ref, snb_ref, o_ref, u_ref = refs
    else:
        x_ref, g_ref, sc_ref, sh_ref, w_ref, o_ref, u_ref = refs
    j = pl.program_id(2)

    @pl.when(j == 0)
    def _():
        _normmod_rows(x_ref, g_ref, sc_ref, sh_ref, u_ref)

    acc = _dot(u_ref[...], w_ref[...])
    if not n_rope:
        o_ref[0] = acc
        return

    per_tile = acc.shape[1] // HEAD_DIM
    n_full, n_rem = n_rope // per_tile, n_rope % per_tile

    def store(n_rot):
        cos, sna, snb = cos_ref[...], sna_ref[...], snb_ref[...]
        for hd in range(n_rot):
            sl = acc[:, hd * HEAD_DIM:(hd + 1) * HEAD_DIM]
            rot = sl * cos + pltpu.roll(sl, 96, 1) * sna + pltpu.roll(sl, 32, 1) * snb
            o_ref[0, :, hd * HEAD_DIM:(hd + 1) * HEAD_DIM] = rot
        if n_rot < per_tile:
            o_ref[0, :, n_rot * HEAD_DIM:] = acc[:, n_rot * HEAD_DIM:]

    pl.when(j < n_full)(lambda: store(per_tile))
    pl.when(j == n_full)(lambda: store(n_rem))
    pl.when(j > n_full)(lambda: store(0))


def _inproj(x, g, sc, sh, w, rope, tm, tn):
    bsz, seq, d = x.shape
    n = w.shape[1]
    n_rope = 0
    in_specs = [
        pl.BlockSpec((1, tm, d), lambda b, i, j: (b, i, 0)),
        pl.BlockSpec((1, d), lambda b, i, j: (0, 0)),
        pl.BlockSpec((1, 1, d), lambda b, i, j: (b, 0, 0)),
        pl.BlockSpec((1, 1, d), lambda b, i, j: (b, 0, 0)),
        pl.BlockSpec((d, tn), lambda b, i, j: (0, j)),
    ]
    args = [x, g, sc, sh, w]
    if rope is not None:
        n_rope = A_Q_HEADS + A_KV_HEADS
        assert tn % HEAD_DIM == 0
        in_specs += [pl.BlockSpec((tm, HEAD_DIM), lambda b, i, j: (i, 0))] * 3
        args += list(rope)
    return pl.pallas_call(
        functools.partial(_inproj_kernel, n_rope=n_rope),
        grid=(bsz, seq // tm, n // tn),
        in_specs=in_specs,
        out_specs=pl.BlockSpec((1, tm, tn), lambda b, i, j: (b, i, j)),
        out_shape=jax.ShapeDtypeStruct((bsz, seq, n), F32),
        scratch_shapes=[pltpu.VMEM((tm, d), BF16)],
        compiler_params=_params(("arbitrary", "arbitrary", "arbitrary")),
        name="ab_inproj",
    )(*args)


def _inproj_stream_kernel(x_ref, g_ref, sc_ref, sh_ref, w_ref, cos_ref, sna_ref, snb_ref, o_ref,
                          lhs0_ref, lhs1_ref, *, n_rope):
    j = pl.program_id(0)
    i = pl.program_id(2)
    tm = x_ref.shape[1]
    rb = 32

    @pl.when((j == 0) & (pl.program_id(1) == 0) & (i == 0))
    def _():
        lhs1_ref[...] = jnp.zeros_like(lhs1_ref)

    def step(build_ref, ready_ref):
        gain, shift = g_ref[...] * (1.0 + sc_ref[0]), sh_ref[0]
        for r0 in range(0, tm, rb):
            build_ref[r0:r0 + rb] = _bf(_normscale(x_ref[0, r0:r0 + rb], gain, shift))
        o_ref[0] = _dot(ready_ref[...], w_ref[...])

    pl.when(i % 2 == 0)(lambda: step(lhs0_ref, lhs1_ref))
    pl.when(i % 2 == 1)(lambda: step(lhs1_ref, lhs0_ref))

    @pl.when(j == 0)
    def _():
        cos, sna, snb = cos_ref[...], sna_ref[...], snb_ref[...]
        for hd in range(n_rope):
            sl = o_ref[0, :, hd * HEAD_DIM:(hd + 1) * HEAD_DIM]
            rot = sl * cos + pltpu.roll(sl, 96, 1) * sna + pltpu.roll(sl, 32, 1) * snb
            o_ref[0, :, hd * HEAD_DIM:(hd + 1) * HEAD_DIM] = rot


def _inproj_stream(x, g, sc, sh, w, rope, tm, tn):
    bsz, seq, d = x.shape
    n = w.shape[1]
    ni = seq // tm
    n_rope = A_Q_HEADS + A_KV_HEADS
    assert n_rope * HEAD_DIM <= tn and n % tn == 0
    cur = lambda j, b, i: (b, jnp.minimum(i, ni - 1), 0)
    done = lambda i: jnp.maximum(i - 1, 0)
    table = pl.BlockSpec((tm, HEAD_DIM), lambda j, b, i: (done(i), 0))
    return pl.pallas_call(
        functools.partial(_inproj_stream_kernel, n_rope=n_rope),
        grid=(n // tn, bsz, ni + 1),
        in_specs=[
            pl.BlockSpec((1, tm, d), cur),
            pl.BlockSpec((1, d), lambda j, b, i: (0, 0)),
            pl.BlockSpec((1, 1, d), lambda j, b, i: (b, 0, 0)),
            pl.BlockSpec((1, 1, d), lambda j, b, i: (b, 0, 0)),
            pl.BlockSpec((d, tn), lambda j, b, i: (0, j)),
            table, table, table,
        ],
        out_specs=pl.BlockSpec((1, tm, tn), lambda j, b, i: (b, done(i), j)),
        out_shape=jax.ShapeDtypeStruct((bsz, seq, n), F32),
        scratch_shapes=[pltpu.VMEM((tm, d), BF16), pltpu.VMEM((tm, d), BF16)],
        compiler_params=_params(("arbitrary", "arbitrary", "arbitrary")),
        name="ab_inproj_stream",
    )(x, g, sc, sh, w, *rope)


def _softmax_av(s_list, v_list, sink_col):
    m = sink_col
    for s in s_list:
        m = jnp.maximum(m, jnp.max(s, axis=-1, keepdims=True))
    den = jnp.exp2(sink_col - m)
    out = None
    for s, v in zip(s_list, v_list):
        p = jnp.exp2(s - m)
        den = den + jnp.sum(p, axis=-1, keepdims=True)
        o = _dot(_bf(p), v)
        out = o if out is None else out + o
    return out / den


def _sink_column(sink_ref, hk, rows):
    rowh = lax.broadcasted_iota(jnp.int32, (rows, 1), 0) // WINDOW
    col = jnp.full((rows, 1), sink_ref[hk * A_GROUP + A_GROUP - 1], F32)
    for g in range(A_GROUP - 1):
        col = jnp.where(rowh == g, sink_ref[hk * A_GROUP + g], col)
    return col * LOG2E


def _win_attn_kernel(sink_ref, q_ref, kp_ref, kc_ref, kn_ref, vp_ref, vc_ref, vn_ref,
                     ck_ref, cv_ref, o_ref):
    n = pl.program_id(1)
    nb = pl.num_programs(1)
    scale = HEAD_DIM ** -0.5 * LOG2E
    q = q_ref[0]
    kband = jnp.concatenate([kp_ref[0], kc_ref[0], kn_ref[0]], axis=0)
    vband = jnp.concatenate([vp_ref[0], vc_ref[0], vn_ref[0]], axis=0)
    rows = A_GROUP * WINDOW
    tq = lax.broadcasted_iota(jnp.int32, (rows, 3 * WINDOW), 0) % WINDOW
    tk = lax.broadcasted_iota(jnp.int32, (rows, 3 * WINDOW), 1)
    rel = tk - WINDOW - tq
    valid = (jnp.abs(rel) <= WINDOW) & ((tk >= WINDOW) | (n > 0)) & ((tk < 2 * WINDOW) | (n < nb - 1))
    for hk in range(A_KV_HEADS):
        qs = jnp.concatenate(
            [q[:, (hk * A_GROUP + g) * HEAD_DIM:(hk * A_GROUP + g + 1) * HEAD_DIM] for g in range(A_GROUP)],
            axis=0)
        qs = _bf(qs)
        hs = slice(hk * HEAD_DIM, (hk + 1) * HEAD_DIM)
        s_win = _dot_nt(qs, _bf(kband[:, hs])) * scale
        s_win = jnp.where(valid, s_win, -jnp.inf)
        s_ctx = _dot_nt(qs, _bf(ck_ref[0][:, hs])) * scale
        o = _softmax_av([s_win, s_ctx], [_bf(vband[:, hs]), _bf(cv_ref[0][:, hs])],
                        _sink_column(sink_ref, hk, rows))
        for g in range(A_GROUP):
            h = hk * A_GROUP + g
            o_ref[0, :, h * HEAD_DIM:(h + 1) * HEAD_DIM] = _bf(o[g * WINDOW:(g + 1) * WINDOW])


def _win_attn(sink, proj, proj_ctx):
    bsz, seq, _ = proj.shape
    lc = proj_ctx.shape[1]
    nb = seq // WINDOW
    kcol, vcol = A_Q // A_KV, A_Q // A_KV + 1
    prev = lambda b, n: (b, jnp.maximum(n - 1, 0))
    nxt = lambda b, n: (b, jnp.minimum(n + 1, nb - 1))
    cur = lambda b, n: (b, n)

    def band(rowfn, col):
        return pl.BlockSpec((1, WINDOW, A_KV), lambda b, n: rowfn(b, n) + (col,))

    return pl.pallas_call(
        _win_attn_kernel,
        grid=(bsz, nb),
        in_specs=[
            pl.BlockSpec(memory_space=pltpu.SMEM),
            pl.BlockSpec((1, WINDOW, A_Q), lambda b, n: (b, n, 0)),
            band(prev, kcol), band(cur, kcol), band(nxt, kcol),
            band(prev, vcol), band(cur, vcol), band(nxt, vcol),
            pl.BlockSpec((1, lc, A_KV), lambda b, n: (b, 0, kcol)),
            pl.BlockSpec((1, lc, A_KV), lambda b, n: (b, 0, vcol)),
        ],
        out_specs=pl.BlockSpec((1, WINDOW, A_Q), lambda b, n: (b, n, 0)),
        out_shape=jax.ShapeDtypeStruct((bsz, seq, A_Q), BF16),
        compiler_params=_params(("arbitrary", "arbitrary")),
        name="window_attention",
    )(sink, proj, proj, proj, proj, proj, proj, proj, proj_ctx, proj_ctx)


def _ctx_attn_kernel(sink_ref, q_ref, k_ref, v_ref, o_ref):
    scale = HEAD_DIM ** -0.5 * LOG2E
    q = q_ref[0]
    lc = q.shape[0]
    for h in range(A_Q_HEADS):
        hk = h // A_GROUP
        hs = slice(hk * HEAD_DIM, (hk + 1) * HEAD_DIM)
        qs = _bf(q[:, h * HEAD_DIM:(h + 1) * HEAD_DIM])
        s = _dot_nt(qs, _bf(k_ref[0][:, hs])) * scale
        sink_col = jnp.full((lc, 1), sink_ref[h], F32) * LOG2E
        o = _softmax_av([s], [_bf(v_ref[0][:, hs])], sink_col)
        o_ref[0, :, h * HEAD_DIM:(h + 1) * HEAD_DIM] = _bf(o)


def _ctx_attn(sink, proj_ctx):
    bsz, lc, _ = proj_ctx.shape
    kcol, vcol = A_Q // A_KV, A_Q // A_KV + 1
    return pl.pallas_call(
        _ctx_attn_kernel,
        grid=(bsz,),
        in_specs=[
            pl.BlockSpec(memory_space=pltpu.SMEM),
            pl.BlockSpec((1, lc, A_Q), lambda b: (b, 0, 0)),
            pl.BlockSpec((1, lc, A_KV), lambda b: (b, 0, kcol)),
            pl.BlockSpec((1, lc, A_KV), lambda b: (b, 0, vcol)),
        ],
        out_specs=pl.BlockSpec((1, lc, A_Q), lambda b: (b, 0, 0)),
        out_shape=jax.ShapeDtypeStruct((bsz, lc, A_Q), BF16),
        compiler_params=_params(("arbitrary",)),
        name="context_attention",
    )(sink, proj_ctx, proj_ctx, proj_ctx)


def _hgrn_chunk(bq, bi, bf, lb, st, rev):
    heads = range(len(bq))
    q = [_silu(x) for x in bq]
    v = bi
    f = [lb[h] + (1.0 - lb[h]) * _sigmoid(bf[h]) for h in heads]
    k = [1.0 - x for x in f]
    g = [jnp.log(x) * LOG2E for x in f]
    b = [_cumsum_rows(x, rev) for x in g]
    btot = [x[0:1] if rev else x[CHUNK - 1:CHUNK] for x in b]
    vb = [_bf(x) for x in v]
    stb = [_bf(x) for x in st]
    o = [_dot_nt(_bf(q[h] * jnp.exp2(b[h])), stb[h]) for h in heads]
    khat = [_bf(k[h] * jnp.exp2(btot[h] - b[h])) for h in heads]
    dst = [_dot_tn(vb[h], khat[h]) for h in heads]
    st_new = [st[h] * jnp.exp2(btot[h]) + dst[h] for h in heads]
    nsub = CHUNK // SUB
    row8 = lax.broadcasted_iota(jnp.int32, (8, 1), 0)
    outs = [[] for _ in heads]
    for blk in range(nsub):
        r0 = blk * SUB
        rs = slice(r0, r0 + SUB)
        acc = [o[h][rs] for h in heads]
        if rev and blk < nsub - 1:
            ref_row, lo, hi = r0 + SUB, r0 + SUB, CHUNK
        elif (not rev) and blk > 0:
            ref_row, lo, hi = r0 - 1, 0, r0
        else:
            ref_row = None
        if ref_row is not None:
            bref = [b[h][ref_row:ref_row + 1] for h in heads]
            qn = [_bf(q[h][rs] * jnp.exp2(b[h][rs] - bref[h])) for h in heads]
            kn = [_bf(k[h][lo:hi] * jnp.exp2(bref[h] - b[h][lo:hi])) for h in heads]
            att = [_bf(_dot_nt(qn[h], kn[h])) for h in heads]
            acc = [acc[h] + _dot(att[h], vb[h][lo:hi]) for h in heads]
        piece = 8
        acc = [[a[p0:p0 + piece] for p0 in range(0, SUB, piece)] for a in acc]
        for s in range(SUB):
            reached = range(0, s // piece + 1) if rev else range(s // piece, SUB // piece)
            for pc in reached:
                t0 = r0 + pc * piece
                mask = (row8 + pc * piece <= s) if rev else (row8 + pc * piece >= s)
                for h in heads:
                    rel = b[h][t0:t0 + piece] - b[h][r0 + s:r0 + s + 1]
                    dec = jnp.exp2(jnp.where(mask, rel, -jnp.inf))
                    w = jnp.sum(q[h][t0:t0 + piece] * dec * k[h][r0 + s:r0 + s + 1], axis=-1, keepdims=True)
                    acc[h][pc] = acc[h][pc] + w * v[h][r0 + s:r0 + s + 1]
        for h in heads:
            outs[h].extend(acc[h])
    return [jnp.concatenate(x, axis=0) for x in outs], st_new


def _hgrn_kernel(q_ref, i_ref, f_ref, lb_ref, s0_ref, o_ref, sfin_ref, st_ref, *, rev, nchunk, nhead):
    c = pl.program_id(2)

    @pl.when(c == 0)
    def _():
        st_ref[...] = s0_ref[0]

    lanes = [slice(h * HEAD_DIM, (h + 1) * HEAD_DIM) for h in range(nhead)]
    lb = [lb_ref[:, ln] for ln in lanes]

    def body(ci, carry):
        cc = (nchunk - 1 - ci) if rev else ci
        rows = pl.ds(pl.multiple_of(cc * CHUNK, CHUNK), CHUNK)
        o, st_new = _hgrn_chunk([q_ref[0, rows, ln] for ln in lanes], [i_ref[0, rows, ln] for ln in lanes],
                                [f_ref[0, rows, ln] for ln in lanes], lb,
                                [st_ref[h] for h in range(nhead)], rev)
        for h in range(nhead):
            o_ref[0, rows, lanes[h]] = o[h]
            st_ref[h] = st_new[h]
        return carry

    lax.fori_loop(0, nchunk, body, 0, unroll=2 if nchunk % 2 == 0 else 1)

    @pl.when(c == pl.num_programs(2) - 1)
    def _():
        sfin_ref[0] = st_ref[...]


def _hgrn_scan(proj, lb, s0, rev, tblk, nhead):
    bsz, seq, _ = proj.shape
    nblk = seq // tblk
    width = nhead * HEAD_DIM
    base = A_Q + 2 * A_KV
    assert base % width == 0 and B_W % width == 0
    qc, ic = base // width, (base + B_W) // width
    fc = (base + (3 if rev else 2) * B_W) // width
    blk = (lambda c: nblk - 1 - c) if rev else (lambda c: c)

    def col(c0):
        return pl.BlockSpec((1, tblk, width), lambda b, h, c: (b, blk(c), c0 + h))

    state = pl.BlockSpec((1, nhead, HEAD_DIM, HEAD_DIM), lambda b, h, c: (b, h, 0, 0))
    return pl.pallas_call(
        functools.partial(_hgrn_kernel, rev=rev, nchunk=tblk // CHUNK, nhead=nhead),
        grid=(bsz, B_HEADS // nhead, nblk),
        in_specs=[
            col(qc), col(ic), col(fc),
            pl.BlockSpec((1, width), lambda b, h, c: (0, h)),
            state,
        ],
        out_specs=[
            pl.BlockSpec((1, tblk, width), lambda b, h, c: (b, blk(c), h)),
            state,
        ],
        out_shape=[
            jax.ShapeDtypeStruct((bsz, seq, B_W), F32),
            jax.ShapeDtypeStruct((bsz, B_HEADS, HEAD_DIM, HEAD_DIM), F32),
        ],
        scratch_shapes=[pltpu.VMEM((nhead, HEAD_DIM, HEAD_DIM), F32)],
        compiler_params=_params(("arbitrary", "arbitrary", "arbitrary")),
        name="hgrn2_bwd" if rev else "hgrn2_fwd",
    )(proj, proj, proj, lb, s0)


def _ab_out_kernel(oa_ref, of_ref, ob_ref, g0_ref, g1_ref, on_ref, w_ref, h_ref, gt_ref, o_ref, lhs_ref):
    lhs_ref[:, :A_Q] = oa_ref[0]
    half = B_W // 2
    for hd in range(B_HEADS):
        sl = slice(hd * HEAD_DIM, (hd + 1) * HEAD_DIM)
        o = of_ref[0, :, sl] + ob_ref[0, :, sl]
        o = o * lax.rsqrt(jnp.mean(o * o, axis=-1, keepdims=True) + EPS)
        o = o * on_ref[:, sl]
        gref = g0_ref if hd * HEAD_DIM < half else g1_ref
        gs = slice(hd * HEAD_DIM % half, hd * HEAD_DIM % half + HEAD_DIM)
        o = o * _silu(gref[0, :, gs])
        lhs_ref[:, A_Q + hd * HEAD_DIM:A_Q + (hd + 1) * HEAD_DIM] = _bf(o)
    y = _dot(lhs_ref[...], w_ref[...])
    o_ref[0] = h_ref[0] + gt_ref[0] * y


def _ab_out(oa, of, ob, proj, onorm, w_out, h, gt, tm):
    bsz, seq, d = h.shape
    half = B_W // 2
    gcol = (A_Q + 2 * A_KV + 4 * B_W) // half
    row = lambda b, i: (b, i, 0)
    return pl.pallas_call(
        _ab_out_kernel,
        grid=(bsz, seq // tm),
        in_specs=[
            pl.BlockSpec((1, tm, A_Q), row),
            pl.BlockSpec((1, tm, B_W), row),
            pl.BlockSpec((1, tm, B_W), row),
            pl.BlockSpec((1, tm, half), lambda b, i: (b, i, gcol)),
            pl.BlockSpec((1, tm, half), lambda b, i: (b, i, gcol + 1)),
            pl.BlockSpec((1, B_W), lambda b, i: (0, 0)),
            pl.BlockSpec((A_Q + B_W, d), lambda b, i: (0, 0)),
            pl.BlockSpec((1, tm, d), row),
            pl.BlockSpec((1, 1, d), lambda b, i: (b, 0, 0)),
        ],
        out_specs=pl.BlockSpec((1, tm, d), row),
        out_shape=jax.ShapeDtypeStruct((bsz, seq, d), F32),
        scratch_shapes=[pltpu.VMEM((tm, A_Q + B_W), BF16)],
        compiler_params=_params(("arbitrary", "arbitrary")),
        name="ab_outproj",
    )(oa, of, ob, proj, proj, onorm, w_out, h, gt)


def _mlp_kernel(*refs, final):
    if final:
        x_ref, g_ref, sc_ref, sh_ref, gt_ref, wu_ref, wd_ref, fn_ref, o_ref, u_ref = refs
    else:
        x_ref, g_ref, sc_ref, sh_ref, gt_ref, wu_ref, wd_ref, o_ref, u_ref = refs
    j = pl.program_id(2)

    tm = x_ref.shape[1]

    @pl.when(j == 0)
    def _():
        _normmod_rows(x_ref, g_ref, sc_ref, sh_ref, u_ref)

    hid = jnp.maximum(_dot(u_ref[...], wu_ref[...]), 0.0)
    hid = _bf(hid * hid)
    ncol = 512

    @pl.when(j == 0)
    def _():
        for n0 in range(0, o_ref.shape[2], ncol):
            o_ref[0, :, n0:n0 + ncol] = _dot(hid, wd_ref[:, n0:n0 + ncol])

    @pl.when(j != 0)
    def _():
        for n0 in range(0, o_ref.shape[2], ncol):
            o_ref[0, :, n0:n0 + ncol] += _dot(hid, wd_ref[:, n0:n0 + ncol])

    @pl.when(j == pl.num_programs(2) - 1)
    def _():
        piece = 128
        for r0 in range(0, tm, piece):
            rs = slice(r0, r0 + piece)
            y = x_ref[0, rs] + gt_ref[0] * o_ref[0, rs]
            if final:
                y = (y * lax.rsqrt(jnp.mean(y * y, axis=-1, keepdims=True) + EPS)) * fn_ref[...]
            o_ref[0, rs] = y


def _mlp(x, g, sc, sh, gt, w_up, w_down, final_gain, tm, tf):
    bsz, seq, d = x.shape
    dff = w_up.shape[1]
    vec = pl.BlockSpec((1, 1, d), lambda b, i, j: (b, 0, 0))
    in_specs = [
        pl.BlockSpec((1, tm, d), lambda b, i, j: (b, i, 0)),
        pl.BlockSpec((1, d), lambda b, i, j: (0, 0)),
        vec, vec, vec,
        pl.BlockSpec((d, tf), lambda b, i, j: (0, j)),
        pl.BlockSpec((tf, d), lambda b, i, j: (j, 0)),
    ]
    args = [x, g, sc, sh, gt, w_up, w_down]
    if final_gain is not None:
        in_specs.append(pl.BlockSpec((1, d), lambda b, i, j: (0, 0)))
        args.append(final_gain)
    return pl.pallas_call(
        functools.partial(_mlp_kernel, final=final_gain is not None),
        grid=(bsz, seq // tm, dff // tf),
        in_specs=in_specs,
        out_specs=pl.BlockSpec((1, tm, d), lambda b, i, j: (b, i, 0)),
        out_shape=jax.ShapeDtypeStruct((bsz, seq, d), F32),
        scratch_shapes=[pltpu.VMEM((tm, d), BF16)],
        compiler_params=_params(("arbitrary", "arbitrary", "arbitrary")),
        name="sqrelu_mlp",
    )(*args)


def _rk_proj_kernel(x_ref, xp_ref, xn_ref, g_ref, sc_ref, sh_ref, mix_ref, wb_ref, ws_ref, ob_ref, os_ref,
                    u_ref, xx_ref, lhs_ref, *, nbig, nsmall):
    i = pl.program_id(1)
    j = pl.program_id(2)
    tm = x_ref.shape[1]
    rb = 32

    def build(m, slot):
        mixrow = mix_ref[pl.ds(m, 1), :]
        for r0 in range(0, tm, rb):
            lhs_ref[slot, r0:r0 + rb] = _bf(u_ref[8 + r0:8 + r0 + rb] + xx_ref[r0:r0 + rb] * mixrow)

    @pl.when(j == 0)
    def _():
        g, sc, sh = g_ref[...], sc_ref[0], sh_ref[0]
        _normmod_rows(x_ref, g_ref, sc_ref, sh_ref, u_ref, row0=8)
        up = _normmod(xp_ref[0], g, sc, sh)[7:8]
        un = _normmod(xn_ref[0], g, sc, sh)[0:1]
        u_ref[7:8] = jnp.where(i == 0, 0.0, up)
        u_ref[8 + tm:9 + tm] = jnp.where(i == pl.num_programs(1) - 1, 0.0, un)
        mix0 = mix_ref[0:1, :]
        for r0 in range(0, tm, ROW_PIECE):
            u = u_ref[8 + r0:8 + r0 + ROW_PIECE]
            xx = 0.5 * (u_ref[7 + r0:7 + r0 + ROW_PIECE] + u_ref[9 + r0:9 + r0 + ROW_PIECE]) - u
            xx_ref[r0:r0 + ROW_PIECE] = xx
            lhs_ref[0, r0:r0 + ROW_PIECE] = _bf(u + xx * mix0)

    @pl.when(j < nbig)
    def _():
        build(j + 1, (j + 1) % 2)
        ob_ref[0, 0] = _dot(lhs_ref[j % 2], wb_ref[0]).astype(ob_ref.dtype)

    @pl.when(j == nbig)
    def _():
        for s in range(nsmall):
            os_ref[s, 0] = _dot(lhs_ref[(nbig + s) % 2], ws_ref[s])
            if s + 1 < nsmall:
                build(nbig + s + 1, (nbig + s + 1) % 2)


def _rk_proj(x, g, sc, sh, mix, w_big, w_small, tm):
    bsz, seq, d = x.shape
    nbig, _, n = w_big.shape
    nsmall, _, ns = w_small.shape
    r8 = tm // 8
    last8 = seq // 8 - 1
    vec = pl.BlockSpec((1, 1, d), lambda b, i, j: (b, 0, 0))
    return pl.pallas_call(
        functools.partial(_rk_proj_kernel, nbig=nbig, nsmall=nsmall),
        grid=(bsz, seq // tm, nbig + 1),
        in_specs=[
            pl.BlockSpec((1, tm, d), lambda b, i, j: (b, i, 0)),
            pl.BlockSpec((1, 8, d), lambda b, i, j: (b, jnp.maximum(i * r8 - 1, 0), 0)),
            pl.BlockSpec((1, 8, d), lambda b, i, j: (b, jnp.minimum((i + 1) * r8, last8), 0)),
            pl.BlockSpec((1, d), lambda b, i, j: (0, 0)),
            vec, vec,
            pl.BlockSpec((nbig + nsmall, d), lambda b, i, j: (0, 0)),
            pl.BlockSpec((1, d, n), lambda b, i, j: (jnp.minimum(j, nbig - 1), 0, 0)),
            pl.BlockSpec((nsmall, d, ns), lambda b, i, j: (0, 0, 0)),
        ],
        out_specs=[
            pl.BlockSpec((1, 1, tm, n), lambda b, i, j: (jnp.minimum(j, nbig - 1), b, i, 0)),
            pl.BlockSpec((nsmall, 1, tm, ns), lambda b, i, j: (0, b, i, 0)),
        ],
        out_shape=[
            jax.ShapeDtypeStruct((nbig, bsz, seq, n), BF16),
            jax.ShapeDtypeStruct((nsmall, bsz, seq, ns), F32),
        ],
        scratch_shapes=[pltpu.VMEM((tm + 16, d), F32), pltpu.VMEM((tm, d), F32), pltpu.VMEM((2, tm, d), BF16)],
        compiler_params=_params(("arbitrary", "arbitrary", "arbitrary")),
        name="rwkv_proj",
    )(x, x, x, g, sc, sh, mix, w_big, w_small)


DECAY_SCALE = float(np.exp(-0.5))


def _pair_consts(rev):
    n = 2 * CHUNK
    ri = lax.broadcasted_iota(jnp.int32, (n, n), 0)
    ci = lax.broadcasted_iota(jnp.int32, (n, n), 1)
    same = (ri // CHUNK) == (ci // CHUNK)
    rt, ct = ri % CHUNK, ci % CHUNK
    strict = same & ((ct > rt) if rev else (ct < rt))
    incl = same & ((ct >= rt) if rev else (ct <= rt))
    eye = jnp.where(ri == ci, 1.0, 0.0).astype(F32)
    lane = lax.broadcasted_iota(jnp.int32, (1, LANES), 1)
    lane_lo = lane < C_HEAD
    m0 = jnp.where(lane_lo, 1.0, 0.0).astype(F32)
    m1 = 1.0 - m0
    blk2 = (ri // 2) == (ci // 2)
    level = {}
    m = 2
    while m < CHUNK:
        if m % 8:
            rsel, csel = ri, ci
        else:
            rc = lax.broadcasted_iota(jnp.int32, (n // 2, n), 0)
            csel = lax.broadcasted_iota(jnp.int32, (n // 2, n), 1)
            rsel = (rc // m) * (2 * m) + (0 if rev else m) + rc % m
        level[m] = ((rsel // (2 * m)) == (csel // (2 * m))) & ((rsel // m) != (csel // m))
        m *= 2
    return dict(strict=strict, incl=incl, eye=eye, m0=m0, m1=m1, lane_lo=lane_lo, blk2=blk2, level=level)


def _head_sums(x, cs):
    s0 = jnp.sum(x * cs["m0"], axis=-1, keepdims=True)
    s1 = jnp.sum(x * cs["m1"], axis=-1, keepdims=True)
    return jnp.where(cs["lane_lo"], s0, s1)


def _unit_tri_inverse(n, cs, rev):
    size = 2 * CHUNK
    t = [cs["eye"] + jnp.where(cs["blk2"], x, 0.0) for x in n]
    nb = [_bf(x) for x in n]
    m = 2
    while m < CHUNK:
        tb = [_bf(x) for x in t]
        sel = cs["level"][m]
        if m % 8:
            q = [_bf(_dot(a, b)) for a, b in zip(tb, nb)]
            q = [_dot(a, b) for a, b in zip(q, tb)]
            t = [a + jnp.where(sel, b, 0.0) for a, b in zip(t, q)]
        else:
            starts = range(0 if rev else m, size, 2 * m)
            pick = lambda x: jnp.concatenate([x[r0:r0 + m] for r0 in starts], axis=0)
            q = [_bf(_dot(_bf(pick(x)), b)) for x, b in zip(t, nb)]
            q = [_dot(a, b) for a, b in zip(q, tb)]
            t = [_add_rows(x, jnp.where(sel, y, 0.0), starts, m) for x, y in zip(t, q)]
        m *= 2
    return t


def _add_rows(x, upd, starts, m):
    pieces, pos = [], 0
    for k, r0 in enumerate(starts):
        if r0 > pos:
            pieces.append(x[pos:r0])
        pieces.append(x[r0:r0 + m] + upd[k * m:(k + 1) * m])
        pos = r0 + m
    if pos < x.shape[0]:
        pieces.append(x[pos:])
    return jnp.concatenate(pieces, axis=0)


def _halves(top, bottom):
    return jnp.concatenate([top[:CHUNK], bottom[CHUNK:]], axis=0)


def _stack2(x, cs):
    return jnp.concatenate([x * cs["m0"], x * cs["m1"]], axis=0)


def _rwkv_chunk(r, k, v, lw, la, w2, a2, w0, a0, kkv, kav, ht, cs, rev):
    pairs = range(len(r))
    tl, lab = _bf(jnp.tanh(lw)), _bf(la)
    z = [w0[p] + _dot(tl, w2[p]) for p in pairs]
    za = [_dot(lab, a2[p]) for p in pairs]
    kkr = [k[p] * kkv[p] for p in pairs]
    ss = [_head_sums(x * x, cs) for x in kkr]
    ld = [-(DECAY_SCALE * LOG2E) * _sigmoid(x) for x in z]
    cl = [_cumsum_rows(x, rev) for x in ld]
    a = [_sigmoid(a0[p] + za[p]) for p in pairs]
    kk = [x / jnp.maximum(jnp.sqrt(s), 1e-12) for x, s in zip(kkr, ss)]
    kd = [k[p] * (1.0 + (a[p] - 1.0) * kav[p]) for p in pairs]
    bb = [x * y for x, y in zip(kk, a)]
    ctot = [x[0:1] if rev else x[CHUNK - 1:CHUNK] for x in cl]
    e_neg = [jnp.exp2(-x) for x in cl]
    e_tail = [jnp.exp2(c - x) for c, x in zip(ctot, cl)]
    at = [_bf(_stack2(-kk[p] * jnp.exp2(cl[p] - ld[p]), cs)) for p in pairs]
    rt = [_bf(_stack2(r[p] * jnp.exp2(cl[p]), cs)) for p in pairs]
    rhs = [_bf(jnp.concatenate([bb[p] * e_neg[p], kd[p] * e_neg[p]], axis=0)) for p in pairs]
    g = [_dot_nt(jnp.concatenate([x, y], axis=0), w) for x, y, w in zip(at, rt, rhs)]
    g1 = [x[:2 * CHUNK] for x in g]
    g2 = [x[2 * CHUNK:] for x in g]
    strict, incl = cs["strict"], cs["incl"]
    g1r = [pltpu.roll(x, C_HEAD, 1) for x in g1]
    n_ab = [jnp.where(strict, _halves(x, y), 0.0) for x, y in zip(g1, g1r)]
    n_ak = [_bf(jnp.where(strict, _halves(y, x), 0.0)) for x, y in zip(g1, g1r)]
    t = _unit_tri_inverse(n_ab, cs, rev)
    vs = [_bf(_stack2(x, cs)) for x in v]
    hkv = [_bf(x.T) for x in ht]
    xx = [_bf(_dot(jnp.concatenate([at[p], n_ak[p]], axis=1), jnp.concatenate([hkv[p], vs[p]], axis=0)))
          for p in pairs]
    u =[_bf(_dot(_bf(a_), b_)) for a_, b_ in zip(t, xx)]
    bk = [_bf(jnp.concatenate([_stack2(bb[p] * e_tail[p], cs), _stack2(kd[p] * e_tail[p], cs)], axis=0))
          for p in pairs]
    uv = [jnp.concatenate([a_, b_], axis=0) for a_, b_ in zip(u, vs)]
    dh = [_dot_tn(a_, b_) for a_, b_ in zip(uv, bk)]
    ht_new = [ht[p] * jnp.exp2(ctot[p]) + dh[p] for p in pairs]
    g2r = [pltpu.roll(x, C_HEAD, 1) for x in g2]
    n_rb = [_bf(jnp.where(incl, _halves(x, y), 0.0)) for x, y in zip(g2, g2r)]
    n_rk = [_bf(jnp.where(incl, _halves(y, x), 0.0)) for x, y in zip(g2, g2r)]
    ysum = [_dot(jnp.concatenate([rt[p], n_rb[p], n_rk[p]], axis=1),
                 jnp.concatenate([hkv[p], u[p], vs[p]], axis=0)) for p in pairs]
    y = [x[:CHUNK] + x[CHUNK:] for x in ysum]
    return y, ht_new, kd


def _rwkv_combine(y, yf, r, k, v, kd, laf, a2f, a0f, kav, rkv, lng, lnb, cs):
    pairs = range(len(y))
    inv_n = 1.0 / C_HEAD
    lafb = _bf(laf)
    zf = [_dot(lafb, a2f[p]) for p in pairs]
    ysum = [a + b for a, b in zip(y, yf)]
    mu = [_head_sums(x, cs) * inv_n for x in ysum]
    dev = [a - b for a, b in zip(ysum, mu)]
    var = [_head_sums(x * x, cs) * inv_n for x in dev]
    a_f = [_sigmoid(a0f[p] + zf[p]) for p in pairs]
    kd_f = [k[p] * (1.0 + (a_f[p] - 1.0) * kav[p]) for p in pairs]
    bsum = [_head_sums(r[p] * (kd_f[p] + kd[p]) * rkv[p], cs) for p in pairs]
    return [dev[p] * lax.rsqrt(var[p] + GN_EPS) * lng[p] + lnb[p] + bsum[p] * v[p] for p in pairs]


def _rwkv_scan_kernel(*refs, rev, nchunk, npair, combine):
    if combine:
        (r_ref, k_ref, v_ref, lw_ref, la_ref, w2_ref, a2_ref, w0_ref, a0_ref, kk_ref, ka_ref, s0_ref,
         yf_ref, laf_ref, a2f_ref, a0f_ref, rk_ref, lng_ref, lnb_ref, o_ref, sfin_ref, ht_ref) = refs
    else:
        (r_ref, k_ref, v_ref, lw_ref, la_ref, w2_ref, a2_ref, w0_ref, a0_ref, kk_ref, ka_ref, s0_ref,
         o_ref, sfin_ref, ht_ref) = refs
    c = pl.program_id(2)

    @pl.when(c == 0)
    def _():
        ht_ref[...] = s0_ref[0]

    cs = _pair_consts(rev)

    def body(ci, carry):
        cc = (nchunk - 1 - ci) if rev else ci
        rows = pl.ds(pl.multiple_of(cc * CHUNK, CHUNK), CHUNK)
        lw = lw_ref[0, 0, rows, :]
        la = la_ref[0, 0, rows, :]
        lanes = [slice(pr * LANES, (pr + 1) * LANES) for pr in range(npair)]
        r = [r_ref[0, 0, rows, ln].astype(F32) for ln in lanes]
        k = [k_ref[0, 0, rows, ln].astype(F32) for ln in lanes]
        v = [v_ref[0, 0, rows, ln].astype(F32) for ln in lanes]
        kav = [ka_ref[:, ln] for ln in lanes]
        y, ht_new, kd = _rwkv_chunk(
            r, k, v, lw, la, [w2_ref[0, :, ln] for ln in lanes], [a2_ref[0, :, ln] for ln in lanes],
            [w0_ref[0, :, ln] for ln in lanes], [a0_ref[0, :, ln] for ln in lanes],
            [kk_ref[:, ln] for ln in lanes], kav, [ht_ref[pr] for pr in range(npair)], cs, rev)
        for pr in range(npair):
            ht_ref[pr] = ht_new[pr]
        if combine:
            y = _rwkv_combine(
                y, [yf_ref[0, rows, ln] for ln in lanes], r, k, v, kd, laf_ref[0, 0, rows, :],
                [a2f_ref[0, :, ln] for ln in lanes], [a0f_ref[0, :, ln] for ln in lanes], kav,
                [rk_ref[:, ln] for ln in lanes], [lng_ref[:, ln] for ln in lanes],
                [lnb_ref[:, ln] for ln in lanes], cs)
        for pr in range(npair):
            o_ref[0, rows, lanes[pr]] = y[pr]
        return carry

    lax.fori_loop(0, nchunk, body, 0, unroll=2 if nchunk % 2 == 0 else 1)

    @pl.when(c == pl.num_programs(2) - 1)
    def _():
        sfin_ref[0] = ht_ref[...]


def _rwkv_scan(rkv, small, prm, s0, rev, tblk, npair, y_fwd=None):
    _, bsz, seq, d = rkv.shape
    nblk = seq // tblk
    width = npair * LANES
    e = 1 if rev else 0
    blk = (lambda c: nblk - 1 - c) if rev else (lambda c: c)
    combine = y_fwd is not None

    def tok(m):
        return pl.BlockSpec((1, 1, tblk, width), lambda b, p, c: (m, b, blk(c), p))

    def lora(m, half):
        return pl.BlockSpec((1, 1, tblk, LANES), lambda b, p, c: (m, b, blk(c), half))

    def mat(idx):
        return pl.BlockSpec((1, LANES, width), lambda b, p, c: (idx, 0, p))

    def vec3(idx):
        return pl.BlockSpec((1, 1, width), lambda b, p, c: (idx, 0, p))

    vec = pl.BlockSpec((1, width), lambda b, p, c: (0, p))
    state = pl.BlockSpec((1, npair, LANES, LANES), lambda b, p, c: (b, p, 0, 0))
    out_tok = pl.BlockSpec((1, tblk, width), lambda b, p, c: (b, blk(c), p))
    in_specs = [tok(0), tok(1), tok(2), lora(1, e), lora(2, e), mat(e), mat(e), vec3(e), vec3(e), vec, vec, state]
    args = [rkv, rkv, rkv, small, small, prm["w2"], prm["a2"], prm["w0"], prm["a0"], prm["k_k"], prm["k_a"], s0]
    if combine:
        in_specs += [out_tok, lora(2, 0), mat(0), vec3(0), vec, vec, vec]
        args += [y_fwd, small, prm["a2"], prm["a0"], prm["r_k"], prm["ln_g"], prm["ln_b"]]
    return pl.pallas_call(
        functools.partial(_rwkv_scan_kernel, rev=rev, nchunk=tblk // CHUNK, npair=npair, combine=combine),
        grid=(bsz, d // width, nblk),
        in_specs=in_specs,
        out_specs=[out_tok, state],
        out_shape=[
            jax.ShapeDtypeStruct((bsz, seq, d), F32),
            jax.ShapeDtypeStruct((bsz, d // LANES, LANES, LANES), F32),
        ],
        scratch_shapes=[pltpu.VMEM((npair, LANES, LANES), F32)],
        compiler_params=_params(("arbitrary", "arbitrary", "arbitrary")),
        name="rwkv7_bwd" if rev else "rwkv7_fwd",
    )(*args)


def _rk_out_kernel(z_ref, gs_ref, g2_ref, wo_ref, h_ref, gt_ref, o_ref):
    gate = _dot(_bf(_sigmoid(gs_ref[0, 0])), g2_ref[...])
    y = _dot(_bf(z_ref[0] * gate), wo_ref[...])
    o_ref[0] = h_ref[0] + gt_ref[0] * y


def _rk_out(z, small, g2, wo, h, gt, tm):
    bsz, seq, d = h.shape
    row = lambda b, i: (b, i, 0)
    glora = g2.shape[0]
    return pl.pallas_call(
        _rk_out_kernel,
        grid=(bsz, seq // tm),
        in_specs=[
            pl.BlockSpec((1, tm, d), row),
            pl.BlockSpec((1, 1, tm, glora), lambda b, i: (0, b, i, 0)),
            pl.BlockSpec((glora, d), lambda b, i: (0, 0)),
            pl.BlockSpec((d, d), lambda b, i: (0, 0)),
            pl.BlockSpec((1, tm, d), row),
            pl.BlockSpec((1, 1, d), lambda b, i: (b, 0, 0)),
        ],
        out_specs=pl.BlockSpec((1, tm, d), row),
        out_shape=jax.ShapeDtypeStruct((bsz, seq, d), F32),
        compiler_params=_params(("arbitrary", "arbitrary")),
        name="rwkv_outproj",
    )(z, small, g2, wo, h, gt)


def _rope_tables(seq):
    t = jnp.arange(seq, dtype=jnp.int32)
    rows = (t // GRID_W).astype(F32)
    cols = (t % GRID_W).astype(F32)
    half = HEAD_DIM // 2
    n_freq = half // 2
    inv = ROPE_BASE ** (-jnp.arange(n_freq, dtype=F32) / n_freq)
    lane = jnp.arange(HEAD_DIM)
    pos = jnp.where((lane < half)[None, :], rows[:, None], cols[:, None])
    ang = pos * inv[lane % n_freq][None, :]
    cos, sin = jnp.cos(ang), jnp.sin(ang)
    first = ((lane % half) < n_freq)[None, :]
    return cos, jnp.where(first, -sin, 0.0), jnp.where(first, 0.0, sin)


def _pad_lanes(w, axis):
    pad = [(0, 0)] * w.ndim
    pad[axis] = (0, LANES - w.shape[axis])
    return jnp.pad(w, pad)


class _Tiles(NamedTuple):
    rows: int
    mlp_rows: int
    ff: int
    in_cols: int
    hgrn_tokens: int
    rwkv_tokens: int


def _tiles(seq, n_in, latent):
    if latent:
        return _Tiles(rows=512, mlp_rows=1024, ff=512, in_cols=n_in // 2, hgrn_tokens=512, rwkv_tokens=256)
    return _Tiles(rows=seq, mlp_rows=seq, ff=512, in_cols=512, hgrn_tokens=seq, rwkv_tokens=seq)


HGRN_HEADS_PER_STEP = 4
RWKV_PAIRS_PER_STEP = 16

def kernel(x, c, ctx, c_ctx, mod_w, mod_b, norm_mix, norm_ffn, ffn_up, ffn_down, ab_w_in, ab_w_out, attn_sink, hgrn_lb, hgrn_onorm, rk_mix, rk_wr, rk_wk, rk_wv, rk_wo, rk_w0, rk_w1, rk_w2, rk_a0, rk_a1, rk_a2, rk_g1, rk_g2, rk_kk, rk_ka, rk_rk, rk_ln_g, rk_ln_b, final_norm):
    bsz, seq, d = x.shape
    lc = ctx.shape[1]
    depth = mod_w.shape[0]
    n_in = ab_w_in.shape[2]
    tl, tc = _tiles(seq, n_in, latent=True), _tiles(lc, n_in, latent=False)
    assert bsz + 1 <= 8 and seq % tl.mlp_rows == 0 and lc % CHUNK == 0

    cond8 = jnp.zeros((8, d), F32).at[:bsz].set(c).at[bsz].set(c_ctx)
    mod = _modulation(cond8, mod_w, mod_b)
    lb_all = jnp.cumsum(jax.nn.softmax(hgrn_lb.astype(F32), axis=0), axis=0)
    rope = _rope_tables(seq)

    h, hc = x, ctx
    for layer in range(depth):
        last = layer == depth - 1
        jl = layer // 2
        m_lat = mod[layer, :bsz].reshape(bsz, 1, 6, d)
        m_ctx = jnp.broadcast_to(mod[layer, bsz].reshape(1, 1, 6, d), (bsz, 1, 6, d))
        sh1, sc1, gt1, sh2, sc2, gt2 = (m_lat[:, :, i] for i in range(6))
        csh1, csc1, cgt1, csh2, csc2, cgt2 = (m_ctx[:, :, i] for i in range(6))
        g_mix = norm_mix[layer].reshape(1, d)
        g_ffn = norm_ffn[layer].reshape(1, d)
        if layer % 2 == 0:
            w_in = _bf(ab_w_in[jl])
            w_out = _bf(ab_w_out[jl])
            sink = attn_sink[jl].astype(F32)
            lb = lb_all[jl].reshape(1, B_W)
            onorm = hgrn_onorm[jl].reshape(1, B_W).astype(F32)
            pc = _inproj(hc, g_mix, csc1, csh1, w_in, None, tc.rows, tc.in_cols)
            pl_ = _inproj_stream(h, g_mix, sc1, sh1, w_in, rope, tl.rows, tl.in_cols)
            oa = _win_attn(sink, pl_, pc)
            zeros = jnp.zeros((bsz, B_HEADS, HEAD_DIM, HEAD_DIM), F32)
            nhead = HGRN_HEADS_PER_STEP
            ocf, scf = _hgrn_scan(pc, lb, zeros, False, tc.hgrn_tokens, nhead)
            ocb, scb = _hgrn_scan(pc, lb, zeros, True, tc.hgrn_tokens, nhead)
            olf, _ = _hgrn_scan(pl_, lb, scf, False, tl.hgrn_tokens, nhead)
            olb, _ = _hgrn_scan(pl_, lb, scb, True, tl.hgrn_tokens, nhead)
            h = _ab_out(oa, olf, olb, pl_, onorm, w_out, h, gt1, tl.rows)
            if not last:
                oca = _ctx_attn(sink, pc)
                hc = _ab_out(oca, ocf, ocb, pc, onorm, w_out, hc, cgt1, tc.rows)
        else:
            w_big = _bf(jnp.stack([rk_wr[jl], rk_wk[jl], rk_wv[jl]]))
            w_small = _bf(jnp.stack([
                rk_g1[jl],
                jnp.concatenate([_pad_lanes(rk_w1[jl, 0], 1), _pad_lanes(rk_w1[jl, 1], 1)], axis=1),
                jnp.concatenate([_pad_lanes(rk_a1[jl, 0], 1), _pad_lanes(rk_a1[jl, 1], 1)], axis=1),
            ]))
            assert w_small.shape[-1] == 2 * LANES
            mix = rk_mix[jl]
            prm = dict(
                w2=_bf(_pad_lanes(rk_w2[jl], 1)), a2=_bf(_pad_lanes(rk_a2[jl], 1)),
                w0=rk_w0[jl].reshape(2, 1, d), a0=rk_a0[jl].reshape(2, 1, d),
                k_k=rk_kk[jl].reshape(1, d), k_a=rk_ka[jl].reshape(1, d), r_k=rk_rk[jl].reshape(1, d),
                ln_g=rk_ln_g[jl].reshape(1, d), ln_b=rk_ln_b[jl].reshape(1, d),
            )
            g2 = _bf(rk_g2[jl])
            wo = _bf(rk_wo[jl])
            mix = mix[jnp.array([0, 2, 3, 5, 1, 4])]
            rkv_c, sm_c = _rk_proj(hc, g_mix, csc1, csh1, mix, w_big, w_small, tc.rows)
            rkv_l, sm_l = _rk_proj(h, g_mix, sc1, sh1, mix, w_big, w_small, tl.rows)
            zeros = jnp.zeros((bsz, d // LANES, LANES, LANES), F32)
            npair = RWKV_PAIRS_PER_STEP
            ycf, s_f = _rwkv_scan(rkv_c, sm_c, prm, zeros, False, tc.rwkv_tokens, npair)
            zc, s_b = _rwkv_scan(rkv_c, sm_c, prm, zeros, True, tc.rwkv_tokens, npair, y_fwd=ycf)
            ylf, _ = _rwkv_scan(rkv_l, sm_l, prm, s_f, False, tl.rwkv_tokens, npair)
            zl, _ = _rwkv_scan(rkv_l, sm_l, prm, s_b, True, tl.rwkv_tokens, npair, y_fwd=ylf)
            h = _rk_out(zl, sm_l, g2, wo, h, gt1, tl.rows)
            if not last:
                hc = _rk_out(zc, sm_c, g2, wo, hc, cgt1, tc.rows)
        w_up = _bf(ffn_up[layer])
        w_dn = _bf(ffn_down[layer])
        final_gain = final_norm.reshape(1, d) if last else None
        h = _mlp(h, g_ffn, sc2, sh2, gt2, w_up, w_dn, final_gain, tl.mlp_rows, tl.ff)
        if not last:
            hc = _mlp(hc, g_ffn, csc2, csh2, cgt2, w_up, w_dn, None, tc.mlp_rows, tc.ff)
    return h
```

```python
import functools
from typing import NamedTuple

import jax
import jax.numpy as jnp
import numpy as np
from jax import lax
from jax.experimental import pallas as pl
from jax.experimental.pallas import tpu as pltpu

F32 = jnp.float32
BF16 = jnp.bfloat16

LANES = 128
HEAD_DIM = 128
GRID_W = 64
WINDOW = 128
ROPE_BASE = 10000.0
A_Q_HEADS = 8
A_KV_HEADS = 2
A_GROUP = A_Q_HEADS // A_KV_HEADS
A_Q = A_Q_HEADS * HEAD_DIM
A_KV = A_KV_HEADS * HEAD_DIM
B_HEADS = 8
B_W = B_HEADS * HEAD_DIM
C_HEAD = 64
CHUNK = 64
SUB = 16
EPS = 1e-6
GN_EPS = 64e-5
VMEM_LIMIT = 58 * 1024 * 1024
LOG2E = float(np.log2(np.e))

NT_DIMS = (((1,), (1,)), ((), ()))
TN_DIMS = (((0,), (0,)), ((), ()))


def _dot(a, b, **kw):
    return jnp.dot(a, b, preferred_element_type=F32, **kw)


def _dot_nt(a, b):
    return lax.dot_general(a, b, NT_DIMS, preferred_element_type=F32)


def _dot_tn(a, b):
    return lax.dot_general(a, b, TN_DIMS, preferred_element_type=F32)


def _bf(x):
    return x.astype(BF16)


def _sigmoid(x):
    return 1.0 / (1.0 + jnp.exp(-x))


def _silu(x):
    return x * _sigmoid(x)


def _normmod(x, g, sc, sh):
    return _normscale(x, g * (1.0 + sc), sh)


def _normscale(x, gain, shift):
    ms = jnp.mean(x * x, axis=-1, keepdims=True)
    return (x * lax.rsqrt(ms + EPS)) * gain + shift


ROW_PIECE = 16


def _for_row_pieces(rows, fn):
    def body(p, carry):
        fn(pl.ds(pl.multiple_of(p * ROW_PIECE, ROW_PIECE), ROW_PIECE))
        return carry

    lax.fori_loop(0, rows // ROW_PIECE, body, 0, unroll=8)


def _normmod_rows(x_ref, g_ref, sc_ref, sh_ref, u_ref, row0=0):
    gain, shift = g_ref[...] * (1.0 + sc_ref[0]), sh_ref[0]

    def piece(rs):
        dst = rs if row0 == 0 else pl.ds(rs.start + row0, ROW_PIECE)
        u_ref[dst] = _normscale(x_ref[0, rs], gain, shift).astype(u_ref.dtype)

    _for_row_pieces(x_ref.shape[1], piece)


def _cumsum_rows(x, rev):
    n = x.shape[0]
    row = lax.broadcasted_iota(jnp.int32, (n, 1), 0)
    s = 1
    while s < n:
        if rev:
            x = x + jnp.where(row < n - s, pltpu.roll(x, n - s, 0), 0.0)
        else:
            x = x + jnp.where(row >= s, pltpu.roll(x, s, 0), 0.0)
        s *= 2
    return x


def _params(sem):
    return pltpu.CompilerParams(dimension_semantics=sem, vmem_limit_bytes=VMEM_LIMIT)


def _mod_kernel(c_ref, w_ref, b_ref, o_ref):
    s = _bf(_silu(c_ref[...]))
    o_ref[0] = _dot(s, _bf(w_ref[0])) + b_ref[0]


def _modulation(cond8, mod_w, mod_b):
    depth, d, n = mod_w.shape
    tn = 1024
    return pl.pallas_call(
        _mod_kernel,
        grid=(depth, n // tn),
        in_specs=[
            pl.BlockSpec((8, d), lambda l, j: (0, 0)),
            pl.BlockSpec((1, d, tn), lambda l, j: (l, 0, j)),
            pl.BlockSpec((1, 1, tn), lambda l, j: (l, 0, j)),
        ],
        out_specs=pl.BlockSpec((1, 8, tn), lambda l, j: (l, 0, j)),
        out_shape=jax.ShapeDtypeStruct((depth, 8, n), F32),
        compiler_params=_params(("arbitrary", "arbitrary")),
        name="modulation",
    )(cond8, mod_w, mod_b.reshape(depth, 1, n))


def _inproj_kernel(*refs, n_rope):
    if n_rope:
        x_ref, g_ref, sc_ref, sh_ref, w_ref, cos_ref, sna_ref, snb_ref, o_ref, u_ref = refs
    else:
        x_ref, g_ref, sc_ref, sh_ref, w_ref, o_ref, u_ref = refs
    j = pl.program_id(2)

    @pl.when(j == 0)
    def _():
        _normmod_rows(x_ref, g_ref, sc_ref, sh_ref, u_ref)

    acc = _dot(u_ref[...], w_ref[...])
    if not n_rope:
        o_ref[0] = acc
        return

    per_tile = acc.shape[1] // HEAD_DIM
    n_full, n_rem = n_rope // per_tile, n_rope % per_tile

    def store(n_rot):
        cos, sna, snb = cos_ref[...], sna_ref[...], snb_ref[...]
        for hd in range(n_rot):
            sl = acc[:, hd * HEAD_DIM:(hd + 1) * HEAD_DIM]
            rot = sl * cos + pltpu.roll(sl, 96, 1) * sna + pltpu.roll(sl, 32, 1) * snb
            o_ref[0, :, hd * HEAD_DIM:(hd + 1) * HEAD_DIM] = rot
        if n_rot < per_tile:
            o_ref[0, :, n_rot * HEAD_DIM:] = acc[:, n_rot * HEAD_DIM:]

    pl.when(j < n_full)(lambda: store(per_tile))
    pl.when(j == n_full)(lambda: store(n_rem))
    pl.when(j > n_full)(lambda: store(0))


def _inproj(x, g, sc, sh, w, rope, tm, tn):
    bsz, seq, d = x.shape
    n = w.shape[1]
    n_rope = 0
    in_specs = [
        pl.BlockSpec((1, tm, d), lambda b, i, j: (b, i, 0)),
        pl.BlockSpec((1, d), lambda b, i, j: (0, 0)),
        pl.BlockSpec((1, 1, d), lambda b, i, j: (b, 0, 0)),
        pl.BlockSpec((1, 1, d), lambda b, i, j: (b, 0, 0)),
        pl.BlockSpec((d, tn), lambda b, i, j: (0, j)),
    ]
    args = [x, g, sc, sh, w]
    if rope is not None:
        n_rope = A_Q_HEADS + A_KV_HEADS
        assert tn % HEAD_DIM == 0
        in_specs += [pl.BlockSpec((tm, HEAD_DIM), lambda b, i, j: (i, 0))] * 3
        args += list(rope)
    return pl.pallas_call(
        functools.partial(_inproj_kernel, n_rope=n_rope),
        grid=(bsz, seq // tm, n // tn),
        in_specs=in_specs,
        out_specs=pl.BlockSpec((1, tm, tn), lambda b, i, j: (b, i, j)),
        out_shape=jax.ShapeDtypeStruct((bsz, seq, n), F32),
        scratch_shapes=[pltpu.VMEM((tm, d), BF16)],
        compiler_params=_params(("arbitrary", "arbitrary", "arbitrary")),
        name="ab_inproj",
    )(*args)


def _inproj_stream_kernel(x_ref, g_ref, sc_ref, sh_ref, w_ref, cos_ref, sna_ref, snb_ref, o_ref,
                          lhs0_ref, lhs1_ref, *, n_rope):
    j = pl.program_id(0)
    i = pl.program_id(2)
    tm = x_ref.shape[1]
    rb = 32

    @pl.when((j == 0) & (pl.program_id(1) == 0) & (i == 0))
    def _():
        lhs1_ref[...] = jnp.zeros_like(lhs1_ref)

    def step(build_ref, ready_ref):
        gain, shift = g_ref[...] * (1.0 + sc_ref[0]), sh_ref[0]
        for r0 in range(0, tm, rb):
            build_ref[r0:r0 + rb] = _bf(_normscale(x_ref[0, r0:r0 + rb], gain, shift))
        o_ref[0] = _dot(ready_ref[...], w_ref[...])

    pl.when(i % 2 == 0)(lambda: step(lhs0_ref, lhs1_ref))
    pl.when(i % 2 == 1)(lambda: step(lhs1_ref, lhs0_ref))

    @pl.when(j == 0)
    def _():
        cos, sna, snb = cos_ref[...], sna_ref[...], snb_ref[...]
        for hd in range(n_rope):
            sl = o_ref[0, :, hd * HEAD_DIM:(hd + 1) * HEAD_DIM]
            rot = sl * cos + pltpu.roll(sl, 96, 1) * sna + pltpu.roll(sl, 32, 1) * snb
            o_ref[0, :, hd * HEAD_DIM:(hd + 1) * HEAD_DIM] = rot


def _inproj_stream(x, g, sc, sh, w, rope, tm, tn):
    bsz, seq, d = x.shape
    n = w.shape[1]
    ni = seq // tm
    n_rope = A_Q_HEADS + A_KV_HEADS
    assert n_rope * HEAD_DIM <= tn and n % tn == 0
    cur = lambda j, b, i: (b, jnp.minimum(i, ni - 1), 0)
    done = lambda i: jnp.maximum(i - 1, 0)
    table = pl.BlockSpec((tm, HEAD_DIM), lambda j, b, i: (done(i), 0))
    return pl.pallas_call(
        functools.partial(_inproj_stream_kernel, n_rope=n_rope),
        grid=(n // tn, bsz, ni + 1),
        in_specs=[
            pl.BlockSpec((1, tm, d), cur),
            pl.BlockSpec((1, d), lambda j, b, i: (0, 0)),
            pl.BlockSpec((1, 1, d), lambda j, b, i: (b, 0, 0)),
            pl.BlockSpec((1, 1, d), lambda j, b, i: (b, 0, 0)),
            pl.BlockSpec((d, tn), lambda j, b, i: (0, j)),
            table, table, table,
        ],
        out_specs=pl.BlockSpec((1, tm, tn), lambda j, b, i: (b, done(i), j)),
        out_shape=jax.ShapeDtypeStruct((bsz, seq, n), F32),
        scratch_shapes=[pltpu.VMEM((tm, d), BF16), pltpu.VMEM((tm, d), BF16)],
        compiler_params=_params(("arbitrary", "arbitrary", "arbitrary")),
        name="ab_inproj_stream",
    )(x, g, sc, sh, w, *rope)


def _softmax_av(s_list, v_list, sink_col):
    m = sink_col
    for s in s_list:
        m = jnp.maximum(m, jnp.max(s, axis=-1, keepdims=True))
    den = jnp.exp2(sink_col - m)
    out = None
    for s, v in zip(s_list, v_list):
        p = jnp.exp2(s - m)
        den = den + jnp.sum(p, axis=-1, keepdims=True)
        o = _dot(_bf(p), v)
        out = o if out is None else out + o
    return out / den


def _sink_column(sink_ref, hk, rows):
    rowh = lax.broadcasted_iota(jnp.int32, (rows, 1), 0) // WINDOW
    col = jnp.full((rows, 1), sink_ref[hk * A_GROUP + A_GROUP - 1], F32)
    for g in range(A_GROUP - 1):
        col = jnp.where(rowh == g, sink_ref[hk * A_GROUP + g], col)
    return col * LOG2E


def _win_attn_kernel(sink_ref, q_ref, kp_ref, kc_ref, kn_ref, vp_ref, vc_ref, vn_ref,
                     ck_ref, cv_ref, o_ref):
    n = pl.program_id(1)
    nb = pl.num_programs(1)
    scale = HEAD_DIM ** -0.5 * LOG2E
    q = q_ref[0]
    kband = jnp.concatenate([kp_ref[0], kc_ref[0], kn_ref[0]], axis=0)
    vband = jnp.concatenate([vp_ref[0], vc_ref[0], vn_ref[0]], axis=0)
    rows = A_GROUP * WINDOW
    tq = lax.broadcasted_iota(jnp.int32, (rows, 3 * WINDOW), 0) % WINDOW
    tk = lax.broadcasted_iota(jnp.int32, (rows, 3 * WINDOW), 1)
    rel = tk - WINDOW - tq
    valid = (jnp.abs(rel) <= WINDOW) & ((tk >= WINDOW) | (n > 0)) & ((tk < 2 * WINDOW) | (n < nb - 1))
    for hk in range(A_KV_HEADS):
        qs = jnp.concatenate(
            [q[:, (hk * A_GROUP + g) * HEAD_DIM:(hk * A_GROUP + g + 1) * HEAD_DIM] for g in range(A_GROUP)],
            axis=0)
        qs = _bf(qs)
        hs = slice(hk * HEAD_DIM, (hk + 1) * HEAD_DIM)
        s_win = _dot_nt(qs, _bf(kband[:, hs])) * scale
        s_win = jnp.where(valid, s_win, -jnp.inf)
        s_ctx = _dot_nt(qs, _bf(ck_ref[0][:, hs])) * scale
        o = _softmax_av([s_win, s_ctx], [_bf(vband[:, hs]), _bf(cv_ref[0][:, hs])],
                        _sink_column(sink_ref, hk, rows))
        for g in range(A_GROUP):
            h = hk * A_GROUP + g
            o_ref[0, :, h * HEAD_DIM:(h + 1) * HEAD_DIM] = _bf(o[g * WINDOW:(g + 1) * WINDOW])


def _win_attn(sink, proj, proj_ctx):
    bsz, seq, _ = proj.shape
    lc = proj_ctx.shape[1]
    nb = seq // WINDOW
    kcol, vcol = A_Q // A_KV, A_Q // A_KV + 1
    prev = lambda b, n: (b, jnp.maximum(n - 1, 0))
    nxt = lambda b, n: (b, jnp.minimum(n + 1, nb - 1))
    cur = lambda b, n: (b, n)

    def band(rowfn, col):
        return pl.BlockSpec((1, WINDOW, A_KV), lambda b, n: rowfn(b, n) + (col,))

    return pl.pallas_call(
        _win_attn_kernel,
        grid=(bsz, nb),
        in_specs=[
            pl.BlockSpec(memory_space=pltpu.SMEM),
            pl.BlockSpec((1, WINDOW, A_Q), lambda b, n: (b, n, 0)),
            band(prev, kcol), band(cur, kcol), band(nxt, kcol),
            band(prev, vcol), band(cur, vcol), band(nxt, vcol),
            pl.BlockSpec((1, lc, A_KV), lambda b, n: (b, 0, kcol)),
            pl.BlockSpec((1, lc, A_KV), lambda b, n: (b, 0, vcol)),
        ],
        out_specs=pl.BlockSpec((1, WINDOW, A_Q), lambda b, n: (b, n, 0)),
        out_shape=jax.ShapeDtypeStruct((bsz, seq, A_Q), BF16),
        compiler_params=_params(("arbitrary", "arbitrary")),
        name="window_attention",
    )(sink, proj, proj, proj, proj, proj, proj, proj, proj_ctx, proj_ctx)


def _ctx_attn_kernel(sink_ref, q_ref, k_ref, v_ref, o_ref):
    scale = HEAD_DIM ** -0.5 * LOG2E
    q = q_ref[0]
    lc = q.shape[0]
    for h in range(A_Q_HEADS):
        hk = h // A_GROUP
        hs = slice(hk * HEAD_DIM, (hk + 1) * HEAD_DIM)
        qs = _bf(q[:, h * HEAD_DIM:(h + 1) * HEAD_DIM])
        s = _dot_nt(qs, _bf(k_ref[0][:, hs])) * scale
        sink_col = jnp.full((lc, 1), sink_ref[h], F32) * LOG2E
        o = _softmax_av([s], [_bf(v_ref[0][:, hs])], sink_col)
        o_ref[0, :, h * HEAD_DIM:(h + 1) * HEAD_DIM] = _bf(o)


def _ctx_attn(sink, proj_ctx):
    bsz, lc, _ = proj_ctx.shape
    kcol, vcol = A_Q // A_KV, A_Q // A_KV + 1
    return pl.pallas_call(
        _ctx_attn_kernel,
        grid=(bsz,),
        in_specs=[
            pl.BlockSpec(memory_space=pltpu.SMEM),
            pl.BlockSpec((1, lc, A_Q), lambda b: (b, 0, 0)),
            pl.BlockSpec((1, lc, A_KV), lambda b: (b, 0, kcol)),
            pl.BlockSpec((1, lc, A_KV), lambda b: (b, 0, vcol)),
        ],
        out_specs=pl.BlockSpec((1, lc, A_Q), lambda b: (b, 0, 0)),
        out_shape=jax.ShapeDtypeStruct((bsz, lc, A_Q), BF16),
        compiler_params=_params(("arbitrary",)),
        name="context_attention",
    )(sink, proj_ctx, proj_ctx, proj_ctx)


def _hgrn_chunk(bq, bi, bf, lb, st, rev):
    heads = range(len(bq))
    q = [_silu(x) for x in bq]
    v = bi
    f = [lb[h] + (1.0 - lb[h]) * _sigmoid(bf[h]) for h in heads]
    k = [1.0 - x for x in f]
    g = [jnp.log(x) * LOG2E for x in f]
    b = [_cumsum_rows(x, rev) for x in g]
    btot = [x[0:1] if rev else x[CHUNK - 1:CHUNK] for x in b]
    vb = [_bf(x) for x in v]
    stb = [_bf(x) for x in st]
    o = [_dot_nt(_bf(q[h] * jnp.exp2(b[h])), stb[h]) for h in heads]
    khat = [_bf(k[h] * jnp.exp2(btot[h] - b[h])) for h in heads]
    dst = [_dot_tn(vb[h], khat[h]) for h in heads]
    st_new = [st[h] * jnp.exp2(btot[h]) + dst[h] for h in heads]
    nsub = CHUNK // SUB
    row8 = lax.broadcasted_iota(jnp.int32, (8, 1), 0)
    outs = [[] for _ in heads]
    for blk in range(nsub):
        r0 = blk * SUB
        rs = slice(r0, r0 + SUB)
        acc = [o[h][rs] for h in heads]
        if rev and blk < nsub - 1:
            ref_row, lo, hi = r0 + SUB, r0 + SUB, CHUNK
        elif (not rev) and blk > 0:
            ref_row, lo, hi = r0 - 1, 0, r0
        else:
            ref_row = None
        if ref_row is not None:
            bref = [b[h][ref_row:ref_row + 1] for h in heads]
            qn = [_bf(q[h][rs] * jnp.exp2(b[h][rs] - bref[h])) for h in heads]
            kn = [_bf(k[h][lo:hi] * jnp.exp2(bref[h] - b[h][lo:hi])) for h in heads]
            att = [_bf(_dot_nt(qn[h], kn[h])) for h in heads]
            acc = [acc[h] + _dot(att[h], vb[h][lo:hi]) for h in heads]
        piece = 8
        acc = [[a[p0:p0 + piece] for p0 in range(0, SUB, piece)] for a in acc]
        for s in range(SUB):
            reached = range(0, s // piece + 1) if rev else range(s // piece, SUB // piece)
            for pc in reached:
                t0 = r0 + pc * piece
                mask = (row8 + pc * piece <= s) if rev else (row8 + pc * piece >= s)
                for h in heads:
                    rel = b[h][t0:t0 + piece] - b[h][r0 + s:r0 + s + 1]
                    dec = jnp.exp2(jnp.where(mask, rel, -jnp.inf))
                    w = jnp.sum(q[h][t0:t0 + piece] * dec * k[h][r0 + s:r0 + s + 1], axis=-1, keepdims=True)
                    acc[h][pc] = acc[h][pc] + w * v[h][r0 + s:r0 + s + 1]
        for h in heads:
            outs[h].extend(acc[h])
    return [jnp.concatenate(x, axis=0) for x in outs], st_new


def _hgrn_kernel(q_ref, i_ref, f_ref, lb_ref, s0_ref, o_ref, sfin_ref, st_ref, *, rev, nchunk, nhead):
    c = pl.program_id(2)

    @pl.when(c == 0)
    def _():
        st_ref[...] = s0_ref[0]

    lanes = [slice(h * HEAD_DIM, (h + 1) * HEAD_DIM) for h in range(nhead)]
    lb = [lb_ref[:, ln] for ln in lanes]

    def body(ci, carry):
        cc = (nchunk - 1 - ci) if rev else ci
        rows = pl.ds(pl.multiple_of(cc * CHUNK, CHUNK), CHUNK)
        o, st_new = _hgrn_chunk([q_ref[0, rows, ln] for ln in lanes], [i_ref[0, rows, ln] for ln in lanes],
                                [f_ref[0, rows, ln] for ln in lanes], lb,
                                [st_ref[h] for h in range(nhead)], rev)
        for h in range(nhead):
            o_ref[0, rows, lanes[h]] = o[h]
            st_ref[h] = st_new[h]
        return carry

    lax.fori_loop(0, nchunk, body, 0, unroll=2 if nchunk % 2 == 0 else 1)

    @pl.when(c == pl.num_programs(2) - 1)
    def _():
        sfin_ref[0] = st_ref[...]


def _hgrn_scan(proj, lb, s0, rev, tblk, nhead):
    bsz, seq, _ = proj.shape
    nblk = seq // tblk
    width = nhead * HEAD_DIM
    base = A_Q + 2 * A_KV
    assert base % width == 0 and B_W % width == 0
    qc, ic = base // width, (base + B_W) // width
    fc = (base + (3 if rev else 2) * B_W) // width
    blk = (lambda c: nblk - 1 - c) if rev else (lambda c: c)

    def col(c0):
        return pl.BlockSpec((1, tblk, width), lambda b, h, c: (b, blk(c), c0 + h))

    state = pl.BlockSpec((1, nhead, HEAD_DIM, HEAD_DIM), lambda b, h, c: (b, h, 0, 0))
    return pl.pallas_call(
        functools.partial(_hgrn_kernel, rev=rev, nchunk=tblk // CHUNK, nhead=nhead),
        grid=(bsz, B_HEADS // nhead, nblk),
        in_specs=[
            col(qc), col(ic), col(fc),
            pl.BlockSpec((1, width), lambda b, h, c: (0, h)),
            state,
        ],
        out_specs=[
            pl.BlockSpec((1, tblk, width), lambda b, h, c: (b, blk(c), h)),
            state,
        ],
        out_shape=[
            jax.ShapeDtypeStruct((bsz, seq, B_W), F32),
            jax.ShapeDtypeStruct((bsz, B_HEADS, HEAD_DIM, HEAD_DIM), F32),
        ],
        scratch_shapes=[pltpu.VMEM((nhead, HEAD_DIM, HEAD_DIM), F32)],
        compiler_params=_params(("arbitrary", "arbitrary", "arbitrary")),
        name="hgrn2_bwd" if rev else "hgrn2_fwd",
    )(proj, proj, proj, lb, s0)


def _ab_out_kernel(oa_ref, of_ref, ob_ref, g0_ref, g1_ref, on_ref, w_ref, h_ref, gt_ref, o_ref, lhs_ref):
    lhs_ref[:, :A_Q] = oa_ref[0]
    half = B_W // 2
    for hd in range(B_HEADS):
        sl = slice(hd * HEAD_DIM, (hd + 1) * HEAD_DIM)
        o = of_ref[0, :, sl] + ob_ref[0, :, sl]
        o = o * lax.rsqrt(jnp.mean(o * o, axis=-1, keepdims=True) + EPS)
        o = o * on_ref[:, sl]
        gref = g0_ref if hd * HEAD_DIM < half else g1_ref
        gs = slice(hd * HEAD_DIM % half, hd * HEAD_DIM % half + HEAD_DIM)
        o = o * _silu(gref[0, :, gs])
        lhs_ref[:, A_Q + hd * HEAD_DIM:A_Q + (hd + 1) * HEAD_DIM] = _bf(o)
    y = _dot(lhs_ref[...], w_ref[...])
    o_ref[0] = h_ref[0] + gt_ref[0] * y


def _ab_out(oa, of, ob, proj, onorm, w_out, h, gt, tm):
    bsz, seq, d = h.shape
    half = B_W // 2
    gcol = (A_Q + 2 * A_KV + 4 * B_W) // half
    row = lambda b, i: (b, i, 0)
    return pl.pallas_call(
        _ab_out_kernel,
        grid=(bsz, seq // tm),
        in_specs=[
            pl.BlockSpec((1, tm, A_Q), row),
            pl.BlockSpec((1, tm, B_W), row),
            pl.BlockSpec((1, tm, B_W), row),
            pl.BlockSpec((1, tm, half), lambda b, i: (b, i, gcol)),
            pl.BlockSpec((1, tm, half), lambda b, i: (b, i, gcol + 1)),
            pl.BlockSpec((1, B_W), lambda b, i: (0, 0)),
            pl.BlockSpec((A_Q + B_W, d), lambda b, i: (0, 0)),
            pl.BlockSpec((1, tm, d), row),
            pl.BlockSpec((1, 1, d), lambda b, i: (b, 0, 0)),
        ],
        out_specs=pl.BlockSpec((1, tm, d), row),
        out_shape=jax.ShapeDtypeStruct((bsz, seq, d), F32),
        scratch_shapes=[pltpu.VMEM((tm, A_Q + B_W), BF16)],
        compiler_params=_params(("arbitrary", "arbitrary")),
        name="ab_outproj",
    )(oa, of, ob, proj, proj, onorm, w_out, h, gt)


def _mlp_kernel(*refs, final):
    if final:
        x_ref, g_ref, sc_ref, sh_ref, gt_ref, wu_ref, wd_ref, fn_ref, o_ref, u_ref = refs
    else:
        x_ref, g_ref, sc_ref, sh_ref, gt_ref, wu_ref, wd_ref, o_ref, u_ref = refs
    j = pl.program_id(2)

    tm = x_ref.shape[1]

    @pl.when(j == 0)
    def _():
        _normmod_rows(x_ref, g_ref, sc_ref, sh_ref, u_ref)

    hid = jnp.maximum(_dot(u_ref[...], wu_ref[...]), 0.0)
    hid = _bf(hid * hid)
    ncol = 512

    @pl.when(j == 0)
    def _():
        for n0 in range(0, o_ref.shape[2], ncol):
            o_ref[0, :, n0:n0 + ncol] = _dot(hid, wd_ref[:, n0:n0 + ncol])

    @pl.when(j != 0)
    def _():
        for n0 in range(0, o_ref.shape[2], ncol):
            o_ref[0, :, n0:n0 + ncol] += _dot(hid, wd_ref[:, n0:n0 + ncol])

    @pl.when(j == pl.num_programs(2) - 1)
    def _():
        piece = 128
        for r0 in range(0, tm, piece):
            rs = slice(r0, r0 + piece)
            y = x_ref[0, rs] + gt_ref[0] * o_ref[0, rs]
            if final:
                y = (y * lax.rsqrt(jnp.mean(y * y, axis=-1, keepdims=True) + EPS)) * fn_ref[...]
            o_ref[0, rs] = y


def _mlp(x, g, sc, sh, gt, w_up, w_down, final_gain, tm, tf):
    bsz, seq, d = x.shape
    dff = w_up.shape[1]
    vec = pl.BlockSpec((1, 1, d), lambda b, i, j: (b, 0, 0))
    in_specs = [
        pl.BlockSpec((1, tm, d), lambda b, i, j: (b, i, 0), pipeline_mode=pl.Buffered(1)),
        pl.BlockSpec((1, d), lambda b, i, j: (0, 0)),
        vec, vec, vec,
        pl.BlockSpec((d, tf), lambda b, i, j: (0, j)),
        pl.BlockSpec((tf, d), lambda b, i, j: (j, 0)),
    ]
    args = [x, g, sc, sh, gt, w_up, w_down]
    if final_gain is not None:
        in_specs.append(pl.BlockSpec((1, d), lambda b, i, j: (0, 0)))
        args.append(final_gain)
    return pl.pallas_call(
        functools.partial(_mlp_kernel, final=final_gain is not None),
        grid=(bsz, seq // tm, dff // tf),
        in_specs=in_specs,
        out_specs=pl.BlockSpec((1, tm, d), lambda b, i, j: (b, i, 0)),
        out_shape=jax.ShapeDtypeStruct((bsz, seq, d), F32),
        scratch_shapes=[pltpu.VMEM((tm, d), BF16)],
        compiler_params=_params(("arbitrary", "arbitrary", "arbitrary")),
        name="sqrelu_mlp",
    )(*args)


def _rk_proj_kernel(x_ref, xp_ref, xn_ref, g_ref, sc_ref, sh_ref, mix_ref, wb_ref, ws_ref, ob_ref, os_ref,
                    u_ref, xx_ref, lhs_ref, *, nbig, nsmall):
    i = pl.program_id(1)
    j = pl.program_id(2)
    tm = x_ref.shape[1]
    rb = 32

    def build(m, slot):
        mixrow = mix_ref[pl.ds(m, 1), :]
        for r0 in range(0, tm, rb):
            lhs_ref[slot, r0:r0 + rb] = _bf(u_ref[8 + r0:8 + r0 + rb] + xx_ref[r0:r0 + rb] * mixrow)

    @pl.when(j == 0)
    def _():
        g, sc, sh = g_ref[...], sc_ref[0], sh_ref[0]
        _normmod_rows(x_ref, g_ref, sc_ref, sh_ref, u_ref, row0=8)
        up = _normmod(xp_ref[0], g, sc, sh)[7:8]
        un = _normmod(xn_ref[0], g, sc, sh)[0:1]
        u_ref[7:8] = jnp.where(i == 0, 0.0, up)
        u_ref[8 + tm:9 + tm] = jnp.where(i == pl.num_programs(1) - 1, 0.0, un)
        mix0 = mix_ref[0:1, :]
        for r0 in range(0, tm, ROW_PIECE):
            u = u_ref[8 + r0:8 + r0 + ROW_PIECE]
            xx = 0.5 * (u_ref[7 + r0:7 + r0 + ROW_PIECE] + u_ref[9 + r0:9 + r0 + ROW_PIECE]) - u
            xx_ref[r0:r0 + ROW_PIECE] = xx
            lhs_ref[0, r0:r0 + ROW_PIECE] = _bf(u + xx * mix0)

    @pl.when(j < nbig)
    def _():
        build(j + 1, (j + 1) % 2)
        ob_ref[0, 0] = _dot(lhs_ref[j % 2], wb_ref[0])

    @pl.when(j == nbig)
    def _():
        for s in range(nsmall):
            os_ref[s, 0] = _dot(lhs_ref[(nbig + s) % 2], ws_ref[s])
            if s + 1 < nsmall:
                build(nbig + s + 1, (nbig + s + 1) % 2)


def _rk_proj(x, g, sc, sh, mix, w_big, w_small, tm):
    bsz, seq, d = x.shape
    nbig, _, n = w_big.shape
    nsmall, _, ns = w_small.shape
    r8 = tm // 8
    last8 = seq // 8 - 1
    vec = pl.BlockSpec((1, 1, d), lambda b, i, j: (b, 0, 0))
    return pl.pallas_call(
        functools.partial(_rk_proj_kernel, nbig=nbig, nsmall=nsmall),
        grid=(bsz, seq // tm, nbig + 1),
        in_specs=[
            pl.BlockSpec((1, tm, d), lambda b, i, j: (b, i, 0)),
            pl.BlockSpec((1, 8, d), lambda b, i, j: (b, jnp.maximum(i * r8 - 1, 0), 0)),
            pl.BlockSpec((1, 8, d), lambda b, i, j: (b, jnp.minimum((i + 1) * r8, last8), 0)),
            pl.BlockSpec((1, d), lambda b, i, j: (0, 0)),
            vec, vec,
            pl.BlockSpec((nbig + nsmall, d), lambda b, i, j: (0, 0)),
            pl.BlockSpec((1, d, n), lambda b, i, j: (jnp.minimum(j, nbig - 1), 0, 0)),
            pl.BlockSpec((nsmall, d, ns), lambda b, i, j: (0, 0, 0)),
        ],
        out_specs=[
            pl.BlockSpec((1, 1, tm, n), lambda b, i, j: (jnp.minimum(j, nbig - 1), b, i, 0)),
            pl.BlockSpec((nsmall, 1, tm, ns), lambda b, i, j: (0, b, i, 0)),
        ],
        out_shape=[
            jax.ShapeDtypeStruct((nbig, bsz, seq, n), F32),
            jax.ShapeDtypeStruct((nsmall, bsz, seq, ns), F32),
        ],
        scratch_shapes=[pltpu.VMEM((tm + 16, d), F32), pltpu.VMEM((tm, d), F32), pltpu.VMEM((2, tm, d), BF16)],
        compiler_params=_params(("arbitrary", "arbitrary", "arbitrary")),
        name="rwkv_proj",
    )(x, x, x, g, sc, sh, mix, w_big, w_small)


DECAY_SCALE = float(np.exp(-0.5))


def _pair_consts(rev):
    n = 2 * CHUNK
    ri = lax.broadcasted_iota(jnp.int32, (n, n), 0)
    ci = lax.broadcasted_iota(jnp.int32, (n, n), 1)
    same = (ri // CHUNK) == (ci // CHUNK)
    rt, ct = ri % CHUNK, ci % CHUNK
    strict = same & ((ct > rt) if rev else (ct < rt))
    incl = same & ((ct >= rt) if rev else (ct <= rt))
    eye = jnp.where(ri == ci, 1.0, 0.0).astype(F32)
    lane = lax.broadcasted_iota(jnp.int32, (1, LANES), 1)
    lane_lo = lane < C_HEAD
    m0 = jnp.where(lane_lo, 1.0, 0.0).astype(F32)
    m1 = 1.0 - m0
    blk2 = (ri // 2) == (ci // 2)
    level = {}
    m = 2
    while m < CHUNK:
        if m % 8:
            rsel, csel = ri, ci
        else:
            rc = lax.broadcasted_iota(jnp.int32, (n // 2, n), 0)
            csel = lax.broadcasted_iota(jnp.int32, (n // 2, n), 1)
            rsel = (rc // m) * (2 * m) + (0 if rev else m) + rc % m
        level[m] = ((rsel // (2 * m)) == (csel // (2 * m))) & ((rsel // m) != (csel // m))
        m *= 2
    return dict(strict=strict, incl=incl, eye=eye, m0=m0, m1=m1, lane_lo=lane_lo, blk2=blk2, level=level)


def _head_sums(x, cs):
    s0 = jnp.sum(x * cs["m0"], axis=-1, keepdims=True)
    s1 = jnp.sum(x * cs["m1"], axis=-1, keepdims=True)
    return jnp.where(cs["lane_lo"], s0, s1)


def _unit_tri_inverse(n, cs, rev):
    size = 2 * CHUNK
    t = [cs["eye"] + jnp.where(cs["blk2"], x, 0.0) for x in n]
    nb = [_bf(x) for x in n]
    m = 2
    while m < CHUNK:
        tb = [_bf(x) for x in t]
        sel = cs["level"][m]
        if m % 8:
            q = [_bf(_dot(a, b)) for a, b in zip(tb, nb)]
            q = [_dot(a, b) for a, b in zip(q, tb)]
            t = [a + jnp.where(sel, b, 0.0) for a, b in zip(t, q)]
        else:
            starts = range(0 if rev else m, size, 2 * m)
            pick = lambda x: jnp.concatenate([x[r0:r0 + m] for r0 in starts], axis=0)
            q = [_bf(_dot(_bf(pick(x)), b)) for x, b in zip(t, nb)]
            q = [_dot(a, b) for a, b in zip(q, tb)]
            t = [_add_rows(x, jnp.where(sel, y, 0.0), starts, m) for x, y in zip(t, q)]
        m *= 2
    return t


def _add_rows(x, upd, starts, m):
    pieces, pos = [], 0
    for k, r0 in enumerate(starts):
        if r0 > pos:
            pieces.append(x[pos:r0])
        pieces.append(x[r0:r0 + m] + upd[k * m:(k + 1) * m])
        pos = r0 + m
    if pos < x.shape[0]:
        pieces.append(x[pos:])
    return jnp.concatenate(pieces, axis=0)


def _halves(top, bottom):
    return jnp.concatenate([top[:CHUNK], bottom[CHUNK:]], axis=0)


def _stack2(x, cs):
    return jnp.concatenate([x * cs["m0"], x * cs["m1"]], axis=0)


def _rwkv_chunk(r, k, v, lw, la, w2, a2, w0, a0, kkv, kav, ht, cs, rev):
    pairs = range(len(r))
    tl, lab = _bf(jnp.tanh(lw)), _bf(la)
    z = [w0[p] + _dot(tl, w2[p]) for p in pairs]
    za = [_dot(lab, a2[p]) for p in pairs]
    kkr = [k[p] * kkv[p] for p in pairs]
    ss = [_head_sums(x * x, cs) for x in kkr]
    ld = [-(DECAY_SCALE * LOG2E) * _sigmoid(x) for x in z]
    cl = [_cumsum_rows(x, rev) for x in ld]
    a = [_sigmoid(a0[p] + za[p]) for p in pairs]
    kk = [x / jnp.maximum(jnp.sqrt(s), 1e-12) for x, s in zip(kkr, ss)]
    kd = [k[p] * (1.0 + (a[p] - 1.0) * kav[p]) for p in pairs]
    bb = [x * y for x, y in zip(kk, a)]
    ctot = [x[0:1] if rev else x[CHUNK - 1:CHUNK] for x in cl]
    e_neg = [jnp.exp2(-x) for x in cl]
    e_tail = [jnp.exp2(c - x) for c, x in zip(ctot, cl)]
    at = [_bf(_stack2(-kk[p] * jnp.exp2(cl[p] - ld[p]), cs)) for p in pairs]
    rt = [_bf(_stack2(r[p] * jnp.exp2(cl[p]), cs)) for p in pairs]
    rhs = [_bf(jnp.concatenate([bb[p] * e_neg[p], kd[p] * e_neg[p]], axis=0)) for p in pairs]
    g = [_dot_nt(jnp.concatenate([x, y], axis=0), w) for x, y, w in zip(at, rt, rhs)]
    g1 = [x[:2 * CHUNK] for x in g]
    g2 = [x[2 * CHUNK:] for x in g]
    strict, incl = cs["strict"], cs["incl"]
    g1r = [pltpu.roll(x, C_HEAD, 1) for x in g1]
    n_ab = [jnp.where(strict, _halves(x, y), 0.0) for x, y in zip(g1, g1r)]
    n_ak = [_bf(jnp.where(strict, _halves(y, x), 0.0)) for x, y in zip(g1, g1r)]
    t = _unit_tri_inverse(n_ab, cs, rev)
    vs = [_bf(_stack2(x, cs)) for x in v]
    hkv = [_bf(x.T) for x in ht]
    xx = [_bf(_dot(jnp.concatenate([at[p], n_ak[p]], axis=1), jnp.concatenate([hkv[p], vs[p]], axis=0)))
          for p in pairs]
    u =[_bf(_dot(_bf(a_), b_)) for a_, b_ in zip(t, xx)]
    bk = [_bf(jnp.concatenate([_stack2(bb[p] * e_tail[p], cs), _stack2(kd[p] * e_tail[p], cs)], axis=0))
          for p in pairs]
    uv = [jnp.concatenate([a_, b_], axis=0) for a_, b_ in zip(u, vs)]
    dh = [_dot_tn(a_, b_) for a_, b_ in zip(uv, bk)]
    ht_new = [ht[p] * jnp.exp2(ctot[p]) + dh[p] for p in pairs]
    g2r = [pltpu.roll(x, C_HEAD, 1) for x in g2]
    n_rb = [_bf(jnp.where(incl, _halves(x, y), 0.0)) for x, y in zip(g2, g2r)]
    n_rk = [_bf(jnp.where(incl, _halves(y, x), 0.0)) for x, y in zip(g2, g2r)]
    ysum = [_dot(jnp.concatenate([rt[p], n_rb[p], n_rk[p]], axis=1),
                 jnp.concatenate([hkv[p], u[p], vs[p]], axis=0)) for p in pairs]
    y = [x[:CHUNK] + x[CHUNK:] for x in ysum]
    return y, ht_new, kd


def _rwkv_combine(y, yf, r, k, v, kd, laf, a2f, a0f, kav, rkv, lng, lnb, cs):
    pairs = range(len(y))
    inv_n = 1.0 / C_HEAD
    lafb = _bf(laf)
    zf = [_dot(lafb, a2f[p]) for p in pairs]
    ysum = [a + b for a, b in zip(y, yf)]
    mu = [_head_sums(x, cs) * inv_n for x in ysum]
    dev = [a - b for a, b in zip(ysum, mu)]
    var = [_head_sums(x * x, cs) * inv_n for x in dev]
    a_f = [_sigmoid(a0f[p] + zf[p]) for p in pairs]
    kd_f = [k[p] * (1.0 + (a_f[p] - 1.0) * kav[p]) for p in pairs]
    bsum = [_head_sums(r[p] * (kd_f[p] + kd[p]) * rkv[p], cs) for p in pairs]
    return [dev[p] * lax.rsqrt(var[p] + GN_EPS) * lng[p] + lnb[p] + bsum[p] * v[p] for p in pairs]


def _rwkv_scan_kernel(*refs, rev, nchunk, npair, combine):
    if combine:
        (r_ref, k_ref, v_ref, lw_ref, la_ref, w2_ref, a2_ref, w0_ref, a0_ref, kk_ref, ka_ref, s0_ref,
         yf_ref, laf_ref, a2f_ref, a0f_ref, rk_ref, lng_ref, lnb_ref, o_ref, sfin_ref, ht_ref) = refs
    else:
        (r_ref, k_ref, v_ref, lw_ref, la_ref, w2_ref, a2_ref, w0_ref, a0_ref, kk_ref, ka_ref, s0_ref,
         o_ref, sfin_ref, ht_ref) = refs
    c = pl.program_id(2)

    @pl.when(c == 0)
    def _():
        ht_ref[...] = s0_ref[0]

    cs = _pair_consts(rev)

    def body(ci, carry):
        cc = (nchunk - 1 - ci) if rev else ci
        rows = pl.ds(pl.multiple_of(cc * CHUNK, CHUNK), CHUNK)
        lw = lw_ref[0, 0, rows, :]
        la = la_ref[0, 0, rows, :]
        lanes = [slice(pr * LANES, (pr + 1) * LANES) for pr in range(npair)]
        r = [r_ref[0, 0, rows, ln] for ln in lanes]
        k = [k_ref[0, 0, rows, ln] for ln in lanes]
        v = [v_ref[0, 0, rows, ln] for ln in lanes]
        kav = [ka_ref[:, ln] for ln in lanes]
        y, ht_new, kd = _rwkv_chunk(
            r, k, v, lw, la, [w2_ref[0, :, ln] for ln in lanes], [a2_ref[0, :, ln] for ln in lanes],
            [w0_ref[0, :, ln] for ln in lanes], [a0_ref[0, :, ln] for ln in lanes],
            [kk_ref[:, ln] for ln in lanes], kav, [ht_ref[pr] for pr in range(npair)], cs, rev)
        for pr in range(npair):
            ht_ref[pr] = ht_new[pr]
        if combine:
            y = _rwkv_combine(
                y, [yf_ref[0, rows, ln] for ln in lanes], r, k, v, kd, laf_ref[0, 0, rows, :],
                [a2f_ref[0, :, ln] for ln in lanes], [a0f_ref[0, :, ln] for ln in lanes], kav,
                [rk_ref[:, ln] for ln in lanes], [lng_ref[:, ln] for ln in lanes],
                [lnb_ref[:, ln] for ln in lanes], cs)
        for pr in range(npair):
            o_ref[0, rows, lanes[pr]] = y[pr]
        return carry

    lax.fori_loop(0, nchunk, body, 0, unroll=2 if nchunk % 2 == 0 else 1)

    @pl.when(c == pl.num_programs(2) - 1)
    def _():
        sfin_ref[0] = ht_ref[...]


def _rwkv_scan(rkv, small, prm, s0, rev, tblk, npair, y_fwd=None):
    _, bsz, seq, d = rkv.shape
    nblk = seq // tblk
    width = npair * LANES
    e = 1 if rev else 0
    blk = (lambda c: nblk - 1 - c) if rev else (lambda c: c)
    combine = y_fwd is not None

    def tok(m):
        return pl.BlockSpec((1, 1, tblk, width), lambda b, p, c: (m, b, blk(c), p))

    def lora(m, half):
        return pl.BlockSpec((1, 1, tblk, LANES), lambda b, p, c: (m, b, blk(c), half))

    def mat(idx):
        return pl.BlockSpec((1, LANES, width), lambda b, p, c: (idx, 0, p))

    def vec3(idx):
        return pl.BlockSpec((1, 1, width), lambda b, p, c: (idx, 0, p))

    vec = pl.BlockSpec((1, width), lambda b, p, c: (0, p))
    state = pl.BlockSpec((1, npair, LANES, LANES), lambda b, p, c: (b, p, 0, 0))
    out_tok = pl.BlockSpec((1, tblk, width), lambda b, p, c: (b, blk(c), p))
    in_specs = [tok(0), tok(1), tok(2), lora(1, e), lora(2, e), mat(e), mat(e), vec3(e), vec3(e), vec, vec, state]
    args = [rkv, rkv, rkv, small, small, prm["w2"], prm["a2"], prm["w0"], prm["a0"], prm["k_k"], prm["k_a"], s0]
    if combine:
        in_specs += [out_tok, lora(2, 0), mat(0), vec3(0), vec, vec, vec]
        args += [y_fwd, small, prm["a2"], prm["a0"], prm["r_k"], prm["ln_g"], prm["ln_b"]]
    return pl.pallas_call(
        functools.partial(_rwkv_scan_kernel, rev=rev, nchunk=tblk // CHUNK, npair=npair, combine=combine),
        grid=(bsz, d // width, nblk),
        in_specs=in_specs,
        out_specs=[out_tok, state],
        out_shape=[
            jax.ShapeDtypeStruct((bsz, seq, d), F32),
            jax.ShapeDtypeStruct((bsz, d // LANES, LANES, LANES), F32),
        ],
        scratch_shapes=[pltpu.VMEM((npair, LANES, LANES), F32)],
        compiler_params=_params(("arbitrary", "arbitrary", "arbitrary")),
        name="rwkv7_bwd" if rev else "rwkv7_fwd",
    )(*args)


def _rk_out_kernel(z_ref, gs_ref, g2_ref, wo_ref, h_ref, gt_ref, o_ref):
    gate = _dot(_bf(_sigmoid(gs_ref[0, 0])), g2_ref[...])
    y = _dot(_bf(z_ref[0] * gate), wo_ref[...])
    o_ref[0] = h_ref[0] + gt_ref[0] * y


def _rk_out(z, small, g2, wo, h, gt, tm):
    bsz, seq, d = h.shape
    row = lambda b, i: (b, i, 0)
    glora = g2.shape[0]
    return pl.pallas_call(
        _rk_out_kernel,
        grid=(bsz, seq // tm),
        in_specs=[
            pl.BlockSpec((1, tm, d), row),
            pl.BlockSpec((1, 1, tm, glora), lambda b, i: (0, b, i, 0)),
            pl.BlockSpec((glora, d), lambda b, i: (0, 0)),
            pl.BlockSpec((d, d), lambda b, i: (0, 0)),
            pl.BlockSpec((1, tm, d), row),
            pl.BlockSpec((1, 1, d), lambda b, i: (b, 0, 0)),
        ],
        out_specs=pl.BlockSpec((1, tm, d), row),
        out_shape=jax.ShapeDtypeStruct((bsz, seq, d), F32),
        compiler_params=_params(("arbitrary", "arbitrary")),
        name="rwkv_outproj",
    )(z, small, g2, wo, h, gt)


def _rope_tables(seq):
    t = jnp.arange(seq, dtype=jnp.int32)
    rows = (t // GRID_W).astype(F32)
    cols = (t % GRID_W).astype(F32)
    half = HEAD_DIM // 2
    n_freq = half // 2
    inv = ROPE_BASE ** (-jnp.arange(n_freq, dtype=F32) / n_freq)
    lane = jnp.arange(HEAD_DIM)
    pos = jnp.where((lane < half)[None, :], rows[:, None], cols[:, None])
    ang = pos * inv[lane % n_freq][None, :]
    cos, sin = jnp.cos(ang), jnp.sin(ang)
    first = ((lane % half) < n_freq)[None, :]
    return cos, jnp.where(first, -sin, 0.0), jnp.where(first, 0.0, sin)


def _pad_lanes(w, axis):
    pad = [(0, 0)] * w.ndim
    pad[axis] = (0, LANES - w.shape[axis])
    return jnp.pad(w, pad)


class _Tiles(NamedTuple):
    rows: int
    mlp_rows: int
    ff: int
    in_cols: int
    hgrn_tokens: int
    rwkv_tokens: int


def _tiles(seq, n_in, latent):
    if latent:
        return _Tiles(rows=512, mlp_rows=1024, ff=1024, in_cols=n_in // 2, hgrn_tokens=512, rwkv_tokens=256)
    return _Tiles(rows=seq, mlp_rows=seq, ff=512, in_cols=512, hgrn_tokens=seq, rwkv_tokens=seq)


HGRN_HEADS_PER_STEP = 4
RWKV_PAIRS_PER_STEP = 16

def kernel(x, c, ctx, c_ctx, mod_w, mod_b, norm_mix, norm_ffn, ffn_up, ffn_down, ab_w_in, ab_w_out, attn_sink, hgrn_lb, hgrn_onorm, rk_mix, rk_wr, rk_wk, rk_wv, rk_wo, rk_w0, rk_w1, rk_w2, rk_a0, rk_a1, rk_a2, rk_g1, rk_g2, rk_kk, rk_ka, rk_rk, rk_ln_g, rk_ln_b, final_norm):
    bsz, seq, d = x.shape
    lc = ctx.shape[1]
    depth = mod_w.shape[0]
    n_in = ab_w_in.shape[2]
    tl, tc = _tiles(seq, n_in, latent=True), _tiles(lc, n_in, latent=False)
    assert bsz + 1 <= 8 and seq % tl.mlp_rows == 0 and lc % CHUNK == 0

    cond8 = jnp.zeros((8, d), F32).at[:bsz].set(c).at[bsz].set(c_ctx)
    mod = _modulation(cond8, mod_w, mod_b)
    lb_all = jnp.cumsum(jax.nn.softmax(hgrn_lb.astype(F32), axis=0), axis=0)
    rope = _rope_tables(seq)

    h, hc = x, ctx
    for layer in range(depth):
        last = layer == depth - 1
        jl = layer // 2
        m_lat = mod[layer, :bsz].reshape(bsz, 1, 6, d)
        m_ctx = jnp.broadcast_to(mod[layer, bsz].reshape(1, 1, 6, d), (bsz, 1, 6, d))
        sh1, sc1, gt1, sh2, sc2, gt2 = (m_lat[:, :, i] for i in range(6))
        csh1, csc1, cgt1, csh2, csc2, cgt2 = (m_ctx[:, :, i] for i in range(6))
        g_mix = norm_mix[layer].reshape(1, d)
        g_ffn = norm_ffn[layer].reshape(1, d)
        if layer % 2 == 0:
            w_in = _bf(ab_w_in[jl])
            w_out = _bf(ab_w_out[jl])
            sink = attn_sink[jl].astype(F32)
            lb = lb_all[jl].reshape(1, B_W)
            onorm = hgrn_onorm[jl].reshape(1, B_W).astype(F32)
            pc = _inproj(hc, g_mix, csc1, csh1, w_in, None, tc.rows, tc.in_cols)
            pl_ = _inproj_stream(h, g_mix, sc1, sh1, w_in, rope, tl.rows, tl.in_cols)
            oa = _win_attn(sink, pl_, pc)
            zeros = jnp.zeros((bsz, B_HEADS, HEAD_DIM, HEAD_DIM), F32)
            nhead = HGRN_HEADS_PER_STEP
            ocf, scf = _hgrn_scan(pc, lb, zeros, False, tc.hgrn_tokens, nhead)
            ocb, scb = _hgrn_scan(pc, lb, zeros, True, tc.hgrn_tokens, nhead)
            olf, _ = _hgrn_scan(pl_, lb, scf, False, tl.hgrn_tokens, nhead)
            olb, _ = _hgrn_scan(pl_, lb, scb, True, tl.hgrn_tokens, nhead)
            h = _ab_out(oa, olf, olb, pl_, onorm, w_out, h, gt1, tl.rows)
            if not last:
                oca = _ctx_attn(sink, pc)
                hc = _ab_out(oca, ocf, ocb, pc, onorm, w_out, hc, cgt1, tc.rows)
        else:
            w_big = _bf(jnp.stack([rk_wr[jl], rk_wk[jl], rk_wv[jl]]))
            w_small = _bf(jnp.stack([
                rk_g1[jl],
                jnp.concatenate([_pad_lanes(rk_w1[jl, 0], 1), _pad_lanes(rk_w1[jl, 1], 1)], axis=1),
                jnp.concatenate([_pad_lanes(rk_a1[jl, 0], 1), _pad_lanes(rk_a1[jl, 1], 1)], axis=1),
            ]))
            assert w_small.shape[-1] == 2 * LANES
            mix = rk_mix[jl]
            prm = dict(
                w2=_bf(_pad_lanes(rk_w2[jl], 1)), a2=_bf(_pad_lanes(rk_a2[jl], 1)),
                w0=rk_w0[jl].reshape(2, 1, d), a0=rk_a0[jl].reshape(2, 1, d),
                k_k=rk_kk[jl].reshape(1, d), k_a=rk_ka[jl].reshape(1, d), r_k=rk_rk[jl].reshape(1, d),
                ln_g=rk_ln_g[jl].reshape(1, d), ln_b=rk_ln_b[jl].reshape(1, d),
            )
            g2 = _bf(rk_g2[jl])
            wo = _bf(rk_wo[jl])
            mix = mix[jnp.array([0, 2, 3, 5, 1, 4])]
            rkv_c, sm_c = _rk_proj(hc, g_mix, csc1, csh1, mix, w_big, w_small, tc.rows)
            rkv_l, sm_l = _rk_proj(h, g_mix, sc1, sh1, mix, w_big, w_small, tl.rows)
            zeros = jnp.zeros((bsz, d // LANES, LANES, LANES), F32)
            npair = RWKV_PAIRS_PER_STEP
            ycf, s_f = _rwkv_scan(rkv_c, sm_c, prm, zeros, False, tc.rwkv_tokens, npair)
            zc, s_b = _rwkv_scan(rkv_c, sm_c, prm, zeros, True, tc.rwkv_tokens, npair, y_fwd=ycf)
            ylf, _ = _rwkv_scan(rkv_l, sm_l, prm, s_f, False, tl.rwkv_tokens, npair)
            zl, _ = _rwkv_scan(rkv_l, sm_l, prm, s_b, True, tl.rwkv_tokens, npair, y_fwd=ylf)
            h = _rk_out(zl, sm_l, g2, wo, h, gt1, tl.rows)
            if not last:
                hc = _rk_out(zc, sm_c, g2, wo, hc, cgt1, tc.rows)
        w_up = _bf(ffn_up[layer])
        w_dn = _bf(ffn_down[layer])
        final_gain = final_norm.reshape(1, d) if last else None
        h = _mlp(h, g_ffn, sc2, sh2, gt2, w_up, w_dn, final_gain, tl.mlp_rows, tl.ff)
        if not last:
            hc = _mlp(hc, g_ffn, csc2, csh2, cgt2, w_up, w_dn, None, tc.mlp_rows, tc.ff)
    return h
```
